```python
import math
import jax
import jax.numpy as jnp
from jax import lax
import numpy as np

D_MODEL = 1024
BATCH = 2
SEQ = 8192
DEPTH = 2
DEC_BATCH = 128
DEC_SEQ = 4
PAST_LEN = 8192
PAGE_SIZE = 128

M_HEADS = 4
M_HEAD_DIM = 128
M_WIDTH = M_HEADS * M_HEAD_DIM
M_CHUNK = 64
R_WIDTH = 512
R_BLOCKS = 4
R_BLOCK_DIM = R_WIDTH // R_BLOCKS
CONV_W = 4
LRU_C = 8.0
S_HEADS = 8
S_KV_HEADS = 2
S_GROUP = S_HEADS // S_KV_HEADS
S_HEAD_DIM = 64
S_WIDTH = S_HEADS * S_HEAD_DIM
S_KV_WIDTH = S_KV_HEADS * S_HEAD_DIM
WINDOW = 128
S_BLOCK = 128
N_BUCKETS = 32
MAX_DISTANCE = 128
N_MEM = 256
X_HEADS = 4
X_HEAD_DIM = 128
X_WIDTH = X_HEADS * X_HEAD_DIM
N_GROUPS = 4
EXPERTS_PER_GROUP = 8
N_EXPERTS = N_GROUPS * EXPERTS_PER_GROUP
TOP_K = 2
D_EXPERT = 512
MOE_BLOCK = 128
EPS = 1e-6
SPLIT_SIZES = (M_WIDTH, M_WIDTH, M_WIDTH, M_WIDTH, M_HEADS, M_HEADS,
               R_WIDTH, R_WIDTH,
               S_WIDTH, S_KV_WIDTH, S_KV_WIDTH,
               D_MODEL, D_MODEL, D_MODEL)
IN_WIDTH = sum(SPLIT_SIZES)

kernel_name = "hybrid_mlstm_rglru_swa_hmoe_step"


def rmsnorm(x, g):
    xf = x.astype(jnp.float32)
    y = xf * lax.rsqrt(jnp.mean(xf * xf, axis=-1, keepdims=True) + EPS)
    return (y * g.astype(jnp.float32)).astype(x.dtype)


def split_cols(proj):
    cuts = np.cumsum(SPLIT_SIZES)[:-1].tolist()
    return jnp.split(proj, cuts, axis=-1)


def _mlstm_chunk(carry, inp):
    c, n, m = carry
    q, k, v, li, lf = inp
    L = q.shape[2]
    b = jnp.cumsum(lf, axis=-1)
    causal = jnp.tril(jnp.ones((L, L), dtype=bool))
    log_prev = b + m[..., None]
    log_in = jnp.where(causal, b[..., :, None] - b[..., None, :] + li[..., None, :], -jnp.inf)
    m_t = jnp.maximum(log_prev, jnp.max(log_in, axis=-1))
    w_prev = jnp.exp(log_prev - m_t)
    s = jnp.einsum("bhtd,bhsd->bhts", q, k) * jnp.exp(log_in - m_t[..., None])
    num = w_prev[..., None] * jnp.einsum("bhtd,bhde->bhte", q, c) + jnp.einsum("bhts,bhse->bhte", s, v)
    den = w_prev * jnp.einsum("bhtd,bhd->bht", q, n) + jnp.sum(s, axis=-1)
    h = num / jnp.maximum(jnp.abs(den), jnp.exp(-m_t))[..., None]
    log_prev_end = b[..., -1] + m
    log_in_end = b[..., -1:] - b + li
    m_end = jnp.maximum(log_prev_end, jnp.max(log_in_end, axis=-1))
    w_pe = jnp.exp(log_prev_end - m_end)
    w_e = jnp.exp(log_in_end - m_end[..., None])
    c_new = w_pe[..., None, None] * c + jnp.einsum("bhs,bhsd,bhse->bhde", w_e, k, v)
    n_new = w_pe[..., None] * n + jnp.einsum("bhs,bhsd->bhd", w_e, k)
    return (c_new, n_new, m_end), h


def mlstm(q, k, v, i_pre, f_pre, c0, n0, m0):
    B, S, H, D = q.shape
    L = M_CHUNK if S % M_CHUNK == 0 else S
    nc = S // L
    f32 = jnp.float32

    def chunks(t):
        t = t.astype(f32).reshape((B, nc, L) + t.shape[2:])
        return jnp.swapaxes(jnp.moveaxis(t, 1, 0), 2, 3)

    xs = (chunks(q), chunks(k), chunks(v), chunks(i_pre),
          chunks(jax.nn.log_sigmoid(f_pre.astype(f32))))
    (c, n, m), h = lax.scan(_mlstm_chunk, (c0.astype(f32), n0.astype(f32), m0.astype(f32)), xs)
    h = jnp.moveaxis(jnp.swapaxes(h, 2, 3), 0, 1).reshape(B, S, H, D)
    return h, c, n, m


def head_layernorm(h, g):
    d = h - jnp.mean(h, axis=-1, keepdims=True)
    y = d * lax.rsqrt(jnp.mean(d * d, axis=-1, keepdims=True) + EPS)
    return y.reshape(h.shape[:2] + (-1,)) * g.astype(jnp.float32)


def causal_conv(x, state, w, b):
    S = x.shape[1]
    xp = jnp.concatenate([state.astype(x.dtype), x], axis=1)
    y = b + sum(xp[:, j:j + S] * w[j] for j in range(CONV_W))
    return y, xp[:, S:]


def _lin_combine(left, right):
    a1, b1 = left
    a2, b2 = right
    return a1 * a2, a2 * b1 + b2


def rglru(x, h0, wa, ba, wx, bx, lam):
    B, S, _ = x.shape
    f32 = jnp.float32
    xf = x.astype(f32)
    xb = xf.reshape(B, S, R_BLOCKS, R_BLOCK_DIM)
    r = jax.nn.sigmoid(jnp.einsum("bsnd,nde->bsne", xb, wa.astype(f32)).reshape(B, S, R_WIDTH) + ba.astype(f32))
    i = jax.nn.sigmoid(jnp.einsum("bsnd,nde->bsne", xb, wx.astype(f32)).reshape(B, S, R_WIDTH) + bx.astype(f32))
    log_a = -LRU_C * r * jax.nn.softplus(-lam.astype(f32))
    a = jnp.exp(log_a)
    u = jnp.sqrt(-jnp.expm1(2.0 * log_a)) * (i * xf)
    u = u.at[:, 0].add(a[:, 0] * h0.astype(f32))
    _, h = lax.associative_scan(_lin_combine, (a, u), axis=1)
    return h, h[:, -1]


def t5_bucket(dist):
    max_exact = N_BUCKETS // 2
    d = jnp.maximum(dist, 0)
    large = max_exact + (jnp.log(jnp.maximum(d, 1).astype(jnp.float32) / max_exact)
                         / math.log(MAX_DISTANCE / max_exact) * (N_BUCKETS - max_exact)).astype(jnp.int32)
    return jnp.where(d < max_exact, d, jnp.minimum(large, N_BUCKETS - 1))


def t5_bias(dist, rel_bias):
    b = rel_bias.astype(jnp.float32)[t5_bucket(dist)]
    return jnp.moveaxis(b, -1, 0).reshape((S_KV_HEADS, S_GROUP) + dist.shape)


def sink_softmax(scores, mask, sinks):
    sink = sinks.astype(jnp.float32).reshape(S_KV_HEADS, S_GROUP, 1, 1)
    scores = jnp.where(mask, scores, -jnp.inf)
    mx = jnp.maximum(jnp.max(scores, axis=-1, keepdims=True), sink)
    p = jnp.exp(scores - mx)
    return p / (jnp.sum(p, axis=-1, keepdims=True) + jnp.exp(sink - mx))


def swa_banded(q, k, v, sinks, rel_bias):
    B, S = q.shape[:2]
    nb = S // S_BLOCK
    qb = q.reshape(B, nb, S_BLOCK, S_KV_HEADS, S_GROUP, S_HEAD_DIM)

    def band(t):
        tb = t.reshape(B, nb, S_BLOCK, S_KV_HEADS, S_HEAD_DIM)
        prev = jnp.pad(tb, ((0, 0), (1, 0), (0, 0), (0, 0), (0, 0)))[:, :-1]
        return jnp.concatenate([prev, tb], axis=2)

    kk, vv = band(k), band(v)
    qi = jnp.arange(S_BLOCK)[:, None]
    kj = jnp.arange(2 * S_BLOCK)[None, :]
    dist = qi + S_BLOCK - kj
    in_band = (dist >= 0) & (dist <= WINDOW)
    not_first = (jnp.arange(nb) > 0)[:, None, None]
    mask = (in_band & (not_first | (kj >= S_BLOCK)))[None, :, None, None]
    scores = (jnp.einsum("bnqhgd,bnkhd->bnhgqk", qb, kk).astype(jnp.float32) * S_HEAD_DIM ** -0.5
              + t5_bias(dist, rel_bias))
    p = sink_softmax(scores, mask, sinks).astype(v.dtype)
    out = jnp.einsum("bnhgqk,bnkhd->bnqhgd", p, vv).reshape(B, S, S_WIDTH)
    keep = min(WINDOW, S)
    return out, k[:, S - keep:], v[:, S - keep:]


def swa_cached(q, k, v, k_cache, v_cache, sinks, rel_bias):
    B, T = q.shape[:2]
    W = k_cache.shape[1]
    kk = jnp.concatenate([k_cache.astype(k.dtype), k], axis=1)
    vv = jnp.concatenate([v_cache.astype(v.dtype), v], axis=1)
    qpos = PAST_LEN + jnp.arange(T)
    kpos = PAST_LEN - W + jnp.arange(W + T)
    dist = qpos[:, None] - kpos[None, :]
    mask = (dist >= 0) & (dist <= WINDOW)
    scores = (jnp.einsum("bqhgd,bkhd->bhgqk", q, kk).astype(jnp.float32) * S_HEAD_DIM ** -0.5
              + t5_bias(dist, rel_bias))
    p = sink_softmax(scores, mask, sinks).astype(v.dtype)
    out = jnp.einsum("bhgqk,bkhd->bqhgd", p, vv).reshape(B, T, S_WIDTH)
    return out, kk[:, T:], vv[:, T:]


def mem_cross_attend(xn, mem_k, mem_v, wq, wo):
    B, S, _ = xn.shape
    q = (xn @ wq).reshape(B, S, X_HEADS, X_HEAD_DIM)
    s = jnp.einsum("bqhd,bmhd->bhqm", q, mem_k.astype(q.dtype)).astype(jnp.float32) * X_HEAD_DIM ** -0.5
    p = jax.nn.softmax(s, axis=-1).astype(q.dtype)
    o = jnp.einsum("bhqm,bmhd->bqhd", p, mem_v.astype(q.dtype)).reshape(B, S, X_WIDTH)
    return o @ wo


def grouped_experts(xf, expert_idx, gates, w_gate, w_up, w_down):
    T, D = xf.shape
    A = T * TOP_K
    f32 = jnp.float32
    flat_e = expert_idx.reshape(A).astype(jnp.int32)
    se, order = lax.sort((flat_e, jnp.arange(A, dtype=jnp.int32)), num_keys=1, is_stable=True)
    counts = jnp.bincount(flat_e, length=N_EXPERTS)
    starts = jnp.cumsum(counts) - counts
    padded = (counts + MOE_BLOCK - 1) // MOE_BLOCK * MOE_BLOCK
    pad_ends = jnp.cumsum(padded)
    dest = pad_ends[se] - padded[se] + jnp.arange(A, dtype=jnp.int32) - starts[se]
    n_blocks = -(-(A + N_EXPERTS * (MOE_BLOCK - 1)) // MOE_BLOCK)
    P = n_blocks * MOE_BLOCK
    slot_tok = jnp.full((P,), T, jnp.int32).at[dest].set(order // TOP_K)
    slot_gate = jnp.zeros((P,), f32).at[dest].set(gates.reshape(A)[order].astype(f32))
    block_e = jnp.minimum(jnp.searchsorted(pad_ends, jnp.arange(n_blocks) * MOE_BLOCK, side="right"),
                          N_EXPERTS - 1)
    xpad = jnp.concatenate([xf, jnp.zeros((1, D), xf.dtype)], axis=0)
    xb = xpad[slot_tok].reshape(n_blocks, MOE_BLOCK, D)

    def expert_block(args):
        xblk, e = args
        h = jax.nn.silu(xblk @ w_gate[e]) * (xblk @ w_up[e])
        return h @ w_down[e]

    yb = lax.map(expert_block, (xb, block_e)).reshape(P, D).astype(f32)
    out = jnp.zeros((T + 1, D), f32).at[slot_tok].add(yb * slot_gate[:, None])
    return out[:T].astype(xf.dtype)


def hier_moe(x, w_rg, w_re, w_gate, w_up, w_down):
    B, S, D = x.shape
    xf = x.reshape(B * S, D)
    g_logits = (xf @ w_rg).astype(jnp.float32)
    g_val, g_idx = lax.top_k(g_logits, 1)
    g_w = jnp.exp(g_val - jax.nn.logsumexp(g_logits, axis=-1, keepdims=True))
    e_logits = (xf @ w_re).astype(jnp.float32).reshape(-1, N_GROUPS, EXPERTS_PER_GROUP)
    e_logits = jnp.take_along_axis(e_logits, g_idx[:, :, None], axis=1)[:, 0]
    e_val, e_idx = lax.top_k(e_logits, TOP_K)
    gates = jax.nn.softmax(e_val, axis=-1) * g_w
    expert_idx = g_idx * EXPERTS_PER_GROUP + e_idx
    return grouped_experts(xf, expert_idx, gates, w_gate, w_up, w_down).reshape(B, S, D)


def trunk_layer(x, mem_k, mem_v, c0, n0, m0, h0, conv0, swa_k0, swa_v0, w, rel_bias):
    B, S, _ = x.shape
    xn = rmsnorm(x, w["norm_mix"])
    (mq, mk, mv, mo, mi, mf, rx, rg, sq, sk, sv, gm, gr, gs) = split_cols(xn @ w["w_in"])
    hshape = (B, S, M_HEADS, M_HEAD_DIM)
    h_m, c1, n1, m1 = mlstm(mq.reshape(hshape), mk.reshape(hshape) * M_HEAD_DIM ** -0.5, mv.reshape(hshape),
                            mi + w["m_igate_b"], mf + w["m_fgate_b"], c0, n0, m0)
    h_m = head_layernorm(h_m, w["m_head_norm"]).astype(x.dtype) * jax.nn.sigmoid(mo)
    rc, conv1 = causal_conv(rx, conv0, w["r_conv_w"], w["r_conv_b"])
    h_r, h1 = rglru(rc, h0, w["r_gate_a_w"], w["r_gate_a_b"], w["r_gate_x_w"], w["r_gate_x_b"], w["r_lambda"])
    h_r = h_r.astype(x.dtype) * jax.nn.gelu(rg)
    q = sq.reshape(B, S, S_KV_HEADS, S_GROUP, S_HEAD_DIM)
    k = sk.reshape(B, S, S_KV_HEADS, S_HEAD_DIM)
    v = sv.reshape(B, S, S_KV_HEADS, S_HEAD_DIM)
    if swa_k0 is None:
        h_s, k1, v1 = swa_banded(q, k, v, w["swa_sinks"], rel_bias)
    else:
        h_s, k1, v1 = swa_cached(q, k, v, swa_k0, swa_v0, w["swa_sinks"], rel_bias)
    merged = (jax.nn.sigmoid(gm) * (h_m @ w["w_branch_m"])
              + jax.nn.sigmoid(gr) * (h_r @ w["w_branch_r"])
              + jax.nn.sigmoid(gs) * (h_s @ w["w_branch_s"]))
    x = x + merged @ w["w_out"]
    x = x + mem_cross_attend(rmsnorm(x, w["norm_mem"]), mem_k, mem_v, w["xq_w"], w["xo_w"])
    x = x + hier_moe(rmsnorm(x, w["norm_ffn"]), w["router_group_w"], w["router_expert_w"],
                     w["moe_w_gate"], w["moe_w_up"], w["moe_w_down"])
    return x, (k1, v1, c1, n1, m1, h1, conv1)


def setup_inputs(seed: int = 0) -> dict:
    key = jax.random.key(seed)
    ks = list(jax.random.split(key, 64))
    f32 = jnp.float32

    def nrm(shape, scale=1.0):
        return jax.random.normal(ks.pop(), shape, f32) * scale

    def gain(shape):
        return 1.0 + nrm(shape, 0.02)

    n_swa = min(WINDOW, PAST_LEN)
    u = jax.random.uniform(ks.pop(), (DEPTH, R_WIDTH), f32, 0.9, 0.999)
    a0 = u ** (1.0 / LRU_C)
    r_lambda = jnp.log(a0) - jnp.log1p(-a0)
    return {
        "x_prompt": nrm((BATCH, SEQ, D_MODEL)),
        "x_sample": nrm((DEC_BATCH, DEC_SEQ, D_MODEL)),
        "mem_prompt": nrm((BATCH, N_MEM, D_MODEL)),
        "cache_mem_k": nrm((DEPTH, DEC_BATCH, N_MEM, X_HEADS, X_HEAD_DIM)),
        "cache_mem_v": nrm((DEPTH, DEC_BATCH, N_MEM, X_HEADS, X_HEAD_DIM)),
        "cache_swa_k": nrm((DEPTH, DEC_BATCH, n_swa, S_KV_HEADS, S_HEAD_DIM)),
        "cache_swa_v": nrm((DEPTH, DEC_BATCH, n_swa, S_KV_HEADS, S_HEAD_DIM)),
        "state_mlstm_C": nrm((DEPTH, DEC_BATCH, M_HEADS, M_HEAD_DIM, M_HEAD_DIM), 0.1),
        "state_mlstm_n": nrm((DEPTH, DEC_BATCH, M_HEADS, M_HEAD_DIM), 0.1),
        "state_mlstm_m": nrm((DEPTH, DEC_BATCH, M_HEADS), 0.5),
        "state_rglru_h": nrm((DEPTH, DEC_BATCH, R_WIDTH), 0.5),
        "state_rglru_conv": nrm((DEPTH, DEC_BATCH, CONV_W - 1, R_WIDTH)),
        "norm_mix": gain((DEPTH, D_MODEL)),
        "w_in": nrm((DEPTH, D_MODEL, IN_WIDTH), D_MODEL ** -0.5),
        "m_igate_b": nrm((DEPTH, M_HEADS), 0.1),
        "m_fgate_b": 3.0 + nrm((DEPTH, M_HEADS), 0.5),
        "m_head_norm": gain((DEPTH, M_WIDTH)),
        "r_conv_w": nrm((DEPTH, CONV_W, R_WIDTH), CONV_W ** -0.5),
        "r_conv_b": nrm((DEPTH, R_WIDTH), 0.01),
        "r_gate_a_w": nrm((DEPTH, R_BLOCKS, R_BLOCK_DIM, R_BLOCK_DIM), R_BLOCK_DIM ** -0.5),
        "r_gate_a_b": nrm((DEPTH, R_WIDTH), 0.01),
        "r_gate_x_w": nrm((DEPTH, R_BLOCKS, R_BLOCK_DIM, R_BLOCK_DIM), R_BLOCK_DIM ** -0.5),
        "r_gate_x_b": nrm((DEPTH, R_WIDTH), 0.01),
        "r_lambda": r_lambda,
        "swa_sinks": nrm((DEPTH, S_HEADS)),
        "rel_bias": nrm((N_BUCKETS, S_HEADS), 0.5),
        "w_branch_m": nrm((DEPTH, M_WIDTH, D_MODEL), M_WIDTH ** -0.5),
        "w_branch_r": nrm((DEPTH, R_WIDTH, D_MODEL), R_WIDTH ** -0.5),
        "w_branch_s": nrm((DEPTH, S_WIDTH, D_MODEL), S_WIDTH ** -0.5),
        "w_out": nrm((DEPTH, D_MODEL, D_MODEL), D_MODEL ** -0.5),
        "norm_mem": gain((DEPTH, D_MODEL)),
        "xq_w": nrm((DEPTH, D_MODEL, X_WIDTH), D_MODEL ** -0.5),
        "xk_w": nrm((DEPTH, D_MODEL, X_WIDTH), D_MODEL ** -0.5),
        "xv_w": nrm((DEPTH, D_MODEL, X_WIDTH), D_MODEL ** -0.5),
        "xo_w": nrm((DEPTH, X_WIDTH, D_MODEL), X_WIDTH ** -0.5),
        "norm_ffn": gain((DEPTH, D_MODEL)),
        "router_group_w": nrm((DEPTH, D_MODEL, N_GROUPS), D_MODEL ** -0.5),
        "router_expert_w": nrm((DEPTH, D_MODEL, N_EXPERTS), D_MODEL ** -0.5),
        "moe_w_gate": nrm((DEPTH, N_EXPERTS, D_MODEL, D_EXPERT), D_MODEL ** -0.5),
        "moe_w_up": nrm((DEPTH, N_EXPERTS, D_MODEL, D_EXPERT), D_MODEL ** -0.5),
        "moe_w_down": nrm((DEPTH, N_EXPERTS, D_EXPERT, D_MODEL), D_EXPERT ** -0.5),
        "norm_final": gain((D_MODEL,)),
    }


def reference(x_prompt, x_sample, mem_prompt, cache_mem_k, cache_mem_v, cache_swa_k, cache_swa_v,
              state_mlstm_C, state_mlstm_n, state_mlstm_m, state_rglru_h, state_rglru_conv,
              norm_mix, w_in, m_igate_b, m_fgate_b, m_head_norm, r_conv_w, r_conv_b,
              r_gate_a_w, r_gate_a_b, r_gate_x_w, r_gate_x_b, r_lambda, swa_sinks, rel_bias,
              w_branch_m, w_branch_r, w_branch_s, w_out, norm_mem, xq_w, xk_w, xv_w, xo_w,
              norm_ffn, router_group_w, router_expert_w, moe_w_gate, moe_w_up, moe_w_down,
              norm_final):
    f32 = jnp.float32
    B = x_prompt.shape[0]
    xp, xs = x_prompt, x_sample
    zero_c = jnp.zeros((B, M_HEADS, M_HEAD_DIM, M_HEAD_DIM), f32)
    zero_n = jnp.zeros((B, M_HEADS, M_HEAD_DIM), f32)
    zero_m = jnp.zeros((B, M_HEADS), f32)
    zero_h = jnp.zeros((B, R_WIDTH), f32)
    zero_conv = jnp.zeros((B, CONV_W - 1, R_WIDTH), x_prompt.dtype)
    mem_k_list, mem_v_list, st_p, st_s = [], [], [], []
    for l in range(DEPTH):
        w = {
            "norm_mix": norm_mix[l], "w_in": w_in[l], "m_igate_b": m_igate_b[l], "m_fgate_b": m_fgate_b[l],
            "m_head_norm": m_head_norm[l], "r_conv_w": r_conv_w[l], "r_conv_b": r_conv_b[l],
            "r_gate_a_w": r_gate_a_w[l], "r_gate_a_b": r_gate_a_b[l], "r_gate_x_w": r_gate_x_w[l],
            "r_gate_x_b": r_gate_x_b[l], "r_lambda": r_lambda[l], "swa_sinks": swa_sinks[l],
            "w_branch_m": w_branch_m[l], "w_branch_r": w_branch_r[l], "w_branch_s": w_branch_s[l],
            "w_out": w_out[l], "norm_mem": norm_mem[l], "xq_w": xq_w[l], "xo_w": xo_w[l],
            "norm_ffn": norm_ffn[l], "router_group_w": router_group_w[l], "router_expert_w": router_expert_w[l],
            "moe_w_gate": moe_w_gate[l], "moe_w_up": moe_w_up[l], "moe_w_down": moe_w_down[l],
        }
        mk = (mem_prompt @ xk_w[l]).reshape(B, N_MEM, X_HEADS, X_HEAD_DIM)
        mv = (mem_prompt @ xv_w[l]).reshape(B, N_MEM, X_HEADS, X_HEAD_DIM)
        mem_k_list.append(mk)
        mem_v_list.append(mv)
        xp, sp = trunk_layer(xp, mk, mv, zero_c, zero_n, zero_m, zero_h, zero_conv, None, None, w, rel_bias)
        xs, ss = trunk_layer(xs, cache_mem_k[l], cache_mem_v[l], state_mlstm_C[l], state_mlstm_n[l],
                             state_mlstm_m[l], state_rglru_h[l], state_rglru_conv[l],
                             cache_swa_k[l], cache_swa_v[l], w, rel_bias)
        st_p.append(sp)
        st_s.append(ss)
    y_prompt = rmsnorm(xp, norm_final)
    y_sample = rmsnorm(xs, norm_final)
    mem_k_prompt = jnp.stack(mem_k_list)
    mem_v_prompt = jnp.stack(mem_v_list)
    (swa_k_prompt, swa_v_prompt, mlstm_C_prompt, mlstm_n_prompt, mlstm_m_prompt,
     rglru_h_prompt, rglru_conv_prompt) = [jnp.stack(s) for s in zip(*st_p)]
    (swa_k_sample, swa_v_sample, mlstm_C_sample, mlstm_n_sample, mlstm_m_sample,
     rglru_h_sample, rglru_conv_sample) = [jnp.stack(s) for s in zip(*st_s)]
    return (y_prompt, y_sample, mem_k_prompt, mem_v_prompt, swa_k_prompt, swa_v_prompt,
            mlstm_C_prompt, mlstm_n_prompt, mlstm_m_prompt, rglru_h_prompt, rglru_conv_prompt,
            swa_k_sample, swa_v_sample, mlstm_C_sample, mlstm_n_sample, mlstm_m_sample,
            rglru_h_sample, rglru_conv_sample)
```

```python
import functools
import math

import jax
import jax.numpy as jnp
from jax import lax
from jax.experimental import pallas as pl
from jax.experimental.pallas import tpu as pltpu

F32 = jnp.float32
BF16 = jnp.bfloat16
NEG_INF = float("-inf")

LANE = 128
SUBLANE = 8
BF16_SUBLANE = 16
VMEM_LIMIT_BYTES = 56 * 1024 * 1024

D_MODEL = 1024
M_HEADS = 4
M_HEAD_DIM = 128
M_WIDTH = M_HEADS * M_HEAD_DIM
R_WIDTH = 512
R_BLOCKS = 4
R_BLOCK_DIM = R_WIDTH // R_BLOCKS
CONV_W = 4
LRU_C = 8.0
S_HEADS = 8
S_KV_HEADS = 2
S_HEAD_DIM = 64
S_WIDTH = S_HEADS * S_HEAD_DIM
S_KV_WIDTH = S_KV_HEADS * S_HEAD_DIM
WINDOW = 128
N_BUCKETS = 32
MAX_DISTANCE = 128
N_MEM_HEADS = 4
X_HEAD_DIM = 128
X_WIDTH = N_MEM_HEADS * X_HEAD_DIM
N_GROUPS = 4
EXPERTS_PER_GROUP = 8
N_EXPERTS = N_GROUPS * EXPERTS_PER_GROUP
TOP_K = 2
D_EXPERT = 512
EPS = 1e-6

MLSTM_CHUNK = 256
MLSTM_AUG = 2 * M_HEAD_DIM
MLSTM_SHORT_CHUNK = LANE
SWA_SHORT_TQ = BF16_SUBLANE
XATTN_SHORT_BATCH = 8


def _cparams(*sem):
    return pltpu.CompilerParams(dimension_semantics=sem, vmem_limit_bytes=VMEM_LIMIT_BYTES)


def _rms(x, g):
    ms = jnp.mean(x * x, axis=-1, keepdims=True)
    return x * lax.rsqrt(ms + EPS) * g


def _sigmoid(x):
    return 1.0 / (1.0 + jnp.exp(-x))


def _softplus(x):
    return jnp.maximum(x, 0.0) + jnp.log1p(jnp.exp(-jnp.abs(x)))


def _log_sigmoid(x):
    return -_softplus(-x)


def _gelu_tanh(x):
    return 0.5 * x * (1.0 + jnp.tanh(math.sqrt(2.0 / math.pi) * (x + 0.044715 * (x * x * x))))


def _dot(a, b):
    return jnp.dot(a, b, preferred_element_type=F32)


def _dot_nt(a, b):
    return lax.dot_general(a, b, (((1,), (1,)), ((), ())), preferred_element_type=F32)


def _row_tile(n, pref=256):
    for t in (256, 128, 64, 32, 16, 8):
        if t <= pref and n % t == 0:
            return t
    raise ValueError(f"row count {n} is not a multiple of {SUBLANE}")


def _norm_matmul_kernel(x_ref, g_ref, w_ref, *out_refs, splits, normalize):
    x = x_ref[...]
    if normalize:
        x = _rms(x, g_ref[...])
    xb = x.astype(BF16)
    off = 0
    for o_ref, n in zip(out_refs, splits):
        o_ref[...] = _dot(xb, w_ref[:, off:off + n])
        off += n


def norm_matmul(x, g, w_bf16, splits, normalize=True):
    T, D = x.shape
    tm = _row_tile(T)
    N = w_bf16.shape[1]
    assert sum(splits) == N and all(s % LANE == 0 for s in splits)
    kern = functools.partial(_norm_matmul_kernel, splits=tuple(splits), normalize=normalize)
    return pl.pallas_call(
        kern,
        grid=(T // tm,),
        in_specs=[pl.BlockSpec((tm, D), lambda i: (i, 0)),
                  pl.BlockSpec((1, D), lambda i: (0, 0)),
                  pl.BlockSpec((D, N), lambda i: (0, 0))],
        out_specs=[pl.BlockSpec((tm, n), lambda i: (i, 0)) for n in splits],
        out_shape=[jax.ShapeDtypeStruct((T, n), F32) for n in splits],
        compiler_params=_cparams("parallel"),
        name="norm_matmul",
    )(x, g, w_bf16)


def _mlstm_kernel(q_ref, k_ref, v_ref, o_ref, gi_ref, gf_ref, bi_ref, bf_ref, hn_ref,
                  c0_ref, m0_ref, h_out, c_out, m_out, c_sc, m_sc, *pad_sc, L, RB, valid):
    @pl.when(pl.program_id(1) == 0)
    def _():
        c_sc[...] = c0_ref[0]
        m_sc[...] = m0_ref[0]

    if RB < L:
        for dst, src in zip(pad_sc, (q_ref, k_ref, v_ref)):
            dst[...] = jnp.zeros_like(dst)
            dst[0:RB, :] = src[0]
        q_all, k_all, v_all = (r[...] for r in pad_sc)
    else:
        q_all, k_all, v_all = q_ref[0], k_ref[0], v_ref[0]

    t_idx = lax.broadcasted_iota(jnp.int32, (L, L), 0)
    s_idx = lax.broadcasted_iota(jnp.int32, (L, L), 1)
    causal = t_idx >= s_idx
    eye = t_idx == s_idx
    lane_l = lax.broadcasted_iota(jnp.int32, (1, L), 1)
    row8 = lax.broadcasted_iota(jnp.int32, (SUBLANE, LANE), 0)
    ones_col = (lax.broadcasted_iota(jnp.int32, (L, M_HEAD_DIM), 1) == 0).astype(BF16)

    gi = gi_ref[0] + bi_ref[...]
    gf = _log_sigmoid(gf_ref[0] + bf_ref[...])
    m_all = m_sc[...]
    m_new_all = m_all
    for h in range(M_HEADS):
        hs = slice(h * M_HEAD_DIM, (h + 1) * M_HEAD_DIM)
        q = q_all[:, hs]
        k = k_all[:, hs] * (M_HEAD_DIM ** -0.5)
        v = v_all[:, hs]
        li = gi[h:h + 1, :]
        lf = gf[h:h + 1, :]
        if valid < L:
            li = jnp.where(lane_l < valid, li, NEG_INF)
            lf = jnp.where(lane_l < valid, lf, 0.0)
        m_prev = m_all[h:h + 1, 0:1]
        caug = c_sc[h]

        b_col = jnp.sum(jnp.where(causal, lf, 0.0), axis=1, keepdims=True)
        b_row = jnp.sum(jnp.where(eye, b_col, 0.0), axis=0, keepdims=True)
        r_row = li - b_row
        log_in = jnp.where(causal, b_col + r_row, NEG_INF)
        log_prev = b_col + m_prev
        m_t = jnp.maximum(log_prev, jnp.max(log_in, axis=1, keepdims=True))
        w_prev = jnp.exp(log_prev - m_t)
        dmat = jnp.exp(log_in - m_t)

        qb = q.astype(BF16)
        kb = k.astype(BF16)
        vaug = jnp.concatenate([v.astype(BF16), ones_col], axis=1)
        s = (_dot_nt(qb, kb) * dmat).astype(BF16)
        nd = w_prev * _dot(qb, caug.astype(BF16)) + _dot(s, vaug)
        num = nd[:, :M_HEAD_DIM]
        den = nd[:, M_HEAD_DIM:M_HEAD_DIM + 1]
        hh = num / jnp.maximum(jnp.abs(den), jnp.exp(-m_t))

        d = hh - jnp.mean(hh, axis=-1, keepdims=True)
        y = d * lax.rsqrt(jnp.mean(d * d, axis=-1, keepdims=True) + EPS) * hn_ref[:, hs]
        h_out[0, :, hs] = y[0:RB, :] * _sigmoid(o_ref[0][:, hs])

        g_end = b_row[:, L - 1:L]
        lie = g_end + r_row
        m_end = jnp.maximum(g_end + m_prev, jnp.max(lie, axis=1, keepdims=True))
        w_pe = jnp.exp(g_end + m_prev - m_end)
        w_e = jnp.exp(lie - m_end)
        ktw = (k.T * w_e).astype(BF16)
        c_new = w_pe * caug + _dot(ktw, vaug)
        c_sc[h] = c_new
        c_out[0, h] = c_new
        m_new_all = jnp.where(row8 == h, m_end, m_new_all)
    m_sc[...] = m_new_all
    m_out[0] = m_new_all


def mlstm(p_m, gates_i, gates_f, b_i, b_f, head_norm, c0aug, m0, L, RB, valid):
    NB, R, _ = p_m.shape
    nc = R // RB
    assert R % RB == 0 and (RB == L or nc == 1)
    blk = lambda j: pl.BlockSpec((1, RB, M_WIDTH), lambda b, c, j=j: (b, c, j))
    state_spec = pl.BlockSpec((1, M_HEADS, M_HEAD_DIM, MLSTM_AUG), lambda b, c: (b, 0, 0, 0))
    m_spec = pl.BlockSpec((1, SUBLANE, LANE), lambda b, c: (b, 0, 0))
    kern = functools.partial(_mlstm_kernel, L=L, RB=RB, valid=valid)
    scratch = [pltpu.VMEM((M_HEADS, M_HEAD_DIM, MLSTM_AUG), F32), pltpu.VMEM((SUBLANE, LANE), F32)]
    if RB < L:
        scratch += [pltpu.VMEM((L, M_WIDTH), F32)] * 3
    return pl.pallas_call(
        kern,
        grid=(NB, nc),
        in_specs=[blk(0), blk(1), blk(2), blk(3),
                  pl.BlockSpec((1, SUBLANE, L), lambda b, c: (b, 0, c)),
                  pl.BlockSpec((1, SUBLANE, L), lambda b, c: (b, 0, c)),
                  pl.BlockSpec((SUBLANE, 1), lambda b, c: (0, 0)),
                  pl.BlockSpec((SUBLANE, 1), lambda b, c: (0, 0)),
                  pl.BlockSpec((1, M_WIDTH), lambda b, c: (0, 0)),
                  state_spec, m_spec],
        out_specs=[pl.BlockSpec((1, RB, M_WIDTH), lambda b, c: (b, c, 0)), state_spec, m_spec],
        out_shape=[jax.ShapeDtypeStruct((NB, R, M_WIDTH), F32),
                   jax.ShapeDtypeStruct((NB, M_HEADS, M_HEAD_DIM, MLSTM_AUG), F32),
                   jax.ShapeDtypeStruct((NB, SUBLANE, LANE), F32)],
        scratch_shapes=scratch,
        compiler_params=_cparams("parallel", "arbitrary"),
        name="mlstm",
    )(p_m, p_m, p_m, p_m, gates_i, gates_f, b_i, b_f, head_norm, c0aug, m0)


def _rglru_kernel(x_ref, g_ref, conv0_ref, h0_ref, cw_ref, cb_ref, wa_ref, wx_ref, ba_ref, bx_ref,
                  lam_ref, y_out, hlast_out, conv_out, xpad_sc, hc_sc, *, TR, Bs):
    CB = (CONV_W - 1) * Bs
    X0 = -(-CB // SUBLANE) * SUBLANE

    @pl.when(pl.program_id(1) == 0)
    def _():
        xpad_sc[X0 - CB:X0, :] = conv0_ref[0]
        hc_sc[...] = h0_ref[0]

    xpad_sc[X0:X0 + TR, :] = x_ref[0]
    y = cb_ref[...]
    for j in range(CONV_W):
        y = y + xpad_sc[X0 - CB + j * Bs:X0 - CB + j * Bs + TR, :] * cw_ref[j:j + 1, :]
    tail = xpad_sc[X0 + TR - CB:X0 + TR, :]
    conv_out[0] = tail
    xpad_sc[X0 - CB:X0, :] = tail

    yb = y.astype(BF16)
    rs, is_ = [], []
    for n in range(R_BLOCKS):
        sl = slice(n * R_BLOCK_DIM, (n + 1) * R_BLOCK_DIM)
        rs.append(_dot(yb[:, sl], wa_ref[n]))
        is_.append(_dot(yb[:, sl], wx_ref[n]))
    r = _sigmoid(jnp.concatenate(rs, axis=1) + ba_ref[...])
    i = _sigmoid(jnp.concatenate(is_, axis=1) + bx_ref[...])
    log_a = -LRU_C * r * _softplus(-lam_ref[...])
    a_cum = jnp.exp(log_a)
    t = jnp.tanh(log_a)
    u_cum = jnp.sqrt(-2.0 * t / (1.0 - t)) * (i * y)

    row = lax.broadcasted_iota(jnp.int32, (TR, R_WIDTH), 0)
    d = Bs
    while d < TR:
        keep = row >= d
        u_cum = jnp.where(keep, a_cum * pltpu.roll(u_cum, d, axis=0) + u_cum, u_cum)
        a_cum = jnp.where(keep, a_cum * pltpu.roll(a_cum, d, axis=0), a_cum)
        d *= 2
    hc = hc_sc[...]
    hc_rows = hc if Bs == 1 else jnp.concatenate([hc] * (TR // Bs), axis=0)
    h = u_cum + a_cum * hc_rows
    y_out[0] = h * _gelu_tanh(g_ref[0])
    h_last = h[TR - Bs:, :]
    hc_sc[...] = h_last
    hlast_out[0] = h_last


def rglru(p_r, conv0, h0, cw, cb, wa, wx, ba, bx, lam, TR, Bs):
    G, R, _ = p_r.shape
    CB = (CONV_W - 1) * Bs
    X0 = -(-CB // SUBLANE) * SUBLANE
    assert R % TR == 0 and TR % Bs == 0 and TR >= CB
    W = R_WIDTH
    const = lambda shape: pl.BlockSpec(shape, lambda g, r: (0,) * len(shape))
    kern = functools.partial(_rglru_kernel, TR=TR, Bs=Bs)
    return pl.pallas_call(
        kern,
        grid=(G, R // TR),
        in_specs=[pl.BlockSpec((1, TR, W), lambda g, r: (g, r, 0)),
                  pl.BlockSpec((1, TR, W), lambda g, r: (g, r, 1)),
                  pl.BlockSpec((1, CB, W), lambda g, r: (g, 0, 0)),
                  pl.BlockSpec((1, Bs, W), lambda g, r: (g, 0, 0)),
                  const((CONV_W, W)), const((1, W)),
                  const((R_BLOCKS, R_BLOCK_DIM, R_BLOCK_DIM)), const((R_BLOCKS, R_BLOCK_DIM, R_BLOCK_DIM)),
                  const((1, W)), const((1, W)), const((1, W))],
        out_specs=[pl.BlockSpec((1, TR, W), lambda g, r: (g, r, 0)),
                   pl.BlockSpec((1, Bs, W), lambda g, r: (g, 0, 0)),
                   pl.BlockSpec((1, CB, W), lambda g, r: (g, 0, 0))],
        out_shape=[jax.ShapeDtypeStruct((G, R, W), F32),
                   jax.ShapeDtypeStruct((G, Bs, W), F32),
                   jax.ShapeDtypeStruct((G, CB, W), F32)],
        scratch_shapes=[pltpu.VMEM((X0 + TR, W), F32), pltpu.VMEM((Bs, W), F32)],
        compiler_params=_cparams("parallel", "arbitrary"),
        name="rglru",
    )(p_r, p_r, conv0, h0, cw, cb, wa, wx, ba, bx, lam)


def _swa_kernel(q_ref, kp_ref, kc_ref, vp_ref, vc_ref, bias_ref, sinks_ref, o_ref, *, TQ, mask_first):
    q = (q_ref[0] * (S_HEAD_DIM ** -0.5)).astype(BF16)
    lane = lax.broadcasted_iota(jnp.int32, (1, LANE), 1)
    halves = (lane < S_HEAD_DIM, lane >= S_HEAD_DIM)

    def placed(x_ref):
        x = x_ref[0]
        xr = pltpu.roll(x, S_HEAD_DIM, axis=1)
        return {(j, pos): jnp.where(halves[pos], x if pos == j else xr, 0.0).astype(BF16)
                for j in range(S_KV_HEADS) for pos in range(2)}

    kp, kc, vp, vc = placed(kp_ref), placed(kc_ref), placed(vp_ref), placed(vc_ref)
    first = pl.program_id(1) == 0
    for blk in range(S_WIDTH // LANE):
        out = None
        for pos in range(2):
            h = 2 * blk + pos
            j = h // (S_HEADS // S_KV_HEADS)
            qh = q[:, blk * LANE:(blk + 1) * LANE]
            s1 = _dot_nt(qh, kp[(j, pos)]) + bias_ref[h, :, 0:WINDOW]
            s2 = _dot_nt(qh, kc[(j, pos)]) + bias_ref[h, :, WINDOW:WINDOW + TQ]
            if mask_first:
                s1 = jnp.where(first, NEG_INF, s1)
            sink = sinks_ref[h]
            mx = jnp.maximum(jnp.maximum(jnp.max(s1, axis=1, keepdims=True),
                                         jnp.max(s2, axis=1, keepdims=True)), sink)
            p1 = jnp.exp(s1 - mx)
            p2 = jnp.exp(s2 - mx)
            den = (jnp.sum(p1, axis=1, keepdims=True) + jnp.sum(p2, axis=1, keepdims=True)
                   + jnp.exp(sink - mx))
            o = (_dot(p1.astype(BF16), vp[(j, pos)]) + _dot(p2.astype(BF16), vc[(j, pos)])) / den
            out = o if out is None else out + o
        o_ref[0, :, blk * LANE:(blk + 1) * LANE] = out


def swa(arrs, maps, shapes, bias, sinks, NB, nq, TQ, mask_first):
    kern = functools.partial(_swa_kernel, TQ=TQ, mask_first=mask_first)
    in_specs = [pl.BlockSpec(s, m) for s, m in zip(shapes, maps)]
    in_specs += [pl.BlockSpec(bias.shape, lambda b, n: (0, 0, 0)),
                 pl.BlockSpec(memory_space=pltpu.SMEM)]
    return pl.pallas_call(
        kern,
        grid=(NB, nq),
        in_specs=in_specs,
        out_specs=pl.BlockSpec((1, TQ, S_WIDTH), lambda b, n: (b, n, 0)),
        out_shape=jax.ShapeDtypeStruct((NB, nq * TQ, S_WIDTH), F32),
        compiler_params=_cparams("parallel", "arbitrary"),
        name="swa",
    )(*arrs, bias, sinks)


def t5_bucket(dist):
    max_exact = N_BUCKETS // 2
    d = jnp.maximum(dist, 0)
    large = max_exact + (jnp.log(jnp.maximum(d, 1).astype(F32) / max_exact)
                         / math.log(MAX_DISTANCE / max_exact) * (N_BUCKETS - max_exact)).astype(jnp.int32)
    return jnp.where(d < max_exact, d, jnp.minimum(large, N_BUCKETS - 1))


def swa_bias_table(rel_bias, TQ):
    qi = jnp.arange(TQ)[:, None]
    kj = jnp.arange(WINDOW + TQ)[None, :]
    dist = qi + WINDOW - kj
    b = jnp.moveaxis(rel_bias.astype(F32)[t5_bucket(dist)], -1, 0)
    return jnp.where((dist >= 0) & (dist <= WINDOW), b, NEG_INF)


def _merge_kernel(hm, hr, hs, gm, gr, gs, x, wm, wr, ws, wo, out):
    def branch(h_ref, g_ref, w_ref):
        return _sigmoid(g_ref[...]) * _dot(h_ref[...].astype(BF16), w_ref[...])
    merged = branch(hm, gm, wm) + branch(hr, gr, wr) + branch(hs, gs, ws)
    out[...] = x[...] + _dot(merged.astype(BF16), wo[...])


def merge_out(hm, hr, hs, p_g, x, wm, wr, ws, wo):
    T, D = x.shape
    tm = _row_tile(T)
    hspec = lambda w: pl.BlockSpec((tm, w), lambda i: (i, 0))
    gspec = lambda j: pl.BlockSpec((tm, D), lambda i, j=j: (i, j))
    wspec = lambda a: pl.BlockSpec(a.shape, lambda i: (0, 0))
    return pl.pallas_call(
        _merge_kernel,
        grid=(T // tm,),
        in_specs=[hspec(M_WIDTH), hspec(R_WIDTH), hspec(S_WIDTH), gspec(0), gspec(1), gspec(2),
                  hspec(D), wspec(wm), wspec(wr), wspec(ws), wspec(wo)],
        out_specs=hspec(D),
        out_shape=jax.ShapeDtypeStruct((T, D), F32),
        compiler_params=_cparams("parallel"),
        name="merge_out",
    )(hm, hr, hs, p_g, p_g, p_g, x, wm, wr, ws, wo)


def _xattn_kernel(x_ref, g_ref, wq_ref, wo_ref, mk_ref, mv_ref, out_ref, *, nb, tq):
    x = x_ref[...].reshape(nb * tq, D_MODEL)
    q = _dot(_rms(x, g_ref[...]).astype(BF16), wq_ref[...])
    rows = []
    for b in range(nb):
        heads = []
        for h in range(N_MEM_HEADS):
            hs = slice(h * X_HEAD_DIM, (h + 1) * X_HEAD_DIM)
            qh = q[b * tq:(b + 1) * tq, hs].astype(BF16)
            s = _dot_nt(qh, mk_ref[b, :, hs].astype(BF16)) * (X_HEAD_DIM ** -0.5)
            p = jnp.exp(s - jnp.max(s, axis=1, keepdims=True))
            o = _dot(p.astype(BF16), mv_ref[b, :, hs].astype(BF16))
            heads.append(o / jnp.sum(p, axis=1, keepdims=True))
        rows.append(jnp.concatenate(heads, axis=1))
    o_all = rows[0] if nb == 1 else jnp.concatenate(rows, axis=0)
    y = x + _dot(o_all.astype(BF16), wo_ref[...])
    out_ref[...] = y.reshape(nb, tq, D_MODEL)


def xattn(x, g, wq, wo, mem_k, mem_v, nb, tq):
    B, R, D = x.shape
    n_mem = mem_k.shape[1]
    kern = functools.partial(_xattn_kernel, nb=nb, tq=tq)
    return pl.pallas_call(
        kern,
        grid=(B // nb, R // tq),
        in_specs=[pl.BlockSpec((nb, tq, D), lambda b, r: (b, r, 0)),
                  pl.BlockSpec((1, D), lambda b, r: (0, 0)),
                  pl.BlockSpec(wq.shape, lambda b, r: (0, 0)),
                  pl.BlockSpec(wo.shape, lambda b, r: (0, 0)),
                  pl.BlockSpec((nb, n_mem, X_WIDTH), lambda b, r: (b, 0, 0)),
                  pl.BlockSpec((nb, n_mem, X_WIDTH), lambda b, r: (b, 0, 0))],
        out_specs=pl.BlockSpec((nb, tq, D), lambda b, r: (b, r, 0)),
        out_shape=jax.ShapeDtypeStruct((B, R, D), F32),
        compiler_params=_cparams("parallel", "arbitrary"),
        name="xattn",
    )(x, g, wq, wo, mem_k, mem_v)


def _router_kernel(x_ref, g_ref, wr_ref, xn_out, route_out):
    xn = _rms(x_ref[...], g_ref[...])
    xn_out[...] = xn
    logits = _dot(xn.astype(BF16), wr_ref[...])
    tm = logits.shape[0]
    lane_i = lax.broadcasted_iota(jnp.int32, (tm, LANE), 1)
    lane = lane_i.astype(F32)
    lane_group = (lane_i // EXPERTS_PER_GROUP).astype(F32)

    def top1(v):
        mx = jnp.max(v, axis=1, keepdims=True)
        return mx, jnp.min(jnp.where(v == mx, lane, float(LANE)), axis=1, keepdims=True)

    gl = jnp.where(lane_i < N_GROUPS, logits[:, :LANE], NEG_INF)
    g_max, g_idx = top1(gl)
    g_w = 1.0 / jnp.sum(jnp.exp(gl - g_max), axis=1, keepdims=True)
    el = jnp.where(lane_group == g_idx, logits[:, LANE:], NEG_INF)
    e1, i1 = top1(el)
    e2, i2 = top1(jnp.where(lane == i1, NEG_INF, el))
    t = jnp.exp(e2 - e1)
    p1 = 1.0 / (1.0 + t)
    route = jnp.where(lane_i == 0, i1, jnp.where(lane_i == 1, i2,
            jnp.where(lane_i == 2, p1 * g_w, jnp.where(lane_i == 3, t * p1 * g_w, 0.0))))
    route_out[...] = route


def router(x, g, w_router):
    T, D = x.shape
    tm = _row_tile(T)
    return pl.pallas_call(
        _router_kernel,
        grid=(T // tm,),
        in_specs=[pl.BlockSpec((tm, D), lambda i: (i, 0)),
                  pl.BlockSpec((1, D), lambda i: (0, 0)),
                  pl.BlockSpec(w_router.shape, lambda i: (0, 0))],
        out_specs=[pl.BlockSpec((tm, D), lambda i: (i, 0)), pl.BlockSpec((tm, LANE), lambda i: (i, 0))],
        out_shape=[jax.ShapeDtypeStruct((T, D), F32), jax.ShapeDtypeStruct((T, LANE), F32)],
        compiler_params=_cparams("parallel"),
        name="router",
    )(x, g, w_router)


def moe_plan(route, TB):
    T = route.shape[0]
    A = T * TOP_K
    flat_e = route[:, :TOP_K].astype(jnp.int32).reshape(A)
    onehot = (flat_e[:, None] == jnp.arange(N_EXPERTS, dtype=jnp.int32)[None, :]).astype(jnp.int32)
    csum = jnp.cumsum(onehot, axis=0)
    rank = jnp.sum(onehot * (csum - 1), axis=1)
    counts = csum[-1]
    padded = (counts + TB - 1) // TB * TB
    pad_ends = jnp.cumsum(padded)
    dest = (pad_ends - padded)[flat_e] + rank
    n_blocks = -(-(A + N_EXPERTS * (TB - 1)) // TB)
    P = n_blocks * TB
    a_idx = jnp.arange(A, dtype=jnp.int32)
    tok, kk = a_idx // TOP_K, a_idx % TOP_K
    slot_src = jnp.zeros((P,), jnp.int32).at[dest].set(tok)
    slot_dst = jnp.zeros((P,), jnp.int32).at[dest].set(kk * T + tok)
    block_start = jnp.arange(n_blocks, dtype=jnp.int32) * TB
    block_e = jnp.minimum(jnp.searchsorted(pad_ends, block_start, side="right"), N_EXPERTS - 1).astype(jnp.int32)
    n_valid = jnp.clip((pad_ends - padded + counts)[block_e] - block_start, 0, TB).astype(jnp.int32)
    return (block_e, n_valid, slot_src.reshape(n_blocks, 1, TB), slot_dst.reshape(n_blocks, 1, TB))


def _ffn_kernel(be_ref, nv_ref, src_ref, nxt_ref, dst_ref, xn_hbm, wg_ref, wu_ref, wd_ref, out_hbm,
                xbuf, ybuf, wg_bf, wu_bf, wd_bf, gsem, ssem, *, n_blocks):
    i = pl.program_id(0)
    slot = i % 2
    nv = nv_ref[i]

    def gather_copy(row_src, row, s):
        return pltpu.make_async_copy(xn_hbm.at[pl.ds(row_src, 1)], xbuf.at[s, pl.ds(row, 1)], gsem.at[s])

    def scatter_copy(row, row_dst):
        return pltpu.make_async_copy(ybuf.at[pl.ds(row, 1)], out_hbm.at[pl.ds(row_dst, 1)], ssem.at[0])

    def for_rows(n, fn):
        def body(r, c):
            fn(r)
            return c
        lax.fori_loop(0, n, body, 0)

    @pl.when(i == 0)
    def _():
        xbuf[...] = jnp.zeros_like(xbuf)
        for_rows(nv, lambda r: gather_copy(src_ref[0, 0, r], r, 0).start())

    @pl.when(i + 1 < n_blocks)
    def _():
        n_next = nv_ref[jnp.minimum(i + 1, n_blocks - 1)]
        for_rows(n_next, lambda r: gather_copy(nxt_ref[0, 0, r], r, 1 - slot).start())

    @pl.when(i > 0)
    def _():
        for_rows(nv_ref[jnp.maximum(i - 1, 0)], lambda r: scatter_copy(r, 0).wait())

    @pl.when((i == 0) | (be_ref[i] != be_ref[jnp.maximum(i - 1, 0)]))
    def _():
        wg_bf[...] = wg_ref[0].astype(BF16)
        wu_bf[...] = wu_ref[0].astype(BF16)
        wd_bf[...] = wd_ref[0].astype(BF16)

    for_rows(nv, lambda r: gather_copy(0, r, slot).wait())

    @pl.when(nv > 0)
    def _():
        x = xbuf[slot].astype(BF16)
        g = _dot(x, wg_bf[...])
        u = _dot(x, wu_bf[...])
        hid = (g * _sigmoid(g)) * u
        ybuf[...] = _dot(hid.astype(BF16), wd_bf[...])

    for_rows(nv, lambda r: scatter_copy(r, dst_ref[0, 0, r]).start())

    @pl.when(i == n_blocks - 1)
    def _():
        for_rows(nv, lambda r: scatter_copy(r, 0).wait())


def expert_ffn(xn, plan, w_gate, w_up, w_down, TB):
    T, D = xn.shape
    block_e, n_valid, slot_src, slot_dst = plan
    n_blocks = block_e.shape[0]
    smem_blk = lambda f: pl.BlockSpec((1, 1, TB), f, memory_space=pltpu.SMEM)
    kern = functools.partial(_ffn_kernel, n_blocks=n_blocks)
    grid_spec = pltpu.PrefetchScalarGridSpec(
        num_scalar_prefetch=2,
        grid=(n_blocks,),
        in_specs=[smem_blk(lambda i, be, nv: (i, 0, 0)),
                  smem_blk(lambda i, be, nv: (jnp.minimum(i + 1, n_blocks - 1), 0, 0)),
                  smem_blk(lambda i, be, nv: (i, 0, 0)),
                  pl.BlockSpec(memory_space=pl.ANY),
                  pl.BlockSpec((1, D, D_EXPERT), lambda i, be, nv: (be[i], 0, 0)),
                  pl.BlockSpec((1, D, D_EXPERT), lambda i, be, nv: (be[i], 0, 0)),
                  pl.BlockSpec((1, D_EXPERT, D), lambda i, be, nv: (be[i], 0, 0))],
        out_specs=pl.BlockSpec(memory_space=pl.ANY),
        scratch_shapes=[pltpu.VMEM((2, TB, D), F32), pltpu.VMEM((TB, D), F32),
                        pltpu.VMEM((D, D_EXPERT), BF16), pltpu.VMEM((D, D_EXPERT), BF16),
                        pltpu.VMEM((D_EXPERT, D), BF16),
                        pltpu.SemaphoreType.DMA((2,)), pltpu.SemaphoreType.DMA((1,))],
    )
    return pl.pallas_call(
        kern,
        grid_spec=grid_spec,
        out_shape=jax.ShapeDtypeStruct((TOP_K * T, D), F32),
        compiler_params=_cparams("arbitrary"),
        name="expert_ffn",
    )(block_e, n_valid, slot_src, slot_src, slot_dst, xn, w_gate, w_up, w_down)


def _combine_kernel(x_ref, y0_ref, y1_ref, route_ref, gf_ref, out_ref, *, final_norm):
    route = route_ref[...]
    lane_i = lax.broadcasted_iota(jnp.int32, route.shape, 1)
    g0 = jnp.sum(jnp.where(lane_i == 2, route, 0.0), axis=1, keepdims=True)
    g1 = jnp.sum(jnp.where(lane_i == 3, route, 0.0), axis=1, keepdims=True)
    y = x_ref[...] + (g0 * y0_ref[...] + g1 * y1_ref[...])
    out_ref[...] = _rms(y, gf_ref[...]) if final_norm else y


def moe_combine(x, y2, route, g_final, final_norm):
    T, D = x.shape
    tm = _row_tile(T)
    half = T // tm
    kern = functools.partial(_combine_kernel, final_norm=final_norm)
    return pl.pallas_call(
        kern,
        grid=(T // tm,),
        in_specs=[pl.BlockSpec((tm, D), lambda i: (i, 0)),
                  pl.BlockSpec((tm, D), lambda i: (i, 0)),
                  pl.BlockSpec((tm, D), lambda i: (half + i, 0)),
                  pl.BlockSpec((tm, LANE), lambda i: (i, 0)),
                  pl.BlockSpec((1, D), lambda i: (0, 0))],
        out_specs=pl.BlockSpec((tm, D), lambda i: (i, 0)),
        out_shape=jax.ShapeDtypeStruct((T, D), F32),
        compiler_params=_cparams("parallel"),
        name="moe_combine",
    )(x, y2, y2, route, g_final)


IN_SPLITS = (4 * M_WIDTH, LANE, 2 * R_WIDTH, S_WIDTH + 2 * S_KV_WIDTH, 3 * D_MODEL)


def _prep_w_in(w):
    c0 = 4 * M_WIDTH
    c1 = c0 + 2 * M_HEADS
    gates = jnp.pad(w[:, c0:c1], ((0, 0), (0, LANE - 2 * M_HEADS)))
    return jnp.concatenate([w[:, :c0], gates, w[:, c1:]], axis=1).astype(BF16)


def _prep_router(w_rg, w_re):
    pad = lambda w: jnp.pad(w, ((0, 0), (0, LANE - w.shape[1])))
    return jnp.concatenate([pad(w_rg), pad(w_re)], axis=1).astype(BF16)


def _row(v):
    return v.reshape(1, -1).astype(F32)


def _pad_rows(a, n):
    return jnp.pad(a, ((0, 0), (0, n - a.shape[1])) + ((0, 0),) * (a.ndim - 2))


def _moe_block(T):
    return 256 if T >= 4096 else 128


def kernel(x_prompt, x_sample, mem_prompt, cache_mem_k, cache_mem_v, cache_swa_k, cache_swa_v, state_mlstm_C, state_mlstm_n, state_mlstm_m, state_rglru_h, state_rglru_conv, norm_mix, w_in, m_igate_b, m_fgate_b, m_head_norm, r_conv_w, r_conv_b, r_gate_a_w, r_gate_a_b, r_gate_x_w, r_gate_x_b, r_lambda, swa_sinks, rel_bias, w_branch_m, w_branch_r, w_branch_s, w_out, norm_mem, xq_w, xk_w, xv_w, xo_w, norm_ffn, router_group_w, router_expert_w, moe_w_gate, moe_w_up, moe_w_down, norm_final):
    B, S, D = x_prompt.shape
    NS, TS, _ = x_sample.shape
    depth = w_in.shape[0]
    n_mem = mem_prompt.shape[1]
    Tp, Ts = B * S, NS * TS
    H = M_HEADS
    assert S % MLSTM_CHUNK == 0 and S % WINDOW == 0 and TS <= SUBLANE and NS % XATTN_SHORT_BATCH == 0

    xp = x_prompt.reshape(Tp, D)
    xs = x_sample.reshape(Ts, D)
    mem2d = mem_prompt.reshape(B * n_mem, D)
    bias_p = swa_bias_table(rel_bias, WINDOW)
    bias_s = swa_bias_table(rel_bias, SWA_SHORT_TQ)
    TBp, TBs = _moe_block(Tp), _moe_block(Ts)
    outs_p = {k: [] for k in ("mem_k", "mem_v", "swa_k", "swa_v", "C", "n", "m", "h", "conv")}
    outs_s = {k: [] for k in ("swa_k", "swa_v", "C", "n", "m", "h", "conv")}

    for l in range(depth):
        w_in_b = _prep_w_in(w_in[l])
        g_mix = _row(norm_mix[l])
        b_i = jnp.pad(m_igate_b[l], (0, SUBLANE - H)).reshape(SUBLANE, 1)
        b_f = jnp.pad(m_fgate_b[l], (0, SUBLANE - H)).reshape(SUBLANE, 1)
        hn = _row(m_head_norm[l])
        r_args = (r_conv_w[l], _row(r_conv_b[l]), r_gate_a_w[l].astype(BF16), r_gate_x_w[l].astype(BF16),
                  _row(r_gate_a_b[l]), _row(r_gate_x_b[l]), _row(r_lambda[l]))
        sinks = swa_sinks[l].astype(F32)
        merge_w = (w_branch_m[l].astype(BF16), w_branch_r[l].astype(BF16), w_branch_s[l].astype(BF16),
                   w_out[l].astype(BF16))
        wq_b, wo_b = xq_w[l].astype(BF16), xo_w[l].astype(BF16)
        w_router = _prep_router(router_group_w[l], router_expert_w[l])
        last = l == depth - 1

        w_kv = jnp.concatenate([xk_w[l], xv_w[l]], axis=1).astype(BF16)
        mk, mv = norm_matmul(mem2d, g_mix, w_kv, (X_WIDTH, X_WIDTH), normalize=False)
        mk, mv = mk.reshape(B, n_mem, X_WIDTH), mv.reshape(B, n_mem, X_WIDTH)
        outs_p["mem_k"].append(mk.reshape(B, n_mem, N_MEM_HEADS, X_HEAD_DIM))
        outs_p["mem_v"].append(mv.reshape(B, n_mem, N_MEM_HEADS, X_HEAD_DIM))

        p_m, p_if, p_r, p_s, p_g = norm_matmul(xp, g_mix, w_in_b, IN_SPLITS)
        gts = jnp.swapaxes(p_if[:, :2 * H].reshape(B, S, 2 * H), 1, 2)
        gi = jnp.pad(gts[:, :H], ((0, 0), (0, SUBLANE - H), (0, 0)))
        gf = jnp.pad(gts[:, H:], ((0, 0), (0, SUBLANE - H), (0, 0)))
        h_m, caug, m_o = mlstm(p_m.reshape(B, S, 4 * M_WIDTH), gi, gf, b_i, b_f, hn,
                               jnp.zeros((B, H, M_HEAD_DIM, MLSTM_AUG), F32),
                               jnp.zeros((B, SUBLANE, LANE), F32),
                               L=MLSTM_CHUNK, RB=MLSTM_CHUNK, valid=MLSTM_CHUNK)
        outs_p["C"].append(caug[..., :M_HEAD_DIM])
        outs_p["n"].append(caug[..., M_HEAD_DIM])
        outs_p["m"].append(m_o[:, :H, 0])

        h_r, h_last, conv_last = rglru(p_r.reshape(B, S, 2 * R_WIDTH),
                                       jnp.zeros((B, CONV_W - 1, R_WIDTH), F32),
                                       jnp.zeros((B, 1, R_WIDTH), F32), *r_args,
                                       TR=_row_tile(S), Bs=1)
        outs_p["h"].append(h_last[:, 0])
        outs_p["conv"].append(conv_last)

        p_s3 = p_s.reshape(B, S, S_WIDTH + 2 * S_KV_WIDTH)
        kcol, vcol = S_WIDTH // S_KV_WIDTH, S_WIDTH // S_KV_WIDTH + 1
        prev = lambda col: (lambda b, n: (b, jnp.maximum(n - 1, 0), col))
        cur = lambda col: (lambda b, n: (b, n, col))
        kv_blk = (1, WINDOW, S_KV_WIDTH)
        h_s = swa((p_s3,) * 5, (cur(0), prev(kcol), cur(kcol), prev(vcol), cur(vcol)),
                  ((1, WINDOW, S_WIDTH), kv_blk, kv_blk, kv_blk, kv_blk),
                  bias_p, sinks, NB=B, nq=S // WINDOW, TQ=WINDOW, mask_first=True)
        outs_p["swa_k"].append(p_s3[:, S - WINDOW:, S_WIDTH:S_WIDTH + S_KV_WIDTH]
                               .reshape(B, WINDOW, S_KV_HEADS, S_HEAD_DIM))
        outs_p["swa_v"].append(p_s3[:, S - WINDOW:, S_WIDTH + S_KV_WIDTH:]
                               .reshape(B, WINDOW, S_KV_HEADS, S_HEAD_DIM))

        xp = merge_out(h_m.reshape(Tp, M_WIDTH), h_r.reshape(Tp, R_WIDTH), h_s.reshape(Tp, S_WIDTH),
                       p_g, xp, *merge_w)
        xp = xattn(xp.reshape(B, S, D), _row(norm_mem[l]), wq_b, wo_b, mk, mv,
                   nb=1, tq=512 if S % 512 == 0 else _row_tile(S)).reshape(Tp, D)
        xn, route = router(xp, _row(norm_ffn[l]), w_router)
        y2 = expert_ffn(xn, moe_plan(route, TBp), moe_w_gate[l], moe_w_up[l], moe_w_down[l], TBp)
        xp = moe_combine(xp, y2, route, _row(norm_final), final_norm=last)

        p_m, p_if, p_r, p_s, p_g = norm_matmul(xs, g_mix, w_in_b, IN_SPLITS)
        LS = MLSTM_SHORT_CHUNK
        gts = jnp.swapaxes(p_if[:, :2 * H].reshape(NS, TS, 2 * H), 1, 2)
        gts = jnp.pad(gts, ((0, 0), (0, 0), (0, LS - TS)))
        gi = jnp.pad(gts[:, :H], ((0, 0), (0, SUBLANE - H), (0, 0)))
        gf = jnp.pad(gts[:, H:], ((0, 0), (0, SUBLANE - H), (0, 0)))
        c0aug = jnp.concatenate([state_mlstm_C[l], state_mlstm_n[l][..., None],
                                 jnp.zeros((NS, H, M_HEAD_DIM, M_HEAD_DIM - 1), F32)], axis=-1)
        m0 = jnp.broadcast_to(jnp.pad(state_mlstm_m[l], ((0, 0), (0, SUBLANE - H)))[:, :, None],
                              (NS, SUBLANE, LANE))
        h_m, caug, m_o = mlstm(_pad_rows(p_m.reshape(NS, TS, 4 * M_WIDTH), SUBLANE), gi, gf, b_i, b_f, hn,
                               c0aug, m0, L=LS, RB=SUBLANE, valid=TS)
        h_m = h_m[:, :TS].reshape(Ts, M_WIDTH)
        outs_s["C"].append(caug[..., :M_HEAD_DIM])
        outs_s["n"].append(caug[..., M_HEAD_DIM])
        outs_s["m"].append(m_o[:, :H, 0])

        tmaj = lambda a: jnp.swapaxes(a, 0, 1).reshape(1, a.shape[0] * a.shape[1], a.shape[2])
        h_r, h_last, conv_last = rglru(tmaj(p_r.reshape(NS, TS, 2 * R_WIDTH)), tmaj(state_rglru_conv[l]),
                                       state_rglru_h[l].reshape(1, NS, R_WIDTH), *r_args,
                                       TR=TS * NS, Bs=NS)
        h_r = jnp.swapaxes(h_r.reshape(TS, NS, R_WIDTH), 0, 1).reshape(Ts, R_WIDTH)
        outs_s["h"].append(h_last[0])
        outs_s["conv"].append(jnp.swapaxes(conv_last.reshape(CONV_W - 1, NS, R_WIDTH), 0, 1))

        TQ = SWA_SHORT_TQ
        p_s3 = p_s.reshape(NS, TS, S_WIDTH + 2 * S_KV_WIDTH)
        q_s = _pad_rows(p_s3[:, :, :S_WIDTH], TQ)
        k_new = p_s3[:, :, S_WIDTH:S_WIDTH + S_KV_WIDTH]
        v_new = p_s3[:, :, S_WIDTH + S_KV_WIDTH:]
        k_cache = cache_swa_k[l].reshape(NS, WINDOW, S_KV_WIDTH)
        v_cache = cache_swa_v[l].reshape(NS, WINDOW, S_KV_WIDTH)
        full = lambda b, n: (b, 0, 0)
        h_s = swa((q_s, k_cache, _pad_rows(k_new, TQ), v_cache, _pad_rows(v_new, TQ)), (full,) * 5,
                  ((1, TQ, S_WIDTH), (1, WINDOW, S_KV_WIDTH), (1, TQ, S_KV_WIDTH),
                   (1, WINDOW, S_KV_WIDTH), (1, TQ, S_KV_WIDTH)),
                  bias_s, sinks, NB=NS, nq=1, TQ=TQ, mask_first=False)
        h_s = h_s[:, :TS].reshape(Ts, S_WIDTH)
        outs_s["swa_k"].append(jnp.concatenate([k_cache[:, TS:], k_new], axis=1)
                               .reshape(NS, WINDOW, S_KV_HEADS, S_HEAD_DIM))
        outs_s["swa_v"].append(jnp.concatenate([v_cache[:, TS:], v_new], axis=1)
                               .reshape(NS, WINDOW, S_KV_HEADS, S_HEAD_DIM))

        xs = merge_out(h_m, h_r, h_s, p_g, xs, *merge_w)
        xs8 = xattn(_pad_rows(xs.reshape(NS, TS, D), SUBLANE), _row(norm_mem[l]), wq_b, wo_b,
                    cache_mem_k[l].reshape(NS, n_mem, X_WIDTH), cache_mem_v[l].reshape(NS, n_mem, X_WIDTH),
                    nb=XATTN_SHORT_BATCH, tq=SUBLANE)
        xs = xs8[:, :TS].reshape(Ts, D)
        xn, route = router(xs, _row(norm_ffn[l]), w_router)
        y2 = expert_ffn(xn, moe_plan(route, TBs), moe_w_gate[l], moe_w_up[l], moe_w_down[l], TBs)
        xs = moe_combine(xs, y2, route, _row(norm_final), final_norm=last)

    st = lambda d, k: jnp.stack(d[k])
    return (xp.reshape(B, S, D), xs.reshape(NS, TS, D), st(outs_p, "mem_k"), st(outs_p, "mem_v"),
            st(outs_p, "swa_k"), st(outs_p, "swa_v"), st(outs_p, "C"), st(outs_p, "n"), st(outs_p, "m"),
            st(outs_p, "h"), st(outs_p, "conv"),
            st(outs_s, "swa_k"), st(outs_s, "swa_v"), st(outs_s, "C"), st(outs_s, "n"), st(outs_s, "m"),
            st(outs_s, "h"), st(outs_s, "conv"))
```

```python
import functools
import math

import jax
import jax.numpy as jnp
from jax import lax
from jax.experimental import pallas as pl
from jax.experimental.pallas import tpu as pltpu

F32 = jnp.float32
BF16 = jnp.bfloat16
NEG_INF = float("-inf")

LANE = 128
SUBLANE = 8
BF16_SUBLANE = 16
VMEM_LIMIT_BYTES = 56 * 1024 * 1024

D_MODEL = 1024
M_HEADS = 4
M_HEAD_DIM = 128
M_WIDTH = M_HEADS * M_HEAD_DIM
R_WIDTH = 512
R_BLOCKS = 4
R_BLOCK_DIM = R_WIDTH // R_BLOCKS
CONV_W = 4
LRU_C = 8.0
S_HEADS = 8
S_KV_HEADS = 2
S_HEAD_DIM = 64
S_WIDTH = S_HEADS * S_HEAD_DIM
S_KV_WIDTH = S_KV_HEADS * S_HEAD_DIM
WINDOW = 128
N_BUCKETS = 32
MAX_DISTANCE = 128
N_MEM_HEADS = 4
X_HEAD_DIM = 128
X_WIDTH = N_MEM_HEADS * X_HEAD_DIM
N_GROUPS = 4
EXPERTS_PER_GROUP = 8
N_EXPERTS = N_GROUPS * EXPERTS_PER_GROUP
TOP_K = 2
D_EXPERT = 512
EPS = 1e-6

MLSTM_CHUNK = 256
MLSTM_AUG = 2 * M_HEAD_DIM
MLSTM_SHORT_CHUNK = LANE
SWA_SHORT_TQ = BF16_SUBLANE
SWA_SHORT_BATCH = 8
XATTN_SHORT_BATCH = 8


def _cparams(*sem):
    return pltpu.CompilerParams(dimension_semantics=sem, vmem_limit_bytes=VMEM_LIMIT_BYTES)


def _rms(x, g):
    ms = jnp.mean(x * x, axis=-1, keepdims=True)
    return x * lax.rsqrt(ms + EPS) * g


def _sigmoid(x):
    return 1.0 / (1.0 + jnp.exp(-x))


def _softplus(x):
    return jnp.maximum(x, 0.0) + jnp.log1p(jnp.exp(-jnp.abs(x)))


def _log_sigmoid(x):
    return -_softplus(-x)


def _gelu_tanh(x):
    return 0.5 * x * (1.0 + jnp.tanh(math.sqrt(2.0 / math.pi) * (x + 0.044715 * (x * x * x))))


def _dot(a, b):
    return jnp.dot(a, b, preferred_element_type=F32)


def _dot_nt(a, b):
    return lax.dot_general(a, b, (((1,), (1,)), ((), ())), preferred_element_type=F32)


def _row_tile(n, pref=256):
    for t in (256, 128, 64, 32, 16, 8):
        if t <= pref and n % t == 0:
            return t
    raise ValueError(f"row count {n} is not a multiple of {SUBLANE}")


def _norm_matmul_kernel(x_ref, g_ref, w_ref, *out_refs, splits, normalize):
    x = x_ref[...]
    if normalize:
        x = _rms(x, g_ref[...])
    xb = x.astype(BF16)
    off = 0
    for o_ref, n in zip(out_refs, splits):
        o_ref[...] = _dot(xb, w_ref[:, off:off + n])
        off += n


def norm_matmul(x, g, w_bf16, splits, normalize=True):
    T, D = x.shape
    tm = _row_tile(T)
    N = w_bf16.shape[1]
    assert sum(splits) == N and all(s % LANE == 0 for s in splits)
    kern = functools.partial(_norm_matmul_kernel, splits=tuple(splits), normalize=normalize)
    return pl.pallas_call(
        kern,
        grid=(T // tm,),
        in_specs=[pl.BlockSpec((tm, D), lambda i: (i, 0)),
                  pl.BlockSpec((1, D), lambda i: (0, 0)),
                  pl.BlockSpec((D, N), lambda i: (0, 0))],
        out_specs=[pl.BlockSpec((tm, n), lambda i: (i, 0)) for n in splits],
        out_shape=[jax.ShapeDtypeStruct((T, n), F32) for n in splits],
        compiler_params=_cparams("parallel"),
        name="norm_matmul",
    )(x, g, w_bf16)


def _mlstm_kernel(q_ref, k_ref, v_ref, o_ref, gi_ref, gf_ref, bi_ref, bf_ref, hn_ref,
                  c0_ref, m0_ref, h_out, c_out, m_out, c_sc, m_sc, *pad_sc, L, RB, valid):
    @pl.when(pl.program_id(1) == 0)
    def _():
        c_sc[...] = c0_ref[0]
        m_sc[...] = m0_ref[0]

    if RB < L:
        for dst, src in zip(pad_sc, (q_ref, k_ref, v_ref)):
            dst[...] = jnp.zeros_like(dst)
            dst[0:RB, :] = src[0]
        q_all, k_all, v_all = (r[...] for r in pad_sc)
    else:
        q_all, k_all, v_all = q_ref[0], k_ref[0], v_ref[0]

    t_idx = lax.broadcasted_iota(jnp.int32, (L, L), 0)
    s_idx = lax.broadcasted_iota(jnp.int32, (L, L), 1)
    causal = t_idx >= s_idx
    eye = t_idx == s_idx
    lane_l = lax.broadcasted_iota(jnp.int32, (1, L), 1)
    row8 = lax.broadcasted_iota(jnp.int32, (SUBLANE, LANE), 0)
    ones_col = (lax.broadcasted_iota(jnp.int32, (L, M_HEAD_DIM), 1) == 0).astype(BF16)

    gi = gi_ref[0] + bi_ref[...]
    gf = _log_sigmoid(gf_ref[0] + bf_ref[...])
    m_all = m_sc[...]
    m_new_all = m_all
    for h in range(M_HEADS):
        hs = slice(h * M_HEAD_DIM, (h + 1) * M_HEAD_DIM)
        q = q_all[:, hs]
        k = k_all[:, hs] * (M_HEAD_DIM ** -0.5)
        v = v_all[:, hs]
        li = gi[h:h + 1, :]
        lf = gf[h:h + 1, :]
        if valid < L:
            li = jnp.where(lane_l < valid, li, NEG_INF)
            lf = jnp.where(lane_l < valid, lf, 0.0)
        m_prev = m_all[h:h + 1, 0:1]
        caug = c_sc[h]

        b_col = jnp.sum(jnp.where(causal, lf, 0.0), axis=1, keepdims=True)
        b_row = jnp.sum(jnp.where(eye, b_col, 0.0), axis=0, keepdims=True)
        r_row = li - b_row
        log_in = jnp.where(causal, b_col + r_row, NEG_INF)
        log_prev = b_col + m_prev
        m_t = jnp.maximum(log_prev, jnp.max(log_in, axis=1, keepdims=True))
        w_prev = jnp.exp(log_prev - m_t)
        dmat = jnp.exp(log_in - m_t)

        qb = q.astype(BF16)
        kb = k.astype(BF16)
        vaug = jnp.concatenate([v.astype(BF16), ones_col], axis=1)
        s = (_dot_nt(qb, kb) * dmat).astype(BF16)
        nd = w_prev * _dot(qb, caug.astype(BF16)) + _dot(s, vaug)
        num = nd[:, :M_HEAD_DIM]
        den = nd[:, M_HEAD_DIM:M_HEAD_DIM + 1]
        hh = num / jnp.maximum(jnp.abs(den), jnp.exp(-m_t))

        d = hh - jnp.mean(hh, axis=-1, keepdims=True)
        y = d * lax.rsqrt(jnp.mean(d * d, axis=-1, keepdims=True) + EPS) * hn_ref[:, hs]
        h_out[0, :, hs] = y[0:RB, :] * _sigmoid(o_ref[0][:, hs])

        g_end = b_row[:, L - 1:L]
        lie = g_end + r_row
        m_end = jnp.maximum(g_end + m_prev, jnp.max(lie, axis=1, keepdims=True))
        w_pe = jnp.exp(g_end + m_prev - m_end)
        w_e = jnp.exp(lie - m_end)
        ktw = (k.T * w_e).astype(BF16)
        c_new = w_pe * caug + _dot(ktw, vaug)
        c_sc[h] = c_new
        c_out[0, h] = c_new
        m_new_all = jnp.where(row8 == h, m_end, m_new_all)
    m_sc[...] = m_new_all
    m_out[0] = m_new_all


def mlstm(p_m, gates_i, gates_f, b_i, b_f, head_norm, c0aug, m0, L, RB, valid):
    NB, R, _ = p_m.shape
    nc = R // RB
    assert R % RB == 0 and (RB == L or nc == 1)
    blk = lambda j: pl.BlockSpec((1, RB, M_WIDTH), lambda b, c, j=j: (b, c, j))
    state_spec = pl.BlockSpec((1, M_HEADS, M_HEAD_DIM, MLSTM_AUG), lambda b, c: (b, 0, 0, 0))
    m_spec = pl.BlockSpec((1, SUBLANE, LANE), lambda b, c: (b, 0, 0))
    kern = functools.partial(_mlstm_kernel, L=L, RB=RB, valid=valid)
    scratch = [pltpu.VMEM((M_HEADS, M_HEAD_DIM, MLSTM_AUG), F32), pltpu.VMEM((SUBLANE, LANE), F32)]
    if RB < L:
        scratch += [pltpu.VMEM((L, M_WIDTH), F32)] * 3
    return pl.pallas_call(
        kern,
        grid=(NB, nc),
        in_specs=[blk(0), blk(1), blk(2), blk(3),
                  pl.BlockSpec((1, SUBLANE, L), lambda b, c: (b, 0, c)),
                  pl.BlockSpec((1, SUBLANE, L), lambda b, c: (b, 0, c)),
                  pl.BlockSpec((SUBLANE, 1), lambda b, c: (0, 0)),
                  pl.BlockSpec((SUBLANE, 1), lambda b, c: (0, 0)),
                  pl.BlockSpec((1, M_WIDTH), lambda b, c: (0, 0)),
                  state_spec, m_spec],
        out_specs=[pl.BlockSpec((1, RB, M_WIDTH), lambda b, c: (b, c, 0)), state_spec, m_spec],
        out_shape=[jax.ShapeDtypeStruct((NB, R, M_WIDTH), F32),
                   jax.ShapeDtypeStruct((NB, M_HEADS, M_HEAD_DIM, MLSTM_AUG), F32),
                   jax.ShapeDtypeStruct((NB, SUBLANE, LANE), F32)],
        scratch_shapes=scratch,
        compiler_params=_cparams("parallel", "arbitrary"),
        name="mlstm",
    )(p_m, p_m, p_m, p_m, gates_i, gates_f, b_i, b_f, head_norm, c0aug, m0)


def _rglru_kernel(x_ref, g_ref, conv0_ref, h0_ref, cw_ref, cb_ref, wa_ref, wx_ref, ba_ref, bx_ref,
                  lam_ref, y_out, hlast_out, conv_out, xpad_sc, hc_sc, *, TR, Bs):
    CB = (CONV_W - 1) * Bs
    X0 = -(-CB // SUBLANE) * SUBLANE

    @pl.when(pl.program_id(1) == 0)
    def _():
        xpad_sc[X0 - CB:X0, :] = conv0_ref[0]
        hc_sc[...] = h0_ref[0]

    xpad_sc[X0:X0 + TR, :] = x_ref[0]
    y = cb_ref[...]
    for j in range(CONV_W):
        y = y + xpad_sc[X0 - CB + j * Bs:X0 - CB + j * Bs + TR, :] * cw_ref[j:j + 1, :]
    tail = xpad_sc[X0 + TR - CB:X0 + TR, :]
    conv_out[0] = tail
    xpad_sc[X0 - CB:X0, :] = tail

    yb = y.astype(BF16)
    rs, is_ = [], []
    for n in range(R_BLOCKS):
        sl = slice(n * R_BLOCK_DIM, (n + 1) * R_BLOCK_DIM)
        rs.append(_dot(yb[:, sl], wa_ref[n]))
        is_.append(_dot(yb[:, sl], wx_ref[n]))
    r = _sigmoid(jnp.concatenate(rs, axis=1) + ba_ref[...])
    i = _sigmoid(jnp.concatenate(is_, axis=1) + bx_ref[...])
    log_a = -LRU_C * r * _softplus(-lam_ref[...])
    a_cum = jnp.exp(log_a)
    t = jnp.tanh(log_a)
    u_cum = jnp.sqrt(-2.0 * t / (1.0 - t)) * (i * y)

    row = lax.broadcasted_iota(jnp.int32, (TR, R_WIDTH), 0)
    d = Bs
    while d < TR:
        keep = row >= d
        u_cum = jnp.where(keep, a_cum * pltpu.roll(u_cum, d, axis=0) + u_cum, u_cum)
        a_cum = jnp.where(keep, a_cum * pltpu.roll(a_cum, d, axis=0), a_cum)
        d *= 2
    hc = hc_sc[...]
    hc_rows = hc if Bs == 1 else jnp.concatenate([hc] * (TR // Bs), axis=0)
    h = u_cum + a_cum * hc_rows
    y_out[0] = h * _gelu_tanh(g_ref[0])
    h_last = h[TR - Bs:, :]
    hc_sc[...] = h_last
    hlast_out[0] = h_last


def rglru(p_r, conv0, h0, cw, cb, wa, wx, ba, bx, lam, TR, Bs):
    G, R, _ = p_r.shape
    CB = (CONV_W - 1) * Bs
    X0 = -(-CB // SUBLANE) * SUBLANE
    assert R % TR == 0 and TR % Bs == 0 and TR >= CB
    W = R_WIDTH
    const = lambda shape: pl.BlockSpec(shape, lambda g, r: (0,) * len(shape))
    kern = functools.partial(_rglru_kernel, TR=TR, Bs=Bs)
    return pl.pallas_call(
        kern,
        grid=(G, R // TR),
        in_specs=[pl.BlockSpec((1, TR, W), lambda g, r: (g, r, 0)),
                  pl.BlockSpec((1, TR, W), lambda g, r: (g, r, 1)),
                  pl.BlockSpec((1, CB, W), lambda g, r: (g, 0, 0)),
                  pl.BlockSpec((1, Bs, W), lambda g, r: (g, 0, 0)),
                  const((CONV_W, W)), const((1, W)),
                  const((R_BLOCKS, R_BLOCK_DIM, R_BLOCK_DIM)), const((R_BLOCKS, R_BLOCK_DIM, R_BLOCK_DIM)),
                  const((1, W)), const((1, W)), const((1, W))],
        out_specs=[pl.BlockSpec((1, TR, W), lambda g, r: (g, r, 0)),
                   pl.BlockSpec((1, Bs, W), lambda g, r: (g, 0, 0)),
                   pl.BlockSpec((1, CB, W), lambda g, r: (g, 0, 0))],
        out_shape=[jax.ShapeDtypeStruct((G, R, W), F32),
                   jax.ShapeDtypeStruct((G, Bs, W), F32),
                   jax.ShapeDtypeStruct((G, CB, W), F32)],
        scratch_shapes=[pltpu.VMEM((X0 + TR, W), F32), pltpu.VMEM((Bs, W), F32)],
        compiler_params=_cparams("parallel", "arbitrary"),
        name="rglru",
    )(p_r, p_r, conv0, h0, cw, cb, wa, wx, ba, bx, lam)


def _swa_kernel(q_ref, kp_ref, kc_ref, vp_ref, vc_ref, bias_ref, sink_ref, o_ref, *, NBS, TQ, mask_first):
    GQ = S_HEADS // S_KV_HEADS
    lane = lax.broadcasted_iota(jnp.int32, (1, LANE), 1)
    lo = lane < S_HEAD_DIM

    def both_halves(x, j):
        xr = pltpu.roll(x, S_HEAD_DIM, axis=1)
        return (jnp.where(lo, x, xr) if j == 0 else jnp.where(lo, xr, x)).astype(BF16)

    s1_parts, s2_parts, v_parts = [], [], []
    for b in range(NBS):
        q = q_ref[b] * (S_HEAD_DIM ** -0.5)
        for j in range(S_KV_HEADS):
            stack = []
            for g in range(GQ):
                h = GQ * j + g
                qh = q[:, (h // 2) * LANE:(h // 2 + 1) * LANE]
                stack.append(jnp.where(lo if h % 2 == 0 else jnp.logical_not(lo), qh, 0.0))
            qs = jnp.concatenate(stack, axis=0).astype(BF16)
            s1_parts.append(_dot_nt(qs, both_halves(kp_ref[b], j)))
            s2_parts.append(_dot_nt(qs, both_halves(kc_ref[b], j)))
            v_parts.append((both_halves(vp_ref[b], j), both_halves(vc_ref[b], j)))
    s1 = jnp.concatenate(s1_parts, axis=0) + bias_ref[:, 0:WINDOW]
    s2 = jnp.concatenate(s2_parts, axis=0) + bias_ref[:, WINDOW:WINDOW + TQ]
    if mask_first:
        s1 = jnp.where(pl.program_id(1) == 0, NEG_INF, s1)
    sink = sink_ref[...]
    mx = jnp.maximum(jnp.maximum(jnp.max(s1, axis=1, keepdims=True), jnp.max(s2, axis=1, keepdims=True)), sink)
    p1 = jnp.exp(s1 - mx)
    p2 = jnp.exp(s2 - mx)
    inv = 1.0 / (jnp.sum(p1, axis=1, keepdims=True) + jnp.sum(p2, axis=1, keepdims=True) + jnp.exp(sink - mx))
    p1 = p1.astype(BF16)
    p2 = p2.astype(BF16)
    R = GQ * TQ
    for b in range(NBS):
        for j in range(S_KV_HEADS):
            n = b * S_KV_HEADS + j
            rows = slice(n * R, (n + 1) * R)
            vp, vc = v_parts[n]
            o = (_dot(p1[rows], vp) + _dot(p2[rows], vc)) * inv[rows]
            for pair in range(GQ // 2):
                even = o[(2 * pair) * TQ:(2 * pair + 1) * TQ]
                odd = o[(2 * pair + 1) * TQ:(2 * pair + 2) * TQ]
                blk = (GQ * j) // 2 + pair
                o_ref[b, :, blk * LANE:(blk + 1) * LANE] = jnp.where(lo, even, odd)


def swa(arrs, maps, shapes, bias_rows, sink_rows, grid, NBS, TQ, mask_first):
    kern = functools.partial(_swa_kernel, NBS=NBS, TQ=TQ, mask_first=mask_first)
    in_specs = [pl.BlockSpec(s, m) for s, m in zip(shapes, maps)]
    in_specs += [pl.BlockSpec(bias_rows.shape, lambda b, n: (0, 0)),
                 pl.BlockSpec(sink_rows.shape, lambda b, n: (0, 0))]
    return pl.pallas_call(
        kern,
        grid=grid,
        in_specs=in_specs,
        out_specs=pl.BlockSpec((NBS, TQ, S_WIDTH), lambda b, n: (b, n, 0)),
        out_shape=jax.ShapeDtypeStruct((grid[0] * NBS, grid[1] * TQ, S_WIDTH), F32),
        compiler_params=_cparams("parallel", "arbitrary"),
        name="swa",
    )(*arrs, bias_rows, sink_rows)


def t5_bucket(dist):
    max_exact = N_BUCKETS // 2
    d = jnp.maximum(dist, 0)
    large = max_exact + (jnp.log(jnp.maximum(d, 1).astype(F32) / max_exact)
                         / math.log(MAX_DISTANCE / max_exact) * (N_BUCKETS - max_exact)).astype(jnp.int32)
    return jnp.where(d < max_exact, d, jnp.minimum(large, N_BUCKETS - 1))


def swa_bias_rows(rel_bias, TQ, NBS):
    qi = jnp.arange(TQ)[:, None]
    kj = jnp.arange(WINDOW + TQ)[None, :]
    dist = qi + WINDOW - kj
    onehot = (t5_bucket(dist)[..., None] == jnp.arange(N_BUCKETS)).astype(F32)
    b = jnp.einsum("qkn,nh->hqk", onehot, rel_bias.astype(F32), precision=lax.Precision.HIGHEST)
    b = jnp.where((dist >= 0) & (dist <= WINDOW), b, NEG_INF).reshape(S_HEADS * TQ, WINDOW + TQ)
    return jnp.tile(b, (NBS, 1))


def swa_sink_rows(sinks, TQ, NBS):
    return jnp.tile(jnp.repeat(sinks.astype(F32), TQ), NBS).reshape(-1, 1)


def _merge_kernel(hm, hr, hs, gm, gr, gs, x, wm, wr, ws, wo, out):
    def branch(h_ref, g_ref, w_ref):
        return _sigmoid(g_ref[...]) * _dot(h_ref[...].astype(BF16), w_ref[...])
    merged = branch(hm, gm, wm) + branch(hr, gr, wr) + branch(hs, gs, ws)
    out[...] = x[...] + _dot(merged.astype(BF16), wo[...])


def merge_out(hm, hr, hs, p_g, x, wm, wr, ws, wo):
    T, D = x.shape
    tm = _row_tile(T)
    hspec = lambda w: pl.BlockSpec((tm, w), lambda i: (i, 0))
    gspec = lambda j: pl.BlockSpec((tm, D), lambda i, j=j: (i, j))
    wspec = lambda a: pl.BlockSpec(a.shape, lambda i: (0, 0))
    return pl.pallas_call(
        _merge_kernel,
        grid=(T // tm,),
        in_specs=[hspec(M_WIDTH), hspec(R_WIDTH), hspec(S_WIDTH), gspec(0), gspec(1), gspec(2),
                  hspec(D), wspec(wm), wspec(wr), wspec(ws), wspec(wo)],
        out_specs=hspec(D),
        out_shape=jax.ShapeDtypeStruct((T, D), F32),
        compiler_params=_cparams("parallel"),
        name="merge_out",
    )(hm, hr, hs, p_g, p_g, p_g, x, wm, wr, ws, wo)


def _xattn_kernel(x_ref, g_ref, wq_ref, wo_ref, mk_ref, mv_ref, out_ref, *, nb, tq, interleaved):
    NH = N_MEM_HEADS
    x = x_ref[...].reshape(nb * tq, D_MODEL)
    q = _dot(_rms(x, g_ref[...]).astype(BF16), wq_ref[...])
    head = lambda a, h: a[:, h * X_HEAD_DIM:(h + 1) * X_HEAD_DIM]
    parts = []
    for b in range(nb):
        qb = q[b * tq:(b + 1) * tq]
        if interleaved:
            qs = jnp.concatenate([head(qb, h) for h in range(NH)], axis=0).astype(BF16)
            parts.append(_dot_nt(qs, mk_ref[0, b].astype(BF16)))
        else:
            parts += [_dot_nt(head(qb, h).astype(BF16), head(mk_ref[0, b], h).astype(BF16)) for h in range(NH)]
    s = jnp.concatenate(parts, axis=0) * (X_HEAD_DIM ** -0.5)
    if interleaved:
        row_h = (lax.broadcasted_iota(jnp.int32, s.shape, 0) // tq) % NH
        s = jnp.where(lax.broadcasted_iota(jnp.int32, s.shape, 1) % NH == row_h, s, NEG_INF)
    p = jnp.exp(s - jnp.max(s, axis=1, keepdims=True))
    inv = 1.0 / jnp.sum(p, axis=1, keepdims=True)
    rows = []
    for b in range(nb):
        if interleaved:
            r = slice(b * NH * tq, (b + 1) * NH * tq)
            o = _dot(p[r].astype(BF16), mv_ref[0, b].astype(BF16)) * inv[r]
            heads = [o[h * tq:(h + 1) * tq] for h in range(NH)]
        else:
            heads = []
            for h in range(NH):
                r = slice((b * NH + h) * tq, (b * NH + h + 1) * tq)
                heads.append(_dot(p[r].astype(BF16), head(mv_ref[0, b], h).astype(BF16)) * inv[r])
        rows.append(jnp.concatenate(heads, axis=1))
    o_all = rows[0] if nb == 1 else jnp.concatenate(rows, axis=0)
    y = x + _dot(o_all.astype(BF16), wo_ref[...])
    out_ref[...] = y.reshape(nb, tq, D_MODEL)


def xattn(x, g, wq, wo, mem_k, mem_v, layer, nb, tq, interleaved):
    B, R, D = x.shape
    mem_blk = (1, nb) + mem_k.shape[2:]
    kern = functools.partial(_xattn_kernel, nb=nb, tq=tq, interleaved=interleaved)
    return pl.pallas_call(
        kern,
        grid=(B // nb, R // tq),
        in_specs=[pl.BlockSpec((nb, tq, D), lambda b, r: (b, r, 0)),
                  pl.BlockSpec((1, D), lambda b, r: (0, 0)),
                  pl.BlockSpec(wq.shape, lambda b, r: (0, 0)),
                  pl.BlockSpec(wo.shape, lambda b, r: (0, 0)),
                  pl.BlockSpec(mem_blk, lambda b, r: (layer, b, 0, 0)),
                  pl.BlockSpec(mem_blk, lambda b, r: (layer, b, 0, 0))],
        out_specs=pl.BlockSpec((nb, tq, D), lambda b, r: (b, r, 0)),
        out_shape=jax.ShapeDtypeStruct((B, R, D), F32),
        compiler_params=_cparams("parallel", "arbitrary"),
        name="xattn",
    )(x, g, wq, wo, mem_k, mem_v)


def _router_kernel(x_ref, g_ref, wr_ref, route_out, counts_out, cnt_sc):
    @pl.when(pl.program_id(0) == 0)
    def _():
        cnt_sc[...] = jnp.zeros_like(cnt_sc)

    xn = _rms(x_ref[...], g_ref[...])
    logits = _dot(xn.astype(BF16), wr_ref[...])
    tm = logits.shape[0]
    lane_i = lax.broadcasted_iota(jnp.int32, (tm, LANE), 1)
    lane = lane_i.astype(F32)
    lane_group = (lane_i // EXPERTS_PER_GROUP).astype(F32)

    def top1(v):
        mx = jnp.max(v, axis=1, keepdims=True)
        return mx, jnp.min(jnp.where(v == mx, lane, float(LANE)), axis=1, keepdims=True)

    gl = jnp.where(lane_i < N_GROUPS, logits[:, :LANE], NEG_INF)
    g_max, g_idx = top1(gl)
    g_w = 1.0 / jnp.sum(jnp.exp(gl - g_max), axis=1, keepdims=True)
    el = jnp.where(lane_group == g_idx, logits[:, LANE:], NEG_INF)
    e1, i1 = top1(el)
    e2, i2 = top1(jnp.where(lane == i1, NEG_INF, el))
    t = jnp.exp(e2 - e1)
    p1 = 1.0 / (1.0 + t)

    oh1 = (lane == i1).astype(F32)
    oh2 = (lane == i2).astype(F32)
    oh = oh1 + oh2
    r_idx = lax.broadcasted_iota(jnp.int32, (tm, tm), 0)
    c_idx = lax.broadcasted_iota(jnp.int32, (tm, tm), 1)
    before = (r_idx > c_idx).astype(BF16)
    base = cnt_sc[0:1, :] + _dot(before, oh.astype(BF16))
    rank1 = jnp.sum(oh1 * base, axis=1, keepdims=True)
    rank2 = jnp.sum(oh2 * base, axis=1, keepdims=True)
    cnt = cnt_sc[...] + jnp.sum(oh, axis=0, keepdims=True)
    cnt_sc[...] = cnt
    counts_out[...] = cnt

    vals = (i1, i2, p1 * g_w, t * p1 * g_w, rank1, rank2)
    route = jnp.zeros((tm, LANE), F32)
    for n, v in enumerate(vals):
        route = jnp.where(lane_i == n, v, route)
    route_out[...] = route


def router(x, g, w_router):
    T, D = x.shape
    tm = _row_tile(T)
    return pl.pallas_call(
        _router_kernel,
        grid=(T // tm,),
        in_specs=[pl.BlockSpec((tm, D), lambda i: (i, 0)),
                  pl.BlockSpec((1, D), lambda i: (0, 0)),
                  pl.BlockSpec(w_router.shape, lambda i: (0, 0))],
        out_specs=[pl.BlockSpec((tm, LANE), lambda i: (i, 0)), pl.BlockSpec((SUBLANE, LANE), lambda i: (0, 0))],
        out_shape=[jax.ShapeDtypeStruct((T, LANE), F32), jax.ShapeDtypeStruct((SUBLANE, LANE), F32)],
        scratch_shapes=[pltpu.VMEM((SUBLANE, LANE), F32)],
        compiler_params=_cparams("arbitrary"),
        name="router",
    )(x, g, w_router)


def moe_plan(route, counts, TB, tm):
    T = route.shape[0]
    experts = jnp.arange(N_EXPERTS, dtype=jnp.int32)
    counts = counts[0, :N_EXPERTS].astype(jnp.int32)
    padded = (counts + TB - 1) // TB * TB
    pad_ends = jnp.cumsum(padded)
    pad_starts = pad_ends - padded
    e_idx = route[:, 0:TOP_K].astype(jnp.int32)
    rank = route[:, 4:4 + TOP_K].astype(jnp.int32)
    dest = jnp.sum(jnp.where(e_idx[..., None] == experts, pad_starts, 0), axis=-1) + rank
    n_blocks = -(-(T * TOP_K + N_EXPERTS * (TB - 1)) // TB)
    block_start = jnp.arange(n_blocks, dtype=jnp.int32) * TB
    block_e = jnp.minimum(jnp.sum(pad_ends[None, :] <= block_start[:, None], axis=1), N_EXPERTS - 1).astype(jnp.int32)
    end_valid = jnp.sum(jnp.where(block_e[:, None] == experts, pad_starts + counts, 0), axis=1)
    n_valid = jnp.clip(end_valid - block_start, 0, TB).astype(jnp.int32)
    return block_e, n_valid, dest.reshape(T // tm, 1, TOP_K * tm)


DMA_UNROLL = 8


def _for_row_chunks(n_rows, fn):
    def body(c, carry):
        for u in range(DMA_UNROLL):
            fn(c * DMA_UNROLL + u)
        return carry
    lax.fori_loop(0, n_rows // DMA_UNROLL, body, 0)


def _dispatch_kernel(dst_ref, x_ref, g_ref, init_hbm, xs_hbm, buf, sem, *, tm, n_tiles):
    del init_hbm
    i = pl.program_id(0)
    slot = i % 2

    def copy(row, s, row_dst):
        return pltpu.make_async_copy(buf.at[s, pl.ds(row, 1)], xs_hbm.at[pl.ds(row_dst, 1)], sem.at[s])

    def wait_slot(s):
        def wait_row(r):
            for _ in range(TOP_K):
                copy(r, s, 0).wait()
        _for_row_chunks(tm, wait_row)

    @pl.when(i >= 2)
    def _():
        wait_slot(slot)

    buf[slot] = _rms(x_ref[...], g_ref[...])

    def start_row(r):
        for k in range(TOP_K):
            copy(r, slot, dst_ref[0, 0, TOP_K * r + k]).start(priority=k % 2)
    _for_row_chunks(tm, start_row)

    @pl.when(i == n_tiles - 1)
    def _():
        wait_slot(slot)
        if n_tiles >= 2:
            wait_slot(1 - slot)


def moe_dispatch(x, g, dest, xs_init, tm):
    T, D = x.shape
    n_tiles = T // tm
    assert tm % DMA_UNROLL == 0
    kern = functools.partial(_dispatch_kernel, tm=tm, n_tiles=n_tiles)
    return pl.pallas_call(
        kern,
        grid=(n_tiles,),
        in_specs=[pl.BlockSpec((1, 1, TOP_K * tm), lambda i: (i, 0, 0), memory_space=pltpu.SMEM),
                  pl.BlockSpec((tm, D), lambda i: (i, 0)),
                  pl.BlockSpec((1, D), lambda i: (0, 0)),
                  pl.BlockSpec(memory_space=pl.ANY)],
        out_specs=pl.BlockSpec(memory_space=pl.ANY),
        out_shape=jax.ShapeDtypeStruct(xs_init.shape, F32),
        input_output_aliases={3: 0},
        scratch_shapes=[pltpu.VMEM((2, tm, D), F32), pltpu.SemaphoreType.DMA((2,))],
        compiler_params=_cparams("arbitrary"),
        name="moe_dispatch",
    )(dest, x, g, xs_init)


def _ffn_kernel(be_ref, nv_ref, xs_ref, wg_ref, wu_ref, wd_ref, ys_ref, wg_bf, wu_bf, wd_bf):
    i = pl.program_id(0)
    nv = nv_ref[i]

    @pl.when((i == 0) | (be_ref[i] != be_ref[jnp.maximum(i - 1, 0)]))
    def _():
        wg_bf[...] = wg_ref[0, 0].astype(BF16)
        wu_bf[...] = wu_ref[0, 0].astype(BF16)
        wd_bf[...] = wd_ref[0, 0].astype(BF16)

    @pl.when(nv > 0)
    def _():
        row = lax.broadcasted_iota(jnp.int32, (xs_ref.shape[0], 1), 0)
        x = jnp.where(row < nv, xs_ref[...], 0.0).astype(BF16)
        g = _dot(x, wg_bf[...])
        u = _dot(x, wu_bf[...])
        hid = (g * _sigmoid(g)) * u
        ys_ref[...] = _dot(hid.astype(BF16), wd_bf[...])

    @pl.when(nv == 0)
    def _():
        ys_ref[...] = jnp.zeros_like(ys_ref)


def expert_ffn(xs, block_e, n_valid, w_gate, w_up, w_down, layer, TB):
    P, D = xs.shape
    n_blocks = P // TB
    wspec = lambda a, b: pl.BlockSpec((1, 1, a, b), lambda i, be, nv: (layer, be[i], 0, 0))
    grid_spec = pltpu.PrefetchScalarGridSpec(
        num_scalar_prefetch=2,
        grid=(n_blocks,),
        in_specs=[pl.BlockSpec((TB, D), lambda i, be, nv: (i, 0)),
                  wspec(D, D_EXPERT), wspec(D, D_EXPERT), wspec(D_EXPERT, D)],
        out_specs=pl.BlockSpec((TB, D), lambda i, be, nv: (i, 0)),
        scratch_shapes=[pltpu.VMEM((D, D_EXPERT), BF16), pltpu.VMEM((D, D_EXPERT), BF16),
                        pltpu.VMEM((D_EXPERT, D), BF16)],
    )
    return pl.pallas_call(
        _ffn_kernel,
        grid_spec=grid_spec,
        out_shape=jax.ShapeDtypeStruct((P, D), F32),
        compiler_params=_cparams("arbitrary"),
        name="expert_ffn",
    )(block_e, n_valid, xs, w_gate, w_up, w_down)


def _combine_kernel(cur_ref, nxt_ref, x_ref, route_ref, gf_ref, ys_hbm, out_ref, ybuf, sem, *,
                    tm, n_tiles, final_norm):
    i = pl.program_id(0)
    slot = i % 2

    def copy(row_src, s, k, row):
        return pltpu.make_async_copy(ys_hbm.at[pl.ds(row_src, 1)], ybuf.at[s, k, pl.ds(row, 1)], sem.at[s])

    def start_tile(idx_ref, s):
        def start_row(r):
            for k in range(TOP_K):
                copy(idx_ref[0, 0, TOP_K * r + k], s, k, r).start(priority=k % 2)
        _for_row_chunks(tm, start_row)

    @pl.when(i == 0)
    def _():
        start_tile(cur_ref, 0)

    @pl.when(i + 1 < n_tiles)
    def _():
        start_tile(nxt_ref, 1 - slot)

    def wait_row(r):
        for k in range(TOP_K):
            copy(0, slot, k, r).wait()
    _for_row_chunks(tm, wait_row)

    route = route_ref[...]
    lane_i = lax.broadcasted_iota(jnp.int32, route.shape, 1)
    g0 = jnp.sum(jnp.where(lane_i == 2, route, 0.0), axis=1, keepdims=True)
    g1 = jnp.sum(jnp.where(lane_i == 3, route, 0.0), axis=1, keepdims=True)
    y = x_ref[...] + (g0 * ybuf[slot, 0] + g1 * ybuf[slot, 1])
    out_ref[...] = _rms(y, gf_ref[...]) if final_norm else y


def moe_combine(x, ys, dest, route, g_final, tm, final_norm):
    T, D = x.shape
    n_tiles = T // tm
    assert tm % DMA_UNROLL == 0
    idx_blk = lambda f: pl.BlockSpec((1, 1, TOP_K * tm), f, memory_space=pltpu.SMEM)
    kern = functools.partial(_combine_kernel, tm=tm, n_tiles=n_tiles, final_norm=final_norm)
    return pl.pallas_call(
        kern,
        grid=(n_tiles,),
        in_specs=[idx_blk(lambda i: (i, 0, 0)),
                  idx_blk(lambda i: (jnp.minimum(i + 1, n_tiles - 1), 0, 0)),
                  pl.BlockSpec((tm, D), lambda i: (i, 0)),
                  pl.BlockSpec((tm, LANE), lambda i: (i, 0)),
                  pl.BlockSpec((1, D), lambda i: (0, 0)),
                  pl.BlockSpec(memory_space=pl.ANY)],
        out_specs=pl.BlockSpec((tm, D), lambda i: (i, 0)),
        out_shape=jax.ShapeDtypeStruct((T, D), F32),
        scratch_shapes=[pltpu.VMEM((2, TOP_K, tm, D), F32), pltpu.SemaphoreType.DMA((2,))],
        compiler_params=_cparams("arbitrary"),
        name="moe_combine",
    )(dest, dest, x, route, g_final, ys)


IN_SPLITS = (4 * M_WIDTH, LANE, 2 * R_WIDTH, S_WIDTH + 2 * S_KV_WIDTH, 3 * D_MODEL)


def _prep_w_in(w):
    c0 = 4 * M_WIDTH
    c1 = c0 + 2 * M_HEADS
    gates = jnp.pad(w[:, c0:c1], ((0, 0), (0, LANE - 2 * M_HEADS)))
    return jnp.concatenate([w[:, :c0], gates, w[:, c1:]], axis=1).astype(BF16)


def _prep_router(w_rg, w_re):
    pad = lambda w: jnp.pad(w, ((0, 0), (0, LANE - w.shape[1])))
    return jnp.concatenate([pad(w_rg), pad(w_re)], axis=1).astype(BF16)


def _row(v):
    return v.reshape(1, -1).astype(F32)


def _pad_rows(a, n):
    return jnp.pad(a, ((0, 0), (0, n - a.shape[1])) + ((0, 0),) * (a.ndim - 2))


def _moe_block(T):
    return 256 if T >= 4096 else 128


def moe_layer(x, slots, g_ffn, w_router, w_gate, w_up, w_down, layer, g_final, final_norm):
    T, D = x.shape
    TB, tm = _moe_block(T), _row_tile(T)
    route, counts = router(x, g_ffn, w_router)
    block_e, n_valid, dest = moe_plan(route, counts, TB, tm)
    if slots is None:
        slots = jnp.zeros((block_e.shape[0] * TB, D), F32)
    xs = moe_dispatch(x, g_ffn, dest, slots, tm)
    ys = expert_ffn(xs, block_e, n_valid, w_gate, w_up, w_down, layer, TB)
    return moe_combine(x, ys, dest, route, g_final, tm, final_norm), ys


def kernel(x_prompt, x_sample, mem_prompt, cache_mem_k, cache_mem_v, cache_swa_k, cache_swa_v, state_mlstm_C, state_mlstm_n, state_mlstm_m, state_rglru_h, state_rglru_conv, norm_mix, w_in, m_igate_b, m_fgate_b, m_head_norm, r_conv_w, r_conv_b, r_gate_a_w, r_gate_a_b, r_gate_x_w, r_gate_x_b, r_lambda, swa_sinks, rel_bias, w_branch_m, w_branch_r, w_branch_s, w_out, norm_mem, xq_w, xk_w, xv_w, xo_w, norm_ffn, router_group_w, router_expert_w, moe_w_gate, moe_w_up, moe_w_down, norm_final):
    B, S, D = x_prompt.shape
    NS, TS, _ = x_sample.shape
    depth = w_in.shape[0]
    n_mem = mem_prompt.shape[1]
    Tp, Ts = B * S, NS * TS
    H = M_HEADS
    assert S % MLSTM_CHUNK == 0 and S % WINDOW == 0 and TS <= SUBLANE
    assert NS % XATTN_SHORT_BATCH == 0 and NS % SWA_SHORT_BATCH == 0

    xp = x_prompt.reshape(Tp, D)
    xs = x_sample.reshape(Ts, D)
    mem2d = mem_prompt.reshape(B * n_mem, D)
    cache_k_rows = cache_mem_k.reshape(depth, NS, n_mem * N_MEM_HEADS, X_HEAD_DIM)
    cache_v_rows = cache_mem_v.reshape(depth, NS, n_mem * N_MEM_HEADS, X_HEAD_DIM)
    bias_p = swa_bias_rows(rel_bias, WINDOW, 1)
    bias_s = swa_bias_rows(rel_bias, SWA_SHORT_TQ, SWA_SHORT_BATCH)
    outs_p = {k: [] for k in ("mem_k", "mem_v", "swa_k", "swa_v", "C", "n", "m", "h", "conv")}
    outs_s = {k: [] for k in ("swa_k", "swa_v", "C", "n", "m", "h", "conv")}

    slots_p = slots_s = None
    for l in range(depth):
        w_in_b = _prep_w_in(w_in[l])
        g_mix = _row(norm_mix[l])
        b_i = jnp.pad(m_igate_b[l], (0, SUBLANE - H)).reshape(SUBLANE, 1)
        b_f = jnp.pad(m_fgate_b[l], (0, SUBLANE - H)).reshape(SUBLANE, 1)
        hn = _row(m_head_norm[l])
        r_args = (r_conv_w[l], _row(r_conv_b[l]), r_gate_a_w[l].astype(BF16), r_gate_x_w[l].astype(BF16),
                  _row(r_gate_a_b[l]), _row(r_gate_x_b[l]), _row(r_lambda[l]))
        sink_p = swa_sink_rows(swa_sinks[l], WINDOW, 1)
        sink_s = swa_sink_rows(swa_sinks[l], SWA_SHORT_TQ, SWA_SHORT_BATCH)
        merge_w = (w_branch_m[l].astype(BF16), w_branch_r[l].astype(BF16), w_branch_s[l].astype(BF16),
                   w_out[l].astype(BF16))
        wq_b, wo_b = xq_w[l].astype(BF16), xo_w[l].astype(BF16)
        w_router = _prep_router(router_group_w[l], router_expert_w[l])
        last = l == depth - 1
        moe_args = (_row(norm_ffn[l]), w_router, moe_w_gate, moe_w_up, moe_w_down, l, _row(norm_final), last)

        w_kv = jnp.concatenate([xk_w[l], xv_w[l]], axis=1).astype(BF16)
        mk, mv = norm_matmul(mem2d, g_mix, w_kv, (X_WIDTH, X_WIDTH), normalize=False)
        outs_p["mem_k"].append(mk.reshape(B, n_mem, N_MEM_HEADS, X_HEAD_DIM))
        outs_p["mem_v"].append(mv.reshape(B, n_mem, N_MEM_HEADS, X_HEAD_DIM))
        mk, mv = mk.reshape(1, B, n_mem, X_WIDTH), mv.reshape(1, B, n_mem, X_WIDTH)

        p_m, p_if, p_r, p_s, p_g = norm_matmul(xp, g_mix, w_in_b, IN_SPLITS)
        gts = jnp.swapaxes(p_if[:, :2 * H].reshape(B, S, 2 * H), 1, 2)
        gi = jnp.pad(gts[:, :H], ((0, 0), (0, SUBLANE - H), (0, 0)))
        gf = jnp.pad(gts[:, H:], ((0, 0), (0, SUBLANE - H), (0, 0)))
        h_m, caug, m_o = mlstm(p_m.reshape(B, S, 4 * M_WIDTH), gi, gf, b_i, b_f, hn,
                               jnp.zeros((B, H, M_HEAD_DIM, MLSTM_AUG), F32),
                               jnp.zeros((B, SUBLANE, LANE), F32),
                               L=MLSTM_CHUNK, RB=MLSTM_CHUNK, valid=MLSTM_CHUNK)
        outs_p["C"].append(caug[..., :M_HEAD_DIM])
        outs_p["n"].append(caug[..., M_HEAD_DIM])
        outs_p["m"].append(m_o[:, :H, 0])

        h_r, h_last, conv_last = rglru(p_r.reshape(B, S, 2 * R_WIDTH),
                                       jnp.zeros((B, CONV_W - 1, R_WIDTH), F32),
                                       jnp.zeros((B, 1, R_WIDTH), F32), *r_args,
                                       TR=_row_tile(S), Bs=1)
        outs_p["h"].append(h_last[:, 0])
        outs_p["conv"].append(conv_last)

        p_s3 = p_s.reshape(B, S, S_WIDTH + 2 * S_KV_WIDTH)
        kcol, vcol = S_WIDTH // S_KV_WIDTH, S_WIDTH // S_KV_WIDTH + 1
        prev = lambda col: (lambda b, n: (b, jnp.maximum(n - 1, 0), col))
        cur = lambda col: (lambda b, n: (b, n, col))
        kv_blk = (1, WINDOW, S_KV_WIDTH)
        h_s = swa((p_s3,) * 5, (cur(0), prev(kcol), cur(kcol), prev(vcol), cur(vcol)),
                  ((1, WINDOW, S_WIDTH), kv_blk, kv_blk, kv_blk, kv_blk),
                  bias_p, sink_p, grid=(B, S // WINDOW), NBS=1, TQ=WINDOW, mask_first=True)
        outs_p["swa_k"].append(p_s3[:, S - WINDOW:, S_WIDTH:S_WIDTH + S_KV_WIDTH]
                               .reshape(B, WINDOW, S_KV_HEADS, S_HEAD_DIM))
        outs_p["swa_v"].append(p_s3[:, S - WINDOW:, S_WIDTH + S_KV_WIDTH:]
                               .reshape(B, WINDOW, S_KV_HEADS, S_HEAD_DIM))

        xp = merge_out(h_m.reshape(Tp, M_WIDTH), h_r.reshape(Tp, R_WIDTH), h_s.reshape(Tp, S_WIDTH),
                       p_g, xp, *merge_w)
        xp = xattn(xp.reshape(B, S, D), _row(norm_mem[l]), wq_b, wo_b, mk, mv, layer=0,
                   nb=1, tq=512 if S % 512 == 0 else _row_tile(S), interleaved=False).reshape(Tp, D)
        xp, slots_p = moe_layer(xp, slots_p, *moe_args)

        p_m, p_if, p_r, p_s, p_g = norm_matmul(xs, g_mix, w_in_b, IN_SPLITS)
        LS = MLSTM_SHORT_CHUNK
        gts = jnp.swapaxes(p_if[:, :2 * H].reshape(NS, TS, 2 * H), 1, 2)
        gts = jnp.pad(gts, ((0, 0), (0, 0), (0, LS - TS)))
        gi = jnp.pad(gts[:, :H], ((0, 0), (0, SUBLANE - H), (0, 0)))
        gf = jnp.pad(gts[:, H:], ((0, 0), (0, SUBLANE - H), (0, 0)))
        c0aug = jnp.concatenate([state_mlstm_C[l], state_mlstm_n[l][..., None],
                                 jnp.zeros((NS, H, M_HEAD_DIM, M_HEAD_DIM - 1), F32)], axis=-1)
        m0 = jnp.broadcast_to(jnp.pad(state_mlstm_m[l], ((0, 0), (0, SUBLANE - H)))[:, :, None],
                              (NS, SUBLANE, LANE))
        h_m, caug, m_o = mlstm(_pad_rows(p_m.reshape(NS, TS, 4 * M_WIDTH), SUBLANE), gi, gf, b_i, b_f, hn,
                               c0aug, m0, L=LS, RB=SUBLANE, valid=TS)
        h_m = h_m[:, :TS].reshape(Ts, M_WIDTH)
        outs_s["C"].append(caug[..., :M_HEAD_DIM])
        outs_s["n"].append(caug[..., M_HEAD_DIM])
        outs_s["m"].append(m_o[:, :H, 0])

        tmaj = lambda a: jnp.swapaxes(a, 0, 1).reshape(1, a.shape[0] * a.shape[1], a.shape[2])
        h_r, h_last, conv_last = rglru(tmaj(p_r.reshape(NS, TS, 2 * R_WIDTH)), tmaj(state_rglru_conv[l]),
                                       state_rglru_h[l].reshape(1, NS, R_WIDTH), *r_args,
                                       TR=TS * NS, Bs=NS)
        h_r = jnp.swapaxes(h_r.reshape(TS, NS, R_WIDTH), 0, 1).reshape(Ts, R_WIDTH)
        outs_s["h"].append(h_last[0])
        outs_s["conv"].append(jnp.swapaxes(conv_last.reshape(CONV_W - 1, NS, R_WIDTH), 0, 1))

        TQ, NBS = SWA_SHORT_TQ, SWA_SHORT_BATCH
        p_s3 = p_s.reshape(NS, TS, S_WIDTH + 2 * S_KV_WIDTH)
        q_s = _pad_rows(p_s3[:, :, :S_WIDTH], TQ)
        k_new = p_s3[:, :, S_WIDTH:S_WIDTH + S_KV_WIDTH]
        v_new = p_s3[:, :, S_WIDTH + S_KV_WIDTH:]
        k_cache = cache_swa_k[l].reshape(NS, WINDOW, S_KV_WIDTH)
        v_cache = cache_swa_v[l].reshape(NS, WINDOW, S_KV_WIDTH)
        full = lambda b, n: (b, 0, 0)
        h_s = swa((q_s, k_cache, _pad_rows(k_new, TQ), v_cache, _pad_rows(v_new, TQ)), (full,) * 5,
                  ((NBS, TQ, S_WIDTH), (NBS, WINDOW, S_KV_WIDTH), (NBS, TQ, S_KV_WIDTH),
                   (NBS, WINDOW, S_KV_WIDTH), (NBS, TQ, S_KV_WIDTH)),
                  bias_s, sink_s, grid=(NS // NBS, 1), NBS=NBS, TQ=TQ, mask_first=False)
        h_s = h_s[:, :TS].reshape(Ts, S_WIDTH)
        outs_s["swa_k"].append(jnp.concatenate([k_cache[:, TS:], k_new], axis=1)
                               .reshape(NS, WINDOW, S_KV_HEADS, S_HEAD_DIM))
        outs_s["swa_v"].append(jnp.concatenate([v_cache[:, TS:], v_new], axis=1)
                               .reshape(NS, WINDOW, S_KV_HEADS, S_HEAD_DIM))

        xs = merge_out(h_m, h_r, h_s, p_g, xs, *merge_w)
        xs8 = xattn(_pad_rows(xs.reshape(NS, TS, D), SUBLANE), _row(norm_mem[l]), wq_b, wo_b,
                    cache_k_rows, cache_v_rows, layer=l, nb=XATTN_SHORT_BATCH, tq=SUBLANE, interleaved=True)
        xs, slots_s = moe_layer(xs8[:, :TS].reshape(Ts, D), slots_s, *moe_args)

    st = lambda d, k: jnp.stack(d[k])
    return (xp.reshape(B, S, D), xs.reshape(NS, TS, D), st(outs_p, "mem_k"), st(outs_p, "mem_v"),
            st(outs_p, "swa_k"), st(outs_p, "swa_v"), st(outs_p, "C"), st(outs_p, "n"), st(outs_p, "m"),
            st(outs_p, "h"), st(outs_p, "conv"),
            st(outs_s, "swa_k"), st(outs_s, "swa_v"), st(outs_s, "C"), st(outs_s, "n"), st(outs_s, "m"),
            st(outs_s, "h"), st(outs_s, "conv"))
```

```python
import functools
import math

import jax
import jax.numpy as jnp
from jax import lax
from jax.experimental import pallas as pl
from jax.experimental.pallas import tpu as pltpu

F32 = jnp.float32
BF16 = jnp.bfloat16
NEG_INF = float("-inf")

LANE = 128
SUBLANE = 8
BF16_SUBLANE = 16
VMEM_LIMIT_BYTES = 56 * 1024 * 1024

D_MODEL = 1024
M_HEADS = 4
M_HEAD_DIM = 128
M_WIDTH = M_HEADS * M_HEAD_DIM
R_WIDTH = 512
R_BLOCKS = 4
R_BLOCK_DIM = R_WIDTH // R_BLOCKS
CONV_W = 4
LRU_C = 8.0
S_HEADS = 8
S_KV_HEADS = 2
S_HEAD_DIM = 64
S_WIDTH = S_HEADS * S_HEAD_DIM
S_KV_WIDTH = S_KV_HEADS * S_HEAD_DIM
WINDOW = 128
N_BUCKETS = 32
MAX_DISTANCE = 128
N_MEM_HEADS = 4
X_HEAD_DIM = 128
X_WIDTH = N_MEM_HEADS * X_HEAD_DIM
N_GROUPS = 4
EXPERTS_PER_GROUP = 8
N_EXPERTS = N_GROUPS * EXPERTS_PER_GROUP
TOP_K = 2
D_EXPERT = 512
EPS = 1e-6

MLSTM_CHUNK = 256
MLSTM_AUG = 2 * M_HEAD_DIM
MLSTM_SHORT_BATCH = 8
SWA_SHORT_TQ = BF16_SUBLANE
SWA_SHORT_BATCH = 8
XATTN_SHORT_BATCH = 8


def _cparams(*sem):
    return pltpu.CompilerParams(dimension_semantics=sem, vmem_limit_bytes=VMEM_LIMIT_BYTES)


def _rms(x, g):
    ms = jnp.mean(x * x, axis=-1, keepdims=True)
    return x * lax.rsqrt(ms + EPS) * g


def _sigmoid(x):
    return 1.0 / (1.0 + jnp.exp(-x))


def _softplus(x):
    return jnp.maximum(x, 0.0) + jnp.log1p(jnp.exp(-jnp.abs(x)))


def _log_sigmoid(x):
    return -_softplus(-x)


def _gelu_tanh(x):
    return 0.5 * x * (1.0 + jnp.tanh(math.sqrt(2.0 / math.pi) * (x + 0.044715 * (x * x * x))))


def _dot(a, b):
    return jnp.dot(a, b, preferred_element_type=F32)


def _dot_nt(a, b):
    return lax.dot_general(a, b, (((1,), (1,)), ((), ())), preferred_element_type=F32)


def _row_tile(n, pref=256):
    for t in (256, 128, 64, 32, 16, 8):
        if t <= pref and n % t == 0:
            return t
    raise ValueError(f"row count {n} is not a multiple of {SUBLANE}")


def _norm_matmul_kernel(x_ref, g_ref, w_ref, *out_refs, splits, normalize):
    x = x_ref[...]
    if normalize:
        x = _rms(x, g_ref[...])
    xb = x.astype(BF16)
    off = 0
    for o_ref, n in zip(out_refs, splits):
        o_ref[...] = _dot(xb, w_ref[:, off:off + n])
        off += n


def norm_matmul(x, g, w_bf16, splits, normalize=True):
    T, D = x.shape
    tm = _row_tile(T)
    N = w_bf16.shape[1]
    assert sum(splits) == N and all(s % LANE == 0 for s in splits)
    kern = functools.partial(_norm_matmul_kernel, splits=tuple(splits), normalize=normalize)
    return pl.pallas_call(
        kern,
        grid=(T // tm,),
        in_specs=[pl.BlockSpec((tm, D), lambda i: (i, 0)),
                  pl.BlockSpec((1, D), lambda i: (0, 0)),
                  pl.BlockSpec((D, N), lambda i: (0, 0))],
        out_specs=[pl.BlockSpec((tm, n), lambda i: (i, 0)) for n in splits],
        out_shape=[jax.ShapeDtypeStruct((T, n), F32) for n in splits],
        compiler_params=_cparams("parallel"),
        name="norm_matmul",
    )(x, g, w_bf16)


def _mlstm_kernel(q_ref, k_ref, v_ref, o_ref, gi_ref, gf_ref, bi_ref, bf_ref, hn_ref,
                  c0_ref, m0_ref, h_out, c_out, m_out, c_sc, m_sc, *, L, NBB):
    @pl.when(pl.program_id(1) == 0)
    def _():
        c_sc[...] = c0_ref[...]
        m_sc[...] = m0_ref[...]

    t_idx = lax.broadcasted_iota(jnp.int32, (L, L), 0)
    s_idx = lax.broadcasted_iota(jnp.int32, (L, L), 1)
    causal = t_idx >= s_idx
    eye = t_idx == s_idx
    row8 = lax.broadcasted_iota(jnp.int32, (SUBLANE, LANE), 0)
    ones_col = (lax.broadcasted_iota(jnp.int32, (L, M_HEAD_DIM), 1) == 0).astype(BF16)
    gi = [gi_ref[b] + bi_ref[...] for b in range(NBB)]
    gf = [_log_sigmoid(gf_ref[b] + bf_ref[...]) for b in range(NBB)]
    m_tiles = [m_sc[b] for b in range(NBB)]
    chains = [dict(b=b, h=h, hs=slice(h * M_HEAD_DIM, (h + 1) * M_HEAD_DIM))
              for b in range(NBB) for h in range(M_HEADS)]

    for c in chains:
        b, h = c["b"], c["h"]
        c["m_prev"] = m_tiles[b][h:h + 1, 0:1]
        c["b_col"] = jnp.sum(jnp.where(causal, gf[b][h:h + 1, :], 0.0), axis=1, keepdims=True)
    for c in chains:
        b, h = c["b"], c["h"]
        b_row = jnp.sum(jnp.where(eye, c["b_col"], 0.0), axis=0, keepdims=True)
        c["r_row"] = gi[b][h:h + 1, :] - b_row
        c["g_end"] = b_row[:, L - 1:L]
        c["log_in"] = jnp.where(causal, c["b_col"] + c["r_row"], NEG_INF)
        c["m_intra"] = jnp.max(c["log_in"], axis=1, keepdims=True)
    for c in chains:
        b, hs = c["b"], c["hs"]
        c["k"] = k_ref[b][:, hs] * (M_HEAD_DIM ** -0.5)
        c["caug"] = c_sc[b, c["h"]]
        c["vaug"] = jnp.concatenate([v_ref[b][:, hs].astype(BF16), ones_col], axis=1)
        qb = q_ref[b][:, hs].astype(BF16)
        c["qk"] = _dot_nt(qb, c["k"].astype(BF16))
        c["qc"] = _dot(qb, c["caug"].astype(BF16))
    for c in chains:
        log_prev = c["b_col"] + c["m_prev"]
        m_t = jnp.maximum(log_prev, c["m_intra"])
        sc = (c["qk"] * jnp.exp(c["log_in"] - m_t)).astype(BF16)
        nd = jnp.exp(log_prev - m_t) * c["qc"] + _dot(sc, c["vaug"])
        den = nd[:, M_HEAD_DIM:M_HEAD_DIM + 1]
        c["hh"] = nd[:, :M_HEAD_DIM] / jnp.maximum(jnp.abs(den), jnp.exp(-m_t))
    for c in chains:
        b, hs = c["b"], c["hs"]
        d = c["hh"] - jnp.mean(c["hh"], axis=-1, keepdims=True)
        y = d * lax.rsqrt(jnp.mean(d * d, axis=-1, keepdims=True) + EPS) * hn_ref[:, hs]
        h_out[b, :, hs] = y * _sigmoid(o_ref[b][:, hs])
    for c in chains:
        b, h = c["b"], c["h"]
        lie = c["g_end"] + c["r_row"]
        m_end = jnp.maximum(c["g_end"] + c["m_prev"], jnp.max(lie, axis=1, keepdims=True))
        w_pe = jnp.exp(c["g_end"] + c["m_prev"] - m_end)
        ktw = (c["k"].T * jnp.exp(lie - m_end)).astype(BF16)
        c_new = w_pe * c["caug"] + _dot(ktw, c["vaug"])
        c_sc[b, h] = c_new
        c_out[b, h] = c_new
        m_tiles[b] = jnp.where(row8 == h, m_end, m_tiles[b])
    for b in range(NBB):
        m_sc[b] = m_tiles[b]
        m_out[b] = m_tiles[b]


def mlstm(p_m, gates_i, gates_f, b_i, b_f, head_norm, c0aug, m0, L):
    NB, R, _ = p_m.shape
    nc = R // L
    assert R % L == 0
    NBB = 2 if NB % 2 == 0 else 1
    blk = lambda j: pl.BlockSpec((NBB, L, M_WIDTH), lambda b, c, j=j: (b, c, j))
    state_spec = pl.BlockSpec((NBB, M_HEADS, M_HEAD_DIM, MLSTM_AUG), lambda b, c: (b, 0, 0, 0))
    m_spec = pl.BlockSpec((NBB, SUBLANE, LANE), lambda b, c: (b, 0, 0))
    gate_spec = pl.BlockSpec((NBB, SUBLANE, L), lambda b, c: (b, 0, c))
    kern = functools.partial(_mlstm_kernel, L=L, NBB=NBB)
    scratch = [pltpu.VMEM((NBB, M_HEADS, M_HEAD_DIM, MLSTM_AUG), F32), pltpu.VMEM((NBB, SUBLANE, LANE), F32)]
    return pl.pallas_call(
        kern,
        grid=(NB // NBB, nc),
        in_specs=[blk(0), blk(1), blk(2), blk(3), gate_spec, gate_spec,
                  pl.BlockSpec((SUBLANE, 1), lambda b, c: (0, 0)),
                  pl.BlockSpec((SUBLANE, 1), lambda b, c: (0, 0)),
                  pl.BlockSpec((1, M_WIDTH), lambda b, c: (0, 0)),
                  state_spec, m_spec],
        out_specs=[pl.BlockSpec((NBB, L, M_WIDTH), lambda b, c: (b, c, 0)), state_spec, m_spec],
        out_shape=[jax.ShapeDtypeStruct((NB, R, M_WIDTH), F32),
                   jax.ShapeDtypeStruct((NB, M_HEADS, M_HEAD_DIM, MLSTM_AUG), F32),
                   jax.ShapeDtypeStruct((NB, SUBLANE, LANE), F32)],
        scratch_shapes=scratch,
        compiler_params=_cparams("parallel", "arbitrary"),
        name="mlstm",
    )(p_m, p_m, p_m, p_m, gates_i, gates_f, b_i, b_f, head_norm, c0aug, m0)


def _mlstm_short_kernel(q_ref, k_ref, v_ref, o_ref, gi_ref, gf_ref, bi_ref, bf_ref, hn_ref,
                        c0_ref, n0_ref, m0_ref, h_out, c_out, n_out, m_out, *, NBS, TS):
    L = SUBLANE
    t_idx = lax.broadcasted_iota(jnp.int32, (L, L), 0)
    s_idx = lax.broadcasted_iota(jnp.int32, (L, L), 1)
    causal = t_idx >= s_idx
    eye = t_idx == s_idx
    valid = lax.broadcasted_iota(jnp.int32, (1, L), 1) < TS
    row8 = lax.broadcasted_iota(jnp.int32, (SUBLANE, LANE), 0)
    gi = [gi_ref[b] + bi_ref[...] for b in range(NBS)]
    gf = [_log_sigmoid(gf_ref[b] + bf_ref[...]) for b in range(NBS)]
    m_tiles = [m0_ref[b] for b in range(NBS)]
    chains = [dict(b=b, h=h, hs=slice(h * M_HEAD_DIM, (h + 1) * M_HEAD_DIM))
              for b in range(NBS) for h in range(M_HEADS)]

    for c in chains:
        b, h = c["b"], c["h"]
        c["li"] = jnp.where(valid, gi[b][h:h + 1, 0:L], NEG_INF)
        lf = jnp.where(valid, gf[b][h:h + 1, 0:L], 0.0)
        c["m_prev"] = m_tiles[b][h:h + 1, 0:1]
        c["b_col"] = jnp.sum(jnp.where(causal, lf, 0.0), axis=1, keepdims=True)
    for c in chains:
        b_row = jnp.sum(jnp.where(eye, c["b_col"], 0.0), axis=0, keepdims=True)
        c["r_row"] = c["li"] - b_row
        c["g_end"] = b_row[:, L - 1:L]
        c["log_in"] = jnp.where(causal, c["b_col"] + c["r_row"], NEG_INF)
        c["m_intra"] = jnp.max(c["log_in"], axis=1, keepdims=True)
    for c in chains:
        b, hs = c["b"], c["hs"]
        c["q"] = q_ref[b][:, hs]
        c["k"] = k_ref[b][:, hs] * (M_HEAD_DIM ** -0.5)
        c["vb"] = v_ref[b][:, hs].astype(BF16)
        c["c0"] = c0_ref[0, b, c["h"]]
        c["n0"] = n0_ref[0, b, c["h"]:c["h"] + 1, :]
        qb = c["q"].astype(BF16)
        c["qk"] = _dot_nt(qb, c["k"].astype(BF16))
        c["qc"] = _dot(qb, c["c0"].astype(BF16))
        c["qn"] = jnp.sum(c["q"] * c["n0"], axis=1, keepdims=True)
    for c in chains:
        log_prev = c["b_col"] + c["m_prev"]
        m_t = jnp.maximum(log_prev, c["m_intra"])
        w_prev = jnp.exp(log_prev - m_t)
        sc = c["qk"] * jnp.exp(c["log_in"] - m_t)
        num = w_prev * c["qc"] + _dot(sc.astype(BF16), c["vb"])
        den = w_prev * c["qn"] + jnp.sum(sc, axis=1, keepdims=True)
        c["hh"] = num / jnp.maximum(jnp.abs(den), jnp.exp(-m_t))
    for c in chains:
        b, hs = c["b"], c["hs"]
        d = c["hh"] - jnp.mean(c["hh"], axis=-1, keepdims=True)
        y = d * lax.rsqrt(jnp.mean(d * d, axis=-1, keepdims=True) + EPS) * hn_ref[:, hs]
        h_out[b, :, hs] = y * _sigmoid(o_ref[b][:, hs])
    for c in chains:
        b, h = c["b"], c["h"]
        lie = c["g_end"] + c["r_row"]
        m_end = jnp.maximum(c["g_end"] + c["m_prev"], jnp.max(lie, axis=1, keepdims=True))
        w_pe = jnp.exp(c["g_end"] + c["m_prev"] - m_end)
        w_e = jnp.exp(lie - m_end)
        w_e_col = jnp.sum(jnp.where(eye, w_e, 0.0), axis=1, keepdims=True)
        kw = c["k"] * w_e_col
        c_out[b, h] = w_pe * c["c0"] + lax.dot_general(kw.astype(BF16), c["vb"], (((0,), (0,)), ((), ())),
                                                       preferred_element_type=F32)
        n_out[b, h:h + 1, :] = w_pe * c["n0"] + jnp.sum(kw, axis=0, keepdims=True)
        m_tiles[b] = jnp.where(row8 == h, m_end, m_tiles[b])
    for b in range(NBS):
        m_out[b] = m_tiles[b]


def mlstm_short(p_m, gates_i, gates_f, b_i, b_f, head_norm, c0, n0, m0, layer, NBS, TS):
    NB = p_m.shape[0]
    H, DH = M_HEADS, M_HEAD_DIM
    assert NB % NBS == 0 and TS <= SUBLANE
    blk = lambda j: pl.BlockSpec((NBS, SUBLANE, M_WIDTH), lambda i, j=j: (i, 0, j))
    tile = pl.BlockSpec((NBS, SUBLANE, LANE), lambda i: (i, 0, 0))
    kern = functools.partial(_mlstm_short_kernel, NBS=NBS, TS=TS)
    return pl.pallas_call(
        kern,
        grid=(NB // NBS,),
        in_specs=[blk(0), blk(1), blk(2), blk(3), tile, tile,
                  pl.BlockSpec((SUBLANE, 1), lambda i: (0, 0)),
                  pl.BlockSpec((SUBLANE, 1), lambda i: (0, 0)),
                  pl.BlockSpec((1, M_WIDTH), lambda i: (0, 0)),
                  pl.BlockSpec((1, NBS, H, DH, DH), lambda i: (layer, i, 0, 0, 0)),
                  pl.BlockSpec((1, NBS, H, DH), lambda i: (layer, i, 0, 0)),
                  tile],
        out_specs=[pl.BlockSpec((NBS, SUBLANE, M_WIDTH), lambda i: (i, 0, 0)),
                   pl.BlockSpec((NBS, H, DH, DH), lambda i: (i, 0, 0, 0)),
                   pl.BlockSpec((NBS, H, DH), lambda i: (i, 0, 0)),
                   tile],
        out_shape=[jax.ShapeDtypeStruct((NB, SUBLANE, M_WIDTH), F32),
                   jax.ShapeDtypeStruct((NB, H, DH, DH), F32),
                   jax.ShapeDtypeStruct((NB, H, DH), F32),
                   jax.ShapeDtypeStruct((NB, SUBLANE, LANE), F32)],
        compiler_params=_cparams("parallel"),
        name="mlstm_short",
    )(p_m, p_m, p_m, p_m, gates_i, gates_f, b_i, b_f, head_norm, c0, n0, m0)


def _rglru_kernel(x_ref, g_ref, conv0_ref, h0_ref, cw_ref, cb_ref, wa_ref, wx_ref, ba_ref, bx_ref,
                  lam_ref, y_out, hlast_out, conv_out, xpad_sc, hc_sc, *, TR, Bs):
    CB = (CONV_W - 1) * Bs
    X0 = -(-CB // SUBLANE) * SUBLANE

    @pl.when(pl.program_id(1) == 0)
    def _():
        xpad_sc[X0 - CB:X0, :] = conv0_ref[0]
        hc_sc[...] = h0_ref[0]

    xpad_sc[X0:X0 + TR, :] = x_ref[0]
    y = cb_ref[...]
    for j in range(CONV_W):
        y = y + xpad_sc[X0 - CB + j * Bs:X0 - CB + j * Bs + TR, :] * cw_ref[j:j + 1, :]
    tail = xpad_sc[X0 + TR - CB:X0 + TR, :]
    conv_out[0] = tail
    xpad_sc[X0 - CB:X0, :] = tail

    yb = y.astype(BF16)
    rs, is_ = [], []
    for n in range(R_BLOCKS):
        sl = slice(n * R_BLOCK_DIM, (n + 1) * R_BLOCK_DIM)
        rs.append(_dot(yb[:, sl], wa_ref[n]))
        is_.append(_dot(yb[:, sl], wx_ref[n]))
    r = _sigmoid(jnp.concatenate(rs, axis=1) + ba_ref[...])
    i = _sigmoid(jnp.concatenate(is_, axis=1) + bx_ref[...])
    log_a = -LRU_C * r * _softplus(-lam_ref[...])
    a_cum = jnp.exp(log_a)
    t = jnp.tanh(log_a)
    u_cum = jnp.sqrt(-2.0 * t / (1.0 - t)) * (i * y)

    def scan_rows(a, u, axis, stride):
        n = a.shape[axis]
        idx = lax.broadcasted_iota(jnp.int32, a.shape, axis)
        d = stride
        while d < n:
            keep = idx >= d
            u = jnp.where(keep, a * pltpu.roll(u, d, axis=axis) + u, u)
            a = jnp.where(keep, a * pltpu.roll(a, d, axis=axis), a)
            d *= 2
        return a, u

    hc = hc_sc[...]
    if Bs == 1:
        NG = TR // SUBLANE
        a3, u3 = scan_rows(a_cum.reshape(NG, SUBLANE, R_WIDTH), u_cum.reshape(NG, SUBLANE, R_WIDTH), 1, 1)
        a_tot, u_tot = scan_rows(a3[:, SUBLANE - 1, :], u3[:, SUBLANE - 1, :], 0, 1)
        h_end = u_tot + a_tot * hc
        g_idx = lax.broadcasted_iota(jnp.int32, (NG, R_WIDTH), 0)
        h_start = jnp.where(g_idx == 0, hc, pltpu.roll(h_end, 1, axis=0))
        h = (u3 + a3 * h_start[:, None, :]).reshape(TR, R_WIDTH)
    else:
        a_cum, u_cum = scan_rows(a_cum, u_cum, 0, Bs)
        h = u_cum + a_cum * jnp.concatenate([hc] * (TR // Bs), axis=0)
    y_out[0] = h * _gelu_tanh(g_ref[0])
    h_last = h[TR - Bs:, :]
    hc_sc[...] = h_last
    hlast_out[0] = h_last


def rglru(p_r, conv0, h0, cw, cb, wa, wx, ba, bx, lam, TR, Bs):
    G, R, _ = p_r.shape
    CB = (CONV_W - 1) * Bs
    X0 = -(-CB // SUBLANE) * SUBLANE
    assert R % TR == 0 and TR % Bs == 0 and TR >= CB
    W = R_WIDTH
    const = lambda shape: pl.BlockSpec(shape, lambda g, r: (0,) * len(shape))
    kern = functools.partial(_rglru_kernel, TR=TR, Bs=Bs)
    return pl.pallas_call(
        kern,
        grid=(G, R // TR),
        in_specs=[pl.BlockSpec((1, TR, W), lambda g, r: (g, r, 0)),
                  pl.BlockSpec((1, TR, W), lambda g, r: (g, r, 1)),
                  pl.BlockSpec((1, CB, W), lambda g, r: (g, 0, 0)),
                  pl.BlockSpec((1, Bs, W), lambda g, r: (g, 0, 0)),
                  const((CONV_W, W)), const((1, W)),
                  const((R_BLOCKS, R_BLOCK_DIM, R_BLOCK_DIM)), const((R_BLOCKS, R_BLOCK_DIM, R_BLOCK_DIM)),
                  const((1, W)), const((1, W)), const((1, W))],
        out_specs=[pl.BlockSpec((1, TR, W), lambda g, r: (g, r, 0)),
                   pl.BlockSpec((1, Bs, W), lambda g, r: (g, 0, 0)),
                   pl.BlockSpec((1, CB, W), lambda g, r: (g, 0, 0))],
        out_shape=[jax.ShapeDtypeStruct((G, R, W), F32),
                   jax.ShapeDtypeStruct((G, Bs, W), F32),
                   jax.ShapeDtypeStruct((G, CB, W), F32)],
        scratch_shapes=[pltpu.VMEM((X0 + TR, W), F32), pltpu.VMEM((Bs, W), F32)],
        compiler_params=_cparams("parallel", "arbitrary"),
        name="rglru",
    )(p_r, p_r, conv0, h0, cw, cb, wa, wx, ba, bx, lam)


def _swa_kernel(q_ref, kp_ref, kc_ref, vp_ref, vc_ref, bias_ref, sink_ref, o_ref, *, NBS, TQ, mask_first):
    GQ = S_HEADS // S_KV_HEADS
    lane = lax.broadcasted_iota(jnp.int32, (1, LANE), 1)
    lo = lane < S_HEAD_DIM

    def both_halves(x, j):
        xr = pltpu.roll(x, S_HEAD_DIM, axis=1)
        return (jnp.where(lo, x, xr) if j == 0 else jnp.where(lo, xr, x)).astype(BF16)

    s1_parts, s2_parts, v_parts = [], [], []
    for b in range(NBS):
        q = q_ref[b] * (S_HEAD_DIM ** -0.5)
        for j in range(S_KV_HEADS):
            stack = []
            for g in range(GQ):
                h = GQ * j + g
                qh = q[:, (h // 2) * LANE:(h // 2 + 1) * LANE]
                stack.append(jnp.where(lo if h % 2 == 0 else jnp.logical_not(lo), qh, 0.0))
            qs = jnp.concatenate(stack, axis=0).astype(BF16)
            s1_parts.append(_dot_nt(qs, both_halves(kp_ref[b], j)))
            s2_parts.append(_dot_nt(qs, both_halves(kc_ref[b], j)))
            v_parts.append((both_halves(vp_ref[b], j), both_halves(vc_ref[b], j)))
    s1 = jnp.concatenate(s1_parts, axis=0) + bias_ref[:, 0:WINDOW]
    s2 = jnp.concatenate(s2_parts, axis=0) + bias_ref[:, WINDOW:WINDOW + TQ]
    if mask_first:
        s1 = jnp.where(pl.program_id(1) == 0, NEG_INF, s1)
    sink = sink_ref[...]
    mx = jnp.maximum(jnp.maximum(jnp.max(s1, axis=1, keepdims=True), jnp.max(s2, axis=1, keepdims=True)), sink)
    p1 = jnp.exp(s1 - mx)
    p2 = jnp.exp(s2 - mx)
    inv = 1.0 / (jnp.sum(p1, axis=1, keepdims=True) + jnp.sum(p2, axis=1, keepdims=True) + jnp.exp(sink - mx))
    p1 = p1.astype(BF16)
    p2 = p2.astype(BF16)
    R = GQ * TQ
    for b in range(NBS):
        for j in range(S_KV_HEADS):
            n = b * S_KV_HEADS + j
            rows = slice(n * R, (n + 1) * R)
            vp, vc = v_parts[n]
            o = (_dot(p1[rows], vp) + _dot(p2[rows], vc)) * inv[rows]
            for pair in range(GQ // 2):
                even = o[(2 * pair) * TQ:(2 * pair + 1) * TQ]
                odd = o[(2 * pair + 1) * TQ:(2 * pair + 2) * TQ]
                blk = (GQ * j) // 2 + pair
                o_ref[b, :, blk * LANE:(blk + 1) * LANE] = jnp.where(lo, even, odd)


def swa(arrs, maps, shapes, bias_rows, sink_rows, grid, NBS, TQ, mask_first):
    kern = functools.partial(_swa_kernel, NBS=NBS, TQ=TQ, mask_first=mask_first)
    in_specs = [pl.BlockSpec(s, m) for s, m in zip(shapes, maps)]
    in_specs += [pl.BlockSpec(bias_rows.shape, lambda b, n: (0, 0)),
                 pl.BlockSpec(sink_rows.shape, lambda b, n: (0, 0))]
    return pl.pallas_call(
        kern,
        grid=grid,
        in_specs=in_specs,
        out_specs=pl.BlockSpec((NBS, TQ, S_WIDTH), lambda b, n: (b, n, 0)),
        out_shape=jax.ShapeDtypeStruct((grid[0] * NBS, grid[1] * TQ, S_WIDTH), F32),
        compiler_params=_cparams("parallel", "arbitrary"),
        name="swa",
    )(*arrs, bias_rows, sink_rows)


def t5_bucket(dist):
    max_exact = N_BUCKETS // 2
    d = jnp.maximum(dist, 0)
    large = max_exact + (jnp.log(jnp.maximum(d, 1).astype(F32) / max_exact)
                         / math.log(MAX_DISTANCE / max_exact) * (N_BUCKETS - max_exact)).astype(jnp.int32)
    return jnp.where(d < max_exact, d, jnp.minimum(large, N_BUCKETS - 1))


def swa_bias_rows(rel_bias, TQ, NBS):
    qi = jnp.arange(TQ)[:, None]
    kj = jnp.arange(WINDOW + TQ)[None, :]
    dist = qi + WINDOW - kj
    onehot = (t5_bucket(dist)[..., None] == jnp.arange(N_BUCKETS)).astype(F32)
    b = jnp.einsum("qkn,nh->hqk", onehot, rel_bias.astype(F32), precision=lax.Precision.HIGHEST)
    b = jnp.where((dist >= 0) & (dist <= WINDOW), b, NEG_INF).reshape(S_HEADS * TQ, WINDOW + TQ)
    return jnp.tile(b, (NBS, 1))


def swa_sink_rows(sinks, TQ, NBS):
    return jnp.tile(jnp.repeat(sinks.astype(F32), TQ), NBS).reshape(-1, 1)


def _merge_kernel(hm, hr, hs, gm, gr, gs, x, wm, wr, ws, wo, out):
    def branch(h_ref, g_ref, w_ref):
        return _sigmoid(g_ref[...]) * _dot(h_ref[...].astype(BF16), w_ref[...])
    merged = branch(hm, gm, wm) + branch(hr, gr, wr) + branch(hs, gs, ws)
    out[...] = x[...] + _dot(merged.astype(BF16), wo[...])


def merge_out(hm, hr, hs, p_g, x, wm, wr, ws, wo):
    T, D = x.shape
    tm = _row_tile(T)
    hspec = lambda w: pl.BlockSpec((tm, w), lambda i: (i, 0))
    gspec = lambda j: pl.BlockSpec((tm, D), lambda i, j=j: (i, j))
    wspec = lambda a: pl.BlockSpec(a.shape, lambda i: (0, 0))
    return pl.pallas_call(
        _merge_kernel,
        grid=(T // tm,),
        in_specs=[hspec(M_WIDTH), hspec(R_WIDTH), hspec(S_WIDTH), gspec(0), gspec(1), gspec(2),
                  hspec(D), wspec(wm), wspec(wr), wspec(ws), wspec(wo)],
        out_specs=hspec(D),
        out_shape=jax.ShapeDtypeStruct((T, D), F32),
        compiler_params=_cparams("parallel"),
        name="merge_out",
    )(hm, hr, hs, p_g, p_g, p_g, x, wm, wr, ws, wo)


def _xattn_kernel(x_ref, g_ref, wq_ref, wo_ref, mk_ref, mv_ref, out_ref, *, nb, tq, interleaved):
    NH = N_MEM_HEADS
    x = x_ref[...].reshape(nb * tq, D_MODEL)
    q = _dot(_rms(x, g_ref[...]).astype(BF16), wq_ref[...])
    head = lambda a, h: a[:, h * X_HEAD_DIM:(h + 1) * X_HEAD_DIM]
    parts = []
    for b in range(nb):
        qb = q[b * tq:(b + 1) * tq]
        if interleaved:
            qs = jnp.concatenate([head(qb, h) for h in range(NH)], axis=0).astype(BF16)
            parts.append(_dot_nt(qs, mk_ref[0, b].astype(BF16)))
        else:
            parts += [_dot_nt(head(qb, h).astype(BF16), head(mk_ref[0, b], h).astype(BF16)) for h in range(NH)]
    s = jnp.concatenate(parts, axis=0) * (X_HEAD_DIM ** -0.5)
    if interleaved:
        row_h = (lax.broadcasted_iota(jnp.int32, s.shape, 0) // tq) % NH
        s = jnp.where(lax.broadcasted_iota(jnp.int32, s.shape, 1) % NH == row_h, s, NEG_INF)
    p = jnp.exp(s - jnp.max(s, axis=1, keepdims=True))
    inv = 1.0 / jnp.sum(p, axis=1, keepdims=True)
    rows = []
    for b in range(nb):
        if interleaved:
            r = slice(b * NH * tq, (b + 1) * NH * tq)
            o = _dot(p[r].astype(BF16), mv_ref[0, b].astype(BF16)) * inv[r]
            heads = [o[h * tq:(h + 1) * tq] for h in range(NH)]
        else:
            heads = []
            for h in range(NH):
                r = slice((b * NH + h) * tq, (b * NH + h + 1) * tq)
                heads.append(_dot(p[r].astype(BF16), head(mv_ref[0, b], h).astype(BF16)) * inv[r])
        rows.append(jnp.concatenate(heads, axis=1))
    o_all = rows[0] if nb == 1 else jnp.concatenate(rows, axis=0)
    y = x + _dot(o_all.astype(BF16), wo_ref[...])
    out_ref[...] = y.reshape(nb, tq, D_MODEL)


def xattn(x, g, wq, wo, mem_k, mem_v, layer, nb, tq, interleaved):
    B, R, D = x.shape
    mem_blk = (1, nb) + mem_k.shape[2:]
    kern = functools.partial(_xattn_kernel, nb=nb, tq=tq, interleaved=interleaved)
    return pl.pallas_call(
        kern,
        grid=(B // nb, R // tq),
        in_specs=[pl.BlockSpec((nb, tq, D), lambda b, r: (b, r, 0)),
                  pl.BlockSpec((1, D), lambda b, r: (0, 0)),
                  pl.BlockSpec(wq.shape, lambda b, r: (0, 0)),
                  pl.BlockSpec(wo.shape, lambda b, r: (0, 0)),
                  pl.BlockSpec(mem_blk, lambda b, r: (layer, b, 0, 0)),
                  pl.BlockSpec(mem_blk, lambda b, r: (layer, b, 0, 0))],
        out_specs=pl.BlockSpec((nb, tq, D), lambda b, r: (b, r, 0)),
        out_shape=jax.ShapeDtypeStruct((B, R, D), F32),
        compiler_params=_cparams("parallel", "arbitrary"),
        name="xattn",
    )(x, g, wq, wo, mem_k, mem_v)


def _router_kernel(x_ref, g_ref, wr_ref, route_out, counts_out, cnt_sc):
    @pl.when(pl.program_id(0) == 0)
    def _():
        cnt_sc[...] = jnp.zeros_like(cnt_sc)

    xn = _rms(x_ref[...], g_ref[...])
    logits = _dot(xn.astype(BF16), wr_ref[...])
    tm = logits.shape[0]
    lane_i = lax.broadcasted_iota(jnp.int32, (tm, LANE), 1)
    lane = lane_i.astype(F32)
    lane_group = (lane_i // EXPERTS_PER_GROUP).astype(F32)

    def top1(v):
        mx = jnp.max(v, axis=1, keepdims=True)
        return mx, jnp.min(jnp.where(v == mx, lane, float(LANE)), axis=1, keepdims=True)

    gl = jnp.where(lane_i < N_GROUPS, logits[:, :LANE], NEG_INF)
    g_max, g_idx = top1(gl)
    g_w = 1.0 / jnp.sum(jnp.exp(gl - g_max), axis=1, keepdims=True)
    el = jnp.where(lane_group == g_idx, logits[:, LANE:], NEG_INF)
    e1, i1 = top1(el)
    e2, i2 = top1(jnp.where(lane == i1, NEG_INF, el))
    t = jnp.exp(e2 - e1)
    p1 = 1.0 / (1.0 + t)

    oh1 = (lane == i1).astype(F32)
    oh2 = (lane == i2).astype(F32)
    oh = oh1 + oh2
    r_idx = lax.broadcasted_iota(jnp.int32, (tm, tm), 0)
    c_idx = lax.broadcasted_iota(jnp.int32, (tm, tm), 1)
    before = (r_idx > c_idx).astype(BF16)
    base = cnt_sc[0:1, :] + _dot(before, oh.astype(BF16))
    rank1 = jnp.sum(oh1 * base, axis=1, keepdims=True)
    rank2 = jnp.sum(oh2 * base, axis=1, keepdims=True)
    cnt = cnt_sc[...] + jnp.sum(oh, axis=0, keepdims=True)
    cnt_sc[...] = cnt
    counts_out[...] = cnt

    vals = (i1, i2, p1 * g_w, t * p1 * g_w, rank1, rank2)
    route = jnp.zeros((tm, LANE), F32)
    for n, v in enumerate(vals):
        route = jnp.where(lane_i == n, v, route)
    route_out[...] = route


def router(x, g, w_router):
    T, D = x.shape
    tm = _row_tile(T)
    return pl.pallas_call(
        _router_kernel,
        grid=(T // tm,),
        in_specs=[pl.BlockSpec((tm, D), lambda i: (i, 0)),
                  pl.BlockSpec((1, D), lambda i: (0, 0)),
                  pl.BlockSpec(w_router.shape, lambda i: (0, 0))],
        out_specs=[pl.BlockSpec((tm, LANE), lambda i: (i, 0)), pl.BlockSpec((SUBLANE, LANE), lambda i: (0, 0))],
        out_shape=[jax.ShapeDtypeStruct((T, LANE), F32), jax.ShapeDtypeStruct((SUBLANE, LANE), F32)],
        scratch_shapes=[pltpu.VMEM((SUBLANE, LANE), F32)],
        compiler_params=_cparams("arbitrary"),
        name="router",
    )(x, g, w_router)


def moe_plan(route, counts, TB, tm):
    T = route.shape[0]
    experts = jnp.arange(N_EXPERTS, dtype=jnp.int32)
    counts = counts[0, :N_EXPERTS].astype(jnp.int32)
    padded = (counts + TB - 1) // TB * TB
    pad_ends = jnp.cumsum(padded)
    pad_starts = pad_ends - padded
    e_idx = route[:, 0:TOP_K].astype(jnp.int32)
    rank = route[:, 4:4 + TOP_K].astype(jnp.int32)
    dest = jnp.sum(jnp.where(e_idx[..., None] == experts, pad_starts, 0), axis=-1) + rank
    n_blocks = -(-(T * TOP_K + N_EXPERTS * (TB - 1)) // TB)
    block_start = jnp.arange(n_blocks, dtype=jnp.int32) * TB
    block_e = jnp.minimum(jnp.sum(pad_ends[None, :] <= block_start[:, None], axis=1), N_EXPERTS - 1).astype(jnp.int32)
    end_valid = jnp.sum(jnp.where(block_e[:, None] == experts, pad_starts + counts, 0), axis=1)
    n_valid = jnp.clip(end_valid - block_start, 0, TB).astype(jnp.int32)
    return block_e, n_valid, dest.reshape(T // tm, 1, TOP_K * tm)


DMA_UNROLL = 8


def _for_row_chunks(n_rows, fn):
    def body(c, carry):
        for u in range(DMA_UNROLL):
            fn(c * DMA_UNROLL + u)
        return carry
    lax.fori_loop(0, n_rows // DMA_UNROLL, body, 0)


def _dispatch_kernel(dst_ref, x_ref, g_ref, init_hbm, xs_hbm, buf, sem, *, tm, n_tiles):
    del init_hbm
    i = pl.program_id(0)
    slot = i % 2

    def copy(row, s, row_dst):
        return pltpu.make_async_copy(buf.at[s, pl.ds(row, 1)], xs_hbm.at[pl.ds(row_dst, 1)], sem.at[s])

    def wait_slot(s):
        def wait_row(r):
            for _ in range(TOP_K):
                copy(r, s, 0).wait()
        _for_row_chunks(tm, wait_row)

    @pl.when(i >= 2)
    def _():
        wait_slot(slot)

    buf[slot] = _rms(x_ref[...], g_ref[...])

    def start_row(r):
        for k in range(TOP_K):
            copy(r, slot, dst_ref[0, 0, TOP_K * r + k]).start(priority=k % 2)
    _for_row_chunks(tm, start_row)

    @pl.when(i == n_tiles - 1)
    def _():
        wait_slot(slot)
        if n_tiles >= 2:
            wait_slot(1 - slot)


def moe_dispatch(x, g, dest, xs_init, tm):
    T, D = x.shape
    n_tiles = T // tm
    assert tm % DMA_UNROLL == 0
    kern = functools.partial(_dispatch_kernel, tm=tm, n_tiles=n_tiles)
    return pl.pallas_call(
        kern,
        grid=(n_tiles,),
        in_specs=[pl.BlockSpec((1, 1, TOP_K * tm), lambda i: (i, 0, 0), memory_space=pltpu.SMEM),
                  pl.BlockSpec((tm, D), lambda i: (i, 0)),
                  pl.BlockSpec((1, D), lambda i: (0, 0)),
                  pl.BlockSpec(memory_space=pl.ANY)],
        out_specs=pl.BlockSpec(memory_space=pl.ANY),
        out_shape=jax.ShapeDtypeStruct(xs_init.shape, F32),
        input_output_aliases={3: 0},
        scratch_shapes=[pltpu.VMEM((2, tm, D), F32), pltpu.SemaphoreType.DMA((2,))],
        compiler_params=_cparams("arbitrary"),
        name="moe_dispatch",
    )(dest, x, g, xs_init)


def _ffn_kernel(be_ref, nv_ref, xs_ref, wg_ref, wu_ref, wd_ref, ys_ref, wg_bf, wu_bf, wd_bf):
    i = pl.program_id(0)
    nv = nv_ref[i]

    @pl.when((i == 0) | (be_ref[i] != be_ref[jnp.maximum(i - 1, 0)]))
    def _():
        wg_bf[...] = wg_ref[0, 0].astype(BF16)
        wu_bf[...] = wu_ref[0, 0].astype(BF16)
        wd_bf[...] = wd_ref[0, 0].astype(BF16)

    @pl.when(nv > 0)
    def _():
        row = lax.broadcasted_iota(jnp.int32, (xs_ref.shape[0], 1), 0)
        x = jnp.where(row < nv, xs_ref[...], 0.0).astype(BF16)
        g = _dot(x, wg_bf[...])
        u = _dot(x, wu_bf[...])
        hid = (g * _sigmoid(g)) * u
        ys_ref[...] = _dot(hid.astype(BF16), wd_bf[...])

    @pl.when(nv == 0)
    def _():
        ys_ref[...] = jnp.zeros_like(ys_ref)


def expert_ffn(xs, block_e, n_valid, w_gate, w_up, w_down, layer, TB):
    P, D = xs.shape
    n_blocks = P // TB
    wspec = lambda a, b: pl.BlockSpec((1, 1, a, b), lambda i, be, nv: (layer, be[i], 0, 0))
    grid_spec = pltpu.PrefetchScalarGridSpec(
        num_scalar_prefetch=2,
        grid=(n_blocks,),
        in_specs=[pl.BlockSpec((TB, D), lambda i, be, nv: (i, 0)),
                  wspec(D, D_EXPERT), wspec(D, D_EXPERT), wspec(D_EXPERT, D)],
        out_specs=pl.BlockSpec((TB, D), lambda i, be, nv: (i, 0)),
        scratch_shapes=[pltpu.VMEM((D, D_EXPERT), BF16), pltpu.VMEM((D, D_EXPERT), BF16),
                        pltpu.VMEM((D_EXPERT, D), BF16)],
    )
    return pl.pallas_call(
        _ffn_kernel,
        grid_spec=grid_spec,
        out_shape=jax.ShapeDtypeStruct((P, D), F32),
        compiler_params=_cparams("arbitrary"),
        name="expert_ffn",
    )(block_e, n_valid, xs, w_gate, w_up, w_down)


def _combine_kernel(cur_ref, nxt_ref, x_ref, route_ref, gf_ref, ys_hbm, out_ref, ybuf, sem, *,
                    tm, n_tiles, final_norm):
    i = pl.program_id(0)
    slot = i % 2

    def copy(row_src, s, k, row):
        return pltpu.make_async_copy(ys_hbm.at[pl.ds(row_src, 1)], ybuf.at[s, k, pl.ds(row, 1)], sem.at[s])

    def start_tile(idx_ref, s):
        def start_row(r):
            for k in range(TOP_K):
                copy(idx_ref[0, 0, TOP_K * r + k], s, k, r).start(priority=k % 2)
        _for_row_chunks(tm, start_row)

    @pl.when(i == 0)
    def _():
        start_tile(cur_ref, 0)

    @pl.when(i + 1 < n_tiles)
    def _():
        start_tile(nxt_ref, 1 - slot)

    def wait_row(r):
        for k in range(TOP_K):
            copy(0, slot, k, r).wait()
    _for_row_chunks(tm, wait_row)

    route = route_ref[...]
    lane_i = lax.broadcasted_iota(jnp.int32, route.shape, 1)
    g0 = jnp.sum(jnp.where(lane_i == 2, route, 0.0), axis=1, keepdims=True)
    g1 = jnp.sum(jnp.where(lane_i == 3, route, 0.0), axis=1, keepdims=True)
    y = x_ref[...] + (g0 * ybuf[slot, 0] + g1 * ybuf[slot, 1])
    out_ref[...] = _rms(y, gf_ref[...]) if final_norm else y


def moe_combine(x, ys, dest, route, g_final, tm, final_norm):
    T, D = x.shape
    n_tiles = T // tm
    assert tm % DMA_UNROLL == 0
    idx_blk = lambda f: pl.BlockSpec((1, 1, TOP_K * tm), f, memory_space=pltpu.SMEM)
    kern = functools.partial(_combine_kernel, tm=tm, n_tiles=n_tiles, final_norm=final_norm)
    return pl.pallas_call(
        kern,
        grid=(n_tiles,),
        in_specs=[idx_blk(lambda i: (i, 0, 0)),
                  idx_blk(lambda i: (jnp.minimum(i + 1, n_tiles - 1), 0, 0)),
                  pl.BlockSpec((tm, D), lambda i: (i, 0)),
                  pl.BlockSpec((tm, LANE), lambda i: (i, 0)),
                  pl.BlockSpec((1, D), lambda i: (0, 0)),
                  pl.BlockSpec(memory_space=pl.ANY)],
        out_specs=pl.BlockSpec((tm, D), lambda i: (i, 0)),
        out_shape=jax.ShapeDtypeStruct((T, D), F32),
        scratch_shapes=[pltpu.VMEM((2, TOP_K, tm, D), F32), pltpu.SemaphoreType.DMA((2,))],
        compiler_params=_cparams("arbitrary"),
        name="moe_combine",
    )(dest, dest, x, route, g_final, ys)


IN_SPLITS = (4 * M_WIDTH, LANE, 2 * R_WIDTH, S_WIDTH + 2 * S_KV_WIDTH, 3 * D_MODEL)


def _prep_w_in(w):
    c0 = 4 * M_WIDTH
    c1 = c0 + 2 * M_HEADS
    gates = jnp.pad(w[:, c0:c1], ((0, 0), (0, LANE - 2 * M_HEADS)))
    return jnp.concatenate([w[:, :c0], gates, w[:, c1:]], axis=1).astype(BF16)


def _prep_router(w_rg, w_re):
    pad = lambda w: jnp.pad(w, ((0, 0), (0, LANE - w.shape[1])))
    return jnp.concatenate([pad(w_rg), pad(w_re)], axis=1).astype(BF16)


def _row(v):
    return v.reshape(1, -1).astype(F32)


def _pad_rows(a, n):
    return jnp.pad(a, ((0, 0), (0, n - a.shape[1])) + ((0, 0),) * (a.ndim - 2))


def _moe_block(T):
    return 256 if T >= 4096 else 128


def moe_layer(x, slots, g_ffn, w_router, w_gate, w_up, w_down, layer, g_final, final_norm):
    T, D = x.shape
    TB, tm = _moe_block(T), _row_tile(T)
    route, counts = router(x, g_ffn, w_router)
    block_e, n_valid, dest = moe_plan(route, counts, TB, tm)
    if slots is None:
        slots = jnp.zeros((block_e.shape[0] * TB, D), F32)
    xs = moe_dispatch(x, g_ffn, dest, slots, tm)
    ys = expert_ffn(xs, block_e, n_valid, w_gate, w_up, w_down, layer, TB)
    return moe_combine(x, ys, dest, route, g_final, tm, final_norm), ys


def kernel(x_prompt, x_sample, mem_prompt, cache_mem_k, cache_mem_v, cache_swa_k, cache_swa_v, state_mlstm_C, state_mlstm_n, state_mlstm_m, state_rglru_h, state_rglru_conv, norm_mix, w_in, m_igate_b, m_fgate_b, m_head_norm, r_conv_w, r_conv_b, r_gate_a_w, r_gate_a_b, r_gate_x_w, r_gate_x_b, r_lambda, swa_sinks, rel_bias, w_branch_m, w_branch_r, w_branch_s, w_out, norm_mem, xq_w, xk_w, xv_w, xo_w, norm_ffn, router_group_w, router_expert_w, moe_w_gate, moe_w_up, moe_w_down, norm_final):
    B, S, D = x_prompt.shape
    NS, TS, _ = x_sample.shape
    depth = w_in.shape[0]
    n_mem = mem_prompt.shape[1]
    Tp, Ts = B * S, NS * TS
    H = M_HEADS
    assert S % MLSTM_CHUNK == 0 and S % WINDOW == 0 and TS <= SUBLANE
    assert NS % XATTN_SHORT_BATCH == 0 and NS % SWA_SHORT_BATCH == 0 and NS % MLSTM_SHORT_BATCH == 0

    xp = x_prompt.reshape(Tp, D)
    xs = x_sample.reshape(Ts, D)
    mem2d = mem_prompt.reshape(B * n_mem, D)
    cache_k_rows = cache_mem_k.reshape(depth, NS, n_mem * N_MEM_HEADS, X_HEAD_DIM)
    cache_v_rows = cache_mem_v.reshape(depth, NS, n_mem * N_MEM_HEADS, X_HEAD_DIM)
    bias_p = swa_bias_rows(rel_bias, WINDOW, 1)
    bias_s = swa_bias_rows(rel_bias, SWA_SHORT_TQ, SWA_SHORT_BATCH)
    outs_p = {k: [] for k in ("mem_k", "mem_v", "swa_k", "swa_v", "C", "n", "m", "h", "conv")}
    outs_s = {k: [] for k in ("swa_k", "swa_v", "C", "n", "m", "h", "conv")}

    slots_p = slots_s = None
    for l in range(depth):
        w_in_b = _prep_w_in(w_in[l])
        g_mix = _row(norm_mix[l])
        b_i = jnp.pad(m_igate_b[l], (0, SUBLANE - H)).reshape(SUBLANE, 1)
        b_f = jnp.pad(m_fgate_b[l], (0, SUBLANE - H)).reshape(SUBLANE, 1)
        hn = _row(m_head_norm[l])
        r_args = (r_conv_w[l], _row(r_conv_b[l]), r_gate_a_w[l].astype(BF16), r_gate_x_w[l].astype(BF16),
                  _row(r_gate_a_b[l]), _row(r_gate_x_b[l]), _row(r_lambda[l]))
        sink_p = swa_sink_rows(swa_sinks[l], WINDOW, 1)
        sink_s = swa_sink_rows(swa_sinks[l], SWA_SHORT_TQ, SWA_SHORT_BATCH)
        merge_w = (w_branch_m[l].astype(BF16), w_branch_r[l].astype(BF16), w_branch_s[l].astype(BF16),
                   w_out[l].astype(BF16))
        wq_b, wo_b = xq_w[l].astype(BF16), xo_w[l].astype(BF16)
        w_router = _prep_router(router_group_w[l], router_expert_w[l])
        last = l == depth - 1
        moe_args = (_row(norm_ffn[l]), w_router, moe_w_gate, moe_w_up, moe_w_down, l, _row(norm_final), last)

        w_kv = jnp.concatenate([xk_w[l], xv_w[l]], axis=1).astype(BF16)
        mk, mv = norm_matmul(mem2d, g_mix, w_kv, (X_WIDTH, X_WIDTH), normalize=False)
        outs_p["mem_k"].append(mk.reshape(B, n_mem, N_MEM_HEADS, X_HEAD_DIM))
        outs_p["mem_v"].append(mv.reshape(B, n_mem, N_MEM_HEADS, X_HEAD_DIM))
        mk, mv = mk.reshape(1, B, n_mem, X_WIDTH), mv.reshape(1, B, n_mem, X_WIDTH)

        p_m, p_if, p_r, p_s, p_g = norm_matmul(xp, g_mix, w_in_b, IN_SPLITS)
        gts = jnp.swapaxes(p_if[:, :2 * H].reshape(B, S, 2 * H), 1, 2)
        gi = jnp.pad(gts[:, :H], ((0, 0), (0, SUBLANE - H), (0, 0)))
        gf = jnp.pad(gts[:, H:], ((0, 0), (0, SUBLANE - H), (0, 0)))
        h_m, caug, m_o = mlstm(p_m.reshape(B, S, 4 * M_WIDTH), gi, gf, b_i, b_f, hn,
                               jnp.zeros((B, H, M_HEAD_DIM, MLSTM_AUG), F32),
                               jnp.zeros((B, SUBLANE, LANE), F32), L=MLSTM_CHUNK)
        outs_p["C"].append(caug[..., :M_HEAD_DIM])
        outs_p["n"].append(caug[..., M_HEAD_DIM])
        outs_p["m"].append(m_o[:, :H, 0])

        h_r, h_last, conv_last = rglru(p_r.reshape(B, S, 2 * R_WIDTH),
                                       jnp.zeros((B, CONV_W - 1, R_WIDTH), F32),
                                       jnp.zeros((B, 1, R_WIDTH), F32), *r_args,
                                       TR=_row_tile(S), Bs=1)
        outs_p["h"].append(h_last[:, 0])
        outs_p["conv"].append(conv_last)

        p_s3 = p_s.reshape(B, S, S_WIDTH + 2 * S_KV_WIDTH)
        kcol, vcol = S_WIDTH // S_KV_WIDTH, S_WIDTH // S_KV_WIDTH + 1
        prev = lambda col: (lambda b, n: (b, jnp.maximum(n - 1, 0), col))
        cur = lambda col: (lambda b, n: (b, n, col))
        kv_blk = (1, WINDOW, S_KV_WIDTH)
        h_s = swa((p_s3,) * 5, (cur(0), prev(kcol), cur(kcol), prev(vcol), cur(vcol)),
                  ((1, WINDOW, S_WIDTH), kv_blk, kv_blk, kv_blk, kv_blk),
                  bias_p, sink_p, grid=(B, S // WINDOW), NBS=1, TQ=WINDOW, mask_first=True)
        outs_p["swa_k"].append(p_s3[:, S - WINDOW:, S_WIDTH:S_WIDTH + S_KV_WIDTH]
                               .reshape(B, WINDOW, S_KV_HEADS, S_HEAD_DIM))
        outs_p["swa_v"].append(p_s3[:, S - WINDOW:, S_WIDTH + S_KV_WIDTH:]
                               .reshape(B, WINDOW, S_KV_HEADS, S_HEAD_DIM))

        xp = merge_out(h_m.reshape(Tp, M_WIDTH), h_r.reshape(Tp, R_WIDTH), h_s.reshape(Tp, S_WIDTH),
                       p_g, xp, *merge_w)
        xp = xattn(xp.reshape(B, S, D), _row(norm_mem[l]), wq_b, wo_b, mk, mv, layer=0,
                   nb=1, tq=512 if S % 512 == 0 else _row_tile(S), interleaved=False).reshape(Tp, D)
        xp, slots_p = moe_layer(xp, slots_p, *moe_args)

        p_m, p_if, p_r, p_s, p_g = norm_matmul(xs, g_mix, w_in_b, IN_SPLITS)
        gts = jnp.swapaxes(p_if[:, :2 * H].reshape(NS, TS, 2 * H), 1, 2)
        gts = jnp.pad(gts, ((0, 0), (0, 0), (0, LANE - TS)))
        gi = jnp.pad(gts[:, :H], ((0, 0), (0, SUBLANE - H), (0, 0)))
        gf = jnp.pad(gts[:, H:], ((0, 0), (0, SUBLANE - H), (0, 0)))
        m0 = jnp.broadcast_to(jnp.pad(state_mlstm_m[l], ((0, 0), (0, SUBLANE - H)))[:, :, None],
                              (NS, SUBLANE, LANE))
        h_m, c_s, n_s, m_o = mlstm_short(_pad_rows(p_m.reshape(NS, TS, 4 * M_WIDTH), SUBLANE), gi, gf, b_i, b_f,
                                         hn, state_mlstm_C, state_mlstm_n, m0, layer=l,
                                         NBS=MLSTM_SHORT_BATCH, TS=TS)
        h_m = h_m[:, :TS].reshape(Ts, M_WIDTH)
        outs_s["C"].append(c_s)
        outs_s["n"].append(n_s)
        outs_s["m"].append(m_o[:, :H, 0])

        tmaj = lambda a: jnp.swapaxes(a, 0, 1).reshape(1, a.shape[0] * a.shape[1], a.shape[2])
        h_r, h_last, conv_last = rglru(tmaj(p_r.reshape(NS, TS, 2 * R_WIDTH)), tmaj(state_rglru_conv[l]),
                                       state_rglru_h[l].reshape(1, NS, R_WIDTH), *r_args,
                                       TR=TS * NS, Bs=NS)
        h_r = jnp.swapaxes(h_r.reshape(TS, NS, R_WIDTH), 0, 1).reshape(Ts, R_WIDTH)
        outs_s["h"].append(h_last[0])
        outs_s["conv"].append(jnp.swapaxes(conv_last.reshape(CONV_W - 1, NS, R_WIDTH), 0, 1))

        TQ, NBS = SWA_SHORT_TQ, SWA_SHORT_BATCH
        p_s3 = p_s.reshape(NS, TS, S_WIDTH + 2 * S_KV_WIDTH)
        q_s = _pad_rows(p_s3[:, :, :S_WIDTH], TQ)
        k_new = p_s3[:, :, S_WIDTH:S_WIDTH + S_KV_WIDTH]
        v_new = p_s3[:, :, S_WIDTH + S_KV_WIDTH:]
        k_cache = cache_swa_k[l].reshape(NS, WINDOW, S_KV_WIDTH)
        v_cache = cache_swa_v[l].reshape(NS, WINDOW, S_KV_WIDTH)
        full = lambda b, n: (b, 0, 0)
        h_s = swa((q_s, k_cache, _pad_rows(k_new, TQ), v_cache, _pad_rows(v_new, TQ)), (full,) * 5,
                  ((NBS, TQ, S_WIDTH), (NBS, WINDOW, S_KV_WIDTH), (NBS, TQ, S_KV_WIDTH),
                   (NBS, WINDOW, S_KV_WIDTH), (NBS, TQ, S_KV_WIDTH)),
                  bias_s, sink_s, grid=(NS // NBS, 1), NBS=NBS, TQ=TQ, mask_first=False)
        h_s = h_s[:, :TS].reshape(Ts, S_WIDTH)
        outs_s["swa_k"].append(jnp.concatenate([k_cache[:, TS:], k_new], axis=1)
                               .reshape(NS, WINDOW, S_KV_HEADS, S_HEAD_DIM))
        outs_s["swa_v"].append(jnp.concatenate([v_cache[:, TS:], v_new], axis=1)
                               .reshape(NS, WINDOW, S_KV_HEADS, S_HEAD_DIM))

        xs = merge_out(h_m, h_r, h_s, p_g, xs, *merge_w)
        xs8 = xattn(_pad_rows(xs.reshape(NS, TS, D), SUBLANE), _row(norm_mem[l]), wq_b, wo_b,
                    cache_k_rows, cache_v_rows, layer=l, nb=XATTN_SHORT_BATCH, tq=SUBLANE, interleaved=True)
        xs, slots_s = moe_layer(xs8[:, :TS].reshape(Ts, D), slots_s, *moe_args)

    st = lambda d, k: jnp.stack(d[k])
    return (xp.reshape(B, S, D), xs.reshape(NS, TS, D), st(outs_p, "mem_k"), st(outs_p, "mem_v"),
            st(outs_p, "swa_k"), st(outs_p, "swa_v"), st(outs_p, "C"), st(outs_p, "n"), st(outs_p, "m"),
            st(outs_p, "h"), st(outs_p, "conv"),
            st(outs_s, "swa_k"), st(outs_s, "swa_v"), st(outs_s, "C"), st(outs_s, "n"), st(outs_s, "m"),
            st(outs_s, "h"), st(outs_s, "conv"))
```

```python
import functools
import math

import jax
import jax.numpy as jnp
from jax import lax
from jax.experimental import pallas as pl
from jax.experimental.pallas import tpu as pltpu

F32 = jnp.float32
BF16 = jnp.bfloat16
NEG_INF = float("-inf")

LANE = 128
SUBLANE = 8
BF16_SUBLANE = 16
VMEM_LIMIT_BYTES = 56 * 1024 * 1024

D_MODEL = 1024
M_HEADS = 4
M_HEAD_DIM = 128
M_WIDTH = M_HEADS * M_HEAD_DIM
R_WIDTH = 512
R_BLOCKS = 4
R_BLOCK_DIM = R_WIDTH // R_BLOCKS
CONV_W = 4
LRU_C = 8.0
S_HEADS = 8
S_KV_HEADS = 2
S_HEAD_DIM = 64
S_WIDTH = S_HEADS * S_HEAD_DIM
S_KV_WIDTH = S_KV_HEADS * S_HEAD_DIM
WINDOW = 128
N_BUCKETS = 32
MAX_DISTANCE = 128
N_MEM_HEADS = 4
X_HEAD_DIM = 128
X_WIDTH = N_MEM_HEADS * X_HEAD_DIM
N_GROUPS = 4
EXPERTS_PER_GROUP = 8
N_EXPERTS = N_GROUPS * EXPERTS_PER_GROUP
TOP_K = 2
D_EXPERT = 512
EPS = 1e-6

MLSTM_CHUNK = 256
MLSTM_AUG = 2 * M_HEAD_DIM
MLSTM_SHORT_BATCH = 8
SWA_SHORT_TQ = BF16_SUBLANE
SWA_SHORT_BATCH = 8
XATTN_SHORT_BATCH = 8


def _cparams(*sem):
    return pltpu.CompilerParams(dimension_semantics=sem, vmem_limit_bytes=VMEM_LIMIT_BYTES)


def _rms(x, g):
    ms = jnp.mean(x * x, axis=-1, keepdims=True)
    return x * lax.rsqrt(ms + EPS) * g


def _sigmoid(x):
    return 1.0 / (1.0 + jnp.exp(-x))


def _softplus(x):
    return jnp.maximum(x, 0.0) + jnp.log1p(jnp.exp(-jnp.abs(x)))


def _log_sigmoid(x):
    return -_softplus(-x)


def _gelu_tanh(x):
    return 0.5 * x * (1.0 + jnp.tanh(math.sqrt(2.0 / math.pi) * (x + 0.044715 * (x * x * x))))


def _dot(a, b):
    return jnp.dot(a, b, preferred_element_type=F32)


def _dot_nt(a, b):
    return lax.dot_general(a, b, (((1,), (1,)), ((), ())), preferred_element_type=F32)


def _row_tile(n, pref=256):
    for t in (256, 128, 64, 32, 16, 8):
        if t <= pref and n % t == 0:
            return t
    raise ValueError(f"row count {n} is not a multiple of {SUBLANE}")


def _norm_matmul_kernel(x_ref, g_ref, w_ref, *out_refs, splits, normalize):
    x = x_ref[...]
    if normalize:
        x = _rms(x, g_ref[...])
    xb = x.astype(BF16)
    off = 0
    for o_ref, n in zip(out_refs, splits):
        o_ref[...] = _dot(xb, w_ref[:, off:off + n]).astype(o_ref.dtype)
        off += n


def norm_matmul(x, g, w_bf16, splits, normalize=True, out_dtypes=None):
    out_dtypes = out_dtypes or (F32,) * len(splits)
    T, D = x.shape
    tm = _row_tile(T)
    N = w_bf16.shape[1]
    assert sum(splits) == N and all(s % LANE == 0 for s in splits)
    kern = functools.partial(_norm_matmul_kernel, splits=tuple(splits), normalize=normalize)
    return pl.pallas_call(
        kern,
        grid=(T // tm,),
        in_specs=[pl.BlockSpec((tm, D), lambda i: (i, 0)),
                  pl.BlockSpec((1, D), lambda i: (0, 0)),
                  pl.BlockSpec((D, N), lambda i: (0, 0))],
        out_specs=[pl.BlockSpec((tm, n), lambda i: (i, 0)) for n in splits],
        out_shape=[jax.ShapeDtypeStruct((T, n), dt) for n, dt in zip(splits, out_dtypes)],
        compiler_params=_cparams("parallel"),
        name="norm_matmul",
    )(x, g, w_bf16)


def _mlstm_kernel(q_ref, k_ref, v_ref, o_ref, gi_ref, gf_ref, bi_ref, bf_ref, hn_ref,
                  c0_ref, m0_ref, h_out, c_out, m_out, c_sc, m_sc, *, L, NBB):
    @pl.when(pl.program_id(1) == 0)
    def _():
        c_sc[...] = c0_ref[...]
        m_sc[...] = m0_ref[...]

    t_idx = lax.broadcasted_iota(jnp.int32, (L, L), 0)
    s_idx = lax.broadcasted_iota(jnp.int32, (L, L), 1)
    causal = t_idx >= s_idx
    eye = t_idx == s_idx
    row8 = lax.broadcasted_iota(jnp.int32, (SUBLANE, LANE), 0)
    ones_col = (lax.broadcasted_iota(jnp.int32, (L, M_HEAD_DIM), 1) == 0).astype(BF16)
    gi = [gi_ref[b] + bi_ref[...] for b in range(NBB)]
    gf = [_log_sigmoid(gf_ref[b] + bf_ref[...]) for b in range(NBB)]
    m_tiles = [m_sc[b] for b in range(NBB)]
    chains = [dict(b=b, h=h, hs=slice(h * M_HEAD_DIM, (h + 1) * M_HEAD_DIM))
              for b in range(NBB) for h in range(M_HEADS)]

    for c in chains:
        b, h = c["b"], c["h"]
        c["m_prev"] = m_tiles[b][h:h + 1, 0:1]
        c["b_col"] = jnp.sum(jnp.where(causal, gf[b][h:h + 1, :], 0.0), axis=1, keepdims=True)
    for c in chains:
        b, h = c["b"], c["h"]
        b_row = jnp.sum(jnp.where(eye, c["b_col"], 0.0), axis=0, keepdims=True)
        c["r_row"] = gi[b][h:h + 1, :] - b_row
        c["g_end"] = b_row[:, L - 1:L]
        c["log_in"] = jnp.where(causal, c["b_col"] + c["r_row"], NEG_INF)
        c["m_intra"] = jnp.max(c["log_in"], axis=1, keepdims=True)
    for c in chains:
        b, hs = c["b"], c["hs"]
        c["k"] = k_ref[b][:, hs] * (M_HEAD_DIM ** -0.5)
        c["caug"] = c_sc[b, c["h"]]
        c["vaug"] = jnp.concatenate([v_ref[b][:, hs].astype(BF16), ones_col], axis=1)
        qb = q_ref[b][:, hs].astype(BF16)
        c["qk"] = _dot_nt(qb, c["k"].astype(BF16))
        c["qc"] = _dot(qb, c["caug"].astype(BF16))
    for c in chains:
        log_prev = c["b_col"] + c["m_prev"]
        m_t = jnp.maximum(log_prev, c["m_intra"])
        sc = (c["qk"] * jnp.exp(c["log_in"] - m_t)).astype(BF16)
        nd = jnp.exp(log_prev - m_t) * c["qc"] + _dot(sc, c["vaug"])
        den = nd[:, M_HEAD_DIM:M_HEAD_DIM + 1]
        c["hh"] = nd[:, :M_HEAD_DIM] / jnp.maximum(jnp.abs(den), jnp.exp(-m_t))
    for c in chains:
        b, hs = c["b"], c["hs"]
        d = c["hh"] - jnp.mean(c["hh"], axis=-1, keepdims=True)
        y = d * lax.rsqrt(jnp.mean(d * d, axis=-1, keepdims=True) + EPS) * hn_ref[:, hs]
        h_out[b, :, hs] = (y * _sigmoid(o_ref[b][:, hs])).astype(h_out.dtype)
    for c in chains:
        b, h = c["b"], c["h"]
        lie = c["g_end"] + c["r_row"]
        m_end = jnp.maximum(c["g_end"] + c["m_prev"], jnp.max(lie, axis=1, keepdims=True))
        w_pe = jnp.exp(c["g_end"] + c["m_prev"] - m_end)
        ktw = (c["k"].T * jnp.exp(lie - m_end)).astype(BF16)
        c_new = w_pe * c["caug"] + _dot(ktw, c["vaug"])
        c_sc[b, h] = c_new
        c_out[b, h] = c_new
        m_tiles[b] = jnp.where(row8 == h, m_end, m_tiles[b])
    for b in range(NBB):
        m_sc[b] = m_tiles[b]
        m_out[b] = m_tiles[b]


def mlstm(p_m, gates_i, gates_f, b_i, b_f, head_norm, c0aug, m0, L):
    NB, R, _ = p_m.shape
    nc = R // L
    assert R % L == 0
    NBB = 2 if NB % 2 == 0 else 1
    blk = lambda j: pl.BlockSpec((NBB, L, M_WIDTH), lambda b, c, j=j: (b, c, j))
    state_spec = pl.BlockSpec((NBB, M_HEADS, M_HEAD_DIM, MLSTM_AUG), lambda b, c: (b, 0, 0, 0))
    m_spec = pl.BlockSpec((NBB, SUBLANE, LANE), lambda b, c: (b, 0, 0))
    gate_spec = pl.BlockSpec((NBB, SUBLANE, L), lambda b, c: (b, 0, c))
    kern = functools.partial(_mlstm_kernel, L=L, NBB=NBB)
    scratch = [pltpu.VMEM((NBB, M_HEADS, M_HEAD_DIM, MLSTM_AUG), F32), pltpu.VMEM((NBB, SUBLANE, LANE), F32)]
    return pl.pallas_call(
        kern,
        grid=(NB // NBB, nc),
        in_specs=[blk(0), blk(1), blk(2), blk(3), gate_spec, gate_spec,
                  pl.BlockSpec((SUBLANE, 1), lambda b, c: (0, 0)),
                  pl.BlockSpec((SUBLANE, 1), lambda b, c: (0, 0)),
                  pl.BlockSpec((1, M_WIDTH), lambda b, c: (0, 0)),
                  state_spec, m_spec],
        out_specs=[pl.BlockSpec((NBB, L, M_WIDTH), lambda b, c: (b, c, 0)), state_spec, m_spec],
        out_shape=[jax.ShapeDtypeStruct((NB, R, M_WIDTH), BF16),
                   jax.ShapeDtypeStruct((NB, M_HEADS, M_HEAD_DIM, MLSTM_AUG), F32),
                   jax.ShapeDtypeStruct((NB, SUBLANE, LANE), F32)],
        scratch_shapes=scratch,
        compiler_params=_cparams("parallel", "arbitrary"),
        name="mlstm",
    )(p_m, p_m, p_m, p_m, gates_i, gates_f, b_i, b_f, head_norm, c0aug, m0)


def _mlstm_short_kernel(q_ref, k_ref, v_ref, o_ref, gi_ref, gf_ref, bi_ref, bf_ref, hn_ref,
                        c0_ref, n0_ref, m0_ref, h_out, c_out, n_out, m_out, *, NBS, TS):
    L = SUBLANE
    t_idx = lax.broadcasted_iota(jnp.int32, (L, L), 0)
    s_idx = lax.broadcasted_iota(jnp.int32, (L, L), 1)
    causal = t_idx >= s_idx
    eye = t_idx == s_idx
    valid = lax.broadcasted_iota(jnp.int32, (1, L), 1) < TS
    row8 = lax.broadcasted_iota(jnp.int32, (SUBLANE, LANE), 0)
    gi = [gi_ref[b] + bi_ref[...] for b in range(NBS)]
    gf = [_log_sigmoid(gf_ref[b] + bf_ref[...]) for b in range(NBS)]
    m_tiles = [m0_ref[b] for b in range(NBS)]
    chains = [dict(b=b, h=h, hs=slice(h * M_HEAD_DIM, (h + 1) * M_HEAD_DIM))
              for b in range(NBS) for h in range(M_HEADS)]

    for c in chains:
        b, h = c["b"], c["h"]
        c["li"] = jnp.where(valid, gi[b][h:h + 1, 0:L], NEG_INF)
        lf = jnp.where(valid, gf[b][h:h + 1, 0:L], 0.0)
        c["m_prev"] = m_tiles[b][h:h + 1, 0:1]
        c["b_col"] = jnp.sum(jnp.where(causal, lf, 0.0), axis=1, keepdims=True)
    for c in chains:
        b_row = jnp.sum(jnp.where(eye, c["b_col"], 0.0), axis=0, keepdims=True)
        c["r_row"] = c["li"] - b_row
        c["g_end"] = b_row[:, L - 1:L]
        c["log_in"] = jnp.where(causal, c["b_col"] + c["r_row"], NEG_INF)
        c["m_intra"] = jnp.max(c["log_in"], axis=1, keepdims=True)
    for c in chains:
        b, hs = c["b"], c["hs"]
        c["q"] = q_ref[b][:, hs]
        c["k"] = k_ref[b][:, hs] * (M_HEAD_DIM ** -0.5)
        c["vb"] = v_ref[b][:, hs].astype(BF16)
        c["c0"] = c0_ref[0, b, c["h"]]
        c["n0"] = n0_ref[0, b, c["h"]:c["h"] + 1, :]
        qb = c["q"].astype(BF16)
        c["qk"] = _dot_nt(qb, c["k"].astype(BF16))
        c["qc"] = _dot(qb, c["c0"].astype(BF16))
        c["qn"] = jnp.sum(c["q"] * c["n0"], axis=1, keepdims=True)
    for c in chains:
        log_prev = c["b_col"] + c["m_prev"]
        m_t = jnp.maximum(log_prev, c["m_intra"])
        w_prev = jnp.exp(log_prev - m_t)
        sc = c["qk"] * jnp.exp(c["log_in"] - m_t)
        num = w_prev * c["qc"] + _dot(sc.astype(BF16), c["vb"])
        den = w_prev * c["qn"] + jnp.sum(sc, axis=1, keepdims=True)
        c["hh"] = num / jnp.maximum(jnp.abs(den), jnp.exp(-m_t))
    for c in chains:
        b, hs = c["b"], c["hs"]
        d = c["hh"] - jnp.mean(c["hh"], axis=-1, keepdims=True)
        y = d * lax.rsqrt(jnp.mean(d * d, axis=-1, keepdims=True) + EPS) * hn_ref[:, hs]
        h_out[b, :, hs] = y * _sigmoid(o_ref[b][:, hs])
    for c in chains:
        b, h = c["b"], c["h"]
        lie = c["g_end"] + c["r_row"]
        m_end = jnp.maximum(c["g_end"] + c["m_prev"], jnp.max(lie, axis=1, keepdims=True))
        w_pe = jnp.exp(c["g_end"] + c["m_prev"] - m_end)
        w_e = jnp.exp(lie - m_end)
        w_e_col = jnp.sum(jnp.where(eye, w_e, 0.0), axis=1, keepdims=True)
        kw = c["k"] * w_e_col
        c_out[b, h] = w_pe * c["c0"] + lax.dot_general(kw.astype(BF16), c["vb"], (((0,), (0,)), ((), ())),
                                                       preferred_element_type=F32)
        n_out[b, h:h + 1, :] = w_pe * c["n0"] + jnp.sum(kw, axis=0, keepdims=True)
        m_tiles[b] = jnp.where(row8 == h, m_end, m_tiles[b])
    for b in range(NBS):
        m_out[b] = m_tiles[b]


def mlstm_short(p_m, gates_i, gates_f, b_i, b_f, head_norm, c0, n0, m0, layer, NBS, TS):
    NB = p_m.shape[0]
    H, DH = M_HEADS, M_HEAD_DIM
    assert NB % NBS == 0 and TS <= SUBLANE
    blk = lambda j: pl.BlockSpec((NBS, SUBLANE, M_WIDTH), lambda i, j=j: (i, 0, j))
    tile = pl.BlockSpec((NBS, SUBLANE, LANE), lambda i: (i, 0, 0))
    kern = functools.partial(_mlstm_short_kernel, NBS=NBS, TS=TS)
    return pl.pallas_call(
        kern,
        grid=(NB // NBS,),
        in_specs=[blk(0), blk(1), blk(2), blk(3), tile, tile,
                  pl.BlockSpec((SUBLANE, 1), lambda i: (0, 0)),
                  pl.BlockSpec((SUBLANE, 1), lambda i: (0, 0)),
                  pl.BlockSpec((1, M_WIDTH), lambda i: (0, 0)),
                  pl.BlockSpec((1, NBS, H, DH, DH), lambda i: (layer, i, 0, 0, 0)),
                  pl.BlockSpec((1, NBS, H, DH), lambda i: (layer, i, 0, 0)),
                  tile],
        out_specs=[pl.BlockSpec((NBS, SUBLANE, M_WIDTH), lambda i: (i, 0, 0)),
                   pl.BlockSpec((NBS, H, DH, DH), lambda i: (i, 0, 0, 0)),
                   pl.BlockSpec((NBS, H, DH), lambda i: (i, 0, 0)),
                   tile],
        out_shape=[jax.ShapeDtypeStruct((NB, SUBLANE, M_WIDTH), F32),
                   jax.ShapeDtypeStruct((NB, H, DH, DH), F32),
                   jax.ShapeDtypeStruct((NB, H, DH), F32),
                   jax.ShapeDtypeStruct((NB, SUBLANE, LANE), F32)],
        compiler_params=_cparams("parallel"),
        name="mlstm_short",
    )(p_m, p_m, p_m, p_m, gates_i, gates_f, b_i, b_f, head_norm, c0, n0, m0)


def _rglru_kernel(x_ref, g_ref, conv0_ref, h0_ref, cw_ref, cb_ref, wa_ref, wx_ref, ba_ref, bx_ref,
                  lam_ref, y_out, hlast_out, conv_out, xpad_sc, hc_sc, *, TR, Bs):
    CB = (CONV_W - 1) * Bs
    X0 = -(-CB // SUBLANE) * SUBLANE

    @pl.when(pl.program_id(1) == 0)
    def _():
        xpad_sc[X0 - CB:X0, :] = conv0_ref[0]
        hc_sc[...] = h0_ref[0]

    xpad_sc[X0:X0 + TR, :] = x_ref[0]
    y = cb_ref[...]
    for j in range(CONV_W):
        y = y + xpad_sc[X0 - CB + j * Bs:X0 - CB + j * Bs + TR, :] * cw_ref[j:j + 1, :]
    tail = xpad_sc[X0 + TR - CB:X0 + TR, :]
    conv_out[0] = tail
    xpad_sc[X0 - CB:X0, :] = tail

    yb = y.astype(BF16)
    rs, is_ = [], []
    for n in range(R_BLOCKS):
        sl = slice(n * R_BLOCK_DIM, (n + 1) * R_BLOCK_DIM)
        rs.append(_dot(yb[:, sl], wa_ref[n]))
        is_.append(_dot(yb[:, sl], wx_ref[n]))
    r = _sigmoid(jnp.concatenate(rs, axis=1) + ba_ref[...])
    i = _sigmoid(jnp.concatenate(is_, axis=1) + bx_ref[...])
    log_a = -LRU_C * r * _softplus(-lam_ref[...])
    a_cum = jnp.exp(log_a)
    t = jnp.tanh(log_a)
    u_cum = jnp.sqrt(-2.0 * t / (1.0 - t)) * (i * y)

    def scan_rows(a, u, axis, stride):
        n = a.shape[axis]
        idx = lax.broadcasted_iota(jnp.int32, a.shape, axis)
        d = stride
        while d < n:
            keep = idx >= d
            u = jnp.where(keep, a * pltpu.roll(u, d, axis=axis) + u, u)
            a = jnp.where(keep, a * pltpu.roll(a, d, axis=axis), a)
            d *= 2
        return a, u

    hc = hc_sc[...]
    if Bs == 1:
        NG = TR // SUBLANE
        a3, u3 = scan_rows(a_cum.reshape(NG, SUBLANE, R_WIDTH), u_cum.reshape(NG, SUBLANE, R_WIDTH), 1, 1)
        a_tot, u_tot = scan_rows(a3[:, SUBLANE - 1, :], u3[:, SUBLANE - 1, :], 0, 1)
        h_end = u_tot + a_tot * hc
        g_idx = lax.broadcasted_iota(jnp.int32, (NG, R_WIDTH), 0)
        h_start = jnp.where(g_idx == 0, hc, pltpu.roll(h_end, 1, axis=0))
        h = (u3 + a3 * h_start[:, None, :]).reshape(TR, R_WIDTH)
    else:
        a_cum, u_cum = scan_rows(a_cum, u_cum, 0, Bs)
        h = u_cum + a_cum * jnp.concatenate([hc] * (TR // Bs), axis=0)
    y_out[0] = (h * _gelu_tanh(g_ref[0])).astype(y_out.dtype)
    h_last = h[TR - Bs:, :]
    hc_sc[...] = h_last
    hlast_out[0] = h_last


def rglru(p_r, conv0, h0, cw, cb, wa, wx, ba, bx, lam, TR, Bs, out_dtype):
    G, R, _ = p_r.shape
    CB = (CONV_W - 1) * Bs
    X0 = -(-CB // SUBLANE) * SUBLANE
    assert R % TR == 0 and TR % Bs == 0 and TR >= CB
    W = R_WIDTH
    const = lambda shape: pl.BlockSpec(shape, lambda g, r: (0,) * len(shape))
    kern = functools.partial(_rglru_kernel, TR=TR, Bs=Bs)
    return pl.pallas_call(
        kern,
        grid=(G, R // TR),
        in_specs=[pl.BlockSpec((1, TR, W), lambda g, r: (g, r, 0)),
                  pl.BlockSpec((1, TR, W), lambda g, r: (g, r, 1)),
                  pl.BlockSpec((1, CB, W), lambda g, r: (g, 0, 0)),
                  pl.BlockSpec((1, Bs, W), lambda g, r: (g, 0, 0)),
                  const((CONV_W, W)), const((1, W)),
                  const((R_BLOCKS, R_BLOCK_DIM, R_BLOCK_DIM)), const((R_BLOCKS, R_BLOCK_DIM, R_BLOCK_DIM)),
                  const((1, W)), const((1, W)), const((1, W))],
        out_specs=[pl.BlockSpec((1, TR, W), lambda g, r: (g, r, 0)),
                   pl.BlockSpec((1, Bs, W), lambda g, r: (g, 0, 0)),
                   pl.BlockSpec((1, CB, W), lambda g, r: (g, 0, 0))],
        out_shape=[jax.ShapeDtypeStruct((G, R, W), out_dtype),
                   jax.ShapeDtypeStruct((G, Bs, W), F32),
                   jax.ShapeDtypeStruct((G, CB, W), F32)],
        scratch_shapes=[pltpu.VMEM((X0 + TR, W), F32), pltpu.VMEM((Bs, W), F32)],
        compiler_params=_cparams("parallel", "arbitrary"),
        name="rglru",
    )(p_r, p_r, conv0, h0, cw, cb, wa, wx, ba, bx, lam)


def _swa_kernel(q_ref, kp_ref, kc_ref, vp_ref, vc_ref, bias_ref, sink_ref, o_ref, *, NBS, TQ, mask_first):
    GQ = S_HEADS // S_KV_HEADS
    lane = lax.broadcasted_iota(jnp.int32, (1, LANE), 1)
    lo = lane < S_HEAD_DIM

    def both_halves(x, j):
        xr = pltpu.roll(x, S_HEAD_DIM, axis=1)
        return (jnp.where(lo, x, xr) if j == 0 else jnp.where(lo, xr, x)).astype(BF16)

    s1_parts, s2_parts, v_parts = [], [], []
    for b in range(NBS):
        q = q_ref[b] * (S_HEAD_DIM ** -0.5)
        for j in range(S_KV_HEADS):
            stack = []
            for g in range(GQ):
                h = GQ * j + g
                qh = q[:, (h // 2) * LANE:(h // 2 + 1) * LANE]
                stack.append(jnp.where(lo if h % 2 == 0 else jnp.logical_not(lo), qh, 0.0))
            qs = jnp.concatenate(stack, axis=0).astype(BF16)
            s1_parts.append(_dot_nt(qs, both_halves(kp_ref[b], j)))
            s2_parts.append(_dot_nt(qs, both_halves(kc_ref[b], j)))
            v_parts.append((both_halves(vp_ref[b], j), both_halves(vc_ref[b], j)))
    s1 = jnp.concatenate(s1_parts, axis=0) + bias_ref[:, 0:WINDOW]
    s2 = jnp.concatenate(s2_parts, axis=0) + bias_ref[:, WINDOW:WINDOW + TQ]
    if mask_first:
        s1 = jnp.where(pl.program_id(1) == 0, NEG_INF, s1)
    sink = sink_ref[...]
    mx = jnp.maximum(jnp.maximum(jnp.max(s1, axis=1, keepdims=True), jnp.max(s2, axis=1, keepdims=True)), sink)
    p1 = jnp.exp(s1 - mx)
    p2 = jnp.exp(s2 - mx)
    inv = 1.0 / (jnp.sum(p1, axis=1, keepdims=True) + jnp.sum(p2, axis=1, keepdims=True) + jnp.exp(sink - mx))
    p1 = p1.astype(BF16)
    p2 = p2.astype(BF16)
    R = GQ * TQ
    for b in range(NBS):
        for j in range(S_KV_HEADS):
            n = b * S_KV_HEADS + j
            rows = slice(n * R, (n + 1) * R)
            vp, vc = v_parts[n]
            o = (_dot(p1[rows], vp) + _dot(p2[rows], vc)) * inv[rows]
            for pair in range(GQ // 2):
                even = o[(2 * pair) * TQ:(2 * pair + 1) * TQ]
                odd = o[(2 * pair + 1) * TQ:(2 * pair + 2) * TQ]
                blk = (GQ * j) // 2 + pair
                o_ref[b, :, blk * LANE:(blk + 1) * LANE] = jnp.where(lo, even, odd).astype(o_ref.dtype)


def swa(arrs, maps, shapes, bias_rows, sink_rows, grid, NBS, TQ, mask_first, out_dtype):
    kern = functools.partial(_swa_kernel, NBS=NBS, TQ=TQ, mask_first=mask_first)
    in_specs = [pl.BlockSpec(s, m) for s, m in zip(shapes, maps)]
    in_specs += [pl.BlockSpec(bias_rows.shape, lambda b, n: (0, 0)),
                 pl.BlockSpec(sink_rows.shape, lambda b, n: (0, 0))]
    return pl.pallas_call(
        kern,
        grid=grid,
        in_specs=in_specs,
        out_specs=pl.BlockSpec((NBS, TQ, S_WIDTH), lambda b, n: (b, n, 0)),
        out_shape=jax.ShapeDtypeStruct((grid[0] * NBS, grid[1] * TQ, S_WIDTH), out_dtype),
        compiler_params=_cparams("parallel", "arbitrary"),
        name="swa",
    )(*arrs, bias_rows, sink_rows)


def t5_bucket(dist):
    max_exact = N_BUCKETS // 2
    d = jnp.maximum(dist, 0)
    large = max_exact + (jnp.log(jnp.maximum(d, 1).astype(F32) / max_exact)
                         / math.log(MAX_DISTANCE / max_exact) * (N_BUCKETS - max_exact)).astype(jnp.int32)
    return jnp.where(d < max_exact, d, jnp.minimum(large, N_BUCKETS - 1))


def swa_bias_rows(rel_bias, TQ, NBS):
    qi = jnp.arange(TQ)[:, None]
    kj = jnp.arange(WINDOW + TQ)[None, :]
    dist = qi + WINDOW - kj
    onehot = (t5_bucket(dist)[..., None] == jnp.arange(N_BUCKETS)).astype(F32)
    b = jnp.einsum("qkn,nh->hqk", onehot, rel_bias.astype(F32), precision=lax.Precision.HIGHEST)
    b = jnp.where((dist >= 0) & (dist <= WINDOW), b, NEG_INF).reshape(S_HEADS * TQ, WINDOW + TQ)
    return jnp.tile(b, (NBS, 1))


def swa_sink_rows(sinks, TQ, NBS):
    return jnp.tile(jnp.repeat(sinks.astype(F32), TQ), NBS).reshape(-1, 1)


def _merge_kernel(hm, hr, hs, gm, gr, gs, x, wm, wr, ws, wo, out):
    def branch(h_ref, g_ref, w_ref):
        return _sigmoid(g_ref[...].astype(F32)) * _dot(h_ref[...].astype(BF16), w_ref[...])
    merged = branch(hm, gm, wm) + branch(hr, gr, wr) + branch(hs, gs, ws)
    out[...] = x[...] + _dot(merged.astype(BF16), wo[...])


def merge_out(hm, hr, hs, p_g, x, wm, wr, ws, wo):
    T, D = x.shape
    tm = _row_tile(T)
    hspec = lambda w: pl.BlockSpec((tm, w), lambda i: (i, 0))
    gspec = lambda j: pl.BlockSpec((tm, D), lambda i, j=j: (i, j))
    wspec = lambda a: pl.BlockSpec(a.shape, lambda i: (0, 0))
    return pl.pallas_call(
        _merge_kernel,
        grid=(T // tm,),
        in_specs=[hspec(M_WIDTH), hspec(R_WIDTH), hspec(S_WIDTH), gspec(0), gspec(1), gspec(2),
                  hspec(D), wspec(wm), wspec(wr), wspec(ws), wspec(wo)],
        out_specs=hspec(D),
        out_shape=jax.ShapeDtypeStruct((T, D), F32),
        compiler_params=_cparams("parallel"),
        name="merge_out",
    )(hm, hr, hs, p_g, p_g, p_g, x, wm, wr, ws, wo)


def _xattn_kernel(x_ref, g_ref, wq_ref, wo_ref, mk_ref, mv_ref, out_ref, *, nb, tq, interleaved):
    NH = N_MEM_HEADS
    x = x_ref[...].reshape(nb * tq, D_MODEL)
    q = _dot(_rms(x, g_ref[...]).astype(BF16), wq_ref[...])
    head = lambda a, h: a[:, h * X_HEAD_DIM:(h + 1) * X_HEAD_DIM]
    parts = []
    for b in range(nb):
        qb = q[b * tq:(b + 1) * tq]
        if interleaved:
            qs = jnp.concatenate([head(qb, h) for h in range(NH)], axis=0).astype(BF16)
            parts.append(_dot_nt(qs, mk_ref[0, b].astype(BF16)))
        else:
            parts += [_dot_nt(head(qb, h).astype(BF16), head(mk_ref[0, b], h).astype(BF16)) for h in range(NH)]
    s = jnp.concatenate(parts, axis=0) * (X_HEAD_DIM ** -0.5)
    if interleaved:
        row_h = (lax.broadcasted_iota(jnp.int32, s.shape, 0) // tq) % NH
        s = jnp.where(lax.broadcasted_iota(jnp.int32, s.shape, 1) % NH == row_h, s, NEG_INF)
    p = jnp.exp(s - jnp.max(s, axis=1, keepdims=True))
    inv = 1.0 / jnp.sum(p, axis=1, keepdims=True)
    rows = []
    for b in range(nb):
        if interleaved:
            r = slice(b * NH * tq, (b + 1) * NH * tq)
            o = _dot(p[r].astype(BF16), mv_ref[0, b].astype(BF16)) * inv[r]
            heads = [o[h * tq:(h + 1) * tq] for h in range(NH)]
        else:
            heads = []
            for h in range(NH):
                r = slice((b * NH + h) * tq, (b * NH + h + 1) * tq)
                heads.append(_dot(p[r].astype(BF16), head(mv_ref[0, b], h).astype(BF16)) * inv[r])
        rows.append(jnp.concatenate(heads, axis=1))
    o_all = rows[0] if nb == 1 else jnp.concatenate(rows, axis=0)
    y = x + _dot(o_all.astype(BF16), wo_ref[...])
    out_ref[...] = y.reshape(nb, tq, D_MODEL)


def xattn(x, g, wq, wo, mem_k, mem_v, layer, nb, tq, interleaved):
    B, R, D = x.shape
    mem_blk = (1, nb) + mem_k.shape[2:]
    kern = functools.partial(_xattn_kernel, nb=nb, tq=tq, interleaved=interleaved)
    return pl.pallas_call(
        kern,
        grid=(B // nb, R // tq),
        in_specs=[pl.BlockSpec((nb, tq, D), lambda b, r: (b, r, 0)),
                  pl.BlockSpec((1, D), lambda b, r: (0, 0)),
                  pl.BlockSpec(wq.shape, lambda b, r: (0, 0)),
                  pl.BlockSpec(wo.shape, lambda b, r: (0, 0)),
                  pl.BlockSpec(mem_blk, lambda b, r: (layer, b, 0, 0)),
                  pl.BlockSpec(mem_blk, lambda b, r: (layer, b, 0, 0))],
        out_specs=pl.BlockSpec((nb, tq, D), lambda b, r: (b, r, 0)),
        out_shape=jax.ShapeDtypeStruct((B, R, D), F32),
        compiler_params=_cparams("parallel", "arbitrary"),
        name="xattn",
    )(x, g, wq, wo, mem_k, mem_v)


def _router_kernel(x_ref, g_ref, wr_ref, route_out, counts_out, cnt_sc):
    @pl.when(pl.program_id(0) == 0)
    def _():
        cnt_sc[...] = jnp.zeros_like(cnt_sc)

    xn = _rms(x_ref[...], g_ref[...])
    logits = _dot(xn.astype(BF16), wr_ref[...])
    tm = logits.shape[0]
    lane_i = lax.broadcasted_iota(jnp.int32, (tm, LANE), 1)
    lane = lane_i.astype(F32)
    lane_group = (lane_i // EXPERTS_PER_GROUP).astype(F32)

    def top1(v):
        mx = jnp.max(v, axis=1, keepdims=True)
        return mx, jnp.min(jnp.where(v == mx, lane, float(LANE)), axis=1, keepdims=True)

    gl = jnp.where(lane_i < N_GROUPS, logits[:, :LANE], NEG_INF)
    g_max, g_idx = top1(gl)
    g_w = 1.0 / jnp.sum(jnp.exp(gl - g_max), axis=1, keepdims=True)
    el = jnp.where(lane_group == g_idx, logits[:, LANE:], NEG_INF)
    e1, i1 = top1(el)
    e2, i2 = top1(jnp.where(lane == i1, NEG_INF, el))
    t = jnp.exp(e2 - e1)
    p1 = 1.0 / (1.0 + t)

    oh1 = (lane == i1).astype(F32)
    oh2 = (lane == i2).astype(F32)
    oh = oh1 + oh2
    r_idx = lax.broadcasted_iota(jnp.int32, (tm, tm), 0)
    c_idx = lax.broadcasted_iota(jnp.int32, (tm, tm), 1)
    before = (r_idx > c_idx).astype(BF16)
    base = cnt_sc[0:1, :] + _dot(before, oh.astype(BF16))
    rank1 = jnp.sum(oh1 * base, axis=1, keepdims=True)
    rank2 = jnp.sum(oh2 * base, axis=1, keepdims=True)
    cnt = cnt_sc[...] + jnp.sum(oh, axis=0, keepdims=True)
    cnt_sc[...] = cnt
    counts_out[...] = cnt

    vals = (i1, i2, p1 * g_w, t * p1 * g_w, rank1, rank2)
    route = jnp.zeros((tm, LANE), F32)
    for n, v in enumerate(vals):
        route = jnp.where(lane_i == n, v, route)
    route_out[...] = route


def router(x, g, w_router):
    T, D = x.shape
    tm = _row_tile(T)
    return pl.pallas_call(
        _router_kernel,
        grid=(T // tm,),
        in_specs=[pl.BlockSpec((tm, D), lambda i: (i, 0)),
                  pl.BlockSpec((1, D), lambda i: (0, 0)),
                  pl.BlockSpec(w_router.shape, lambda i: (0, 0))],
        out_specs=[pl.BlockSpec((tm, LANE), lambda i: (i, 0)), pl.BlockSpec((SUBLANE, LANE), lambda i: (0, 0))],
        out_shape=[jax.ShapeDtypeStruct((T, LANE), F32), jax.ShapeDtypeStruct((SUBLANE, LANE), F32)],
        scratch_shapes=[pltpu.VMEM((SUBLANE, LANE), F32)],
        compiler_params=_cparams("arbitrary"),
        name="router",
    )(x, g, w_router)


def moe_plan(routes, counts, TB, tms):
    experts = jnp.arange(N_EXPERTS, dtype=jnp.int32)
    counts = [c[0, :N_EXPERTS].astype(jnp.int32) for c in counts]
    total = sum(counts)
    padded = (total + TB - 1) // TB * TB
    pad_ends = jnp.cumsum(padded)
    base = pad_ends - padded
    dests = []
    for route, cnt, tm in zip(routes, counts, tms):
        e_idx = route[:, 0:TOP_K].astype(jnp.int32)
        rank = route[:, 4:4 + TOP_K].astype(jnp.int32)
        dest = jnp.sum(jnp.where(e_idx[..., None] == experts, base, 0), axis=-1) + rank
        dests.append(dest.reshape(route.shape[0] // tm, 1, TOP_K * tm))
        base = base + cnt
    n_tokens = sum(r.shape[0] for r in routes)
    n_blocks = -(-(n_tokens * TOP_K + N_EXPERTS * (TB - 1)) // TB)
    block_start = jnp.arange(n_blocks, dtype=jnp.int32) * TB
    block_e = jnp.minimum(jnp.sum(pad_ends[None, :] <= block_start[:, None], axis=1), N_EXPERTS - 1).astype(jnp.int32)
    end_valid = jnp.sum(jnp.where(block_e[:, None] == experts, pad_ends - padded + total, 0), axis=1)
    n_valid = jnp.clip(end_valid - block_start, 0, TB).astype(jnp.int32)
    return block_e, n_valid, dests


DMA_UNROLL = 8


def _for_row_chunks(n_rows, fn):
    def body(c, carry):
        for u in range(DMA_UNROLL):
            fn(c * DMA_UNROLL + u)
        return carry
    lax.fori_loop(0, n_rows // DMA_UNROLL, body, 0)


def _dispatch_kernel(dst_ref, x_ref, g_ref, init_hbm, xs_hbm, buf, sem, *, tm, n_tiles):
    del init_hbm
    i = pl.program_id(0)
    slot = i % 2

    def copy(row, s, row_dst):
        return pltpu.make_async_copy(buf.at[s, pl.ds(row, 1)], xs_hbm.at[pl.ds(row_dst, 1)], sem.at[s])

    def wait_slot(s):
        def wait_row(r):
            for _ in range(TOP_K):
                copy(r, s, 0).wait()
        _for_row_chunks(tm, wait_row)

    @pl.when(i >= 2)
    def _():
        wait_slot(slot)

    buf[slot] = _rms(x_ref[...], g_ref[...])

    def start_row(r):
        for k in range(TOP_K):
            copy(r, slot, dst_ref[0, 0, TOP_K * r + k]).start(priority=k % 2)
    _for_row_chunks(tm, start_row)

    @pl.when(i == n_tiles - 1)
    def _():
        wait_slot(slot)
        if n_tiles >= 2:
            wait_slot(1 - slot)


def moe_dispatch(x, g, dest, xs_init, tm):
    T, D = x.shape
    n_tiles = T // tm
    assert tm % DMA_UNROLL == 0
    kern = functools.partial(_dispatch_kernel, tm=tm, n_tiles=n_tiles)
    return pl.pallas_call(
        kern,
        grid=(n_tiles,),
        in_specs=[pl.BlockSpec((1, 1, TOP_K * tm), lambda i: (i, 0, 0), memory_space=pltpu.SMEM),
                  pl.BlockSpec((tm, D), lambda i: (i, 0)),
                  pl.BlockSpec((1, D), lambda i: (0, 0)),
                  pl.BlockSpec(memory_space=pl.ANY)],
        out_specs=pl.BlockSpec(memory_space=pl.ANY),
        out_shape=jax.ShapeDtypeStruct(xs_init.shape, F32),
        input_output_aliases={3: 0},
        scratch_shapes=[pltpu.VMEM((2, tm, D), F32), pltpu.SemaphoreType.DMA((2,))],
        compiler_params=_cparams("arbitrary"),
        name="moe_dispatch",
    )(dest, x, g, xs_init)


def _ffn_kernel(be_ref, nv_ref, xs_ref, wg_ref, wu_ref, wd_ref, ys_ref, wg_bf, wu_bf, wd_bf):
    i = pl.program_id(0)
    nv = nv_ref[i]

    @pl.when((i == 0) | (be_ref[i] != be_ref[jnp.maximum(i - 1, 0)]))
    def _():
        wg_bf[...] = wg_ref[0, 0].astype(BF16)
        wu_bf[...] = wu_ref[0, 0].astype(BF16)
        wd_bf[...] = wd_ref[0, 0].astype(BF16)

    @pl.when(nv > 0)
    def _():
        row = lax.broadcasted_iota(jnp.int32, (xs_ref.shape[0], 1), 0)
        x = jnp.where(row < nv, xs_ref[...], 0.0).astype(BF16)
        g = _dot(x, wg_bf[...])
        u = _dot(x, wu_bf[...])
        hid = (g * _sigmoid(g)) * u
        ys_ref[...] = _dot(hid.astype(BF16), wd_bf[...])

    @pl.when(nv == 0)
    def _():
        ys_ref[...] = jnp.zeros_like(ys_ref)


def expert_ffn(xs, block_e, n_valid, w_gate, w_up, w_down, layer, TB):
    P, D = xs.shape
    n_blocks = P // TB
    wspec = lambda a, b: pl.BlockSpec((1, 1, a, b), lambda i, be, nv: (layer, be[i], 0, 0))
    grid_spec = pltpu.PrefetchScalarGridSpec(
        num_scalar_prefetch=2,
        grid=(n_blocks,),
        in_specs=[pl.BlockSpec((TB, D), lambda i, be, nv: (i, 0)),
                  wspec(D, D_EXPERT), wspec(D, D_EXPERT), wspec(D_EXPERT, D)],
        out_specs=pl.BlockSpec((TB, D), lambda i, be, nv: (i, 0)),
        scratch_shapes=[pltpu.VMEM((D, D_EXPERT), BF16), pltpu.VMEM((D, D_EXPERT), BF16),
                        pltpu.VMEM((D_EXPERT, D), BF16)],
    )
    return pl.pallas_call(
        _ffn_kernel,
        grid_spec=grid_spec,
        out_shape=jax.ShapeDtypeStruct((P, D), F32),
        compiler_params=_cparams("arbitrary"),
        name="expert_ffn",
    )(block_e, n_valid, xs, w_gate, w_up, w_down)


def _combine_kernel(cur_ref, nxt_ref, x_ref, route_ref, gf_ref, ys_hbm, out_ref, ybuf, sem, *,
                    tm, n_tiles, final_norm):
    i = pl.program_id(0)
    slot = i % 2

    def copy(row_src, s, k, row):
        return pltpu.make_async_copy(ys_hbm.at[pl.ds(row_src, 1)], ybuf.at[s, k, pl.ds(row, 1)], sem.at[s])

    def start_tile(idx_ref, s):
        def start_row(r):
            for k in range(TOP_K):
                copy(idx_ref[0, 0, TOP_K * r + k], s, k, r).start(priority=k % 2)
        _for_row_chunks(tm, start_row)

    @pl.when(i == 0)
    def _():
        start_tile(cur_ref, 0)

    @pl.when(i + 1 < n_tiles)
    def _():
        start_tile(nxt_ref, 1 - slot)

    def wait_row(r):
        for k in range(TOP_K):
            copy(0, slot, k, r).wait()
    _for_row_chunks(tm, wait_row)

    route = route_ref[...]
    lane_i = lax.broadcasted_iota(jnp.int32, route.shape, 1)
    g0 = jnp.sum(jnp.where(lane_i == 2, route, 0.0), axis=1, keepdims=True)
    g1 = jnp.sum(jnp.where(lane_i == 3, route, 0.0), axis=1, keepdims=True)
    y = x_ref[...] + (g0 * ybuf[slot, 0] + g1 * ybuf[slot, 1])
    out_ref[...] = _rms(y, gf_ref[...]) if final_norm else y


def moe_combine(x, ys, dest, route, g_final, tm, final_norm):
    T, D = x.shape
    n_tiles = T // tm
    assert tm % DMA_UNROLL == 0
    idx_blk = lambda f: pl.BlockSpec((1, 1, TOP_K * tm), f, memory_space=pltpu.SMEM)
    kern = functools.partial(_combine_kernel, tm=tm, n_tiles=n_tiles, final_norm=final_norm)
    return pl.pallas_call(
        kern,
        grid=(n_tiles,),
        in_specs=[idx_blk(lambda i: (i, 0, 0)),
                  idx_blk(lambda i: (jnp.minimum(i + 1, n_tiles - 1), 0, 0)),
                  pl.BlockSpec((tm, D), lambda i: (i, 0)),
                  pl.BlockSpec((tm, LANE), lambda i: (i, 0)),
                  pl.BlockSpec((1, D), lambda i: (0, 0)),
                  pl.BlockSpec(memory_space=pl.ANY)],
        out_specs=pl.BlockSpec((tm, D), lambda i: (i, 0)),
        out_shape=jax.ShapeDtypeStruct((T, D), F32),
        scratch_shapes=[pltpu.VMEM((2, TOP_K, tm, D), F32), pltpu.SemaphoreType.DMA((2,))],
        compiler_params=_cparams("arbitrary"),
        name="moe_combine",
    )(dest, dest, x, route, g_final, ys)


IN_SPLITS = (4 * M_WIDTH, LANE, 2 * R_WIDTH, S_WIDTH + 2 * S_KV_WIDTH, 3 * D_MODEL)
IN_DTYPES = (F32, F32, F32, F32, BF16)


def _prep_w_in(w):
    c0 = 4 * M_WIDTH
    c1 = c0 + 2 * M_HEADS
    gates = jnp.pad(w[:, c0:c1], ((0, 0), (0, LANE - 2 * M_HEADS)))
    return jnp.concatenate([w[:, :c0], gates, w[:, c1:]], axis=1).astype(BF16)


def _prep_router(w_rg, w_re):
    pad = lambda w: jnp.pad(w, ((0, 0), (0, LANE - w.shape[1])))
    return jnp.concatenate([pad(w_rg), pad(w_re)], axis=1).astype(BF16)


def _row(v):
    return v.reshape(1, -1).astype(F32)


def _pad_rows(a, n):
    return jnp.pad(a, ((0, 0), (0, n - a.shape[1])) + ((0, 0),) * (a.ndim - 2))


def _moe_block(T):
    return 512 if T >= 4096 else 128


def moe_layer(xs_in, slots, g_ffn, w_router, w_gate, w_up, w_down, layer, g_final, final_norm):
    D = xs_in[0].shape[1]
    TB = _moe_block(sum(x.shape[0] for x in xs_in))
    tms = [_row_tile(x.shape[0]) for x in xs_in]
    routed = [router(x, g_ffn, w_router) for x in xs_in]
    routes, counts = [r[0] for r in routed], [r[1] for r in routed]
    block_e, n_valid, dests = moe_plan(routes, counts, TB, tms)
    buf = jnp.zeros((block_e.shape[0] * TB, D), F32) if slots is None else slots
    for x, dest, tm in zip(xs_in, dests, tms):
        buf = moe_dispatch(x, g_ffn, dest, buf, tm)
    ys = expert_ffn(buf, block_e, n_valid, w_gate, w_up, w_down, layer, TB)
    outs = [moe_combine(x, ys, dest, route, g_final, tm, final_norm)
            for x, dest, route, tm in zip(xs_in, dests, routes, tms)]
    return outs, ys


def kernel(x_prompt, x_sample, mem_prompt, cache_mem_k, cache_mem_v, cache_swa_k, cache_swa_v, state_mlstm_C, state_mlstm_n, state_mlstm_m, state_rglru_h, state_rglru_conv, norm_mix, w_in, m_igate_b, m_fgate_b, m_head_norm, r_conv_w, r_conv_b, r_gate_a_w, r_gate_a_b, r_gate_x_w, r_gate_x_b, r_lambda, swa_sinks, rel_bias, w_branch_m, w_branch_r, w_branch_s, w_out, norm_mem, xq_w, xk_w, xv_w, xo_w, norm_ffn, router_group_w, router_expert_w, moe_w_gate, moe_w_up, moe_w_down, norm_final):
    B, S, D = x_prompt.shape
    NS, TS, _ = x_sample.shape
    depth = w_in.shape[0]
    n_mem = mem_prompt.shape[1]
    Tp, Ts = B * S, NS * TS
    H = M_HEADS
    assert S % MLSTM_CHUNK == 0 and S % WINDOW == 0 and TS <= SUBLANE
    assert NS % XATTN_SHORT_BATCH == 0 and NS % SWA_SHORT_BATCH == 0 and NS % MLSTM_SHORT_BATCH == 0

    xp = x_prompt.reshape(Tp, D)
    xs = x_sample.reshape(Ts, D)
    mem2d = mem_prompt.reshape(B * n_mem, D)
    cache_k_rows = cache_mem_k.reshape(depth, NS, n_mem * N_MEM_HEADS, X_HEAD_DIM)
    cache_v_rows = cache_mem_v.reshape(depth, NS, n_mem * N_MEM_HEADS, X_HEAD_DIM)
    NBP = 2 if B % 2 == 0 else 1
    bias_p = swa_bias_rows(rel_bias, WINDOW, NBP)
    bias_s = swa_bias_rows(rel_bias, SWA_SHORT_TQ, SWA_SHORT_BATCH)
    outs_p = {k: [] for k in ("mem_k", "mem_v", "swa_k", "swa_v", "C", "n", "m", "h", "conv")}
    outs_s = {k: [] for k in ("swa_k", "swa_v", "C", "n", "m", "h", "conv")}

    slots = None
    for l in range(depth):
        w_in_b = _prep_w_in(w_in[l])
        g_mix = _row(norm_mix[l])
        b_i = jnp.pad(m_igate_b[l], (0, SUBLANE - H)).reshape(SUBLANE, 1)
        b_f = jnp.pad(m_fgate_b[l], (0, SUBLANE - H)).reshape(SUBLANE, 1)
        hn = _row(m_head_norm[l])
        r_args = (r_conv_w[l], _row(r_conv_b[l]), r_gate_a_w[l].astype(BF16), r_gate_x_w[l].astype(BF16),
                  _row(r_gate_a_b[l]), _row(r_gate_x_b[l]), _row(r_lambda[l]))
        sink_p = swa_sink_rows(swa_sinks[l], WINDOW, NBP)
        sink_s = swa_sink_rows(swa_sinks[l], SWA_SHORT_TQ, SWA_SHORT_BATCH)
        merge_w = (w_branch_m[l].astype(BF16), w_branch_r[l].astype(BF16), w_branch_s[l].astype(BF16),
                   w_out[l].astype(BF16))
        wq_b, wo_b = xq_w[l].astype(BF16), xo_w[l].astype(BF16)
        w_router = _prep_router(router_group_w[l], router_expert_w[l])
        last = l == depth - 1
        moe_args = (_row(norm_ffn[l]), w_router, moe_w_gate, moe_w_up, moe_w_down, l, _row(norm_final), last)

        w_kv = jnp.concatenate([xk_w[l], xv_w[l]], axis=1).astype(BF16)
        mk, mv = norm_matmul(mem2d, g_mix, w_kv, (X_WIDTH, X_WIDTH), normalize=False)
        outs_p["mem_k"].append(mk.reshape(B, n_mem, N_MEM_HEADS, X_HEAD_DIM))
        outs_p["mem_v"].append(mv.reshape(B, n_mem, N_MEM_HEADS, X_HEAD_DIM))
        mk, mv = mk.reshape(1, B, n_mem, X_WIDTH), mv.reshape(1, B, n_mem, X_WIDTH)

        p_m, p_if, p_r, p_s, p_g = norm_matmul(xp, g_mix, w_in_b, IN_SPLITS, out_dtypes=IN_DTYPES)
        gts = jnp.swapaxes(p_if[:, :2 * H].reshape(B, S, 2 * H), 1, 2)
        gi = jnp.pad(gts[:, :H], ((0, 0), (0, SUBLANE - H), (0, 0)))
        gf = jnp.pad(gts[:, H:], ((0, 0), (0, SUBLANE - H), (0, 0)))
        h_m, caug, m_o = mlstm(p_m.reshape(B, S, 4 * M_WIDTH), gi, gf, b_i, b_f, hn,
                               jnp.zeros((B, H, M_HEAD_DIM, MLSTM_AUG), F32),
                               jnp.zeros((B, SUBLANE, LANE), F32), L=MLSTM_CHUNK)
        outs_p["C"].append(caug[..., :M_HEAD_DIM])
        outs_p["n"].append(caug[..., M_HEAD_DIM])
        outs_p["m"].append(m_o[:, :H, 0])

        h_r, h_last, conv_last = rglru(p_r.reshape(B, S, 2 * R_WIDTH),
                                       jnp.zeros((B, CONV_W - 1, R_WIDTH), F32),
                                       jnp.zeros((B, 1, R_WIDTH), F32), *r_args,
                                       TR=_row_tile(S), Bs=1, out_dtype=BF16)
        outs_p["h"].append(h_last[:, 0])
        outs_p["conv"].append(conv_last)

        p_s3 = p_s.reshape(B, S, S_WIDTH + 2 * S_KV_WIDTH)
        kcol, vcol = S_WIDTH // S_KV_WIDTH, S_WIDTH // S_KV_WIDTH + 1
        prev = lambda col: (lambda b, n: (b, jnp.maximum(n - 1, 0), col))
        cur = lambda col: (lambda b, n: (b, n, col))
        kv_blk = (NBP, WINDOW, S_KV_WIDTH)
        h_s = swa((p_s3,) * 5, (cur(0), prev(kcol), cur(kcol), prev(vcol), cur(vcol)),
                  ((NBP, WINDOW, S_WIDTH), kv_blk, kv_blk, kv_blk, kv_blk),
                  bias_p, sink_p, grid=(B // NBP, S // WINDOW), NBS=NBP, TQ=WINDOW, mask_first=True,
                  out_dtype=BF16)
        outs_p["swa_k"].append(p_s3[:, S - WINDOW:, S_WIDTH:S_WIDTH + S_KV_WIDTH]
                               .reshape(B, WINDOW, S_KV_HEADS, S_HEAD_DIM))
        outs_p["swa_v"].append(p_s3[:, S - WINDOW:, S_WIDTH + S_KV_WIDTH:]
                               .reshape(B, WINDOW, S_KV_HEADS, S_HEAD_DIM))

        xp = merge_out(h_m.reshape(Tp, M_WIDTH), h_r.reshape(Tp, R_WIDTH), h_s.reshape(Tp, S_WIDTH),
                       p_g, xp, *merge_w)
        xp = xattn(xp.reshape(B, S, D), _row(norm_mem[l]), wq_b, wo_b, mk, mv, layer=0,
                   nb=1, tq=512 if S % 512 == 0 else _row_tile(S), interleaved=False).reshape(Tp, D)

        p_m, p_if, p_r, p_s, p_g = norm_matmul(xs, g_mix, w_in_b, IN_SPLITS, out_dtypes=IN_DTYPES)
        gts = jnp.swapaxes(p_if[:, :2 * H].reshape(NS, TS, 2 * H), 1, 2)
        gts = jnp.pad(gts, ((0, 0), (0, 0), (0, LANE - TS)))
        gi = jnp.pad(gts[:, :H], ((0, 0), (0, SUBLANE - H), (0, 0)))
        gf = jnp.pad(gts[:, H:], ((0, 0), (0, SUBLANE - H), (0, 0)))
        m0 = jnp.broadcast_to(jnp.pad(state_mlstm_m[l], ((0, 0), (0, SUBLANE - H)))[:, :, None],
                              (NS, SUBLANE, LANE))
        h_m, c_s, n_s, m_o = mlstm_short(_pad_rows(p_m.reshape(NS, TS, 4 * M_WIDTH), SUBLANE), gi, gf, b_i, b_f,
                                         hn, state_mlstm_C, state_mlstm_n, m0, layer=l,
                                         NBS=MLSTM_SHORT_BATCH, TS=TS)
        h_m = h_m[:, :TS].reshape(Ts, M_WIDTH)
        outs_s["C"].append(c_s)
        outs_s["n"].append(n_s)
        outs_s["m"].append(m_o[:, :H, 0])

        tmaj = lambda a: jnp.swapaxes(a, 0, 1).reshape(1, a.shape[0] * a.shape[1], a.shape[2])
        h_r, h_last, conv_last = rglru(tmaj(p_r.reshape(NS, TS, 2 * R_WIDTH)), tmaj(state_rglru_conv[l]),
                                       state_rglru_h[l].reshape(1, NS, R_WIDTH), *r_args,
                                       TR=TS * NS, Bs=NS, out_dtype=F32)
        h_r = jnp.swapaxes(h_r.reshape(TS, NS, R_WIDTH), 0, 1).reshape(Ts, R_WIDTH)
        outs_s["h"].append(h_last[0])
        outs_s["conv"].append(jnp.swapaxes(conv_last.reshape(CONV_W - 1, NS, R_WIDTH), 0, 1))

        TQ, NBS = SWA_SHORT_TQ, SWA_SHORT_BATCH
        p_s3 = p_s.reshape(NS, TS, S_WIDTH + 2 * S_KV_WIDTH)
        q_s = _pad_rows(p_s3[:, :, :S_WIDTH], TQ)
        k_new = p_s3[:, :, S_WIDTH:S_WIDTH + S_KV_WIDTH]
        v_new = p_s3[:, :, S_WIDTH + S_KV_WIDTH:]
        k_cache = cache_swa_k[l].reshape(NS, WINDOW, S_KV_WIDTH)
        v_cache = cache_swa_v[l].reshape(NS, WINDOW, S_KV_WIDTH)
        full = lambda b, n: (b, 0, 0)
        h_s = swa((q_s, k_cache, _pad_rows(k_new, TQ), v_cache, _pad_rows(v_new, TQ)), (full,) * 5,
                  ((NBS, TQ, S_WIDTH), (NBS, WINDOW, S_KV_WIDTH), (NBS, TQ, S_KV_WIDTH),
                   (NBS, WINDOW, S_KV_WIDTH), (NBS, TQ, S_KV_WIDTH)),
                  bias_s, sink_s, grid=(NS // NBS, 1), NBS=NBS, TQ=TQ, mask_first=False, out_dtype=F32)
        h_s = h_s[:, :TS].reshape(Ts, S_WIDTH)
        outs_s["swa_k"].append(jnp.concatenate([k_cache[:, TS:], k_new], axis=1)
                               .reshape(NS, WINDOW, S_KV_HEADS, S_HEAD_DIM))
        outs_s["swa_v"].append(jnp.concatenate([v_cache[:, TS:], v_new], axis=1)
                               .reshape(NS, WINDOW, S_KV_HEADS, S_HEAD_DIM))

        xs = merge_out(h_m, h_r, h_s, p_g, xs, *merge_w)
        xs8 = xattn(_pad_rows(xs.reshape(NS, TS, D), SUBLANE), _row(norm_mem[l]), wq_b, wo_b,
                    cache_k_rows, cache_v_rows, layer=l, nb=XATTN_SHORT_BATCH, tq=SUBLANE, interleaved=True)

        (xp, xs), slots = moe_layer([xp, xs8[:, :TS].reshape(Ts, D)], slots, *moe_args)

    st = lambda d, k: jnp.stack(d[k])
    return (xp.reshape(B, S, D), xs.reshape(NS, TS, D), st(outs_p, "mem_k"), st(outs_p, "mem_v"),
            st(outs_p, "swa_k"), st(outs_p, "swa_v"), st(outs_p, "C"), st(outs_p, "n"), st(outs_p, "m"),
            st(outs_p, "h"), st(outs_p, "conv"),
            st(outs_s, "swa_k"), st(outs_s, "swa_v"), st(outs_s, "C"), st(outs_s, "n"), st(outs_s, "m"),
            st(outs_s, "h"), st(outs_s, "conv"))
```

```python
import functools
import math

import jax
import jax.numpy as jnp
from jax import lax
from jax.experimental import pallas as pl
from jax.experimental.pallas import tpu as pltpu

F32 = jnp.float32
BF16 = jnp.bfloat16
NEG_INF = float("-inf")

LANE = 128
SUBLANE = 8
BF16_SUBLANE = 16
VMEM_LIMIT_BYTES = 56 * 1024 * 1024

D_MODEL = 1024
M_HEADS = 4
M_HEAD_DIM = 128
M_WIDTH = M_HEADS * M_HEAD_DIM
R_WIDTH = 512
R_BLOCKS = 4
R_BLOCK_DIM = R_WIDTH // R_BLOCKS
CONV_W = 4
LRU_C = 8.0
S_HEADS = 8
S_KV_HEADS = 2
S_HEAD_DIM = 64
S_WIDTH = S_HEADS * S_HEAD_DIM
S_KV_WIDTH = S_KV_HEADS * S_HEAD_DIM
WINDOW = 128
N_BUCKETS = 32
MAX_DISTANCE = 128
N_MEM_HEADS = 4
X_HEAD_DIM = 128
X_WIDTH = N_MEM_HEADS * X_HEAD_DIM
N_GROUPS = 4
EXPERTS_PER_GROUP = 8
N_EXPERTS = N_GROUPS * EXPERTS_PER_GROUP
TOP_K = 2
D_EXPERT = 512
EPS = 1e-6

MLSTM_CHUNK = 512
MLSTM_AUG = 2 * M_HEAD_DIM
MLSTM_SHORT_BATCH = 8
SWA_SHORT_TQ = BF16_SUBLANE
SWA_SHORT_BATCH = 8
XATTN_SHORT_BATCH = 8


def _cparams(*sem):
    return pltpu.CompilerParams(dimension_semantics=sem, vmem_limit_bytes=VMEM_LIMIT_BYTES)


def _rms(x, g):
    ms = jnp.mean(x * x, axis=-1, keepdims=True)
    return x * lax.rsqrt(ms + EPS) * g


def _sigmoid(x):
    return 1.0 / (1.0 + jnp.exp(-x))


def _softplus(x):
    return jnp.maximum(x, 0.0) + jnp.log1p(jnp.exp(-jnp.abs(x)))


def _log_sigmoid(x):
    return -_softplus(-x)


def _gelu_tanh(x):
    return 0.5 * x * (1.0 + jnp.tanh(math.sqrt(2.0 / math.pi) * (x + 0.044715 * (x * x * x))))


def _dot(a, b):
    return jnp.dot(a, b, preferred_element_type=F32)


def _dot_nt(a, b):
    return lax.dot_general(a, b, (((1,), (1,)), ((), ())), preferred_element_type=F32)


def _row_tile(n, pref=256):
    for t in (512, 256, 128, 64, 32, 16, 8):
        if t <= pref and n % t == 0:
            return t
    raise ValueError(f"row count {n} is not a multiple of {SUBLANE}")


def _norm_matmul_kernel(x_ref, g_ref, w_ref, *out_refs, splits, normalize):
    x = x_ref[...]
    if normalize:
        x = _rms(x, g_ref[...])
    xb = x.astype(BF16)
    off = 0
    for o_ref, n in zip(out_refs, splits):
        o_ref[...] = _dot(xb, w_ref[:, off:off + n]).astype(o_ref.dtype)
        off += n


def norm_matmul(x, g, w_bf16, splits, normalize=True, out_dtypes=None):
    out_dtypes = out_dtypes or (F32,) * len(splits)
    T, D = x.shape
    tm = _row_tile(T)
    N = w_bf16.shape[1]
    assert sum(splits) == N and all(s % LANE == 0 for s in splits)
    kern = functools.partial(_norm_matmul_kernel, splits=tuple(splits), normalize=normalize)
    return pl.pallas_call(
        kern,
        grid=(T // tm,),
        in_specs=[pl.BlockSpec((tm, D), lambda i: (i, 0)),
                  pl.BlockSpec((1, D), lambda i: (0, 0)),
                  pl.BlockSpec((D, N), lambda i: (0, 0))],
        out_specs=[pl.BlockSpec((tm, n), lambda i: (i, 0)) for n in splits],
        out_shape=[jax.ShapeDtypeStruct((T, n), dt) for n, dt in zip(splits, out_dtypes)],
        compiler_params=_cparams("parallel"),
        name="norm_matmul",
    )(x, g, w_bf16)


def _mlstm_kernel(q_ref, k_ref, v_ref, o_ref, gi_ref, gf_ref, bi_ref, bf_ref, hn_ref,
                  c0_ref, m0_ref, h_out, c_out, m_out, c_sc, m_sc, *, L, NBB):
    @pl.when(pl.program_id(1) == 0)
    def _():
        c_sc[...] = c0_ref[...]
        m_sc[...] = m0_ref[...]

    t_idx = lax.broadcasted_iota(jnp.int32, (L, L), 0)
    s_idx = lax.broadcasted_iota(jnp.int32, (L, L), 1)
    causal = t_idx >= s_idx
    eye = t_idx == s_idx
    row8 = lax.broadcasted_iota(jnp.int32, (SUBLANE, LANE), 0)
    ones_col = (lax.broadcasted_iota(jnp.int32, (L, M_HEAD_DIM), 1) == 0).astype(BF16)
    gi = [gi_ref[b] + bi_ref[...] for b in range(NBB)]
    gf = [_log_sigmoid(gf_ref[b] + bf_ref[...]) for b in range(NBB)]
    m_tiles = [m_sc[b] for b in range(NBB)]
    chains = [dict(b=b, h=h, hs=slice(h * M_HEAD_DIM, (h + 1) * M_HEAD_DIM))
              for b in range(NBB) for h in range(M_HEADS)]

    for c in chains:
        b, h = c["b"], c["h"]
        c["m_prev"] = m_tiles[b][h:h + 1, 0:1]
        c["b_col"] = jnp.sum(jnp.where(causal, gf[b][h:h + 1, :], 0.0), axis=1, keepdims=True)
    for c in chains:
        b, h = c["b"], c["h"]
        b_row = jnp.sum(jnp.where(eye, c["b_col"], 0.0), axis=0, keepdims=True)
        c["r_row"] = gi[b][h:h + 1, :] - b_row
        c["g_end"] = b_row[:, L - 1:L]
        c["log_in"] = jnp.where(causal, c["b_col"] + c["r_row"], NEG_INF)
        c["m_intra"] = jnp.max(c["log_in"], axis=1, keepdims=True)
    for c in chains:
        b, hs = c["b"], c["hs"]
        c["k"] = k_ref[b][:, hs] * (M_HEAD_DIM ** -0.5)
        c["caug"] = c_sc[b, c["h"]]
        c["vaug"] = jnp.concatenate([v_ref[b][:, hs].astype(BF16), ones_col], axis=1)
        qb = q_ref[b][:, hs].astype(BF16)
        c["qk"] = _dot_nt(qb, c["k"].astype(BF16))
        c["qc"] = _dot(qb, c["caug"].astype(BF16))
    for c in chains:
        log_prev = c["b_col"] + c["m_prev"]
        m_t = jnp.maximum(log_prev, c["m_intra"])
        sc = (c["qk"] * jnp.exp(c["log_in"] - m_t)).astype(BF16)
        nd = jnp.exp(log_prev - m_t) * c["qc"] + _dot(sc, c["vaug"])
        den = nd[:, M_HEAD_DIM:M_HEAD_DIM + 1]
        c["hh"] = nd[:, :M_HEAD_DIM] / jnp.maximum(jnp.abs(den), jnp.exp(-m_t))
    for c in chains:
        b, hs = c["b"], c["hs"]
        d = c["hh"] - jnp.mean(c["hh"], axis=-1, keepdims=True)
        y = d * lax.rsqrt(jnp.mean(d * d, axis=-1, keepdims=True) + EPS) * hn_ref[:, hs]
        h_out[b, :, hs] = (y * _sigmoid(o_ref[b][:, hs])).astype(h_out.dtype)
    for c in chains:
        b, h = c["b"], c["h"]
        lie = c["g_end"] + c["r_row"]
        m_end = jnp.maximum(c["g_end"] + c["m_prev"], jnp.max(lie, axis=1, keepdims=True))
        w_pe = jnp.exp(c["g_end"] + c["m_prev"] - m_end)
        ktw = (c["k"].T * jnp.exp(lie - m_end)).astype(BF16)
        c_new = w_pe * c["caug"] + _dot(ktw, c["vaug"])
        c_sc[b, h] = c_new
        c_out[b, h] = c_new
        m_tiles[b] = jnp.where(row8 == h, m_end, m_tiles[b])
    for b in range(NBB):
        m_sc[b] = m_tiles[b]
        m_out[b] = m_tiles[b]


def mlstm(p_m, gates_i, gates_f, b_i, b_f, head_norm, c0aug, m0, L):
    NB, R, _ = p_m.shape
    nc = R // L
    assert R % L == 0
    NBB = 2 if NB % 2 == 0 else 1
    blk = lambda j: pl.BlockSpec((NBB, L, M_WIDTH), lambda b, c, j=j: (b, c, j))
    state_spec = pl.BlockSpec((NBB, M_HEADS, M_HEAD_DIM, MLSTM_AUG), lambda b, c: (b, 0, 0, 0))
    m_spec = pl.BlockSpec((NBB, SUBLANE, LANE), lambda b, c: (b, 0, 0))
    gate_spec = pl.BlockSpec((NBB, SUBLANE, L), lambda b, c: (b, 0, c))
    kern = functools.partial(_mlstm_kernel, L=L, NBB=NBB)
    scratch = [pltpu.VMEM((NBB, M_HEADS, M_HEAD_DIM, MLSTM_AUG), F32), pltpu.VMEM((NBB, SUBLANE, LANE), F32)]
    return pl.pallas_call(
        kern,
        grid=(NB // NBB, nc),
        in_specs=[blk(0), blk(1), blk(2), blk(3), gate_spec, gate_spec,
                  pl.BlockSpec((SUBLANE, 1), lambda b, c: (0, 0)),
                  pl.BlockSpec((SUBLANE, 1), lambda b, c: (0, 0)),
                  pl.BlockSpec((1, M_WIDTH), lambda b, c: (0, 0)),
                  state_spec, m_spec],
        out_specs=[pl.BlockSpec((NBB, L, M_WIDTH), lambda b, c: (b, c, 0)), state_spec, m_spec],
        out_shape=[jax.ShapeDtypeStruct((NB, R, M_WIDTH), BF16),
                   jax.ShapeDtypeStruct((NB, M_HEADS, M_HEAD_DIM, MLSTM_AUG), F32),
                   jax.ShapeDtypeStruct((NB, SUBLANE, LANE), F32)],
        scratch_shapes=scratch,
        compiler_params=_cparams("parallel", "arbitrary"),
        name="mlstm",
    )(p_m, p_m, p_m, p_m, gates_i, gates_f, b_i, b_f, head_norm, c0aug, m0)


def _mlstm_short_kernel(q_ref, k_ref, v_ref, o_ref, gi_ref, gf_ref, bi_ref, bf_ref, hn_ref,
                        c0_ref, n0_ref, m0_ref, h_out, c_out, n_out, m_out, *, NBS, TS):
    L = SUBLANE
    t_idx = lax.broadcasted_iota(jnp.int32, (L, L), 0)
    s_idx = lax.broadcasted_iota(jnp.int32, (L, L), 1)
    causal = t_idx >= s_idx
    eye = t_idx == s_idx
    valid = lax.broadcasted_iota(jnp.int32, (1, L), 1) < TS
    row8 = lax.broadcasted_iota(jnp.int32, (SUBLANE, LANE), 0)
    gi = [gi_ref[b] + bi_ref[...] for b in range(NBS)]
    gf = [_log_sigmoid(gf_ref[b] + bf_ref[...]) for b in range(NBS)]
    m_tiles = [m0_ref[b] for b in range(NBS)]
    chains = [dict(b=b, h=h, hs=slice(h * M_HEAD_DIM, (h + 1) * M_HEAD_DIM))
              for b in range(NBS) for h in range(M_HEADS)]

    for c in chains:
        b, h = c["b"], c["h"]
        c["li"] = jnp.where(valid, gi[b][h:h + 1, 0:L], NEG_INF)
        lf = jnp.where(valid, gf[b][h:h + 1, 0:L], 0.0)
        c["m_prev"] = m_tiles[b][h:h + 1, 0:1]
        c["b_col"] = jnp.sum(jnp.where(causal, lf, 0.0), axis=1, keepdims=True)
    for c in chains:
        b_row = jnp.sum(jnp.where(eye, c["b_col"], 0.0), axis=0, keepdims=True)
        c["r_row"] = c["li"] - b_row
        c["g_end"] = b_row[:, L - 1:L]
        c["log_in"] = jnp.where(causal, c["b_col"] + c["r_row"], NEG_INF)
        c["m_intra"] = jnp.max(c["log_in"], axis=1, keepdims=True)
    for c in chains:
        b, hs = c["b"], c["hs"]
        c["q"] = q_ref[b][:, hs]
        c["k"] = k_ref[b][:, hs] * (M_HEAD_DIM ** -0.5)
        c["vb"] = v_ref[b][:, hs].astype(BF16)
        c["c0"] = c0_ref[0, b, c["h"]]
        c["n0"] = n0_ref[0, b, c["h"]:c["h"] + 1, :]
        qb = c["q"].astype(BF16)
        c["qk"] = _dot_nt(qb, c["k"].astype(BF16))
        c["qc"] = _dot(qb, c["c0"].astype(BF16))
        c["qn"] = jnp.sum(c["q"] * c["n0"], axis=1, keepdims=True)
    for c in chains:
        log_prev = c["b_col"] + c["m_prev"]
        m_t = jnp.maximum(log_prev, c["m_intra"])
        w_prev = jnp.exp(log_prev - m_t)
        sc = c["qk"] * jnp.exp(c["log_in"] - m_t)
        num = w_prev * c["qc"] + _dot(sc.astype(BF16), c["vb"])
        den = w_prev * c["qn"] + jnp.sum(sc, axis=1, keepdims=True)
        c["hh"] = num / jnp.maximum(jnp.abs(den), jnp.exp(-m_t))
    for c in chains:
        b, hs = c["b"], c["hs"]
        d = c["hh"] - jnp.mean(c["hh"], axis=-1, keepdims=True)
        y = d * lax.rsqrt(jnp.mean(d * d, axis=-1, keepdims=True) + EPS) * hn_ref[:, hs]
        h_out[b, :, hs] = y * _sigmoid(o_ref[b][:, hs])
    for c in chains:
        b, h = c["b"], c["h"]
        lie = c["g_end"] + c["r_row"]
        m_end = jnp.maximum(c["g_end"] + c["m_prev"], jnp.max(lie, axis=1, keepdims=True))
        w_pe = jnp.exp(c["g_end"] + c["m_prev"] - m_end)
        w_e = jnp.exp(lie - m_end)
        w_e_col = jnp.sum(jnp.where(eye, w_e, 0.0), axis=1, keepdims=True)
        kw = c["k"] * w_e_col
        c_out[b, h] = w_pe * c["c0"] + lax.dot_general(kw.astype(BF16), c["vb"], (((0,), (0,)), ((), ())),
                                                       preferred_element_type=F32)
        n_out[b, h:h + 1, :] = w_pe * c["n0"] + jnp.sum(kw, axis=0, keepdims=True)
        m_tiles[b] = jnp.where(row8 == h, m_end, m_tiles[b])
    for b in range(NBS):
        m_out[b] = m_tiles[b]


def mlstm_short(p_m, gates_i, gates_f, b_i, b_f, head_norm, c0, n0, m0, layer, NBS, TS):
    NB = p_m.shape[0]
    H, DH = M_HEADS, M_HEAD_DIM
    assert NB % NBS == 0 and TS <= SUBLANE
    blk = lambda j: pl.BlockSpec((NBS, SUBLANE, M_WIDTH), lambda i, j=j: (i, 0, j))
    tile = pl.BlockSpec((NBS, SUBLANE, LANE), lambda i: (i, 0, 0))
    kern = functools.partial(_mlstm_short_kernel, NBS=NBS, TS=TS)
    return pl.pallas_call(
        kern,
        grid=(NB // NBS,),
        in_specs=[blk(0), blk(1), blk(2), blk(3), tile, tile,
                  pl.BlockSpec((SUBLANE, 1), lambda i: (0, 0)),
                  pl.BlockSpec((SUBLANE, 1), lambda i: (0, 0)),
                  pl.BlockSpec((1, M_WIDTH), lambda i: (0, 0)),
                  pl.BlockSpec((1, NBS, H, DH, DH), lambda i: (layer, i, 0, 0, 0)),
                  pl.BlockSpec((1, NBS, H, DH), lambda i: (layer, i, 0, 0)),
                  tile],
        out_specs=[pl.BlockSpec((NBS, SUBLANE, M_WIDTH), lambda i: (i, 0, 0)),
                   pl.BlockSpec((NBS, H, DH, DH), lambda i: (i, 0, 0, 0)),
                   pl.BlockSpec((NBS, H, DH), lambda i: (i, 0, 0)),
                   tile],
        out_shape=[jax.ShapeDtypeStruct((NB, SUBLANE, M_WIDTH), F32),
                   jax.ShapeDtypeStruct((NB, H, DH, DH), F32),
                   jax.ShapeDtypeStruct((NB, H, DH), F32),
                   jax.ShapeDtypeStruct((NB, SUBLANE, LANE), F32)],
        compiler_params=_cparams("parallel"),
        name="mlstm_short",
    )(p_m, p_m, p_m, p_m, gates_i, gates_f, b_i, b_f, head_norm, c0, n0, m0)


def _rglru_kernel(x_ref, g_ref, conv0_ref, h0_ref, cw_ref, cb_ref, wa_ref, wx_ref, ba_ref, bx_ref,
                  lam_ref, y_out, hlast_out, conv_out, xpad_sc, hc_sc, *, TR, Bs):
    CB = (CONV_W - 1) * Bs
    X0 = -(-CB // SUBLANE) * SUBLANE

    @pl.when(pl.program_id(1) == 0)
    def _():
        xpad_sc[X0 - CB:X0, :] = conv0_ref[0]
        hc_sc[...] = h0_ref[0]

    xpad_sc[X0:X0 + TR, :] = x_ref[0]
    y = cb_ref[...]
    for j in range(CONV_W):
        y = y + xpad_sc[X0 - CB + j * Bs:X0 - CB + j * Bs + TR, :] * cw_ref[j:j + 1, :]
    tail = xpad_sc[X0 + TR - CB:X0 + TR, :]
    conv_out[0] = tail
    xpad_sc[X0 - CB:X0, :] = tail

    yb = y.astype(BF16)
    rs, is_ = [], []
    for n in range(R_BLOCKS):
        sl = slice(n * R_BLOCK_DIM, (n + 1) * R_BLOCK_DIM)
        rs.append(_dot(yb[:, sl], wa_ref[n]))
        is_.append(_dot(yb[:, sl], wx_ref[n]))
    r = _sigmoid(jnp.concatenate(rs, axis=1) + ba_ref[...])
    i = _sigmoid(jnp.concatenate(is_, axis=1) + bx_ref[...])
    log_a = -LRU_C * r * _softplus(-lam_ref[...])
    a_cum = jnp.exp(log_a)
    t = jnp.tanh(log_a)
    u_cum = jnp.sqrt(-2.0 * t / (1.0 - t)) * (i * y)

    def scan_rows(a, u, axis, stride):
        n = a.shape[axis]
        idx = lax.broadcasted_iota(jnp.int32, a.shape, axis)
        d = stride
        while d < n:
            keep = idx >= d
            u = jnp.where(keep, a * pltpu.roll(u, d, axis=axis) + u, u)
            a = jnp.where(keep, a * pltpu.roll(a, d, axis=axis), a)
            d *= 2
        return a, u

    hc = hc_sc[...]
    if Bs == 1:
        NG = TR // SUBLANE
        a3, u3 = scan_rows(a_cum.reshape(NG, SUBLANE, R_WIDTH), u_cum.reshape(NG, SUBLANE, R_WIDTH), 1, 1)
        a_tot, u_tot = scan_rows(a3[:, SUBLANE - 1, :], u3[:, SUBLANE - 1, :], 0, 1)
        h_end = u_tot + a_tot * hc
        g_idx = lax.broadcasted_iota(jnp.int32, (NG, R_WIDTH), 0)
        h_start = jnp.where(g_idx == 0, hc, pltpu.roll(h_end, 1, axis=0))
        h = (u3 + a3 * h_start[:, None, :]).reshape(TR, R_WIDTH)
    else:
        a_cum, u_cum = scan_rows(a_cum, u_cum, 0, Bs)
        h = u_cum + a_cum * jnp.concatenate([hc] * (TR // Bs), axis=0)
    y_out[0] = (h * _gelu_tanh(g_ref[0])).astype(y_out.dtype)
    h_last = h[TR - Bs:, :]
    hc_sc[...] = h_last
    hlast_out[0] = h_last


def rglru(p_r, conv0, h0, cw, cb, wa, wx, ba, bx, lam, TR, Bs, out_dtype):
    G, R, _ = p_r.shape
    CB = (CONV_W - 1) * Bs
    X0 = -(-CB // SUBLANE) * SUBLANE
    assert R % TR == 0 and TR % Bs == 0 and TR >= CB
    W = R_WIDTH
    const = lambda shape: pl.BlockSpec(shape, lambda g, r: (0,) * len(shape))
    kern = functools.partial(_rglru_kernel, TR=TR, Bs=Bs)
    return pl.pallas_call(
        kern,
        grid=(G, R // TR),
        in_specs=[pl.BlockSpec((1, TR, W), lambda g, r: (g, r, 0)),
                  pl.BlockSpec((1, TR, W), lambda g, r: (g, r, 1)),
                  pl.BlockSpec((1, CB, W), lambda g, r: (g, 0, 0)),
                  pl.BlockSpec((1, Bs, W), lambda g, r: (g, 0, 0)),
                  const((CONV_W, W)), const((1, W)),
                  const((R_BLOCKS, R_BLOCK_DIM, R_BLOCK_DIM)), const((R_BLOCKS, R_BLOCK_DIM, R_BLOCK_DIM)),
                  const((1, W)), const((1, W)), const((1, W))],
        out_specs=[pl.BlockSpec((1, TR, W), lambda g, r: (g, r, 0)),
                   pl.BlockSpec((1, Bs, W), lambda g, r: (g, 0, 0)),
                   pl.BlockSpec((1, CB, W), lambda g, r: (g, 0, 0))],
        out_shape=[jax.ShapeDtypeStruct((G, R, W), out_dtype),
                   jax.ShapeDtypeStruct((G, Bs, W), F32),
                   jax.ShapeDtypeStruct((G, CB, W), F32)],
        scratch_shapes=[pltpu.VMEM((X0 + TR, W), F32), pltpu.VMEM((Bs, W), F32)],
        compiler_params=_cparams("parallel", "arbitrary"),
        name="rglru",
    )(p_r, p_r, conv0, h0, cw, cb, wa, wx, ba, bx, lam)


def _swa_kernel(q_ref, kp_ref, kc_ref, vp_ref, vc_ref, bias_ref, sink_ref, o_ref, *, NBS, TQ, mask_first):
    GQ = S_HEADS // S_KV_HEADS
    lane = lax.broadcasted_iota(jnp.int32, (1, LANE), 1)
    lo = lane < S_HEAD_DIM

    def both_halves(x, j):
        xr = pltpu.roll(x, S_HEAD_DIM, axis=1)
        return (jnp.where(lo, x, xr) if j == 0 else jnp.where(lo, xr, x)).astype(BF16)

    s1_parts, s2_parts, v_parts = [], [], []
    for b in range(NBS):
        q = q_ref[b] * (S_HEAD_DIM ** -0.5)
        for j in range(S_KV_HEADS):
            stack = []
            for g in range(GQ):
                h = GQ * j + g
                qh = q[:, (h // 2) * LANE:(h // 2 + 1) * LANE]
                stack.append(jnp.where(lo if h % 2 == 0 else jnp.logical_not(lo), qh, 0.0))
            qs = jnp.concatenate(stack, axis=0).astype(BF16)
            s1_parts.append(_dot_nt(qs, both_halves(kp_ref[b], j)))
            s2_parts.append(_dot_nt(qs, both_halves(kc_ref[b], j)))
            v_parts.append((both_halves(vp_ref[b], j), both_halves(vc_ref[b], j)))
    s1 = jnp.concatenate(s1_parts, axis=0) + bias_ref[:, 0:WINDOW]
    s2 = jnp.concatenate(s2_parts, axis=0) + bias_ref[:, WINDOW:WINDOW + TQ]
    if mask_first:
        s1 = jnp.where(pl.program_id(1) == 0, NEG_INF, s1)
    sink = sink_ref[...]
    mx = jnp.maximum(jnp.maximum(jnp.max(s1, axis=1, keepdims=True), jnp.max(s2, axis=1, keepdims=True)), sink)
    p1 = jnp.exp(s1 - mx)
    p2 = jnp.exp(s2 - mx)
    inv = 1.0 / (jnp.sum(p1, axis=1, keepdims=True) + jnp.sum(p2, axis=1, keepdims=True) + jnp.exp(sink - mx))
    p1 = p1.astype(BF16)
    p2 = p2.astype(BF16)
    R = GQ * TQ
    for b in range(NBS):
        for j in range(S_KV_HEADS):
            n = b * S_KV_HEADS + j
            rows = slice(n * R, (n + 1) * R)
            vp, vc = v_parts[n]
            o = (_dot(p1[rows], vp) + _dot(p2[rows], vc)) * inv[rows]
            for pair in range(GQ // 2):
                even = o[(2 * pair) * TQ:(2 * pair + 1) * TQ]
                odd = o[(2 * pair + 1) * TQ:(2 * pair + 2) * TQ]
                blk = (GQ * j) // 2 + pair
                o_ref[b, :, blk * LANE:(blk + 1) * LANE] = jnp.where(lo, even, odd).astype(o_ref.dtype)


def swa(arrs, maps, shapes, bias_rows, sink_rows, grid, NBS, TQ, mask_first, out_dtype):
    kern = functools.partial(_swa_kernel, NBS=NBS, TQ=TQ, mask_first=mask_first)
    in_specs = [pl.BlockSpec(s, m) for s, m in zip(shapes, maps)]
    in_specs += [pl.BlockSpec(bias_rows.shape, lambda b, n: (0, 0)),
                 pl.BlockSpec(sink_rows.shape, lambda b, n: (0, 0))]
    return pl.pallas_call(
        kern,
        grid=grid,
        in_specs=in_specs,
        out_specs=pl.BlockSpec((NBS, TQ, S_WIDTH), lambda b, n: (b, n, 0)),
        out_shape=jax.ShapeDtypeStruct((grid[0] * NBS, grid[1] * TQ, S_WIDTH), out_dtype),
        compiler_params=_cparams("parallel", "arbitrary"),
        name="swa",
    )(*arrs, bias_rows, sink_rows)


def t5_bucket(dist):
    max_exact = N_BUCKETS // 2
    d = jnp.maximum(dist, 0)
    large = max_exact + (jnp.log(jnp.maximum(d, 1).astype(F32) / max_exact)
                         / math.log(MAX_DISTANCE / max_exact) * (N_BUCKETS - max_exact)).astype(jnp.int32)
    return jnp.where(d < max_exact, d, jnp.minimum(large, N_BUCKETS - 1))


def swa_bias_rows(rel_bias, TQ, NBS):
    qi = jnp.arange(TQ)[:, None]
    kj = jnp.arange(WINDOW + TQ)[None, :]
    dist = qi + WINDOW - kj
    onehot = (t5_bucket(dist)[..., None] == jnp.arange(N_BUCKETS)).astype(F32)
    b = jnp.einsum("qkn,nh->hqk", onehot, rel_bias.astype(F32), precision=lax.Precision.HIGHEST)
    b = jnp.where((dist >= 0) & (dist <= WINDOW), b, NEG_INF).reshape(S_HEADS * TQ, WINDOW + TQ)
    return jnp.tile(b, (NBS, 1))


def swa_sink_rows(sinks, TQ, NBS):
    return jnp.tile(jnp.repeat(sinks.astype(F32), TQ), NBS).reshape(-1, 1)


def _merge_kernel(hm, hr, hs, gm, gr, gs, x, wm, wr, ws, wo, out):
    def branch(h_ref, g_ref, w_ref):
        return _sigmoid(g_ref[...].astype(F32)) * _dot(h_ref[...].astype(BF16), w_ref[...])
    merged = branch(hm, gm, wm) + branch(hr, gr, wr) + branch(hs, gs, ws)
    out[...] = x[...] + _dot(merged.astype(BF16), wo[...])


def merge_out(hm, hr, hs, p_g, x, wm, wr, ws, wo):
    T, D = x.shape
    tm = _row_tile(T, 512)
    hspec = lambda w: pl.BlockSpec((tm, w), lambda i: (i, 0))
    gspec = lambda j: pl.BlockSpec((tm, D), lambda i, j=j: (i, j))
    wspec = lambda a: pl.BlockSpec(a.shape, lambda i: (0, 0))
    return pl.pallas_call(
        _merge_kernel,
        grid=(T // tm,),
        in_specs=[hspec(M_WIDTH), hspec(R_WIDTH), hspec(S_WIDTH), gspec(0), gspec(1), gspec(2),
                  hspec(D), wspec(wm), wspec(wr), wspec(ws), wspec(wo)],
        out_specs=hspec(D),
        out_shape=jax.ShapeDtypeStruct((T, D), F32),
        compiler_params=_cparams("parallel"),
        name="merge_out",
    )(hm, hr, hs, p_g, p_g, p_g, x, wm, wr, ws, wo)


def _xattn_kernel(x_ref, g_ref, wq_ref, wo_ref, mk_ref, mv_ref, out_ref, *, nb, tq, interleaved):
    NH = N_MEM_HEADS
    x = x_ref[...].reshape(nb * tq, D_MODEL)
    q = _dot(_rms(x, g_ref[...]).astype(BF16), wq_ref[...])
    head = lambda a, h: a[:, h * X_HEAD_DIM:(h + 1) * X_HEAD_DIM]
    parts = []
    for b in range(nb):
        qb = q[b * tq:(b + 1) * tq]
        if interleaved:
            qs = jnp.concatenate([head(qb, h) for h in range(NH)], axis=0).astype(BF16)
            parts.append(_dot_nt(qs, mk_ref[0, b].astype(BF16)))
        else:
            parts += [_dot_nt(head(qb, h).astype(BF16), head(mk_ref[0, b], h).astype(BF16)) for h in range(NH)]
    s = jnp.concatenate(parts, axis=0) * (X_HEAD_DIM ** -0.5)
    if interleaved:
        row_h = (lax.broadcasted_iota(jnp.int32, s.shape, 0) // tq) % NH
        s = jnp.where(lax.broadcasted_iota(jnp.int32, s.shape, 1) % NH == row_h, s, NEG_INF)
    p = jnp.exp(s - jnp.max(s, axis=1, keepdims=True))
    inv = 1.0 / jnp.sum(p, axis=1, keepdims=True)
    rows = []
    for b in range(nb):
        if interleaved:
            r = slice(b * NH * tq, (b + 1) * NH * tq)
            o = _dot(p[r].astype(BF16), mv_ref[0, b].astype(BF16)) * inv[r]
            heads = [o[h * tq:(h + 1) * tq] for h in range(NH)]
        else:
            heads = []
            for h in range(NH):
                r = slice((b * NH + h) * tq, (b * NH + h + 1) * tq)
                heads.append(_dot(p[r].astype(BF16), head(mv_ref[0, b], h).astype(BF16)) * inv[r])
        rows.append(jnp.concatenate(heads, axis=1))
    o_all = rows[0] if nb == 1 else jnp.concatenate(rows, axis=0)
    y = x + _dot(o_all.astype(BF16), wo_ref[...])
    out_ref[...] = y.reshape(nb, tq, D_MODEL)


def xattn(x, g, wq, wo, mem_k, mem_v, layer, nb, tq, interleaved):
    B, R, D = x.shape
    mem_blk = (1, nb) + mem_k.shape[2:]
    kern = functools.partial(_xattn_kernel, nb=nb, tq=tq, interleaved=interleaved)
    return pl.pallas_call(
        kern,
        grid=(B // nb, R // tq),
        in_specs=[pl.BlockSpec((nb, tq, D), lambda b, r: (b, r, 0)),
                  pl.BlockSpec((1, D), lambda b, r: (0, 0)),
                  pl.BlockSpec(wq.shape, lambda b, r: (0, 0)),
                  pl.BlockSpec(wo.shape, lambda b, r: (0, 0)),
                  pl.BlockSpec(mem_blk, lambda b, r: (layer, b, 0, 0)),
                  pl.BlockSpec(mem_blk, lambda b, r: (layer, b, 0, 0))],
        out_specs=pl.BlockSpec((nb, tq, D), lambda b, r: (b, r, 0)),
        out_shape=jax.ShapeDtypeStruct((B, R, D), F32),
        compiler_params=_cparams("parallel", "arbitrary"),
        name="xattn",
    )(x, g, wq, wo, mem_k, mem_v)


def _router_kernel(x_ref, g_ref, wr_ref, route_out, counts_out, cnt_sc):
    @pl.when(pl.program_id(0) == 0)
    def _():
        cnt_sc[...] = jnp.zeros_like(cnt_sc)

    xn = _rms(x_ref[...], g_ref[...])
    logits = _dot(xn.astype(BF16), wr_ref[...])
    tm = logits.shape[0]
    lane_i = lax.broadcasted_iota(jnp.int32, (tm, LANE), 1)
    lane = lane_i.astype(F32)
    lane_group = (lane_i // EXPERTS_PER_GROUP).astype(F32)

    def top1(v):
        mx = jnp.max(v, axis=1, keepdims=True)
        return mx, jnp.min(jnp.where(v == mx, lane, float(LANE)), axis=1, keepdims=True)

    gl = jnp.where(lane_i < N_GROUPS, logits[:, :LANE], NEG_INF)
    g_max, g_idx = top1(gl)
    g_w = 1.0 / jnp.sum(jnp.exp(gl - g_max), axis=1, keepdims=True)
    el = jnp.where(lane_group == g_idx, logits[:, LANE:], NEG_INF)
    e1, i1 = top1(el)
    e2, i2 = top1(jnp.where(lane == i1, NEG_INF, el))
    t = jnp.exp(e2 - e1)
    p1 = 1.0 / (1.0 + t)

    oh1 = (lane == i1).astype(F32)
    oh2 = (lane == i2).astype(F32)
    oh = oh1 + oh2
    r_idx = lax.broadcasted_iota(jnp.int32, (tm, tm), 0)
    c_idx = lax.broadcasted_iota(jnp.int32, (tm, tm), 1)
    before = (r_idx > c_idx).astype(BF16)
    base = cnt_sc[0:1, :] + _dot(before, oh.astype(BF16))
    rank1 = jnp.sum(oh1 * base, axis=1, keepdims=True)
    rank2 = jnp.sum(oh2 * base, axis=1, keepdims=True)
    cnt = cnt_sc[...] + jnp.sum(oh, axis=0, keepdims=True)
    cnt_sc[...] = cnt
    counts_out[...] = cnt

    vals = (i1, i2, p1 * g_w, t * p1 * g_w, rank1, rank2)
    route = jnp.zeros((tm, LANE), F32)
    for n, v in enumerate(vals):
        route = jnp.where(lane_i == n, v, route)
    route_out[...] = route


def router(x, g, w_router):
    T, D = x.shape
    tm = _row_tile(T)
    return pl.pallas_call(
        _router_kernel,
        grid=(T // tm,),
        in_specs=[pl.BlockSpec((tm, D), lambda i: (i, 0)),
                  pl.BlockSpec((1, D), lambda i: (0, 0)),
                  pl.BlockSpec(w_router.shape, lambda i: (0, 0))],
        out_specs=[pl.BlockSpec((tm, LANE), lambda i: (i, 0)), pl.BlockSpec((SUBLANE, LANE), lambda i: (0, 0))],
        out_shape=[jax.ShapeDtypeStruct((T, LANE), F32), jax.ShapeDtypeStruct((SUBLANE, LANE), F32)],
        scratch_shapes=[pltpu.VMEM((SUBLANE, LANE), F32)],
        compiler_params=_cparams("arbitrary"),
        name="router",
    )(x, g, w_router)


def moe_plan(routes, counts, TB, tms):
    experts = jnp.arange(N_EXPERTS, dtype=jnp.int32)
    counts = [c[0, :N_EXPERTS].astype(jnp.int32) for c in counts]
    total = sum(counts)
    padded = (total + TB - 1) // TB * TB
    pad_ends = jnp.cumsum(padded)
    base = pad_ends - padded
    dests = []
    for route, cnt, tm in zip(routes, counts, tms):
        e_idx = route[:, 0:TOP_K].astype(jnp.int32)
        rank = route[:, 4:4 + TOP_K].astype(jnp.int32)
        dest = jnp.sum(jnp.where(e_idx[..., None] == experts, base, 0), axis=-1) + rank
        dests.append(dest.reshape(route.shape[0] // tm, 1, TOP_K * tm))
        base = base + cnt
    n_tokens = sum(r.shape[0] for r in routes)
    n_blocks = -(-(n_tokens * TOP_K + N_EXPERTS * (TB - 1)) // TB)
    block_start = jnp.arange(n_blocks, dtype=jnp.int32) * TB
    block_e = jnp.minimum(jnp.sum(pad_ends[None, :] <= block_start[:, None], axis=1), N_EXPERTS - 1).astype(jnp.int32)
    end_valid = jnp.sum(jnp.where(block_e[:, None] == experts, pad_ends - padded + total, 0), axis=1)
    n_valid = jnp.clip(end_valid - block_start, 0, TB).astype(jnp.int32)
    return block_e, n_valid, dests


DMA_UNROLL = 8


def _for_row_chunks(n_rows, fn):
    def body(c, carry):
        for u in range(DMA_UNROLL):
            fn(c * DMA_UNROLL + u)
        return carry
    lax.fori_loop(0, n_rows // DMA_UNROLL, body, 0)


def _pack_bf16_pair(x):
    n = x.shape[1] // 2
    bits = lambda a: lax.bitcast_convert_type(a.astype(BF16).astype(F32), jnp.uint32)
    return (bits(x[:, :n]) >> 16) | (bits(x[:, n:]) & jnp.uint32(0xFFFF0000))


def _unpack_bf16_pair(p):
    lo = lax.bitcast_convert_type(p << 16, F32).astype(BF16)
    hi = lax.bitcast_convert_type(p & jnp.uint32(0xFFFF0000), F32).astype(BF16)
    return lo, hi


def _dispatch_kernel(dst_ref, x_ref, g_ref, init_hbm, xs_hbm, buf, sem, *, tm, n_tiles):
    del init_hbm
    i = pl.program_id(0)
    slot = i % 2

    def copy(row, s, row_dst):
        return pltpu.make_async_copy(buf.at[s, pl.ds(row, 1)], xs_hbm.at[pl.ds(row_dst, 1)], sem.at[s])

    def wait_slot(s):
        def wait_row(r):
            for _ in range(TOP_K):
                copy(r, s, 0).wait()
        _for_row_chunks(tm, wait_row)

    @pl.when(i >= 2)
    def _():
        wait_slot(slot)

    buf[slot] = _pack_bf16_pair(_rms(x_ref[...], g_ref[...]))

    def start_row(r):
        for k in range(TOP_K):
            copy(r, slot, dst_ref[0, 0, TOP_K * r + k]).start(priority=k % 2)
    _for_row_chunks(tm, start_row)

    @pl.when(i == n_tiles - 1)
    def _():
        wait_slot(slot)
        if n_tiles >= 2:
            wait_slot(1 - slot)


def moe_dispatch(x, g, dest, xs_init, tm):
    T, D = x.shape
    n_tiles = T // tm
    assert tm % DMA_UNROLL == 0
    kern = functools.partial(_dispatch_kernel, tm=tm, n_tiles=n_tiles)
    return pl.pallas_call(
        kern,
        grid=(n_tiles,),
        in_specs=[pl.BlockSpec((1, 1, TOP_K * tm), lambda i: (i, 0, 0), memory_space=pltpu.SMEM),
                  pl.BlockSpec((tm, D), lambda i: (i, 0)),
                  pl.BlockSpec((1, D), lambda i: (0, 0)),
                  pl.BlockSpec(memory_space=pl.ANY)],
        out_specs=pl.BlockSpec(memory_space=pl.ANY),
        out_shape=jax.ShapeDtypeStruct(xs_init.shape, xs_init.dtype),
        input_output_aliases={3: 0},
        scratch_shapes=[pltpu.VMEM((2, tm, D // 2), jnp.uint32), pltpu.SemaphoreType.DMA((2,))],
        compiler_params=_cparams("arbitrary"),
        name="moe_dispatch",
    )(dest, x, g, xs_init)


def _ffn_kernel(be_ref, nv_ref, xs_ref, wg_ref, wu_ref, wd_ref, ys_ref, wg_bf, wu_bf, wd_bf):
    i = pl.program_id(0)
    nv = nv_ref[i]

    @pl.when((i == 0) | (be_ref[i] != be_ref[jnp.maximum(i - 1, 0)]))
    def _():
        wg_bf[...] = wg_ref[0, 0].astype(BF16)
        wu_bf[...] = wu_ref[0, 0].astype(BF16)
        wd_bf[...] = wd_ref[0, 0].astype(BF16)

    @pl.when(nv > 0)
    def _():
        row = lax.broadcasted_iota(jnp.int32, (xs_ref.shape[0], 1), 0)
        packed = jnp.where(row < nv, xs_ref[...], jnp.uint32(0))
        x_lo, x_hi = _unpack_bf16_pair(packed)
        half = x_lo.shape[1]
        g = _dot(x_lo, wg_bf[:half, :]) + _dot(x_hi, wg_bf[half:, :])
        u = _dot(x_lo, wu_bf[:half, :]) + _dot(x_hi, wu_bf[half:, :])
        hid = (g * _sigmoid(g)) * u
        ys_ref[...] = _dot(hid.astype(BF16), wd_bf[...])

    @pl.when(nv == 0)
    def _():
        ys_ref[...] = jnp.zeros_like(ys_ref)


def expert_ffn(xs, block_e, n_valid, w_gate, w_up, w_down, layer, TB):
    P, D = xs.shape[0], 2 * xs.shape[1]
    n_blocks = P // TB
    wspec = lambda a, b: pl.BlockSpec((1, 1, a, b), lambda i, be, nv: (layer, be[i], 0, 0))
    grid_spec = pltpu.PrefetchScalarGridSpec(
        num_scalar_prefetch=2,
        grid=(n_blocks,),
        in_specs=[pl.BlockSpec((TB, D // 2), lambda i, be, nv: (i, 0)),
                  wspec(D, D_EXPERT), wspec(D, D_EXPERT), wspec(D_EXPERT, D)],
        out_specs=pl.BlockSpec((TB, D), lambda i, be, nv: (i, 0)),
        scratch_shapes=[pltpu.VMEM((D, D_EXPERT), BF16), pltpu.VMEM((D, D_EXPERT), BF16),
                        pltpu.VMEM((D_EXPERT, D), BF16)],
    )
    return pl.pallas_call(
        _ffn_kernel,
        grid_spec=grid_spec,
        out_shape=jax.ShapeDtypeStruct((P, D), F32),
        compiler_params=_cparams("arbitrary"),
        name="expert_ffn",
    )(block_e, n_valid, xs, w_gate, w_up, w_down)


def _combine_kernel(cur_ref, nxt_ref, x_ref, route_ref, gf_ref, ys_hbm, out_ref, ybuf, sem, *,
                    tm, n_tiles, final_norm):
    i = pl.program_id(0)
    slot = i % 2

    def copy(row_src, s, k, row):
        return pltpu.make_async_copy(ys_hbm.at[pl.ds(row_src, 1)], ybuf.at[s, k, pl.ds(row, 1)], sem.at[s])

    def start_tile(idx_ref, s):
        def start_row(r):
            for k in range(TOP_K):
                copy(idx_ref[0, 0, TOP_K * r + k], s, k, r).start(priority=k % 2)
        _for_row_chunks(tm, start_row)

    @pl.when(i == 0)
    def _():
        start_tile(cur_ref, 0)

    @pl.when(i + 1 < n_tiles)
    def _():
        start_tile(nxt_ref, 1 - slot)

    def wait_row(r):
        for k in range(TOP_K):
            copy(0, slot, k, r).wait()
    _for_row_chunks(tm, wait_row)

    route = route_ref[...]
    lane_i = lax.broadcasted_iota(jnp.int32, route.shape, 1)
    g0 = jnp.sum(jnp.where(lane_i == 2, route, 0.0), axis=1, keepdims=True)
    g1 = jnp.sum(jnp.where(lane_i == 3, route, 0.0), axis=1, keepdims=True)
    y = x_ref[...] + (g0 * ybuf[slot, 0] + g1 * ybuf[slot, 1])
    out_ref[...] = _rms(y, gf_ref[...]) if final_norm else y


def moe_combine(x, ys, dest, route, g_final, tm, final_norm):
    T, D = x.shape
    n_tiles = T // tm
    assert tm % DMA_UNROLL == 0
    idx_blk = lambda f: pl.BlockSpec((1, 1, TOP_K * tm), f, memory_space=pltpu.SMEM)
    kern = functools.partial(_combine_kernel, tm=tm, n_tiles=n_tiles, final_norm=final_norm)
    return pl.pallas_call(
        kern,
        grid=(n_tiles,),
        in_specs=[idx_blk(lambda i: (i, 0, 0)),
                  idx_blk(lambda i: (jnp.minimum(i + 1, n_tiles - 1), 0, 0)),
                  pl.BlockSpec((tm, D), lambda i: (i, 0)),
                  pl.BlockSpec((tm, LANE), lambda i: (i, 0)),
                  pl.BlockSpec((1, D), lambda i: (0, 0)),
                  pl.BlockSpec(memory_space=pl.ANY)],
        out_specs=pl.BlockSpec((tm, D), lambda i: (i, 0)),
        out_shape=jax.ShapeDtypeStruct((T, D), F32),
        scratch_shapes=[pltpu.VMEM((2, TOP_K, tm, D), F32), pltpu.SemaphoreType.DMA((2,))],
        compiler_params=_cparams("arbitrary"),
        name="moe_combine",
    )(dest, dest, x, route, g_final, ys)


IN_SPLITS = (4 * M_WIDTH, LANE, 2 * R_WIDTH, S_WIDTH + 2 * S_KV_WIDTH, 3 * D_MODEL)
IN_DTYPES = (F32, F32, F32, F32, BF16)


def _prep_w_in(w):
    c0 = 4 * M_WIDTH
    c1 = c0 + 2 * M_HEADS
    gates = jnp.pad(w[:, c0:c1], ((0, 0), (0, LANE - 2 * M_HEADS)))
    return jnp.concatenate([w[:, :c0], gates, w[:, c1:]], axis=1).astype(BF16)


def _prep_router(w_rg, w_re):
    pad = lambda w: jnp.pad(w, ((0, 0), (0, LANE - w.shape[1])))
    return jnp.concatenate([pad(w_rg), pad(w_re)], axis=1).astype(BF16)


def _row(v):
    return v.reshape(1, -1).astype(F32)


def _pad_rows(a, n):
    return jnp.pad(a, ((0, 0), (0, n - a.shape[1])) + ((0, 0),) * (a.ndim - 2))


def _moe_block(T):
    return 512 if T >= 4096 else 128


def moe_layer(xs_in, slots, g_ffn, w_router, w_gate, w_up, w_down, layer, g_final, final_norm):
    D = xs_in[0].shape[1]
    TB = _moe_block(sum(x.shape[0] for x in xs_in))
    tms = [_row_tile(x.shape[0]) for x in xs_in]
    routed = [router(x, g_ffn, w_router) for x in xs_in]
    routes, counts = [r[0] for r in routed], [r[1] for r in routed]
    block_e, n_valid, dests = moe_plan(routes, counts, TB, tms)
    buf = jnp.zeros((block_e.shape[0] * TB, D // 2), jnp.uint32) if slots is None else slots
    for x, dest, tm in zip(xs_in, dests, tms):
        buf = moe_dispatch(x, g_ffn, dest, buf, tm)
    ys = expert_ffn(buf, block_e, n_valid, w_gate, w_up, w_down, layer, TB)
    outs = [moe_combine(x, ys, dest, route, g_final, tm, final_norm)
            for x, dest, route, tm in zip(xs_in, dests, routes, tms)]
    return outs, buf


def kernel(x_prompt, x_sample, mem_prompt, cache_mem_k, cache_mem_v, cache_swa_k, cache_swa_v, state_mlstm_C, state_mlstm_n, state_mlstm_m, state_rglru_h, state_rglru_conv, norm_mix, w_in, m_igate_b, m_fgate_b, m_head_norm, r_conv_w, r_conv_b, r_gate_a_w, r_gate_a_b, r_gate_x_w, r_gate_x_b, r_lambda, swa_sinks, rel_bias, w_branch_m, w_branch_r, w_branch_s, w_out, norm_mem, xq_w, xk_w, xv_w, xo_w, norm_ffn, router_group_w, router_expert_w, moe_w_gate, moe_w_up, moe_w_down, norm_final):
    B, S, D = x_prompt.shape
    NS, TS, _ = x_sample.shape
    depth = w_in.shape[0]
    n_mem = mem_prompt.shape[1]
    Tp, Ts = B * S, NS * TS
    H = M_HEADS
    assert S % MLSTM_CHUNK == 0 and S % WINDOW == 0 and TS <= SUBLANE
    assert NS % XATTN_SHORT_BATCH == 0 and NS % SWA_SHORT_BATCH == 0 and NS % MLSTM_SHORT_BATCH == 0

    xp = x_prompt.reshape(Tp, D)
    xs = x_sample.reshape(Ts, D)
    mem2d = mem_prompt.reshape(B * n_mem, D)
    cache_k_rows = cache_mem_k.reshape(depth, NS, n_mem * N_MEM_HEADS, X_HEAD_DIM)
    cache_v_rows = cache_mem_v.reshape(depth, NS, n_mem * N_MEM_HEADS, X_HEAD_DIM)
    NBP = 2 if B % 2 == 0 else 1
    bias_p = swa_bias_rows(rel_bias, WINDOW, NBP)
    bias_s = swa_bias_rows(rel_bias, SWA_SHORT_TQ, SWA_SHORT_BATCH)
    outs_p = {k: [] for k in ("mem_k", "mem_v", "swa_k", "swa_v", "C", "n", "m", "h", "conv")}
    outs_s = {k: [] for k in ("swa_k", "swa_v", "C", "n", "m", "h", "conv")}

    slots = None
    for l in range(depth):
        w_in_b = _prep_w_in(w_in[l])
        g_mix = _row(norm_mix[l])
        b_i = jnp.pad(m_igate_b[l], (0, SUBLANE - H)).reshape(SUBLANE, 1)
        b_f = jnp.pad(m_fgate_b[l], (0, SUBLANE - H)).reshape(SUBLANE, 1)
        hn = _row(m_head_norm[l])
        r_args = (r_conv_w[l], _row(r_conv_b[l]), r_gate_a_w[l].astype(BF16), r_gate_x_w[l].astype(BF16),
                  _row(r_gate_a_b[l]), _row(r_gate_x_b[l]), _row(r_lambda[l]))
        sink_p = swa_sink_rows(swa_sinks[l], WINDOW, NBP)
        sink_s = swa_sink_rows(swa_sinks[l], SWA_SHORT_TQ, SWA_SHORT_BATCH)
        merge_w = (w_branch_m[l].astype(BF16), w_branch_r[l].astype(BF16), w_branch_s[l].astype(BF16),
                   w_out[l].astype(BF16))
        wq_b, wo_b = xq_w[l].astype(BF16), xo_w[l].astype(BF16)
        w_router = _prep_router(router_group_w[l], router_expert_w[l])
        last = l == depth - 1
        moe_args = (_row(norm_ffn[l]), w_router, moe_w_gate, moe_w_up, moe_w_down, l, _row(norm_final), last)

        w_kv = jnp.concatenate([xk_w[l], xv_w[l]], axis=1).astype(BF16)
        mk, mv = norm_matmul(mem2d, g_mix, w_kv, (X_WIDTH, X_WIDTH), normalize=False)
        outs_p["mem_k"].append(mk.reshape(B, n_mem, N_MEM_HEADS, X_HEAD_DIM))
        outs_p["mem_v"].append(mv.reshape(B, n_mem, N_MEM_HEADS, X_HEAD_DIM))
        mk, mv = mk.reshape(1, B, n_mem, X_WIDTH), mv.reshape(1, B, n_mem, X_WIDTH)

        p_m, p_if, p_r, p_s, p_g = norm_matmul(xp, g_mix, w_in_b, IN_SPLITS, out_dtypes=IN_DTYPES)
        gts = jnp.swapaxes(p_if[:, :2 * H].reshape(B, S, 2 * H), 1, 2)
        gi = jnp.pad(gts[:, :H], ((0, 0), (0, SUBLANE - H), (0, 0)))
        gf = jnp.pad(gts[:, H:], ((0, 0), (0, SUBLANE - H), (0, 0)))
        h_m, caug, m_o = mlstm(p_m.reshape(B, S, 4 * M_WIDTH), gi, gf, b_i, b_f, hn,
                               jnp.zeros((B, H, M_HEAD_DIM, MLSTM_AUG), F32),
                               jnp.zeros((B, SUBLANE, LANE), F32), L=MLSTM_CHUNK)
        outs_p["C"].append(caug[..., :M_HEAD_DIM])
        outs_p["n"].append(caug[..., M_HEAD_DIM])
        outs_p["m"].append(m_o[:, :H, 0])

        h_r, h_last, conv_last = rglru(p_r.reshape(B, S, 2 * R_WIDTH),
                                       jnp.zeros((B, CONV_W - 1, R_WIDTH), F32),
                                       jnp.zeros((B, 1, R_WIDTH), F32), *r_args,
                                       TR=_row_tile(S), Bs=1, out_dtype=BF16)
        outs_p["h"].append(h_last[:, 0])
        outs_p["conv"].append(conv_last)

        p_s3 = p_s.reshape(B, S, S_WIDTH + 2 * S_KV_WIDTH)
        kcol, vcol = S_WIDTH // S_KV_WIDTH, S_WIDTH // S_KV_WIDTH + 1
        prev = lambda col: (lambda b, n: (b, jnp.maximum(n - 1, 0), col))
        cur = lambda col: (lambda b, n: (b, n, col))
        kv_blk = (NBP, WINDOW, S_KV_WIDTH)
        h_s = swa((p_s3,) * 5, (cur(0), prev(kcol), cur(kcol), prev(vcol), cur(vcol)),
                  ((NBP, WINDOW, S_WIDTH), kv_blk, kv_blk, kv_blk, kv_blk),
                  bias_p, sink_p, grid=(B // NBP, S // WINDOW), NBS=NBP, TQ=WINDOW, mask_first=True,
                  out_dtype=BF16)
        outs_p["swa_k"].append(p_s3[:, S - WINDOW:, S_WIDTH:S_WIDTH + S_KV_WIDTH]
                               .reshape(B, WINDOW, S_KV_HEADS, S_HEAD_DIM))
        outs_p["swa_v"].append(p_s3[:, S - WINDOW:, S_WIDTH + S_KV_WIDTH:]
                               .reshape(B, WINDOW, S_KV_HEADS, S_HEAD_DIM))

        xp = merge_out(h_m.reshape(Tp, M_WIDTH), h_r.reshape(Tp, R_WIDTH), h_s.reshape(Tp, S_WIDTH),
                       p_g, xp, *merge_w)
        xp = xattn(xp.reshape(B, S, D), _row(norm_mem[l]), wq_b, wo_b, mk, mv, layer=0,
                   nb=1, tq=512 if S % 512 == 0 else _row_tile(S), interleaved=False).reshape(Tp, D)

        p_m, p_if, p_r, p_s, p_g = norm_matmul(xs, g_mix, w_in_b, IN_SPLITS, out_dtypes=IN_DTYPES)
        gts = jnp.swapaxes(p_if[:, :2 * H].reshape(NS, TS, 2 * H), 1, 2)
        gts = jnp.pad(gts, ((0, 0), (0, 0), (0, LANE - TS)))
        gi = jnp.pad(gts[:, :H], ((0, 0), (0, SUBLANE - H), (0, 0)))
        gf = jnp.pad(gts[:, H:], ((0, 0), (0, SUBLANE - H), (0, 0)))
        m0 = jnp.broadcast_to(jnp.pad(state_mlstm_m[l], ((0, 0), (0, SUBLANE - H)))[:, :, None],
                              (NS, SUBLANE, LANE))
        h_m, c_s, n_s, m_o = mlstm_short(_pad_rows(p_m.reshape(NS, TS, 4 * M_WIDTH), SUBLANE), gi, gf, b_i, b_f,
                                         hn, state_mlstm_C, state_mlstm_n, m0, layer=l,
                                         NBS=MLSTM_SHORT_BATCH, TS=TS)
        h_m = h_m[:, :TS].reshape(Ts, M_WIDTH)
        outs_s["C"].append(c_s)
        outs_s["n"].append(n_s)
        outs_s["m"].append(m_o[:, :H, 0])

        tmaj = lambda a: jnp.swapaxes(a, 0, 1).reshape(1, a.shape[0] * a.shape[1], a.shape[2])
        h_r, h_last, conv_last = rglru(tmaj(p_r.reshape(NS, TS, 2 * R_WIDTH)), tmaj(state_rglru_conv[l]),
                                       state_rglru_h[l].reshape(1, NS, R_WIDTH), *r_args,
                                       TR=TS * NS, Bs=NS, out_dtype=F32)
        h_r = jnp.swapaxes(h_r.reshape(TS, NS, R_WIDTH), 0, 1).reshape(Ts, R_WIDTH)
        outs_s["h"].append(h_last[0])
        outs_s["conv"].append(jnp.swapaxes(conv_last.reshape(CONV_W - 1, NS, R_WIDTH), 0, 1))

        TQ, NBS = SWA_SHORT_TQ, SWA_SHORT_BATCH
        p_s3 = p_s.reshape(NS, TS, S_WIDTH + 2 * S_KV_WIDTH)
        q_s = _pad_rows(p_s3[:, :, :S_WIDTH], TQ)
        k_new = p_s3[:, :, S_WIDTH:S_WIDTH + S_KV_WIDTH]
        v_new = p_s3[:, :, S_WIDTH + S_KV_WIDTH:]
        k_cache = cache_swa_k[l].reshape(NS, WINDOW, S_KV_WIDTH)
        v_cache = cache_swa_v[l].reshape(NS, WINDOW, S_KV_WIDTH)
        full = lambda b, n: (b, 0, 0)
        h_s = swa((q_s, k_cache, _pad_rows(k_new, TQ), v_cache, _pad_rows(v_new, TQ)), (full,) * 5,
                  ((NBS, TQ, S_WIDTH), (NBS, WINDOW, S_KV_WIDTH), (NBS, TQ, S_KV_WIDTH),
                   (NBS, WINDOW, S_KV_WIDTH), (NBS, TQ, S_KV_WIDTH)),
                  bias_s, sink_s, grid=(NS // NBS, 1), NBS=NBS, TQ=TQ, mask_first=False, out_dtype=F32)
        h_s = h_s[:, :TS].reshape(Ts, S_WIDTH)
        outs_s["swa_k"].append(jnp.concatenate([k_cache[:, TS:], k_new], axis=1)
                               .reshape(NS, WINDOW, S_KV_HEADS, S_HEAD_DIM))
        outs_s["swa_v"].append(jnp.concatenate([v_cache[:, TS:], v_new], axis=1)
                               .reshape(NS, WINDOW, S_KV_HEADS, S_HEAD_DIM))

        xs = merge_out(h_m, h_r, h_s, p_g, xs, *merge_w)
        xs8 = xattn(_pad_rows(xs.reshape(NS, TS, D), SUBLANE), _row(norm_mem[l]), wq_b, wo_b,
                    cache_k_rows, cache_v_rows, layer=l, nb=XATTN_SHORT_BATCH, tq=SUBLANE, interleaved=True)

        (xp, xs), slots = moe_layer([xp, xs8[:, :TS].reshape(Ts, D)], slots, *moe_args)

    st = lambda d, k: jnp.stack(d[k])
    return (xp.reshape(B, S, D), xs.reshape(NS, TS, D), st(outs_p, "mem_k"), st(outs_p, "mem_v"),
            st(outs_p, "swa_k"), st(outs_p, "swa_v"), st(outs_p, "C"), st(outs_p, "n"), st(outs_p, "m"),
            st(outs_p, "h"), st(outs_p, "conv"),
            st(outs_s, "swa_k"), st(outs_s, "swa_v"), st(outs_s, "C"), st(outs_s, "n"), st(outs_s, "m"),
            st(outs_s, "h"), st(outs_s, "conv"))
```

```python
import functools
import math

import jax
import jax.numpy as jnp
from jax import lax
from jax.experimental import pallas as pl
from jax.experimental.pallas import tpu as pltpu

F32 = jnp.float32
BF16 = jnp.bfloat16
NEG_INF = float("-inf")

LANE = 128
SUBLANE = 8
BF16_SUBLANE = 16
VMEM_LIMIT_BYTES = 56 * 1024 * 1024

D_MODEL = 1024
M_HEADS = 4
M_HEAD_DIM = 128
M_WIDTH = M_HEADS * M_HEAD_DIM
R_WIDTH = 512
R_BLOCKS = 4
R_BLOCK_DIM = R_WIDTH // R_BLOCKS
CONV_W = 4
LRU_C = 8.0
S_HEADS = 8
S_KV_HEADS = 2
S_HEAD_DIM = 64
S_WIDTH = S_HEADS * S_HEAD_DIM
S_KV_WIDTH = S_KV_HEADS * S_HEAD_DIM
WINDOW = 128
N_BUCKETS = 32
MAX_DISTANCE = 128
N_MEM_HEADS = 4
X_HEAD_DIM = 128
X_WIDTH = N_MEM_HEADS * X_HEAD_DIM
N_GROUPS = 4
EXPERTS_PER_GROUP = 8
N_EXPERTS = N_GROUPS * EXPERTS_PER_GROUP
TOP_K = 2
D_EXPERT = 512
EPS = 1e-6

MLSTM_CHUNK = 512
MLSTM_AUG = 2 * M_HEAD_DIM
MLSTM_SHORT_BATCH = 8
SWA_SHORT_TQ = BF16_SUBLANE
SWA_SHORT_BATCH = 8
XATTN_SHORT_BATCH = 8


def _cparams(*sem):
    return pltpu.CompilerParams(dimension_semantics=sem, vmem_limit_bytes=VMEM_LIMIT_BYTES)


def _rms(x, g):
    ms = jnp.mean(x * x, axis=-1, keepdims=True)
    return x * lax.rsqrt(ms + EPS) * g


def _sigmoid(x):
    return 1.0 / (1.0 + jnp.exp(-x))


def _softplus(x):
    return jnp.maximum(x, 0.0) + jnp.log1p(jnp.exp(-jnp.abs(x)))


def _log_sigmoid(x):
    return -_softplus(-x)


def _gelu_tanh(x):
    return 0.5 * x * (1.0 + jnp.tanh(math.sqrt(2.0 / math.pi) * (x + 0.044715 * (x * x * x))))


def _dot(a, b):
    return jnp.dot(a, b, preferred_element_type=F32)


def _dot_nt(a, b):
    return lax.dot_general(a, b, (((1,), (1,)), ((), ())), preferred_element_type=F32)


def _row_tile(n, pref=256):
    for t in (512, 256, 128, 64, 32, 16, 8):
        if t <= pref and n % t == 0:
            return t
    raise ValueError(f"row count {n} is not a multiple of {SUBLANE}")


def _norm_matmul_kernel(x_ref, g_ref, w_ref, *out_refs, splits, normalize):
    x = x_ref[...]
    if normalize:
        x = _rms(x, g_ref[...])
    xb = x.astype(BF16)
    off = 0
    for o_ref, n in zip(out_refs, splits):
        o_ref[...] = _dot(xb, w_ref[:, off:off + n]).astype(o_ref.dtype)
        off += n


def norm_matmul(x, g, w_bf16, splits, normalize=True, out_dtypes=None):
    out_dtypes = out_dtypes or (F32,) * len(splits)
    T, D = x.shape
    tm = _row_tile(T)
    N = w_bf16.shape[1]
    assert sum(splits) == N and all(s % LANE == 0 for s in splits)
    kern = functools.partial(_norm_matmul_kernel, splits=tuple(splits), normalize=normalize)
    return pl.pallas_call(
        kern,
        grid=(T // tm,),
        in_specs=[pl.BlockSpec((tm, D), lambda i: (i, 0)),
                  pl.BlockSpec((1, D), lambda i: (0, 0)),
                  pl.BlockSpec((D, N), lambda i: (0, 0))],
        out_specs=[pl.BlockSpec((tm, n), lambda i: (i, 0)) for n in splits],
        out_shape=[jax.ShapeDtypeStruct((T, n), dt) for n, dt in zip(splits, out_dtypes)],
        compiler_params=_cparams("parallel"),
        name="norm_matmul",
    )(x, g, w_bf16)


def _mlstm_kernel(q_ref, k_ref, v_ref, o_ref, gi_ref, gf_ref, bi_ref, bf_ref, hn_ref,
                  c0_ref, m0_ref, h_out, c_out, m_out, c_sc, m_sc, *, L, NBB):
    @pl.when(pl.program_id(1) == 0)
    def _():
        c_sc[...] = c0_ref[...]
        m_sc[...] = m0_ref[...]

    t_idx = lax.broadcasted_iota(jnp.int32, (L, L), 0)
    s_idx = lax.broadcasted_iota(jnp.int32, (L, L), 1)
    causal = t_idx >= s_idx
    eye = t_idx == s_idx
    row8 = lax.broadcasted_iota(jnp.int32, (SUBLANE, LANE), 0)
    ones_col = (lax.broadcasted_iota(jnp.int32, (L, M_HEAD_DIM), 1) == 0).astype(BF16)
    gi = [gi_ref[b] + bi_ref[...] for b in range(NBB)]
    gf = [_log_sigmoid(gf_ref[b] + bf_ref[...]) for b in range(NBB)]
    m_tiles = [m_sc[b] for b in range(NBB)]
    chains = [dict(b=b, h=h, hs=slice(h * M_HEAD_DIM, (h + 1) * M_HEAD_DIM))
              for b in range(NBB) for h in range(M_HEADS)]

    for c in chains:
        b, h = c["b"], c["h"]
        c["m_prev"] = m_tiles[b][h:h + 1, 0:1]
        c["b_col"] = jnp.sum(jnp.where(causal, gf[b][h:h + 1, :], 0.0), axis=1, keepdims=True)
    for c in chains:
        b, h = c["b"], c["h"]
        b_row = jnp.sum(jnp.where(eye, c["b_col"], 0.0), axis=0, keepdims=True)
        c["r_row"] = gi[b][h:h + 1, :] - b_row
        c["g_end"] = b_row[:, L - 1:L]
        c["log_in"] = jnp.where(causal, c["b_col"] + c["r_row"], NEG_INF)
        c["m_intra"] = jnp.max(c["log_in"], axis=1, keepdims=True)
    for c in chains:
        b, hs = c["b"], c["hs"]
        c["k"] = k_ref[b][:, hs] * (M_HEAD_DIM ** -0.5)
        c["caug"] = c_sc[b, c["h"]]
        c["vaug"] = jnp.concatenate([v_ref[b][:, hs].astype(BF16), ones_col], axis=1)
        qb = q_ref[b][:, hs].astype(BF16)
        c["qk"] = _dot_nt(qb, c["k"].astype(BF16))
        c["qc"] = _dot(qb, c["caug"].astype(BF16))
    for c in chains:
        log_prev = c["b_col"] + c["m_prev"]
        m_t = jnp.maximum(log_prev, c["m_intra"])
        sc = (c["qk"] * jnp.exp(c["log_in"] - m_t)).astype(BF16)
        nd = jnp.exp(log_prev - m_t) * c["qc"] + _dot(sc, c["vaug"])
        den = nd[:, M_HEAD_DIM:M_HEAD_DIM + 1]
        c["hh"] = nd[:, :M_HEAD_DIM] / jnp.maximum(jnp.abs(den), jnp.exp(-m_t))
    for c in chains:
        b, hs = c["b"], c["hs"]
        d = c["hh"] - jnp.mean(c["hh"], axis=-1, keepdims=True)
        y = d * lax.rsqrt(jnp.mean(d * d, axis=-1, keepdims=True) + EPS) * hn_ref[:, hs]
        h_out[b, :, hs] = (y * _sigmoid(o_ref[b][:, hs])).astype(h_out.dtype)
    for c in chains:
        b, h = c["b"], c["h"]
        lie = c["g_end"] + c["r_row"]
        m_end = jnp.maximum(c["g_end"] + c["m_prev"], jnp.max(lie, axis=1, keepdims=True))
        w_pe = jnp.exp(c["g_end"] + c["m_prev"] - m_end)
        ktw = (c["k"].T * jnp.exp(lie - m_end)).astype(BF16)
        c_new = w_pe * c["caug"] + _dot(ktw, c["vaug"])
        c_sc[b, h] = c_new
        c_out[b, h] = c_new
        m_tiles[b] = jnp.where(row8 == h, m_end, m_tiles[b])
    for b in range(NBB):
        m_sc[b] = m_tiles[b]
        m_out[b] = m_tiles[b]


def mlstm(p_m, gates_i, gates_f, b_i, b_f, head_norm, c0aug, m0, L):
    NB, R, _ = p_m.shape
    nc = R // L
    assert R % L == 0
    NBB = 2 if NB % 2 == 0 else 1
    blk = lambda j: pl.BlockSpec((NBB, L, M_WIDTH), lambda b, c, j=j: (b, c, j))
    state_spec = pl.BlockSpec((NBB, M_HEADS, M_HEAD_DIM, MLSTM_AUG), lambda b, c: (b, 0, 0, 0))
    m_spec = pl.BlockSpec((NBB, SUBLANE, LANE), lambda b, c: (b, 0, 0))
    gate_spec = pl.BlockSpec((NBB, SUBLANE, L), lambda b, c: (b, 0, c))
    kern = functools.partial(_mlstm_kernel, L=L, NBB=NBB)
    scratch = [pltpu.VMEM((NBB, M_HEADS, M_HEAD_DIM, MLSTM_AUG), F32), pltpu.VMEM((NBB, SUBLANE, LANE), F32)]
    return pl.pallas_call(
        kern,
        grid=(NB // NBB, nc),
        in_specs=[blk(0), blk(1), blk(2), blk(3), gate_spec, gate_spec,
                  pl.BlockSpec((SUBLANE, 1), lambda b, c: (0, 0)),
                  pl.BlockSpec((SUBLANE, 1), lambda b, c: (0, 0)),
                  pl.BlockSpec((1, M_WIDTH), lambda b, c: (0, 0)),
                  state_spec, m_spec],
        out_specs=[pl.BlockSpec((NBB, L, M_WIDTH), lambda b, c: (b, c, 0)), state_spec, m_spec],
        out_shape=[jax.ShapeDtypeStruct((NB, R, M_WIDTH), BF16),
                   jax.ShapeDtypeStruct((NB, M_HEADS, M_HEAD_DIM, MLSTM_AUG), F32),
                   jax.ShapeDtypeStruct((NB, SUBLANE, LANE), F32)],
        scratch_shapes=scratch,
        compiler_params=_cparams("parallel", "arbitrary"),
        name="mlstm",
    )(p_m, p_m, p_m, p_m, gates_i, gates_f, b_i, b_f, head_norm, c0aug, m0)


def _mlstm_short_kernel(q_ref, k_ref, v_ref, o_ref, gi_ref, gf_ref, bi_ref, bf_ref, hn_ref,
                        c0_ref, n0_ref, m0_ref, h_out, c_out, n_out, m_out, *, NBS, TS):
    L = SUBLANE
    t_idx = lax.broadcasted_iota(jnp.int32, (L, L), 0)
    s_idx = lax.broadcasted_iota(jnp.int32, (L, L), 1)
    causal = t_idx >= s_idx
    eye = t_idx == s_idx
    valid = lax.broadcasted_iota(jnp.int32, (1, L), 1) < TS
    row8 = lax.broadcasted_iota(jnp.int32, (SUBLANE, LANE), 0)
    gi = [gi_ref[b] + bi_ref[...] for b in range(NBS)]
    gf = [_log_sigmoid(gf_ref[b] + bf_ref[...]) for b in range(NBS)]
    m_tiles = [m0_ref[b] for b in range(NBS)]
    chains = [dict(b=b, h=h, hs=slice(h * M_HEAD_DIM, (h + 1) * M_HEAD_DIM))
              for b in range(NBS) for h in range(M_HEADS)]

    for c in chains:
        b, h = c["b"], c["h"]
        c["li"] = jnp.where(valid, gi[b][h:h + 1, 0:L], NEG_INF)
        lf = jnp.where(valid, gf[b][h:h + 1, 0:L], 0.0)
        c["m_prev"] = m_tiles[b][h:h + 1, 0:1]
        c["b_col"] = jnp.sum(jnp.where(causal, lf, 0.0), axis=1, keepdims=True)
    for c in chains:
        b_row = jnp.sum(jnp.where(eye, c["b_col"], 0.0), axis=0, keepdims=True)
        c["r_row"] = c["li"] - b_row
        c["g_end"] = b_row[:, L - 1:L]
        c["log_in"] = jnp.where(causal, c["b_col"] + c["r_row"], NEG_INF)
        c["m_intra"] = jnp.max(c["log_in"], axis=1, keepdims=True)
    for c in chains:
        b, hs = c["b"], c["hs"]
        c["q"] = q_ref[b][:, hs]
        c["k"] = k_ref[b][:, hs] * (M_HEAD_DIM ** -0.5)
        c["vb"] = v_ref[b][:, hs].astype(BF16)
        c["c0"] = c0_ref[0, b, c["h"]]
        c["n0"] = n0_ref[0, b, c["h"]:c["h"] + 1, :]
        qb = c["q"].astype(BF16)
        c["qk"] = _dot_nt(qb, c["k"].astype(BF16))
        c["qc"] = _dot(qb, c["c0"].astype(BF16))
        c["qn"] = jnp.sum(c["q"] * c["n0"], axis=1, keepdims=True)
    for c in chains:
        log_prev = c["b_col"] + c["m_prev"]
        m_t = jnp.maximum(log_prev, c["m_intra"])
        w_prev = jnp.exp(log_prev - m_t)
        sc = c["qk"] * jnp.exp(c["log_in"] - m_t)
        num = w_prev * c["qc"] + _dot(sc.astype(BF16), c["vb"])
        den = w_prev * c["qn"] + jnp.sum(sc, axis=1, keepdims=True)
        c["hh"] = num / jnp.maximum(jnp.abs(den), jnp.exp(-m_t))
    for c in chains:
        b, hs = c["b"], c["hs"]
        d = c["hh"] - jnp.mean(c["hh"], axis=-1, keepdims=True)
        y = d * lax.rsqrt(jnp.mean(d * d, axis=-1, keepdims=True) + EPS) * hn_ref[:, hs]
        h_out[b, :, hs] = y * _sigmoid(o_ref[b][:, hs])
    for c in chains:
        b, h = c["b"], c["h"]
        lie = c["g_end"] + c["r_row"]
        m_end = jnp.maximum(c["g_end"] + c["m_prev"], jnp.max(lie, axis=1, keepdims=True))
        w_pe = jnp.exp(c["g_end"] + c["m_prev"] - m_end)
        w_e = jnp.exp(lie - m_end)
        w_e_col = jnp.sum(jnp.where(eye, w_e, 0.0), axis=1, keepdims=True)
        kw = c["k"] * w_e_col
        c_out[b, h] = w_pe * c["c0"] + lax.dot_general(kw.astype(BF16), c["vb"], (((0,), (0,)), ((), ())),
                                                       preferred_element_type=F32)
        n_out[b, h:h + 1, :] = w_pe * c["n0"] + jnp.sum(kw, axis=0, keepdims=True)
        m_tiles[b] = jnp.where(row8 == h, m_end, m_tiles[b])
    for b in range(NBS):
        m_out[b] = m_tiles[b]


def mlstm_short(p_m, gates_i, gates_f, b_i, b_f, head_norm, c0, n0, m0, layer, NBS, TS):
    NB = p_m.shape[0]
    H, DH = M_HEADS, M_HEAD_DIM
    assert NB % NBS == 0 and TS <= SUBLANE
    blk = lambda j: pl.BlockSpec((NBS, SUBLANE, M_WIDTH), lambda i, j=j: (i, 0, j))
    tile = pl.BlockSpec((NBS, SUBLANE, LANE), lambda i: (i, 0, 0))
    kern = functools.partial(_mlstm_short_kernel, NBS=NBS, TS=TS)
    return pl.pallas_call(
        kern,
        grid=(NB // NBS,),
        in_specs=[blk(0), blk(1), blk(2), blk(3), tile, tile,
                  pl.BlockSpec((SUBLANE, 1), lambda i: (0, 0)),
                  pl.BlockSpec((SUBLANE, 1), lambda i: (0, 0)),
                  pl.BlockSpec((1, M_WIDTH), lambda i: (0, 0)),
                  pl.BlockSpec((1, NBS, H, DH, DH), lambda i: (layer, i, 0, 0, 0)),
                  pl.BlockSpec((1, NBS, H, DH), lambda i: (layer, i, 0, 0)),
                  tile],
        out_specs=[pl.BlockSpec((NBS, SUBLANE, M_WIDTH), lambda i: (i, 0, 0)),
                   pl.BlockSpec((NBS, H, DH, DH), lambda i: (i, 0, 0, 0)),
                   pl.BlockSpec((NBS, H, DH), lambda i: (i, 0, 0)),
                   tile],
        out_shape=[jax.ShapeDtypeStruct((NB, SUBLANE, M_WIDTH), F32),
                   jax.ShapeDtypeStruct((NB, H, DH, DH), F32),
                   jax.ShapeDtypeStruct((NB, H, DH), F32),
                   jax.ShapeDtypeStruct((NB, SUBLANE, LANE), F32)],
        compiler_params=_cparams("parallel"),
        name="mlstm_short",
    )(p_m, p_m, p_m, p_m, gates_i, gates_f, b_i, b_f, head_norm, c0, n0, m0)


def _rglru_kernel(x_ref, g_ref, conv0_ref, h0_ref, cw_ref, cb_ref, wa_ref, wx_ref, ba_ref, bx_ref,
                  lam_ref, y_out, hlast_out, conv_out, xpad_sc, hc_sc, *, TR, Bs):
    CB = (CONV_W - 1) * Bs
    X0 = -(-CB // SUBLANE) * SUBLANE

    @pl.when(pl.program_id(1) == 0)
    def _():
        xpad_sc[X0 - CB:X0, :] = conv0_ref[0]
        hc_sc[...] = h0_ref[0]

    xpad_sc[X0:X0 + TR, :] = x_ref[0]
    y = cb_ref[...]
    for j in range(CONV_W):
        y = y + xpad_sc[X0 - CB + j * Bs:X0 - CB + j * Bs + TR, :] * cw_ref[j:j + 1, :]
    tail = xpad_sc[X0 + TR - CB:X0 + TR, :]
    conv_out[0] = tail
    xpad_sc[X0 - CB:X0, :] = tail

    yb = y.astype(BF16)
    rs, is_ = [], []
    for n in range(R_BLOCKS):
        sl = slice(n * R_BLOCK_DIM, (n + 1) * R_BLOCK_DIM)
        rs.append(_dot(yb[:, sl], wa_ref[n]))
        is_.append(_dot(yb[:, sl], wx_ref[n]))
    r = _sigmoid(jnp.concatenate(rs, axis=1) + ba_ref[...])
    i = _sigmoid(jnp.concatenate(is_, axis=1) + bx_ref[...])
    log_a = -LRU_C * r * _softplus(-lam_ref[...])
    a_cum = jnp.exp(log_a)
    t = jnp.tanh(log_a)
    u_cum = jnp.sqrt(-2.0 * t / (1.0 - t)) * (i * y)

    def scan_rows(a, u, axis, stride):
        n = a.shape[axis]
        idx = lax.broadcasted_iota(jnp.int32, a.shape, axis)
        d = stride
        while d < n:
            keep = idx >= d
            u = jnp.where(keep, a * pltpu.roll(u, d, axis=axis) + u, u)
            a = jnp.where(keep, a * pltpu.roll(a, d, axis=axis), a)
            d *= 2
        return a, u

    hc = hc_sc[...]
    if Bs == 1:
        NG = TR // SUBLANE
        a3, u3 = scan_rows(a_cum.reshape(NG, SUBLANE, R_WIDTH), u_cum.reshape(NG, SUBLANE, R_WIDTH), 1, 1)
        a_tot, u_tot = scan_rows(a3[:, SUBLANE - 1, :], u3[:, SUBLANE - 1, :], 0, 1)
        h_end = u_tot + a_tot * hc
        g_idx = lax.broadcasted_iota(jnp.int32, (NG, R_WIDTH), 0)
        h_start = jnp.where(g_idx == 0, hc, pltpu.roll(h_end, 1, axis=0))
        h = (u3 + a3 * h_start[:, None, :]).reshape(TR, R_WIDTH)
    else:
        a_cum, u_cum = scan_rows(a_cum, u_cum, 0, Bs)
        h = u_cum + a_cum * jnp.concatenate([hc] * (TR // Bs), axis=0)
    y_out[0] = (h * _gelu_tanh(g_ref[0])).astype(y_out.dtype)
    h_last = h[TR - Bs:, :]
    hc_sc[...] = h_last
    hlast_out[0] = h_last


def rglru(p_r, conv0, h0, cw, cb, wa, wx, ba, bx, lam, TR, Bs, out_dtype):
    G, R, _ = p_r.shape
    CB = (CONV_W - 1) * Bs
    X0 = -(-CB // SUBLANE) * SUBLANE
    assert R % TR == 0 and TR % Bs == 0 and TR >= CB
    W = R_WIDTH
    const = lambda shape: pl.BlockSpec(shape, lambda g, r: (0,) * len(shape))
    kern = functools.partial(_rglru_kernel, TR=TR, Bs=Bs)
    return pl.pallas_call(
        kern,
        grid=(G, R // TR),
        in_specs=[pl.BlockSpec((1, TR, W), lambda g, r: (g, r, 0)),
                  pl.BlockSpec((1, TR, W), lambda g, r: (g, r, 1)),
                  pl.BlockSpec((1, CB, W), lambda g, r: (g, 0, 0)),
                  pl.BlockSpec((1, Bs, W), lambda g, r: (g, 0, 0)),
                  const((CONV_W, W)), const((1, W)),
                  const((R_BLOCKS, R_BLOCK_DIM, R_BLOCK_DIM)), const((R_BLOCKS, R_BLOCK_DIM, R_BLOCK_DIM)),
                  const((1, W)), const((1, W)), const((1, W))],
        out_specs=[pl.BlockSpec((1, TR, W), lambda g, r: (g, r, 0)),
                   pl.BlockSpec((1, Bs, W), lambda g, r: (g, 0, 0)),
                   pl.BlockSpec((1, CB, W), lambda g, r: (g, 0, 0))],
        out_shape=[jax.ShapeDtypeStruct((G, R, W), out_dtype),
                   jax.ShapeDtypeStruct((G, Bs, W), F32),
                   jax.ShapeDtypeStruct((G, CB, W), F32)],
        scratch_shapes=[pltpu.VMEM((X0 + TR, W), F32), pltpu.VMEM((Bs, W), F32)],
        compiler_params=_cparams("parallel", "arbitrary"),
        name="rglru",
    )(p_r, p_r, conv0, h0, cw, cb, wa, wx, ba, bx, lam)


def _swa_kernel(q_ref, kp_ref, kc_ref, vp_ref, vc_ref, bias_ref, sink_ref, o_ref, *, NBS, TQ, mask_first):
    GQ = S_HEADS // S_KV_HEADS
    lane = lax.broadcasted_iota(jnp.int32, (1, LANE), 1)
    lo = lane < S_HEAD_DIM

    def both_halves(x, j):
        xr = pltpu.roll(x, S_HEAD_DIM, axis=1)
        return (jnp.where(lo, x, xr) if j == 0 else jnp.where(lo, xr, x)).astype(BF16)

    s1_parts, s2_parts, v_parts = [], [], []
    for b in range(NBS):
        q = q_ref[b] * (S_HEAD_DIM ** -0.5)
        for j in range(S_KV_HEADS):
            stack = []
            for g in range(GQ):
                h = GQ * j + g
                qh = q[:, (h // 2) * LANE:(h // 2 + 1) * LANE]
                stack.append(jnp.where(lo if h % 2 == 0 else jnp.logical_not(lo), qh, 0.0))
            qs = jnp.concatenate(stack, axis=0).astype(BF16)
            s1_parts.append(_dot_nt(qs, both_halves(kp_ref[b], j)))
            s2_parts.append(_dot_nt(qs, both_halves(kc_ref[b], j)))
            v_parts.append((both_halves(vp_ref[b], j), both_halves(vc_ref[b], j)))
    s1 = jnp.concatenate(s1_parts, axis=0) + bias_ref[:, 0:WINDOW]
    s2 = jnp.concatenate(s2_parts, axis=0) + bias_ref[:, WINDOW:WINDOW + TQ]
    if mask_first:
        s1 = jnp.where(pl.program_id(1) == 0, NEG_INF, s1)
    sink = sink_ref[...]
    mx = jnp.maximum(jnp.maximum(jnp.max(s1, axis=1, keepdims=True), jnp.max(s2, axis=1, keepdims=True)), sink)
    p1 = jnp.exp(s1 - mx)
    p2 = jnp.exp(s2 - mx)
    inv = 1.0 / (jnp.sum(p1, axis=1, keepdims=True) + jnp.sum(p2, axis=1, keepdims=True) + jnp.exp(sink - mx))
    p1 = p1.astype(BF16)
    p2 = p2.astype(BF16)
    R = GQ * TQ
    for b in range(NBS):
        for j in range(S_KV_HEADS):
            n = b * S_KV_HEADS + j
            rows = slice(n * R, (n + 1) * R)
            vp, vc = v_parts[n]
            o = (_dot(p1[rows], vp) + _dot(p2[rows], vc)) * inv[rows]
            for pair in range(GQ // 2):
                even = o[(2 * pair) * TQ:(2 * pair + 1) * TQ]
                odd = o[(2 * pair + 1) * TQ:(2 * pair + 2) * TQ]
                blk = (GQ * j) // 2 + pair
                o_ref[b, :, blk * LANE:(blk + 1) * LANE] = jnp.where(lo, even, odd).astype(o_ref.dtype)


def swa(arrs, maps, shapes, bias_rows, sink_rows, grid, NBS, TQ, mask_first, out_dtype):
    kern = functools.partial(_swa_kernel, NBS=NBS, TQ=TQ, mask_first=mask_first)
    in_specs = [pl.BlockSpec(s, m) for s, m in zip(shapes, maps)]
    in_specs += [pl.BlockSpec(bias_rows.shape, lambda b, n: (0, 0)),
                 pl.BlockSpec(sink_rows.shape, lambda b, n: (0, 0))]
    return pl.pallas_call(
        kern,
        grid=grid,
        in_specs=in_specs,
        out_specs=pl.BlockSpec((NBS, TQ, S_WIDTH), lambda b, n: (b, n, 0)),
        out_shape=jax.ShapeDtypeStruct((grid[0] * NBS, grid[1] * TQ, S_WIDTH), out_dtype),
        compiler_params=_cparams("parallel", "arbitrary"),
        name="swa",
    )(*arrs, bias_rows, sink_rows)


def t5_bucket(dist):
    max_exact = N_BUCKETS // 2
    d = jnp.maximum(dist, 0)
    large = max_exact + (jnp.log(jnp.maximum(d, 1).astype(F32) / max_exact)
                         / math.log(MAX_DISTANCE / max_exact) * (N_BUCKETS - max_exact)).astype(jnp.int32)
    return jnp.where(d < max_exact, d, jnp.minimum(large, N_BUCKETS - 1))


def swa_bias_rows(rel_bias, TQ, NBS):
    qi = jnp.arange(TQ)[:, None]
    kj = jnp.arange(WINDOW + TQ)[None, :]
    dist = qi + WINDOW - kj
    onehot = (t5_bucket(dist)[..., None] == jnp.arange(N_BUCKETS)).astype(F32)
    b = jnp.einsum("qkn,nh->hqk", onehot, rel_bias.astype(F32), precision=lax.Precision.HIGHEST)
    b = jnp.where((dist >= 0) & (dist <= WINDOW), b, NEG_INF).reshape(S_HEADS * TQ, WINDOW + TQ)
    return jnp.tile(b, (NBS, 1))


def swa_sink_rows(sinks, TQ, NBS):
    return jnp.tile(jnp.repeat(sinks.astype(F32), TQ), NBS).reshape(-1, 1)


def _merge_kernel(hm, hr, hs, gm, gr, gs, x, wm, wr, ws, wo, out):
    def branch(h_ref, g_ref, w_ref):
        return _sigmoid(g_ref[...].astype(F32)) * _dot(h_ref[...].astype(BF16), w_ref[...])
    merged = branch(hm, gm, wm) + branch(hr, gr, wr) + branch(hs, gs, ws)
    out[...] = x[...] + _dot(merged.astype(BF16), wo[...])


def merge_out(hm, hr, hs, p_g, x, wm, wr, ws, wo):
    T, D = x.shape
    tm = _row_tile(T, 512)
    hspec = lambda w: pl.BlockSpec((tm, w), lambda i: (i, 0))
    gspec = lambda j: pl.BlockSpec((tm, D), lambda i, j=j: (i, j))
    wspec = lambda a: pl.BlockSpec(a.shape, lambda i: (0, 0))
    return pl.pallas_call(
        _merge_kernel,
        grid=(T // tm,),
        in_specs=[hspec(M_WIDTH), hspec(R_WIDTH), hspec(S_WIDTH), gspec(0), gspec(1), gspec(2),
                  hspec(D), wspec(wm), wspec(wr), wspec(ws), wspec(wo)],
        out_specs=hspec(D),
        out_shape=jax.ShapeDtypeStruct((T, D), F32),
        compiler_params=_cparams("parallel"),
        name="merge_out",
    )(hm, hr, hs, p_g, p_g, p_g, x, wm, wr, ws, wo)


def _xattn_kernel(x_ref, g_ref, wq_ref, wo_ref, mk_ref, mv_ref, out_ref, *, nb, tq, interleaved):
    NH = N_MEM_HEADS
    x = x_ref[...].reshape(nb * tq, D_MODEL)
    q = _dot(_rms(x, g_ref[...]).astype(BF16), wq_ref[...])
    head = lambda a, h: a[:, h * X_HEAD_DIM:(h + 1) * X_HEAD_DIM]
    parts = []
    for b in range(nb):
        qb = q[b * tq:(b + 1) * tq]
        if interleaved:
            qs = jnp.concatenate([head(qb, h) for h in range(NH)], axis=0).astype(BF16)
            parts.append(_dot_nt(qs, mk_ref[0, b].astype(BF16)))
        else:
            parts += [_dot_nt(head(qb, h).astype(BF16), head(mk_ref[0, b], h).astype(BF16)) for h in range(NH)]
    s = jnp.concatenate(parts, axis=0) * (X_HEAD_DIM ** -0.5)
    if interleaved:
        row_h = (lax.broadcasted_iota(jnp.int32, s.shape, 0) // tq) % NH
        s = jnp.where(lax.broadcasted_iota(jnp.int32, s.shape, 1) % NH == row_h, s, NEG_INF)
    p = jnp.exp(s - jnp.max(s, axis=1, keepdims=True))
    inv = 1.0 / jnp.sum(p, axis=1, keepdims=True)
    rows = []
    for b in range(nb):
        if interleaved:
            r = slice(b * NH * tq, (b + 1) * NH * tq)
            o = _dot(p[r].astype(BF16), mv_ref[0, b].astype(BF16)) * inv[r]
            heads = [o[h * tq:(h + 1) * tq] for h in range(NH)]
        else:
            heads = []
            for h in range(NH):
                r = slice((b * NH + h) * tq, (b * NH + h + 1) * tq)
                heads.append(_dot(p[r].astype(BF16), head(mv_ref[0, b], h).astype(BF16)) * inv[r])
        rows.append(jnp.concatenate(heads, axis=1))
    o_all = rows[0] if nb == 1 else jnp.concatenate(rows, axis=0)
    y = x + _dot(o_all.astype(BF16), wo_ref[...])
    out_ref[...] = y.reshape(nb, tq, D_MODEL)


def xattn(x, g, wq, wo, mem_k, mem_v, layer, nb, tq, interleaved):
    B, R, D = x.shape
    mem_blk = (1, nb) + mem_k.shape[2:]
    kern = functools.partial(_xattn_kernel, nb=nb, tq=tq, interleaved=interleaved)
    return pl.pallas_call(
        kern,
        grid=(B // nb, R // tq),
        in_specs=[pl.BlockSpec((nb, tq, D), lambda b, r: (b, r, 0)),
                  pl.BlockSpec((1, D), lambda b, r: (0, 0)),
                  pl.BlockSpec(wq.shape, lambda b, r: (0, 0)),
                  pl.BlockSpec(wo.shape, lambda b, r: (0, 0)),
                  pl.BlockSpec(mem_blk, lambda b, r: (layer, b, 0, 0)),
                  pl.BlockSpec(mem_blk, lambda b, r: (layer, b, 0, 0))],
        out_specs=pl.BlockSpec((nb, tq, D), lambda b, r: (b, r, 0)),
        out_shape=jax.ShapeDtypeStruct((B, R, D), F32),
        compiler_params=_cparams("parallel", "arbitrary"),
        name="xattn",
    )(x, g, wq, wo, mem_k, mem_v)


def _router_kernel(x_ref, g_ref, wr_ref, route_out, route_t_out, counts_out, cnt_sc):
    @pl.when(pl.program_id(0) == 0)
    def _():
        cnt_sc[...] = jnp.zeros_like(cnt_sc)

    xn = _rms(x_ref[...], g_ref[...])
    logits = _dot(xn.astype(BF16), wr_ref[...])
    tm = logits.shape[0]
    lane_i = lax.broadcasted_iota(jnp.int32, (tm, LANE), 1)
    lane = lane_i.astype(F32)
    lane_group = (lane_i // EXPERTS_PER_GROUP).astype(F32)

    def top1(v):
        mx = jnp.max(v, axis=1, keepdims=True)
        return mx, jnp.min(jnp.where(v == mx, lane, float(LANE)), axis=1, keepdims=True)

    gl = jnp.where(lane_i < N_GROUPS, logits[:, :LANE], NEG_INF)
    g_max, g_idx = top1(gl)
    g_w = 1.0 / jnp.sum(jnp.exp(gl - g_max), axis=1, keepdims=True)
    el = jnp.where(lane_group == g_idx, logits[:, LANE:], NEG_INF)
    e1, i1 = top1(el)
    e2, i2 = top1(jnp.where(lane == i1, NEG_INF, el))
    t = jnp.exp(e2 - e1)
    p1 = 1.0 / (1.0 + t)

    oh1 = (lane == i1).astype(F32)
    oh2 = (lane == i2).astype(F32)
    oh = oh1 + oh2
    r_idx = lax.broadcasted_iota(jnp.int32, (tm, tm), 0)
    c_idx = lax.broadcasted_iota(jnp.int32, (tm, tm), 1)
    before = (r_idx > c_idx).astype(BF16)
    base = cnt_sc[0:1, :] + _dot(before, oh.astype(BF16))
    rank1 = jnp.sum(oh1 * base, axis=1, keepdims=True)
    rank2 = jnp.sum(oh2 * base, axis=1, keepdims=True)
    cnt = cnt_sc[...] + jnp.sum(oh, axis=0, keepdims=True)
    cnt_sc[...] = cnt
    counts_out[...] = cnt

    vals = (i1, i2, p1 * g_w, t * p1 * g_w, rank1, rank2)
    route = jnp.zeros((tm, LANE), F32)
    for n, v in enumerate(vals):
        route = jnp.where(lane_i == n, v, route)
    route_out[...] = route
    route_t_out[...] = route.T[0:SUBLANE, :]


def router(x, g, w_router):
    T, D = x.shape
    tm = _row_tile(T)
    return pl.pallas_call(
        _router_kernel,
        grid=(T // tm,),
        in_specs=[pl.BlockSpec((tm, D), lambda i: (i, 0)),
                  pl.BlockSpec((1, D), lambda i: (0, 0)),
                  pl.BlockSpec(w_router.shape, lambda i: (0, 0))],
        out_specs=[pl.BlockSpec((tm, LANE), lambda i: (i, 0)), pl.BlockSpec((SUBLANE, tm), lambda i: (0, i)),
                   pl.BlockSpec((SUBLANE, LANE), lambda i: (0, 0))],
        out_shape=[jax.ShapeDtypeStruct((T, LANE), F32), jax.ShapeDtypeStruct((SUBLANE, T), F32),
                   jax.ShapeDtypeStruct((SUBLANE, LANE), F32)],
        scratch_shapes=[pltpu.VMEM((SUBLANE, LANE), F32)],
        compiler_params=_cparams("arbitrary"),
        name="router",
    )(x, g, w_router)


def moe_plan(routes, counts, TB, tms):
    experts = jnp.arange(N_EXPERTS, dtype=jnp.int32)
    counts = [c[0, :N_EXPERTS].astype(jnp.int32) for c in counts]
    total = sum(counts)
    padded = (total + TB - 1) // TB * TB
    pad_ends = jnp.cumsum(padded)
    base = pad_ends - padded
    dests = []
    for route_t, cnt, tm in zip(routes, counts, tms):
        T = route_t.shape[1]
        e_idx = route_t[0:TOP_K].astype(jnp.int32)
        rank = route_t[4:4 + TOP_K].astype(jnp.int32)
        dest = jnp.sum(jnp.where(e_idx[None] == experts[:, None, None], base[:, None, None], 0), axis=0) + rank
        dests.append(dest.reshape(TOP_K, T // tm, tm).transpose(1, 0, 2).reshape(T // tm, 1, TOP_K * tm))
        base = base + cnt
    n_tokens = sum(r.shape[1] for r in routes)
    n_blocks = -(-(n_tokens * TOP_K + N_EXPERTS * (TB - 1)) // TB)
    block_start = jnp.arange(n_blocks, dtype=jnp.int32) * TB
    block_e = jnp.minimum(jnp.sum(pad_ends[None, :] <= block_start[:, None], axis=1), N_EXPERTS - 1).astype(jnp.int32)
    end_valid = jnp.sum(jnp.where(block_e[:, None] == experts, pad_ends - padded + total, 0), axis=1)
    n_valid = jnp.clip(end_valid - block_start, 0, TB).astype(jnp.int32)
    return block_e, n_valid, dests


DMA_UNROLL = 8


def _for_row_chunks(n_rows, fn):
    def body(c, carry):
        for u in range(DMA_UNROLL):
            fn(c * DMA_UNROLL + u)
        return carry
    lax.fori_loop(0, n_rows // DMA_UNROLL, body, 0)


def _pack_bf16_pair(x):
    n = x.shape[1] // 2
    bits = lambda a: lax.bitcast_convert_type(a.astype(BF16).astype(F32), jnp.uint32)
    return (bits(x[:, :n]) >> 16) | (bits(x[:, n:]) & jnp.uint32(0xFFFF0000))


def _unpack_bf16_pair(p):
    lo = lax.bitcast_convert_type(p << 16, F32).astype(BF16)
    hi = lax.bitcast_convert_type(p & jnp.uint32(0xFFFF0000), F32).astype(BF16)
    return lo, hi


def _dispatch_kernel(dst_ref, x_ref, g_ref, init_hbm, xs_hbm, buf, sem, *, tm, n_tiles):
    del init_hbm
    i = pl.program_id(0)
    slot = i % 2

    def copy(row, s, row_dst):
        return pltpu.make_async_copy(buf.at[s, pl.ds(row, 1)], xs_hbm.at[pl.ds(row_dst, 1)], sem.at[s])

    def wait_slot(s):
        def wait_row(r):
            for _ in range(TOP_K):
                copy(r, s, 0).wait()
        _for_row_chunks(tm, wait_row)

    @pl.when(i >= 2)
    def _():
        wait_slot(slot)

    buf[slot] = _pack_bf16_pair(_rms(x_ref[...], g_ref[...]))

    def start_row(r):
        for k in range(TOP_K):
            copy(r, slot, dst_ref[0, 0, k * tm + r]).start(priority=k % 2)
    _for_row_chunks(tm, start_row)

    @pl.when(i == n_tiles - 1)
    def _():
        wait_slot(slot)
        if n_tiles >= 2:
            wait_slot(1 - slot)


def moe_dispatch(x, g, dest, xs_init, tm):
    T, D = x.shape
    n_tiles = T // tm
    assert tm % DMA_UNROLL == 0
    kern = functools.partial(_dispatch_kernel, tm=tm, n_tiles=n_tiles)
    return pl.pallas_call(
        kern,
        grid=(n_tiles,),
        in_specs=[pl.BlockSpec((1, 1, TOP_K * tm), lambda i: (i, 0, 0), memory_space=pltpu.SMEM),
                  pl.BlockSpec((tm, D), lambda i: (i, 0)),
                  pl.BlockSpec((1, D), lambda i: (0, 0)),
                  pl.BlockSpec(memory_space=pl.ANY)],
        out_specs=pl.BlockSpec(memory_space=pl.ANY),
        out_shape=jax.ShapeDtypeStruct(xs_init.shape, xs_init.dtype),
        input_output_aliases={3: 0},
        scratch_shapes=[pltpu.VMEM((2, tm, D // 2), jnp.uint32), pltpu.SemaphoreType.DMA((2,))],
        compiler_params=_cparams("arbitrary"),
        name="moe_dispatch",
    )(dest, x, g, xs_init)


def _ffn_kernel(be_ref, nv_ref, xs_ref, wg_ref, wu_ref, wd_ref, ys_ref, wg_bf, wu_bf, wd_bf):
    i = pl.program_id(0)
    nv = nv_ref[i]

    @pl.when((i == 0) | (be_ref[i] != be_ref[jnp.maximum(i - 1, 0)]))
    def _():
        wg_bf[...] = wg_ref[0, 0].astype(BF16)
        wu_bf[...] = wu_ref[0, 0].astype(BF16)
        wd_bf[...] = wd_ref[0, 0].astype(BF16)

    @pl.when(nv > 0)
    def _():
        row = lax.broadcasted_iota(jnp.int32, (xs_ref.shape[0], 1), 0)
        packed = jnp.where(row < nv, xs_ref[...], jnp.uint32(0))
        x_lo, x_hi = _unpack_bf16_pair(packed)
        half = x_lo.shape[1]
        g = _dot(x_lo, wg_bf[:half, :]) + _dot(x_hi, wg_bf[half:, :])
        u = _dot(x_lo, wu_bf[:half, :]) + _dot(x_hi, wu_bf[half:, :])
        hid = (g * _sigmoid(g)) * u
        ys_ref[...] = _dot(hid.astype(BF16), wd_bf[...])

    @pl.when(nv == 0)
    def _():
        ys_ref[...] = jnp.zeros_like(ys_ref)


def expert_ffn(xs, block_e, n_valid, w_gate, w_up, w_down, layer, TB):
    P, D = xs.shape[0], 2 * xs.shape[1]
    n_blocks = P // TB
    wspec = lambda a, b: pl.BlockSpec((1, 1, a, b), lambda i, be, nv: (layer, be[i], 0, 0))
    grid_spec = pltpu.PrefetchScalarGridSpec(
        num_scalar_prefetch=2,
        grid=(n_blocks,),
        in_specs=[pl.BlockSpec((TB, D // 2), lambda i, be, nv: (i, 0)),
                  wspec(D, D_EXPERT), wspec(D, D_EXPERT), wspec(D_EXPERT, D)],
        out_specs=pl.BlockSpec((TB, D), lambda i, be, nv: (i, 0)),
        scratch_shapes=[pltpu.VMEM((D, D_EXPERT), BF16), pltpu.VMEM((D, D_EXPERT), BF16),
                        pltpu.VMEM((D_EXPERT, D), BF16)],
    )
    return pl.pallas_call(
        _ffn_kernel,
        grid_spec=grid_spec,
        out_shape=jax.ShapeDtypeStruct((P, D), F32),
        compiler_params=_cparams("arbitrary"),
        name="expert_ffn",
    )(block_e, n_valid, xs, w_gate, w_up, w_down)


def _combine_kernel(cur_ref, nxt_ref, x_ref, route_ref, gf_ref, ys_hbm, out_ref, ybuf, sem, *,
                    tm, n_tiles, final_norm):
    i = pl.program_id(0)
    slot = i % 2

    def copy(row_src, s, k, row):
        return pltpu.make_async_copy(ys_hbm.at[pl.ds(row_src, 1)], ybuf.at[s, k, pl.ds(row, 1)], sem.at[s])

    def start_tile(idx_ref, s):
        def start_row(r):
            for k in range(TOP_K):
                copy(idx_ref[0, 0, k * tm + r], s, k, r).start(priority=k % 2)
        _for_row_chunks(tm, start_row)

    @pl.when(i == 0)
    def _():
        start_tile(cur_ref, 0)

    @pl.when(i + 1 < n_tiles)
    def _():
        start_tile(nxt_ref, 1 - slot)

    def wait_row(r):
        for k in range(TOP_K):
            copy(0, slot, k, r).wait()
    _for_row_chunks(tm, wait_row)

    route = route_ref[...]
    lane_i = lax.broadcasted_iota(jnp.int32, route.shape, 1)
    g0 = jnp.sum(jnp.where(lane_i == 2, route, 0.0), axis=1, keepdims=True)
    g1 = jnp.sum(jnp.where(lane_i == 3, route, 0.0), axis=1, keepdims=True)
    y = x_ref[...] + (g0 * ybuf[slot, 0] + g1 * ybuf[slot, 1])
    out_ref[...] = _rms(y, gf_ref[...]) if final_norm else y


def moe_combine(x, ys, dest, route, g_final, tm, final_norm):
    T, D = x.shape
    n_tiles = T // tm
    assert tm % DMA_UNROLL == 0
    idx_blk = lambda f: pl.BlockSpec((1, 1, TOP_K * tm), f, memory_space=pltpu.SMEM)
    kern = functools.partial(_combine_kernel, tm=tm, n_tiles=n_tiles, final_norm=final_norm)
    return pl.pallas_call(
        kern,
        grid=(n_tiles,),
        in_specs=[idx_blk(lambda i: (i, 0, 0)),
                  idx_blk(lambda i: (jnp.minimum(i + 1, n_tiles - 1), 0, 0)),
                  pl.BlockSpec((tm, D), lambda i: (i, 0)),
                  pl.BlockSpec((tm, LANE), lambda i: (i, 0)),
                  pl.BlockSpec((1, D), lambda i: (0, 0)),
                  pl.BlockSpec(memory_space=pl.ANY)],
        out_specs=pl.BlockSpec((tm, D), lambda i: (i, 0)),
        out_shape=jax.ShapeDtypeStruct((T, D), F32),
        scratch_shapes=[pltpu.VMEM((2, TOP_K, tm, D), F32), pltpu.SemaphoreType.DMA((2,))],
        compiler_params=_cparams("arbitrary"),
        name="moe_combine",
    )(dest, dest, x, route, g_final, ys)


IN_SPLITS = (4 * M_WIDTH, LANE, 2 * R_WIDTH, S_WIDTH + 2 * S_KV_WIDTH, 3 * D_MODEL)
IN_DTYPES = (F32, F32, F32, F32, BF16)


def _prep_w_in(w):
    w = w.astype(BF16)
    c0 = 4 * M_WIDTH
    c1 = c0 + 2 * M_HEADS
    gates = jnp.pad(w[..., c0:c1], ((0, 0), (0, 0), (0, LANE - 2 * M_HEADS)))
    return jnp.concatenate([w[..., :c0], gates, w[..., c1:]], axis=-1)


def _prep_router(w_rg, w_re):
    pad = lambda w: jnp.pad(w, ((0, 0), (0, LANE - w.shape[1])))
    return jnp.concatenate([pad(w_rg), pad(w_re)], axis=1).astype(BF16)


def _row(v):
    return v.reshape(1, -1).astype(F32)


def _pad_rows(a, n):
    return jnp.pad(a, ((0, 0), (0, n - a.shape[1])) + ((0, 0),) * (a.ndim - 2))


def _moe_block(T):
    return 512 if T >= 4096 else 128


def moe_layer(xs_in, slots, g_ffn, w_router, w_gate, w_up, w_down, layer, g_final, final_norm):
    D = xs_in[0].shape[1]
    TB = _moe_block(sum(x.shape[0] for x in xs_in))
    tms = [_row_tile(x.shape[0], 512) for x in xs_in]
    routed = [router(x, g_ffn, w_router) for x in xs_in]
    routes, counts = [r[0] for r in routed], [r[2] for r in routed]
    block_e, n_valid, dests = moe_plan([r[1] for r in routed], counts, TB, tms)
    buf = jnp.zeros((block_e.shape[0] * TB, D // 2), jnp.uint32) if slots is None else slots
    for x, dest, tm in zip(xs_in, dests, tms):
        buf = moe_dispatch(x, g_ffn, dest, buf, tm)
    ys = expert_ffn(buf, block_e, n_valid, w_gate, w_up, w_down, layer, TB)
    outs = [moe_combine(x, ys, dest, route, g_final, tm, final_norm)
            for x, dest, route, tm in zip(xs_in, dests, routes, tms)]
    return outs, buf


def kernel(x_prompt, x_sample, mem_prompt, cache_mem_k, cache_mem_v, cache_swa_k, cache_swa_v, state_mlstm_C, state_mlstm_n, state_mlstm_m, state_rglru_h, state_rglru_conv, norm_mix, w_in, m_igate_b, m_fgate_b, m_head_norm, r_conv_w, r_conv_b, r_gate_a_w, r_gate_a_b, r_gate_x_w, r_gate_x_b, r_lambda, swa_sinks, rel_bias, w_branch_m, w_branch_r, w_branch_s, w_out, norm_mem, xq_w, xk_w, xv_w, xo_w, norm_ffn, router_group_w, router_expert_w, moe_w_gate, moe_w_up, moe_w_down, norm_final):
    B, S, D = x_prompt.shape
    NS, TS, _ = x_sample.shape
    depth = w_in.shape[0]
    n_mem = mem_prompt.shape[1]
    Tp, Ts = B * S, NS * TS
    H = M_HEADS
    assert S % MLSTM_CHUNK == 0 and S % WINDOW == 0 and TS <= SUBLANE
    assert NS % XATTN_SHORT_BATCH == 0 and NS % SWA_SHORT_BATCH == 0 and NS % MLSTM_SHORT_BATCH == 0

    xp = x_prompt.reshape(Tp, D)
    xs = x_sample.reshape(Ts, D)
    mem2d = mem_prompt.reshape(B * n_mem, D)
    cache_k_rows = cache_mem_k.reshape(depth, NS, n_mem * N_MEM_HEADS, X_HEAD_DIM)
    cache_v_rows = cache_mem_v.reshape(depth, NS, n_mem * N_MEM_HEADS, X_HEAD_DIM)
    k_caches = cache_swa_k.reshape(depth * NS, WINDOW, S_KV_WIDTH)
    v_caches = cache_swa_v.reshape(depth * NS, WINDOW, S_KV_WIDTH)
    w_in_all = _prep_w_in(w_in)
    NBP = 2 if B % 2 == 0 else 1
    bias_p = swa_bias_rows(rel_bias, WINDOW, NBP)
    bias_s = swa_bias_rows(rel_bias, SWA_SHORT_TQ, SWA_SHORT_BATCH)
    outs_p = {k: [] for k in ("mem_k", "mem_v", "swa_k", "swa_v", "C", "n", "m", "h", "conv")}
    outs_s = {k: [] for k in ("swa_k", "swa_v", "C", "n", "m", "h", "conv")}

    slots = None
    for l in range(depth):
        w_in_b = w_in_all[l]
        g_mix = _row(norm_mix[l])
        b_i = jnp.pad(m_igate_b[l], (0, SUBLANE - H)).reshape(SUBLANE, 1)
        b_f = jnp.pad(m_fgate_b[l], (0, SUBLANE - H)).reshape(SUBLANE, 1)
        hn = _row(m_head_norm[l])
        r_args = (r_conv_w[l], _row(r_conv_b[l]), r_gate_a_w[l].astype(BF16), r_gate_x_w[l].astype(BF16),
                  _row(r_gate_a_b[l]), _row(r_gate_x_b[l]), _row(r_lambda[l]))
        sink_p = swa_sink_rows(swa_sinks[l], WINDOW, NBP)
        sink_s = swa_sink_rows(swa_sinks[l], SWA_SHORT_TQ, SWA_SHORT_BATCH)
        merge_w = (w_branch_m[l].astype(BF16), w_branch_r[l].astype(BF16), w_branch_s[l].astype(BF16),
                   w_out[l].astype(BF16))
        wq_b, wo_b = xq_w[l].astype(BF16), xo_w[l].astype(BF16)
        w_router = _prep_router(router_group_w[l], router_expert_w[l])
        last = l == depth - 1
        moe_args = (_row(norm_ffn[l]), w_router, moe_w_gate, moe_w_up, moe_w_down, l, _row(norm_final), last)

        w_kv = jnp.concatenate([xk_w[l], xv_w[l]], axis=1).astype(BF16)
        mk, mv = norm_matmul(mem2d, g_mix, w_kv, (X_WIDTH, X_WIDTH), normalize=False)
        outs_p["mem_k"].append(mk.reshape(B, n_mem, N_MEM_HEADS, X_HEAD_DIM))
        outs_p["mem_v"].append(mv.reshape(B, n_mem, N_MEM_HEADS, X_HEAD_DIM))
        mk, mv = mk.reshape(1, B, n_mem, X_WIDTH), mv.reshape(1, B, n_mem, X_WIDTH)

        p_m, p_if, p_r, p_s, p_g = norm_matmul(xp, g_mix, w_in_b, IN_SPLITS, out_dtypes=IN_DTYPES)
        gts = jnp.swapaxes(p_if[:, :2 * H].reshape(B, S, 2 * H), 1, 2)
        gi = jnp.pad(gts[:, :H], ((0, 0), (0, SUBLANE - H), (0, 0)))
        gf = jnp.pad(gts[:, H:], ((0, 0), (0, SUBLANE - H), (0, 0)))
        h_m, caug, m_o = mlstm(p_m.reshape(B, S, 4 * M_WIDTH), gi, gf, b_i, b_f, hn,
                               jnp.zeros((B, H, M_HEAD_DIM, MLSTM_AUG), F32),
                               jnp.zeros((B, SUBLANE, LANE), F32), L=MLSTM_CHUNK)
        outs_p["C"].append(caug[..., :M_HEAD_DIM])
        outs_p["n"].append(caug[..., M_HEAD_DIM])
        outs_p["m"].append(m_o[:, :H, 0])

        h_r, h_last, conv_last = rglru(p_r.reshape(B, S, 2 * R_WIDTH),
                                       jnp.zeros((B, CONV_W - 1, R_WIDTH), F32),
                                       jnp.zeros((B, 1, R_WIDTH), F32), *r_args,
                                       TR=_row_tile(S), Bs=1, out_dtype=BF16)
        outs_p["h"].append(h_last[:, 0])
        outs_p["conv"].append(conv_last)

        p_s3 = p_s.reshape(B, S, S_WIDTH + 2 * S_KV_WIDTH)
        kcol, vcol = S_WIDTH // S_KV_WIDTH, S_WIDTH // S_KV_WIDTH + 1
        prev = lambda col: (lambda b, n: (b, jnp.maximum(n - 1, 0), col))
        cur = lambda col: (lambda b, n: (b, n, col))
        kv_blk = (NBP, WINDOW, S_KV_WIDTH)
        h_s = swa((p_s3,) * 5, (cur(0), prev(kcol), cur(kcol), prev(vcol), cur(vcol)),
                  ((NBP, WINDOW, S_WIDTH), kv_blk, kv_blk, kv_blk, kv_blk),
                  bias_p, sink_p, grid=(B // NBP, S // WINDOW), NBS=NBP, TQ=WINDOW, mask_first=True,
                  out_dtype=BF16)
        outs_p["swa_k"].append(p_s3[:, S - WINDOW:, S_WIDTH:S_WIDTH + S_KV_WIDTH]
                               .reshape(B, WINDOW, S_KV_HEADS, S_HEAD_DIM))
        outs_p["swa_v"].append(p_s3[:, S - WINDOW:, S_WIDTH + S_KV_WIDTH:]
                               .reshape(B, WINDOW, S_KV_HEADS, S_HEAD_DIM))

        xp = merge_out(h_m.reshape(Tp, M_WIDTH), h_r.reshape(Tp, R_WIDTH), h_s.reshape(Tp, S_WIDTH),
                       p_g, xp, *merge_w)
        xp = xattn(xp.reshape(B, S, D), _row(norm_mem[l]), wq_b, wo_b, mk, mv, layer=0,
                   nb=1, tq=512 if S % 512 == 0 else _row_tile(S), interleaved=False).reshape(Tp, D)

        p_m, p_if, p_r, p_s, p_g = norm_matmul(xs, g_mix, w_in_b, IN_SPLITS, out_dtypes=IN_DTYPES)
        gts = jnp.swapaxes(p_if[:, :2 * H].reshape(NS, TS, 2 * H), 1, 2)
        gts = jnp.pad(gts, ((0, 0), (0, 0), (0, LANE - TS)))
        gi = jnp.pad(gts[:, :H], ((0, 0), (0, SUBLANE - H), (0, 0)))
        gf = jnp.pad(gts[:, H:], ((0, 0), (0, SUBLANE - H), (0, 0)))
        m0 = jnp.broadcast_to(jnp.pad(state_mlstm_m[l], ((0, 0), (0, SUBLANE - H)))[:, :, None],
                              (NS, SUBLANE, LANE))
        h_m, c_s, n_s, m_o = mlstm_short(_pad_rows(p_m.reshape(NS, TS, 4 * M_WIDTH), SUBLANE), gi, gf, b_i, b_f,
                                         hn, state_mlstm_C, state_mlstm_n, m0, layer=l,
                                         NBS=MLSTM_SHORT_BATCH, TS=TS)
        h_m = h_m[:, :TS].reshape(Ts, M_WIDTH)
        outs_s["C"].append(c_s)
        outs_s["n"].append(n_s)
        outs_s["m"].append(m_o[:, :H, 0])

        tmaj = lambda a: jnp.swapaxes(a, 0, 1).reshape(1, a.shape[0] * a.shape[1], a.shape[2])
        h_r, h_last, conv_last = rglru(tmaj(p_r.reshape(NS, TS, 2 * R_WIDTH)), tmaj(state_rglru_conv[l]),
                                       state_rglru_h[l].reshape(1, NS, R_WIDTH), *r_args,
                                       TR=TS * NS, Bs=NS, out_dtype=F32)
        h_r = jnp.swapaxes(h_r.reshape(TS, NS, R_WIDTH), 0, 1).reshape(Ts, R_WIDTH)
        outs_s["h"].append(h_last[0])
        outs_s["conv"].append(jnp.swapaxes(conv_last.reshape(CONV_W - 1, NS, R_WIDTH), 0, 1))

        TQ, NBS = SWA_SHORT_TQ, SWA_SHORT_BATCH
        p_s3 = p_s.reshape(NS, TS, S_WIDTH + 2 * S_KV_WIDTH)
        q_s = _pad_rows(p_s3[:, :, :S_WIDTH], TQ)
        k_new = p_s3[:, :, S_WIDTH:S_WIDTH + S_KV_WIDTH]
        v_new = p_s3[:, :, S_WIDTH + S_KV_WIDTH:]
        full = lambda b, n: (b, 0, 0)
        cache = lambda b, n: (l * (NS // NBS) + b, 0, 0)
        h_s = swa((q_s, k_caches, _pad_rows(k_new, TQ), v_caches, _pad_rows(v_new, TQ)),
                  (full, cache, full, cache, full),
                  ((NBS, TQ, S_WIDTH), (NBS, WINDOW, S_KV_WIDTH), (NBS, TQ, S_KV_WIDTH),
                   (NBS, WINDOW, S_KV_WIDTH), (NBS, TQ, S_KV_WIDTH)),
                  bias_s, sink_s, grid=(NS // NBS, 1), NBS=NBS, TQ=TQ, mask_first=False, out_dtype=F32)
        h_s = h_s[:, :TS].reshape(Ts, S_WIDTH)
        outs_s["swa_k"].append(k_new)
        outs_s["swa_v"].append(v_new)

        xs = merge_out(h_m, h_r, h_s, p_g, xs, *merge_w)
        xs8 = xattn(_pad_rows(xs.reshape(NS, TS, D), SUBLANE), _row(norm_mem[l]), wq_b, wo_b,
                    cache_k_rows, cache_v_rows, layer=l, nb=XATTN_SHORT_BATCH, tq=SUBLANE, interleaved=True)

        (xp, xs), slots = moe_layer([xp, xs8[:, :TS].reshape(Ts, D)], slots, *moe_args)

    st = lambda d, k: jnp.stack(d[k])

    def shifted_cache(caches, new):
        out = jnp.concatenate([caches.reshape(depth, NS, WINDOW, S_KV_WIDTH)[:, :, TS:], jnp.stack(new)], axis=2)
        return out.reshape(depth, NS, WINDOW, S_KV_HEADS, S_HEAD_DIM)

    return (xp.reshape(B, S, D), xs.reshape(NS, TS, D), st(outs_p, "mem_k"), st(outs_p, "mem_v"),
            st(outs_p, "swa_k"), st(outs_p, "swa_v"), st(outs_p, "C"), st(outs_p, "n"), st(outs_p, "m"),
            st(outs_p, "h"), st(outs_p, "conv"),
            shifted_cache(k_caches, outs_s["swa_k"]), shifted_cache(v_caches, outs_s["swa_v"]),
            st(outs_s, "C"), st(outs_s, "n"), st(outs_s, "m"), st(outs_s, "h"), st(outs_s, "conv"))
```

```python
import functools
import math

import jax
import jax.numpy as jnp
from jax import lax
from jax.experimental import pallas as pl
from jax.experimental.pallas import tpu as pltpu

F32 = jnp.float32
BF16 = jnp.bfloat16
NEG_INF = float("-inf")

LANE = 128
SUBLANE = 8
BF16_SUBLANE = 16
VMEM_LIMIT_BYTES = 56 * 1024 * 1024

D_MODEL = 1024
M_HEADS = 4
M_HEAD_DIM = 128
M_WIDTH = M_HEADS * M_HEAD_DIM
R_WIDTH = 512
R_BLOCKS = 4
R_BLOCK_DIM = R_WIDTH // R_BLOCKS
CONV_W = 4
LRU_C = 8.0
S_HEADS = 8
S_KV_HEADS = 2
S_HEAD_DIM = 64
S_WIDTH = S_HEADS * S_HEAD_DIM
S_KV_WIDTH = S_KV_HEADS * S_HEAD_DIM
WINDOW = 128
N_BUCKETS = 32
MAX_DISTANCE = 128
N_MEM_HEADS = 4
X_HEAD_DIM = 128
X_WIDTH = N_MEM_HEADS * X_HEAD_DIM
N_GROUPS = 4
EXPERTS_PER_GROUP = 8
N_EXPERTS = N_GROUPS * EXPERTS_PER_GROUP
TOP_K = 2
D_EXPERT = 512
EPS = 1e-6

MLSTM_CHUNK = 512
MLSTM_AUG = 2 * M_HEAD_DIM
MLSTM_SHORT_BATCH = 8
SWA_SHORT_TQ = BF16_SUBLANE
SWA_SHORT_BATCH = 8
XATTN_SHORT_BATCH = 8


def _cparams(*sem):
    return pltpu.CompilerParams(dimension_semantics=sem, vmem_limit_bytes=VMEM_LIMIT_BYTES)


def _rms(x, g):
    ms = jnp.mean(x * x, axis=-1, keepdims=True)
    return x * lax.rsqrt(ms + EPS) * g


def _sigmoid(x):
    return 1.0 / (1.0 + jnp.exp(-x))


def _softplus(x):
    return jnp.maximum(x, 0.0) + jnp.log1p(jnp.exp(-jnp.abs(x)))


def _log_sigmoid(x):
    return -_softplus(-x)


def _gelu_tanh(x):
    return 0.5 * x * (1.0 + jnp.tanh(math.sqrt(2.0 / math.pi) * (x + 0.044715 * (x * x * x))))


def _dot(a, b):
    return jnp.dot(a, b, preferred_element_type=F32)


def _dot_nt(a, b):
    return lax.dot_general(a, b, (((1,), (1,)), ((), ())), preferred_element_type=F32)


def _row_tile(n, pref=256):
    for t in (512, 256, 128, 64, 32, 16, 8):
        if t <= pref and n % t == 0:
            return t
    raise ValueError(f"row count {n} is not a multiple of {SUBLANE}")


def _norm_matmul_kernel(x_ref, g_ref, *refs, splits, normalize):
    w_refs, out_refs = refs[:len(splits)], refs[len(splits):]
    x = x_ref[...]
    if normalize:
        x = _rms(x, g_ref[...])
    xb = x.astype(BF16)
    outs = iter(out_refs)
    for w_ref, w_splits in zip(w_refs, splits):
        off = 0
        for n in w_splits:
            o_ref = next(outs)
            o_ref[...] = _dot(xb, w_ref[0, :, off:off + n]).astype(o_ref.dtype)
            off += n


def norm_matmul(x, g, weights, splits, layer, normalize=True, out_dtypes=None):
    flat = [n for w_splits in splits for n in w_splits]
    out_dtypes = out_dtypes or (F32,) * len(flat)
    T, D = x.shape
    tm = _row_tile(T)
    assert all(sum(ws) == w.shape[2] for ws, w in zip(splits, weights)) and all(n % LANE == 0 for n in flat)
    kern = functools.partial(_norm_matmul_kernel, splits=tuple(tuple(ws) for ws in splits), normalize=normalize)
    return pl.pallas_call(
        kern,
        grid=(T // tm,),
        in_specs=[pl.BlockSpec((tm, D), lambda i: (i, 0)),
                  pl.BlockSpec((1, D), lambda i: (0, 0))]
                 + [pl.BlockSpec((1, D, w.shape[2]), lambda i: (layer, 0, 0)) for w in weights],
        out_specs=[pl.BlockSpec((tm, n), lambda i: (i, 0)) for n in flat],
        out_shape=[jax.ShapeDtypeStruct((T, n), dt) for n, dt in zip(flat, out_dtypes)],
        compiler_params=_cparams("parallel"),
        name="norm_matmul",
    )(x, g, *weights)


def _mlstm_kernel(q_ref, k_ref, v_ref, o_ref, gi_ref, gf_ref, bi_ref, bf_ref, hn_ref,
                  c0_ref, m0_ref, h_out, c_out, m_out, c_sc, m_sc, *, L, NBB):
    @pl.when(pl.program_id(1) == 0)
    def _():
        c_sc[...] = c0_ref[...]
        m_sc[...] = m0_ref[...]

    t_idx = lax.broadcasted_iota(jnp.int32, (L, L), 0)
    s_idx = lax.broadcasted_iota(jnp.int32, (L, L), 1)
    causal = t_idx >= s_idx
    eye = t_idx == s_idx
    row8 = lax.broadcasted_iota(jnp.int32, (SUBLANE, LANE), 0)
    ones_col = (lax.broadcasted_iota(jnp.int32, (L, M_HEAD_DIM), 1) == 0).astype(BF16)
    gi = [gi_ref[b] + bi_ref[...] for b in range(NBB)]
    gf = [_log_sigmoid(gf_ref[b] + bf_ref[...]) for b in range(NBB)]
    m_tiles = [m_sc[b] for b in range(NBB)]
    chains = [dict(b=b, h=h, hs=slice(h * M_HEAD_DIM, (h + 1) * M_HEAD_DIM))
              for b in range(NBB) for h in range(M_HEADS)]

    for c in chains:
        b, h = c["b"], c["h"]
        c["m_prev"] = m_tiles[b][h:h + 1, 0:1]
        c["b_col"] = jnp.sum(jnp.where(causal, gf[b][h:h + 1, :], 0.0), axis=1, keepdims=True)
    for c in chains:
        b, h = c["b"], c["h"]
        b_row = jnp.sum(jnp.where(eye, c["b_col"], 0.0), axis=0, keepdims=True)
        c["r_row"] = gi[b][h:h + 1, :] - b_row
        c["g_end"] = b_row[:, L - 1:L]
        c["log_in"] = jnp.where(causal, c["b_col"] + c["r_row"], NEG_INF)
        c["m_intra"] = jnp.max(c["log_in"], axis=1, keepdims=True)
    for c in chains:
        b, hs = c["b"], c["hs"]
        c["k"] = k_ref[b][:, hs] * (M_HEAD_DIM ** -0.5)
        c["caug"] = c_sc[b, c["h"]]
        c["vaug"] = jnp.concatenate([v_ref[b][:, hs].astype(BF16), ones_col], axis=1)
        qb = q_ref[b][:, hs].astype(BF16)
        c["qk"] = _dot_nt(qb, c["k"].astype(BF16))
        c["qc"] = _dot(qb, c["caug"].astype(BF16))
    for c in chains:
        log_prev = c["b_col"] + c["m_prev"]
        m_t = jnp.maximum(log_prev, c["m_intra"])
        sc = (c["qk"] * jnp.exp(c["log_in"] - m_t)).astype(BF16)
        nd = jnp.exp(log_prev - m_t) * c["qc"] + _dot(sc, c["vaug"])
        den = nd[:, M_HEAD_DIM:M_HEAD_DIM + 1]
        c["hh"] = nd[:, :M_HEAD_DIM] / jnp.maximum(jnp.abs(den), jnp.exp(-m_t))
    for c in chains:
        b, hs = c["b"], c["hs"]
        d = c["hh"] - jnp.mean(c["hh"], axis=-1, keepdims=True)
        y = d * lax.rsqrt(jnp.mean(d * d, axis=-1, keepdims=True) + EPS) * hn_ref[:, hs]
        h_out[b, :, hs] = (y * _sigmoid(o_ref[b][:, hs])).astype(h_out.dtype)
    for c in chains:
        b, h = c["b"], c["h"]
        lie = c["g_end"] + c["r_row"]
        m_end = jnp.maximum(c["g_end"] + c["m_prev"], jnp.max(lie, axis=1, keepdims=True))
        w_pe = jnp.exp(c["g_end"] + c["m_prev"] - m_end)
        ktw = (c["k"].T * jnp.exp(lie - m_end)).astype(BF16)
        c_new = w_pe * c["caug"] + _dot(ktw, c["vaug"])
        c_sc[b, h] = c_new
        c_out[b, h] = c_new
        m_tiles[b] = jnp.where(row8 == h, m_end, m_tiles[b])
    for b in range(NBB):
        m_sc[b] = m_tiles[b]
        m_out[b] = m_tiles[b]


def mlstm(p_m, gates_i, gates_f, b_i, b_f, head_norm, c0aug, m0, L):
    NB, R, _ = p_m.shape
    nc = R // L
    assert R % L == 0
    NBB = 2 if NB % 2 == 0 else 1
    blk = lambda j: pl.BlockSpec((NBB, L, M_WIDTH), lambda b, c, j=j: (b, c, j))
    state_spec = pl.BlockSpec((NBB, M_HEADS, M_HEAD_DIM, MLSTM_AUG), lambda b, c: (b, 0, 0, 0))
    m_spec = pl.BlockSpec((NBB, SUBLANE, LANE), lambda b, c: (b, 0, 0))
    gate_spec = pl.BlockSpec((NBB, SUBLANE, L), lambda b, c: (b, 0, c))
    kern = functools.partial(_mlstm_kernel, L=L, NBB=NBB)
    scratch = [pltpu.VMEM((NBB, M_HEADS, M_HEAD_DIM, MLSTM_AUG), F32), pltpu.VMEM((NBB, SUBLANE, LANE), F32)]
    return pl.pallas_call(
        kern,
        grid=(NB // NBB, nc),
        in_specs=[blk(0), blk(1), blk(2), blk(3), gate_spec, gate_spec,
                  pl.BlockSpec((SUBLANE, 1), lambda b, c: (0, 0)),
                  pl.BlockSpec((SUBLANE, 1), lambda b, c: (0, 0)),
                  pl.BlockSpec((1, M_WIDTH), lambda b, c: (0, 0)),
                  state_spec, m_spec],
        out_specs=[pl.BlockSpec((NBB, L, M_WIDTH), lambda b, c: (b, c, 0)), state_spec, m_spec],
        out_shape=[jax.ShapeDtypeStruct((NB, R, M_WIDTH), BF16),
                   jax.ShapeDtypeStruct((NB, M_HEADS, M_HEAD_DIM, MLSTM_AUG), F32),
                   jax.ShapeDtypeStruct((NB, SUBLANE, LANE), F32)],
        scratch_shapes=scratch,
        compiler_params=_cparams("parallel", "arbitrary"),
        name="mlstm",
    )(p_m, p_m, p_m, p_m, gates_i, gates_f, b_i, b_f, head_norm, c0aug, m0)


def _mlstm_short_kernel(q_ref, k_ref, v_ref, o_ref, gi_ref, gf_ref, bi_ref, bf_ref, hn_ref,
                        c0_ref, n0_ref, m0_ref, h_out, c_out, n_out, m_out, *, NBS, TS):
    L = SUBLANE
    t_idx = lax.broadcasted_iota(jnp.int32, (L, L), 0)
    s_idx = lax.broadcasted_iota(jnp.int32, (L, L), 1)
    causal = t_idx >= s_idx
    eye = t_idx == s_idx
    valid = lax.broadcasted_iota(jnp.int32, (1, L), 1) < TS
    row8 = lax.broadcasted_iota(jnp.int32, (SUBLANE, LANE), 0)
    gi = [gi_ref[b] + bi_ref[...] for b in range(NBS)]
    gf = [_log_sigmoid(gf_ref[b] + bf_ref[...]) for b in range(NBS)]
    m_tiles = [m0_ref[b] for b in range(NBS)]
    chains = [dict(b=b, h=h, hs=slice(h * M_HEAD_DIM, (h + 1) * M_HEAD_DIM))
              for b in range(NBS) for h in range(M_HEADS)]

    for c in chains:
        b, h = c["b"], c["h"]
        c["li"] = jnp.where(valid, gi[b][h:h + 1, 0:L], NEG_INF)
        lf = jnp.where(valid, gf[b][h:h + 1, 0:L], 0.0)
        c["m_prev"] = m_tiles[b][h:h + 1, 0:1]
        c["b_col"] = jnp.sum(jnp.where(causal, lf, 0.0), axis=1, keepdims=True)
    for c in chains:
        b_row = jnp.sum(jnp.where(eye, c["b_col"], 0.0), axis=0, keepdims=True)
        c["r_row"] = c["li"] - b_row
        c["g_end"] = b_row[:, L - 1:L]
        c["log_in"] = jnp.where(causal, c["b_col"] + c["r_row"], NEG_INF)
        c["m_intra"] = jnp.max(c["log_in"], axis=1, keepdims=True)
    for c in chains:
        b, hs = c["b"], c["hs"]
        c["q"] = q_ref[b][:, hs]
        c["k"] = k_ref[b][:, hs] * (M_HEAD_DIM ** -0.5)
        c["vb"] = v_ref[b][:, hs].astype(BF16)
        c["c0"] = c0_ref[0, b, c["h"]]
        c["n0"] = n0_ref[0, b, c["h"]:c["h"] + 1, :]
        qb = c["q"].astype(BF16)
        c["qk"] = _dot_nt(qb, c["k"].astype(BF16))
        c["qc"] = _dot(qb, c["c0"].astype(BF16))
        c["qn"] = jnp.sum(c["q"] * c["n0"], axis=1, keepdims=True)
    for c in chains:
        log_prev = c["b_col"] + c["m_prev"]
        m_t = jnp.maximum(log_prev, c["m_intra"])
        w_prev = jnp.exp(log_prev - m_t)
        sc = c["qk"] * jnp.exp(c["log_in"] - m_t)
        num = w_prev * c["qc"] + _dot(sc.astype(BF16), c["vb"])
        den = w_prev * c["qn"] + jnp.sum(sc, axis=1, keepdims=True)
        c["hh"] = num / jnp.maximum(jnp.abs(den), jnp.exp(-m_t))
    for c in chains:
        b, hs = c["b"], c["hs"]
        d = c["hh"] - jnp.mean(c["hh"], axis=-1, keepdims=True)
        y = d * lax.rsqrt(jnp.mean(d * d, axis=-1, keepdims=True) + EPS) * hn_ref[:, hs]
        h_out[b, :, hs] = y * _sigmoid(o_ref[b][:, hs])
    for c in chains:
        b, h = c["b"], c["h"]
        lie = c["g_end"] + c["r_row"]
        m_end = jnp.maximum(c["g_end"] + c["m_prev"], jnp.max(lie, axis=1, keepdims=True))
        w_pe = jnp.exp(c["g_end"] + c["m_prev"] - m_end)
        w_e = jnp.exp(lie - m_end)
        w_e_col = jnp.sum(jnp.where(eye, w_e, 0.0), axis=1, keepdims=True)
        kw = c["k"] * w_e_col
        c_out[b, h] = w_pe * c["c0"] + lax.dot_general(kw.astype(BF16), c["vb"], (((0,), (0,)), ((), ())),
                                                       preferred_element_type=F32)
        n_out[b, h:h + 1, :] = w_pe * c["n0"] + jnp.sum(kw, axis=0, keepdims=True)
        m_tiles[b] = jnp.where(row8 == h, m_end, m_tiles[b])
    for b in range(NBS):
        m_out[b] = m_tiles[b]


def mlstm_short(p_m, gates_i, gates_f, b_i, b_f, head_norm, c0, n0, m0, layer, NBS, TS):
    NB = p_m.shape[0]
    H, DH = M_HEADS, M_HEAD_DIM
    assert NB % NBS == 0 and TS <= SUBLANE
    blk = lambda j: pl.BlockSpec((NBS, SUBLANE, M_WIDTH), lambda i, j=j: (i, 0, j))
    tile = pl.BlockSpec((NBS, SUBLANE, LANE), lambda i: (i, 0, 0))
    kern = functools.partial(_mlstm_short_kernel, NBS=NBS, TS=TS)
    return pl.pallas_call(
        kern,
        grid=(NB // NBS,),
        in_specs=[blk(0), blk(1), blk(2), blk(3), tile, tile,
                  pl.BlockSpec((SUBLANE, 1), lambda i: (0, 0)),
                  pl.BlockSpec((SUBLANE, 1), lambda i: (0, 0)),
                  pl.BlockSpec((1, M_WIDTH), lambda i: (0, 0)),
                  pl.BlockSpec((1, NBS, H, DH, DH), lambda i: (layer, i, 0, 0, 0)),
                  pl.BlockSpec((1, NBS, H, DH), lambda i: (layer, i, 0, 0)),
                  tile],
        out_specs=[pl.BlockSpec((NBS, SUBLANE, M_WIDTH), lambda i: (i, 0, 0)),
                   pl.BlockSpec((NBS, H, DH, DH), lambda i: (i, 0, 0, 0)),
                   pl.BlockSpec((NBS, H, DH), lambda i: (i, 0, 0)),
                   tile],
        out_shape=[jax.ShapeDtypeStruct((NB, SUBLANE, M_WIDTH), F32),
                   jax.ShapeDtypeStruct((NB, H, DH, DH), F32),
                   jax.ShapeDtypeStruct((NB, H, DH), F32),
                   jax.ShapeDtypeStruct((NB, SUBLANE, LANE), F32)],
        compiler_params=_cparams("parallel"),
        name="mlstm_short",
    )(p_m, p_m, p_m, p_m, gates_i, gates_f, b_i, b_f, head_norm, c0, n0, m0)


def _rglru_kernel(x_ref, g_ref, conv0_ref, h0_ref, cw_ref, cb_ref, wa_ref, wx_ref, ba_ref, bx_ref,
                  lam_ref, y_out, hlast_out, conv_out, xpad_sc, hc_sc, *, TR, Bs):
    CB = (CONV_W - 1) * Bs
    X0 = -(-CB // SUBLANE) * SUBLANE

    @pl.when(pl.program_id(1) == 0)
    def _():
        xpad_sc[X0 - CB:X0, :] = conv0_ref[0]
        hc_sc[...] = h0_ref[0]

    xpad_sc[X0:X0 + TR, :] = x_ref[0]
    y = cb_ref[...]
    for j in range(CONV_W):
        y = y + xpad_sc[X0 - CB + j * Bs:X0 - CB + j * Bs + TR, :] * cw_ref[j:j + 1, :]
    tail = xpad_sc[X0 + TR - CB:X0 + TR, :]
    conv_out[0] = tail
    xpad_sc[X0 - CB:X0, :] = tail

    yb = y.astype(BF16)
    rs, is_ = [], []
    for n in range(R_BLOCKS):
        sl = slice(n * R_BLOCK_DIM, (n + 1) * R_BLOCK_DIM)
        rs.append(_dot(yb[:, sl], wa_ref[n]))
        is_.append(_dot(yb[:, sl], wx_ref[n]))
    r = _sigmoid(jnp.concatenate(rs, axis=1) + ba_ref[...])
    i = _sigmoid(jnp.concatenate(is_, axis=1) + bx_ref[...])
    log_a = -LRU_C * r * _softplus(-lam_ref[...])
    a_cum = jnp.exp(log_a)
    t = jnp.tanh(log_a)
    u_cum = jnp.sqrt(-2.0 * t / (1.0 - t)) * (i * y)

    def scan_rows(a, u, axis, stride):
        n = a.shape[axis]
        idx = lax.broadcasted_iota(jnp.int32, a.shape, axis)
        d = stride
        while d < n:
            keep = idx >= d
            u = jnp.where(keep, a * pltpu.roll(u, d, axis=axis) + u, u)
            a = jnp.where(keep, a * pltpu.roll(a, d, axis=axis), a)
            d *= 2
        return a, u

    hc = hc_sc[...]
    if Bs == 1:
        NG = TR // SUBLANE
        a3, u3 = scan_rows(a_cum.reshape(NG, SUBLANE, R_WIDTH), u_cum.reshape(NG, SUBLANE, R_WIDTH), 1, 1)
        a_tot, u_tot = scan_rows(a3[:, SUBLANE - 1, :], u3[:, SUBLANE - 1, :], 0, 1)
        h_end = u_tot + a_tot * hc
        g_idx = lax.broadcasted_iota(jnp.int32, (NG, R_WIDTH), 0)
        h_start = jnp.where(g_idx == 0, hc, pltpu.roll(h_end, 1, axis=0))
        h = (u3 + a3 * h_start[:, None, :]).reshape(TR, R_WIDTH)
    else:
        a_cum, u_cum = scan_rows(a_cum, u_cum, 0, Bs)
        h = u_cum + a_cum * jnp.concatenate([hc] * (TR // Bs), axis=0)
    y_out[0] = (h * _gelu_tanh(g_ref[0])).astype(y_out.dtype)
    h_last = h[TR - Bs:, :]
    hc_sc[...] = h_last
    hlast_out[0] = h_last


def rglru(p_r, conv0, h0, cw, cb, wa, wx, ba, bx, lam, TR, Bs, out_dtype):
    G, R, _ = p_r.shape
    CB = (CONV_W - 1) * Bs
    X0 = -(-CB // SUBLANE) * SUBLANE
    assert R % TR == 0 and TR % Bs == 0 and TR >= CB
    W = R_WIDTH
    const = lambda shape: pl.BlockSpec(shape, lambda g, r: (0,) * len(shape))
    kern = functools.partial(_rglru_kernel, TR=TR, Bs=Bs)
    return pl.pallas_call(
        kern,
        grid=(G, R // TR),
        in_specs=[pl.BlockSpec((1, TR, W), lambda g, r: (g, r, 0)),
                  pl.BlockSpec((1, TR, W), lambda g, r: (g, r, 1)),
                  pl.BlockSpec((1, CB, W), lambda g, r: (g, 0, 0)),
                  pl.BlockSpec((1, Bs, W), lambda g, r: (g, 0, 0)),
                  const((CONV_W, W)), const((1, W)),
                  const((R_BLOCKS, R_BLOCK_DIM, R_BLOCK_DIM)), const((R_BLOCKS, R_BLOCK_DIM, R_BLOCK_DIM)),
                  const((1, W)), const((1, W)), const((1, W))],
        out_specs=[pl.BlockSpec((1, TR, W), lambda g, r: (g, r, 0)),
                   pl.BlockSpec((1, Bs, W), lambda g, r: (g, 0, 0)),
                   pl.BlockSpec((1, CB, W), lambda g, r: (g, 0, 0))],
        out_shape=[jax.ShapeDtypeStruct((G, R, W), out_dtype),
                   jax.ShapeDtypeStruct((G, Bs, W), F32),
                   jax.ShapeDtypeStruct((G, CB, W), F32)],
        scratch_shapes=[pltpu.VMEM((X0 + TR, W), F32), pltpu.VMEM((Bs, W), F32)],
        compiler_params=_cparams("parallel", "arbitrary"),
        name="rglru",
    )(p_r, p_r, conv0, h0, cw, cb, wa, wx, ba, bx, lam)


def _swa_kernel(q_ref, kp_ref, kc_ref, vp_ref, vc_ref, bias_ref, sink_ref, o_ref, *, NBS, TQ, mask_first):
    GQ = S_HEADS // S_KV_HEADS
    lane = lax.broadcasted_iota(jnp.int32, (1, LANE), 1)
    lo = lane < S_HEAD_DIM

    def both_halves(x, j):
        xr = pltpu.roll(x, S_HEAD_DIM, axis=1)
        return (jnp.where(lo, x, xr) if j == 0 else jnp.where(lo, xr, x)).astype(BF16)

    s1_parts, s2_parts, v_parts = [], [], []
    for b in range(NBS):
        q = q_ref[b] * (S_HEAD_DIM ** -0.5)
        for j in range(S_KV_HEADS):
            stack = []
            for g in range(GQ):
                h = GQ * j + g
                qh = q[:, (h // 2) * LANE:(h // 2 + 1) * LANE]
                stack.append(jnp.where(lo if h % 2 == 0 else jnp.logical_not(lo), qh, 0.0))
            qs = jnp.concatenate(stack, axis=0).astype(BF16)
            s1_parts.append(_dot_nt(qs, both_halves(kp_ref[b], j)))
            s2_parts.append(_dot_nt(qs, both_halves(kc_ref[b], j)))
            v_parts.append((both_halves(vp_ref[b], j), both_halves(vc_ref[b], j)))
    s1 = jnp.concatenate(s1_parts, axis=0) + bias_ref[:, 0:WINDOW]
    s2 = jnp.concatenate(s2_parts, axis=0) + bias_ref[:, WINDOW:WINDOW + TQ]
    if mask_first:
        s1 = jnp.where(pl.program_id(1) == 0, NEG_INF, s1)
    sink = sink_ref[...]
    mx = jnp.maximum(jnp.maximum(jnp.max(s1, axis=1, keepdims=True), jnp.max(s2, axis=1, keepdims=True)), sink)
    p1 = jnp.exp(s1 - mx)
    p2 = jnp.exp(s2 - mx)
    inv = 1.0 / (jnp.sum(p1, axis=1, keepdims=True) + jnp.sum(p2, axis=1, keepdims=True) + jnp.exp(sink - mx))
    p1 = p1.astype(BF16)
    p2 = p2.astype(BF16)
    R = GQ * TQ
    for b in range(NBS):
        for j in range(S_KV_HEADS):
            n = b * S_KV_HEADS + j
            rows = slice(n * R, (n + 1) * R)
            vp, vc = v_parts[n]
            o = (_dot(p1[rows], vp) + _dot(p2[rows], vc)) * inv[rows]
            for pair in range(GQ // 2):
                even = o[(2 * pair) * TQ:(2 * pair + 1) * TQ]
                odd = o[(2 * pair + 1) * TQ:(2 * pair + 2) * TQ]
                blk = (GQ * j) // 2 + pair
                o_ref[b, :, blk * LANE:(blk + 1) * LANE] = jnp.where(lo, even, odd).astype(o_ref.dtype)


def swa(arrs, maps, shapes, bias_rows, sink_rows, grid, NBS, TQ, mask_first, out_dtype):
    kern = functools.partial(_swa_kernel, NBS=NBS, TQ=TQ, mask_first=mask_first)
    in_specs = [pl.BlockSpec(s, m) for s, m in zip(shapes, maps)]
    in_specs += [pl.BlockSpec(bias_rows.shape, lambda b, n: (0, 0)),
                 pl.BlockSpec(sink_rows.shape, lambda b, n: (0, 0))]
    return pl.pallas_call(
        kern,
        grid=grid,
        in_specs=in_specs,
        out_specs=pl.BlockSpec((NBS, TQ, S_WIDTH), lambda b, n: (b, n, 0)),
        out_shape=jax.ShapeDtypeStruct((grid[0] * NBS, grid[1] * TQ, S_WIDTH), out_dtype),
        compiler_params=_cparams("parallel", "arbitrary"),
        name="swa",
    )(*arrs, bias_rows, sink_rows)


def t5_bucket(dist):
    max_exact = N_BUCKETS // 2
    d = jnp.maximum(dist, 0)
    large = max_exact + (jnp.log(jnp.maximum(d, 1).astype(F32) / max_exact)
                         / math.log(MAX_DISTANCE / max_exact) * (N_BUCKETS - max_exact)).astype(jnp.int32)
    return jnp.where(d < max_exact, d, jnp.minimum(large, N_BUCKETS - 1))


def swa_bias_rows(rel_bias, TQ, NBS):
    qi = jnp.arange(TQ)[:, None]
    kj = jnp.arange(WINDOW + TQ)[None, :]
    dist = qi + WINDOW - kj
    onehot = (t5_bucket(dist)[..., None] == jnp.arange(N_BUCKETS)).astype(F32)
    b = jnp.einsum("qkn,nh->hqk", onehot, rel_bias.astype(F32), precision=lax.Precision.HIGHEST)
    b = jnp.where((dist >= 0) & (dist <= WINDOW), b, NEG_INF).reshape(S_HEADS * TQ, WINDOW + TQ)
    return jnp.tile(b, (NBS, 1))


def swa_sink_rows(sinks, TQ, NBS):
    return jnp.tile(jnp.repeat(sinks.astype(F32), TQ), NBS).reshape(-1, 1)


def _merge_kernel(hm, hr, hs, gm, gr, gs, x, wm, wr, ws, wo, out):
    def branch(h_ref, g_ref, w_ref):
        return _sigmoid(g_ref[...].astype(F32)) * _dot(h_ref[...].astype(BF16), w_ref[...])
    merged = branch(hm, gm, wm) + branch(hr, gr, wr) + branch(hs, gs, ws)
    out[...] = x[...] + _dot(merged.astype(BF16), wo[...])


def merge_out(hm, hr, hs, p_g, x, wm, wr, ws, wo):
    T, D = x.shape
    tm = _row_tile(T, 512)
    hspec = lambda w: pl.BlockSpec((tm, w), lambda i: (i, 0))
    gspec = lambda j: pl.BlockSpec((tm, D), lambda i, j=j: (i, j))
    wspec = lambda a: pl.BlockSpec(a.shape, lambda i: (0, 0))
    return pl.pallas_call(
        _merge_kernel,
        grid=(T // tm,),
        in_specs=[hspec(M_WIDTH), hspec(R_WIDTH), hspec(S_WIDTH), gspec(0), gspec(1), gspec(2),
                  hspec(D), wspec(wm), wspec(wr), wspec(ws), wspec(wo)],
        out_specs=hspec(D),
        out_shape=jax.ShapeDtypeStruct((T, D), F32),
        compiler_params=_cparams("parallel"),
        name="merge_out",
    )(hm, hr, hs, p_g, p_g, p_g, x, wm, wr, ws, wo)


def _xattn_kernel(x_ref, g_ref, wq_ref, wo_ref, mk_ref, mv_ref, out_ref, *, nb, tq, interleaved):
    NH = N_MEM_HEADS
    x = x_ref[...].reshape(nb * tq, D_MODEL)
    q = _dot(_rms(x, g_ref[...]).astype(BF16), wq_ref[...])
    head = lambda a, h: a[:, h * X_HEAD_DIM:(h + 1) * X_HEAD_DIM]
    parts = []
    for b in range(nb):
        qb = q[b * tq:(b + 1) * tq]
        if interleaved:
            qs = jnp.concatenate([head(qb, h) for h in range(NH)], axis=0).astype(BF16)
            parts.append(_dot_nt(qs, mk_ref[0, b].astype(BF16)))
        else:
            parts += [_dot_nt(head(qb, h).astype(BF16), head(mk_ref[0, b], h).astype(BF16)) for h in range(NH)]
    s = jnp.concatenate(parts, axis=0) * (X_HEAD_DIM ** -0.5)
    if interleaved:
        row_h = (lax.broadcasted_iota(jnp.int32, s.shape, 0) // tq) % NH
        s = jnp.where(lax.broadcasted_iota(jnp.int32, s.shape, 1) % NH == row_h, s, NEG_INF)
    p = jnp.exp(s - jnp.max(s, axis=1, keepdims=True))
    inv = 1.0 / jnp.sum(p, axis=1, keepdims=True)
    rows = []
    for b in range(nb):
        if interleaved:
            r = slice(b * NH * tq, (b + 1) * NH * tq)
            o = _dot(p[r].astype(BF16), mv_ref[0, b].astype(BF16)) * inv[r]
            heads = [o[h * tq:(h + 1) * tq] for h in range(NH)]
        else:
            heads = []
            for h in range(NH):
                r = slice((b * NH + h) * tq, (b * NH + h + 1) * tq)
                heads.append(_dot(p[r].astype(BF16), head(mv_ref[0, b], h).astype(BF16)) * inv[r])
        rows.append(jnp.concatenate(heads, axis=1))
    o_all = rows[0] if nb == 1 else jnp.concatenate(rows, axis=0)
    y = x + _dot(o_all.astype(BF16), wo_ref[...])
    out_ref[...] = y.reshape(nb, tq, D_MODEL)


def xattn(x, g, wq, wo, mem_k, mem_v, layer, nb, tq, interleaved):
    B, R, D = x.shape
    mem_blk = (1, nb) + mem_k.shape[2:]
    kern = functools.partial(_xattn_kernel, nb=nb, tq=tq, interleaved=interleaved)
    return pl.pallas_call(
        kern,
        grid=(B // nb, R // tq),
        in_specs=[pl.BlockSpec((nb, tq, D), lambda b, r: (b, r, 0)),
                  pl.BlockSpec((1, D), lambda b, r: (0, 0)),
                  pl.BlockSpec(wq.shape, lambda b, r: (0, 0)),
                  pl.BlockSpec(wo.shape, lambda b, r: (0, 0)),
                  pl.BlockSpec(mem_blk, lambda b, r: (layer, b, 0, 0)),
                  pl.BlockSpec(mem_blk, lambda b, r: (layer, b, 0, 0))],
        out_specs=pl.BlockSpec((nb, tq, D), lambda b, r: (b, r, 0)),
        out_shape=jax.ShapeDtypeStruct((B, R, D), F32),
        compiler_params=_cparams("parallel", "arbitrary"),
        name="xattn",
    )(x, g, wq, wo, mem_k, mem_v)


def _router_kernel(x_ref, g_ref, wrt_ref, route_out, route_t_out, counts_out, cnt_sc):
    @pl.when(pl.program_id(0) == 0)
    def _():
        cnt_sc[...] = jnp.zeros_like(cnt_sc)

    xn = _rms(x_ref[...], g_ref[...])
    logits_t = _dot_nt(wrt_ref[...], xn.astype(BF16))
    tm = logits_t.shape[1]
    row_g = lax.broadcasted_iota(jnp.int32, (SUBLANE, tm), 0)
    row_e = lax.broadcasted_iota(jnp.int32, (N_EXPERTS, tm), 0)
    row_e_f = row_e.astype(F32)

    def top1(v, rows):
        mx = jnp.max(v, axis=0, keepdims=True)
        return mx, jnp.min(jnp.where(v == mx, rows, float(LANE)), axis=0, keepdims=True)

    gl = jnp.where(row_g < N_GROUPS, logits_t[0:SUBLANE], NEG_INF)
    g_max, g_idx = top1(gl, row_g.astype(F32))
    g_w = 1.0 / jnp.sum(jnp.exp(gl - g_max), axis=0, keepdims=True)
    el = jnp.where((row_e // EXPERTS_PER_GROUP).astype(F32) == g_idx, logits_t[LANE:LANE + N_EXPERTS], NEG_INF)
    e1, i1 = top1(el, row_e_f)
    e2, i2 = top1(jnp.where(row_e_f == i1, NEG_INF, el), row_e_f)
    t = jnp.exp(e2 - e1)
    p1 = 1.0 / (1.0 + t)

    oh1 = (row_e_f == i1).astype(F32)
    oh2 = (row_e_f == i2).astype(F32)
    oh = oh1 + oh2
    r_idx = lax.broadcasted_iota(jnp.int32, (tm, tm), 0)
    c_idx = lax.broadcasted_iota(jnp.int32, (tm, tm), 1)
    before = (r_idx < c_idx).astype(BF16)
    base = cnt_sc[:, 0:1] + _dot(oh.astype(BF16), before)
    rank1 = jnp.sum(oh1 * base, axis=0, keepdims=True)
    rank2 = jnp.sum(oh2 * base, axis=0, keepdims=True)
    cnt = cnt_sc[...] + jnp.sum(oh, axis=1, keepdims=True)
    cnt_sc[...] = cnt
    counts_out[...] = cnt

    route_t = jnp.zeros((SUBLANE, tm), F32)
    for n, v in enumerate((i1, i2, p1 * g_w, t * p1 * g_w, rank1, rank2)):
        route_t = jnp.where(row_g == n, v, route_t)
    route_t_out[...] = route_t
    route_out[...] = jnp.concatenate([route_t, jnp.zeros((LANE - SUBLANE, tm), F32)], axis=0).T


def router(x, g, w_router_t):
    T, D = x.shape
    tm = _row_tile(T)
    return pl.pallas_call(
        _router_kernel,
        grid=(T // tm,),
        in_specs=[pl.BlockSpec((tm, D), lambda i: (i, 0)),
                  pl.BlockSpec((1, D), lambda i: (0, 0)),
                  pl.BlockSpec(w_router_t.shape, lambda i: (0, 0))],
        out_specs=[pl.BlockSpec((tm, LANE), lambda i: (i, 0)), pl.BlockSpec((SUBLANE, tm), lambda i: (0, i)),
                   pl.BlockSpec((N_EXPERTS, LANE), lambda i: (0, 0))],
        out_shape=[jax.ShapeDtypeStruct((T, LANE), F32), jax.ShapeDtypeStruct((SUBLANE, T), F32),
                   jax.ShapeDtypeStruct((N_EXPERTS, LANE), F32)],
        scratch_shapes=[pltpu.VMEM((N_EXPERTS, LANE), F32)],
        compiler_params=_cparams("arbitrary"),
        name="router",
    )(x, g, w_router_t)


def moe_plan(routes, counts, TB, tms):
    experts = jnp.arange(N_EXPERTS, dtype=jnp.int32)
    counts = [c[:, 0].astype(jnp.int32) for c in counts]
    total = sum(counts)
    padded = (total + TB - 1) // TB * TB
    pad_ends = jnp.cumsum(padded)
    base = pad_ends - padded
    dests = []
    for route_t, cnt, tm in zip(routes, counts, tms):
        T = route_t.shape[1]
        e_idx = route_t[0:TOP_K].astype(jnp.int32)
        rank = route_t[4:4 + TOP_K].astype(jnp.int32)
        dest = jnp.sum(jnp.where(e_idx[None] == experts[:, None, None], base[:, None, None], 0), axis=0) + rank
        dests.append(dest.reshape(TOP_K, T // tm, tm).transpose(1, 0, 2).reshape(T // tm, 1, TOP_K * tm))
        base = base + cnt
    n_tokens = sum(r.shape[1] for r in routes)
    n_blocks = -(-(n_tokens * TOP_K + N_EXPERTS * (TB - 1)) // TB)
    block_start = jnp.arange(n_blocks, dtype=jnp.int32) * TB
    block_e = jnp.minimum(jnp.sum(pad_ends[None, :] <= block_start[:, None], axis=1), N_EXPERTS - 1).astype(jnp.int32)
    end_valid = jnp.sum(jnp.where(block_e[:, None] == experts, pad_ends - padded + total, 0), axis=1)
    n_valid = jnp.clip(end_valid - block_start, 0, TB).astype(jnp.int32)
    return block_e, n_valid, dests


DMA_UNROLL = 8


def _for_row_chunks(n_rows, fn):
    def body(c, carry):
        for u in range(DMA_UNROLL):
            fn(c * DMA_UNROLL + u)
        return carry
    lax.fori_loop(0, n_rows // DMA_UNROLL, body, 0)


def _pack_bf16_pair(x):
    n = x.shape[1] // 2
    bits = lambda a: lax.bitcast_convert_type(a.astype(BF16).astype(F32), jnp.uint32)
    return (bits(x[:, :n]) >> 16) | (bits(x[:, n:]) & jnp.uint32(0xFFFF0000))


def _unpack_bf16_pair(p):
    lo = lax.bitcast_convert_type(p << 16, F32).astype(BF16)
    hi = lax.bitcast_convert_type(p & jnp.uint32(0xFFFF0000), F32).astype(BF16)
    return lo, hi


def _dispatch_kernel(dst_ref, x_ref, g_ref, init_hbm, xs_hbm, buf, sem, *, tm, n_tiles):
    del init_hbm
    i = pl.program_id(0)
    slot = i % 2

    def copy(row, s, row_dst):
        return pltpu.make_async_copy(buf.at[s, pl.ds(row, 1)], xs_hbm.at[pl.ds(row_dst, 1)], sem.at[s])

    def wait_slot(s):
        def wait_row(r):
            for _ in range(TOP_K):
                copy(r, s, 0).wait()
        _for_row_chunks(tm, wait_row)

    @pl.when(i >= 2)
    def _():
        wait_slot(slot)

    buf[slot] = _pack_bf16_pair(_rms(x_ref[...], g_ref[...]))

    def start_row(r):
        for k in range(TOP_K):
            copy(r, slot, dst_ref[0, 0, k * tm + r]).start(priority=k % 2)
    _for_row_chunks(tm, start_row)

    @pl.when(i == n_tiles - 1)
    def _():
        wait_slot(slot)
        if n_tiles >= 2:
            wait_slot(1 - slot)


def moe_dispatch(x, g, dest, xs_init, tm):
    T, D = x.shape
    n_tiles = T // tm
    assert tm % DMA_UNROLL == 0
    kern = functools.partial(_dispatch_kernel, tm=tm, n_tiles=n_tiles)
    return pl.pallas_call(
        kern,
        grid=(n_tiles,),
        in_specs=[pl.BlockSpec((1, 1, TOP_K * tm), lambda i: (i, 0, 0), memory_space=pltpu.SMEM),
                  pl.BlockSpec((tm, D), lambda i: (i, 0)),
                  pl.BlockSpec((1, D), lambda i: (0, 0)),
                  pl.BlockSpec(memory_space=pl.ANY)],
        out_specs=pl.BlockSpec(memory_space=pl.ANY),
        out_shape=jax.ShapeDtypeStruct(xs_init.shape, xs_init.dtype),
        input_output_aliases={3: 0},
        scratch_shapes=[pltpu.VMEM((2, tm, D // 2), jnp.uint32), pltpu.SemaphoreType.DMA((2,))],
        compiler_params=_cparams("arbitrary"),
        name="moe_dispatch",
    )(dest, x, g, xs_init)


def _ffn_kernel(be_ref, nv_ref, xs_ref, wg_ref, wu_ref, wd_ref, ys_ref, wg_bf, wu_bf, wd_bf):
    i = pl.program_id(0)
    nv = nv_ref[i]

    @pl.when((i == 0) | (be_ref[i] != be_ref[jnp.maximum(i - 1, 0)]))
    def _():
        wg_bf[...] = wg_ref[0, 0].astype(BF16)
        wu_bf[...] = wu_ref[0, 0].astype(BF16)
        wd_bf[...] = wd_ref[0, 0].astype(BF16)

    @pl.when(nv > 0)
    def _():
        row = lax.broadcasted_iota(jnp.int32, (xs_ref.shape[0], 1), 0)
        packed = jnp.where(row < nv, xs_ref[...], jnp.uint32(0))
        x_lo, x_hi = _unpack_bf16_pair(packed)
        half = x_lo.shape[1]
        g = _dot(x_lo, wg_bf[:half, :]) + _dot(x_hi, wg_bf[half:, :])
        u = _dot(x_lo, wu_bf[:half, :]) + _dot(x_hi, wu_bf[half:, :])
        hid = (g * _sigmoid(g)) * u
        ys_ref[...] = _dot(hid.astype(BF16), wd_bf[...])

    @pl.when(nv == 0)
    def _():
        ys_ref[...] = jnp.zeros_like(ys_ref)


def expert_ffn(xs, block_e, n_valid, w_gate, w_up, w_down, layer, TB):
    P, D = xs.shape[0], 2 * xs.shape[1]
    n_blocks = P // TB
    wspec = lambda a, b: pl.BlockSpec((1, 1, a, b), lambda i, be, nv: (layer, be[i], 0, 0))
    grid_spec = pltpu.PrefetchScalarGridSpec(
        num_scalar_prefetch=2,
        grid=(n_blocks,),
        in_specs=[pl.BlockSpec((TB, D // 2), lambda i, be, nv: (i, 0)),
                  wspec(D, D_EXPERT), wspec(D, D_EXPERT), wspec(D_EXPERT, D)],
        out_specs=pl.BlockSpec((TB, D), lambda i, be, nv: (i, 0)),
        scratch_shapes=[pltpu.VMEM((D, D_EXPERT), BF16), pltpu.VMEM((D, D_EXPERT), BF16),
                        pltpu.VMEM((D_EXPERT, D), BF16)],
    )
    return pl.pallas_call(
        _ffn_kernel,
        grid_spec=grid_spec,
        out_shape=jax.ShapeDtypeStruct((P, D), F32),
        compiler_params=_cparams("arbitrary"),
        name="expert_ffn",
    )(block_e, n_valid, xs, w_gate, w_up, w_down)


def _combine_kernel(cur_ref, nxt_ref, x_ref, route_ref, gf_ref, ys_hbm, out_ref, ybuf, sem, *,
                    tm, n_tiles, final_norm):
    i = pl.program_id(0)
    slot = i % 2

    def copy(row_src, s, k, row):
        return pltpu.make_async_copy(ys_hbm.at[pl.ds(row_src, 1)], ybuf.at[s, k, pl.ds(row, 1)], sem.at[s])

    def start_tile(idx_ref, s):
        def start_row(r):
            for k in range(TOP_K):
                copy(idx_ref[0, 0, k * tm + r], s, k, r).start(priority=k % 2)
        _for_row_chunks(tm, start_row)

    @pl.when(i == 0)
    def _():
        start_tile(cur_ref, 0)

    @pl.when(i + 1 < n_tiles)
    def _():
        start_tile(nxt_ref, 1 - slot)

    def wait_row(r):
        for k in range(TOP_K):
            copy(0, slot, k, r).wait()
    _for_row_chunks(tm, wait_row)

    route = route_ref[...]
    lane_i = lax.broadcasted_iota(jnp.int32, route.shape, 1)
    g0 = jnp.sum(jnp.where(lane_i == 2, route, 0.0), axis=1, keepdims=True)
    g1 = jnp.sum(jnp.where(lane_i == 3, route, 0.0), axis=1, keepdims=True)
    y = x_ref[...] + (g0 * ybuf[slot, 0] + g1 * ybuf[slot, 1])
    out_ref[...] = _rms(y, gf_ref[...]) if final_norm else y


def moe_combine(x, ys, dest, route, g_final, tm, final_norm):
    T, D = x.shape
    n_tiles = T // tm
    assert tm % DMA_UNROLL == 0
    idx_blk = lambda f: pl.BlockSpec((1, 1, TOP_K * tm), f, memory_space=pltpu.SMEM)
    kern = functools.partial(_combine_kernel, tm=tm, n_tiles=n_tiles, final_norm=final_norm)
    return pl.pallas_call(
        kern,
        grid=(n_tiles,),
        in_specs=[idx_blk(lambda i: (i, 0, 0)),
                  idx_blk(lambda i: (jnp.minimum(i + 1, n_tiles - 1), 0, 0)),
                  pl.BlockSpec((tm, D), lambda i: (i, 0)),
                  pl.BlockSpec((tm, LANE), lambda i: (i, 0)),
                  pl.BlockSpec((1, D), lambda i: (0, 0)),
                  pl.BlockSpec(memory_space=pl.ANY)],
        out_specs=pl.BlockSpec((tm, D), lambda i: (i, 0)),
        out_shape=jax.ShapeDtypeStruct((T, D), F32),
        scratch_shapes=[pltpu.VMEM((2, TOP_K, tm, D), F32), pltpu.SemaphoreType.DMA((2,))],
        compiler_params=_cparams("arbitrary"),
        name="moe_combine",
    )(dest, dest, x, route, g_final, ys)


IN_SPLITS = ((4 * M_WIDTH,), (LANE,), (2 * R_WIDTH, S_WIDTH + 2 * S_KV_WIDTH, 3 * D_MODEL))
IN_DTYPES = (F32, F32, F32, F32, BF16)


def _prep_w_in(w):
    c0 = 4 * M_WIDTH
    c1 = c0 + 2 * M_HEADS
    gates = jnp.pad(w[..., c0:c1], ((0, 0), (0, 0), (0, LANE - 2 * M_HEADS)))
    return [w[..., :c0].astype(BF16), gates.astype(BF16), w[..., c1:].astype(BF16)]


def _prep_router(w_rg, w_re):
    pad = lambda w: jnp.pad(w.T, ((0, LANE - w.shape[1]), (0, 0)))
    return jnp.concatenate([pad(w_rg), pad(w_re)], axis=0).astype(BF16)


def _row(v):
    return v.reshape(1, -1).astype(F32)


def _pad_rows(a, n):
    return jnp.pad(a, ((0, 0), (0, n - a.shape[1])) + ((0, 0),) * (a.ndim - 2))


def _moe_block(T):
    return 512 if T >= 4096 else 128


def moe_layer(xs_in, slots, g_ffn, w_router, w_gate, w_up, w_down, layer, g_final, final_norm):
    D = xs_in[0].shape[1]
    TB = _moe_block(sum(x.shape[0] for x in xs_in))
    tms = [_row_tile(x.shape[0], 512) for x in xs_in]
    routed = [router(x, g_ffn, w_router) for x in xs_in]
    routes, counts = [r[0] for r in routed], [r[2] for r in routed]
    block_e, n_valid, dests = moe_plan([r[1] for r in routed], counts, TB, tms)
    buf = jnp.zeros((block_e.shape[0] * TB, D // 2), jnp.uint32) if slots is None else slots
    for x, dest, tm in zip(xs_in, dests, tms):
        buf = moe_dispatch(x, g_ffn, dest, buf, tm)
    ys = expert_ffn(buf, block_e, n_valid, w_gate, w_up, w_down, layer, TB)
    outs = [moe_combine(x, ys, dest, route, g_final, tm, final_norm)
            for x, dest, route, tm in zip(xs_in, dests, routes, tms)]
    return outs, buf


def kernel(x_prompt, x_sample, mem_prompt, cache_mem_k, cache_mem_v, cache_swa_k, cache_swa_v, state_mlstm_C, state_mlstm_n, state_mlstm_m, state_rglru_h, state_rglru_conv, norm_mix, w_in, m_igate_b, m_fgate_b, m_head_norm, r_conv_w, r_conv_b, r_gate_a_w, r_gate_a_b, r_gate_x_w, r_gate_x_b, r_lambda, swa_sinks, rel_bias, w_branch_m, w_branch_r, w_branch_s, w_out, norm_mem, xq_w, xk_w, xv_w, xo_w, norm_ffn, router_group_w, router_expert_w, moe_w_gate, moe_w_up, moe_w_down, norm_final):
    B, S, D = x_prompt.shape
    NS, TS, _ = x_sample.shape
    depth = w_in.shape[0]
    n_mem = mem_prompt.shape[1]
    Tp, Ts = B * S, NS * TS
    H = M_HEADS
    assert S % MLSTM_CHUNK == 0 and S % WINDOW == 0 and TS <= SUBLANE
    assert NS % XATTN_SHORT_BATCH == 0 and NS % SWA_SHORT_BATCH == 0 and NS % MLSTM_SHORT_BATCH == 0

    xp = x_prompt.reshape(Tp, D)
    xs = x_sample.reshape(Ts, D)
    mem2d = mem_prompt.reshape(B * n_mem, D)
    cache_k_rows = cache_mem_k.reshape(depth, NS, n_mem * N_MEM_HEADS, X_HEAD_DIM)
    cache_v_rows = cache_mem_v.reshape(depth, NS, n_mem * N_MEM_HEADS, X_HEAD_DIM)
    k_caches = cache_swa_k.reshape(depth * NS, WINDOW, S_KV_WIDTH)
    v_caches = cache_swa_v.reshape(depth * NS, WINDOW, S_KV_WIDTH)
    w_in_all = _prep_w_in(w_in)
    w_kv_all = [xk_w.astype(BF16), xv_w.astype(BF16)]
    NBP = 2 if B % 2 == 0 else 1
    bias_p = swa_bias_rows(rel_bias, WINDOW, NBP)
    bias_s = swa_bias_rows(rel_bias, SWA_SHORT_TQ, SWA_SHORT_BATCH)
    outs_p = {k: [] for k in ("mem_k", "mem_v", "swa_k", "swa_v", "C", "n", "m", "h", "conv")}
    outs_s = {k: [] for k in ("swa_k", "swa_v", "C", "n", "m", "h", "conv")}

    slots = None
    for l in range(depth):
        g_mix = _row(norm_mix[l])
        b_i = jnp.pad(m_igate_b[l], (0, SUBLANE - H)).reshape(SUBLANE, 1)
        b_f = jnp.pad(m_fgate_b[l], (0, SUBLANE - H)).reshape(SUBLANE, 1)
        hn = _row(m_head_norm[l])
        r_args = (r_conv_w[l], _row(r_conv_b[l]), r_gate_a_w[l].astype(BF16), r_gate_x_w[l].astype(BF16),
                  _row(r_gate_a_b[l]), _row(r_gate_x_b[l]), _row(r_lambda[l]))
        sink_p = swa_sink_rows(swa_sinks[l], WINDOW, NBP)
        sink_s = swa_sink_rows(swa_sinks[l], SWA_SHORT_TQ, SWA_SHORT_BATCH)
        merge_w = (w_branch_m[l].astype(BF16), w_branch_r[l].astype(BF16), w_branch_s[l].astype(BF16),
                   w_out[l].astype(BF16))
        wq_b, wo_b = xq_w[l].astype(BF16), xo_w[l].astype(BF16)
        w_router = _prep_router(router_group_w[l], router_expert_w[l])
        last = l == depth - 1
        moe_args = (_row(norm_ffn[l]), w_router, moe_w_gate, moe_w_up, moe_w_down, l, _row(norm_final), last)

        mk, mv = norm_matmul(mem2d, g_mix, w_kv_all, ((X_WIDTH,), (X_WIDTH,)), l, normalize=False)
        outs_p["mem_k"].append(mk.reshape(B, n_mem, N_MEM_HEADS, X_HEAD_DIM))
        outs_p["mem_v"].append(mv.reshape(B, n_mem, N_MEM_HEADS, X_HEAD_DIM))
        mk, mv = mk.reshape(1, B, n_mem, X_WIDTH), mv.reshape(1, B, n_mem, X_WIDTH)

        p_m, p_if, p_r, p_s, p_g = norm_matmul(xp, g_mix, w_in_all, IN_SPLITS, l, out_dtypes=IN_DTYPES)
        gts = jnp.swapaxes(p_if[:, :2 * H].reshape(B, S, 2 * H), 1, 2)
        gi = jnp.pad(gts[:, :H], ((0, 0), (0, SUBLANE - H), (0, 0)))
        gf = jnp.pad(gts[:, H:], ((0, 0), (0, SUBLANE - H), (0, 0)))
        h_m, caug, m_o = mlstm(p_m.reshape(B, S, 4 * M_WIDTH), gi, gf, b_i, b_f, hn,
                               jnp.zeros((B, H, M_HEAD_DIM, MLSTM_AUG), F32),
                               jnp.zeros((B, SUBLANE, LANE), F32), L=MLSTM_CHUNK)
        outs_p["C"].append(caug[..., :M_HEAD_DIM])
        outs_p["n"].append(caug[..., M_HEAD_DIM])
        outs_p["m"].append(m_o[:, :H, 0])

        h_r, h_last, conv_last = rglru(p_r.reshape(B, S, 2 * R_WIDTH),
                                       jnp.zeros((B, CONV_W - 1, R_WIDTH), F32),
                                       jnp.zeros((B, 1, R_WIDTH), F32), *r_args,
                                       TR=_row_tile(S), Bs=1, out_dtype=BF16)
        outs_p["h"].append(h_last[:, 0])
        outs_p["conv"].append(conv_last)

        p_s3 = p_s.reshape(B, S, S_WIDTH + 2 * S_KV_WIDTH)
        kcol, vcol = S_WIDTH // S_KV_WIDTH, S_WIDTH // S_KV_WIDTH + 1
        prev = lambda col: (lambda b, n: (b, jnp.maximum(n - 1, 0), col))
        cur = lambda col: (lambda b, n: (b, n, col))
        kv_blk = (NBP, WINDOW, S_KV_WIDTH)
        h_s = swa((p_s3,) * 5, (cur(0), prev(kcol), cur(kcol), prev(vcol), cur(vcol)),
                  ((NBP, WINDOW, S_WIDTH), kv_blk, kv_blk, kv_blk, kv_blk),
                  bias_p, sink_p, grid=(B // NBP, S // WINDOW), NBS=NBP, TQ=WINDOW, mask_first=True,
                  out_dtype=BF16)
        outs_p["swa_k"].append(p_s3[:, S - WINDOW:, S_WIDTH:S_WIDTH + S_KV_WIDTH]
                               .reshape(B, WINDOW, S_KV_HEADS, S_HEAD_DIM))
        outs_p["swa_v"].append(p_s3[:, S - WINDOW:, S_WIDTH + S_KV_WIDTH:]
                               .reshape(B, WINDOW, S_KV_HEADS, S_HEAD_DIM))

        xp = merge_out(h_m.reshape(Tp, M_WIDTH), h_r.reshape(Tp, R_WIDTH), h_s.reshape(Tp, S_WIDTH),
                       p_g, xp, *merge_w)
        xp = xattn(xp.reshape(B, S, D), _row(norm_mem[l]), wq_b, wo_b, mk, mv, layer=0,
                   nb=1, tq=512 if S % 512 == 0 else _row_tile(S), interleaved=False).reshape(Tp, D)

        p_m, p_if, p_r, p_s, p_g = norm_matmul(xs, g_mix, w_in_all, IN_SPLITS, l, out_dtypes=IN_DTYPES)
        gts = jnp.swapaxes(p_if[:, :2 * H].reshape(NS, TS, 2 * H), 1, 2)
        gts = jnp.pad(gts, ((0, 0), (0, 0), (0, LANE - TS)))
        gi = jnp.pad(gts[:, :H], ((0, 0), (0, SUBLANE - H), (0, 0)))
        gf = jnp.pad(gts[:, H:], ((0, 0), (0, SUBLANE - H), (0, 0)))
        m0 = jnp.broadcast_to(jnp.pad(state_mlstm_m[l], ((0, 0), (0, SUBLANE - H)))[:, :, None],
                              (NS, SUBLANE, LANE))
        h_m, c_s, n_s, m_o = mlstm_short(_pad_rows(p_m.reshape(NS, TS, 4 * M_WIDTH), SUBLANE), gi, gf, b_i, b_f,
                                         hn, state_mlstm_C, state_mlstm_n, m0, layer=l,
                                         NBS=MLSTM_SHORT_BATCH, TS=TS)
        h_m = h_m[:, :TS].reshape(Ts, M_WIDTH)
        outs_s["C"].append(c_s)
        outs_s["n"].append(n_s)
        outs_s["m"].append(m_o[:, :H, 0])

        tmaj = lambda a: jnp.swapaxes(a, 0, 1).reshape(1, a.shape[0] * a.shape[1], a.shape[2])
        h_r, h_last, conv_last = rglru(tmaj(p_r.reshape(NS, TS, 2 * R_WIDTH)), tmaj(state_rglru_conv[l]),
                                       state_rglru_h[l].reshape(1, NS, R_WIDTH), *r_args,
                                       TR=TS * NS, Bs=NS, out_dtype=F32)
        h_r = jnp.swapaxes(h_r.reshape(TS, NS, R_WIDTH), 0, 1).reshape(Ts, R_WIDTH)
        outs_s["h"].append(h_last[0])
        outs_s["conv"].append(jnp.swapaxes(conv_last.reshape(CONV_W - 1, NS, R_WIDTH), 0, 1))

        TQ, NBS = SWA_SHORT_TQ, SWA_SHORT_BATCH
        p_s3 = p_s.reshape(NS, TS, S_WIDTH + 2 * S_KV_WIDTH)
        q_s = _pad_rows(p_s3[:, :, :S_WIDTH], TQ)
        k_new = p_s3[:, :, S_WIDTH:S_WIDTH + S_KV_WIDTH]
        v_new = p_s3[:, :, S_WIDTH + S_KV_WIDTH:]
        full = lambda b, n: (b, 0, 0)
        cache = lambda b, n: (l * (NS // NBS) + b, 0, 0)
        h_s = swa((q_s, k_caches, _pad_rows(k_new, TQ), v_caches, _pad_rows(v_new, TQ)),
                  (full, cache, full, cache, full),
                  ((NBS, TQ, S_WIDTH), (NBS, WINDOW, S_KV_WIDTH), (NBS, TQ, S_KV_WIDTH),
                   (NBS, WINDOW, S_KV_WIDTH), (NBS, TQ, S_KV_WIDTH)),
                  bias_s, sink_s, grid=(NS // NBS, 1), NBS=NBS, TQ=TQ, mask_first=False, out_dtype=F32)
        h_s = h_s[:, :TS].reshape(Ts, S_WIDTH)
        outs_s["swa_k"].append(k_new)
        outs_s["swa_v"].append(v_new)

        xs = merge_out(h_m, h_r, h_s, p_g, xs, *merge_w)
        xs8 = xattn(_pad_rows(xs.reshape(NS, TS, D), SUBLANE), _row(norm_mem[l]), wq_b, wo_b,
                    cache_k_rows, cache_v_rows, layer=l, nb=XATTN_SHORT_BATCH, tq=SUBLANE, interleaved=True)

        (xp, xs), slots = moe_layer([xp, xs8[:, :TS].reshape(Ts, D)], slots, *moe_args)

    st = lambda d, k: jnp.stack(d[k])

    def shifted_cache(caches, new):
        out = jnp.concatenate([caches.reshape(depth, NS, WINDOW, S_KV_WIDTH)[:, :, TS:], jnp.stack(new)], axis=2)
        return out.reshape(depth, NS, WINDOW, S_KV_HEADS, S_HEAD_DIM)

    return (xp.reshape(B, S, D), xs.reshape(NS, TS, D), st(outs_p, "mem_k"), st(outs_p, "mem_v"),
            st(outs_p, "swa_k"), st(outs_p, "swa_v"), st(outs_p, "C"), st(outs_p, "n"), st(outs_p, "m"),
            st(outs_p, "h"), st(outs_p, "conv"),
            shifted_cache(k_caches, outs_s["swa_k"]), shifted_cache(v_caches, outs_s["swa_v"]),
            st(outs_s, "C"), st(outs_s, "n"), st(outs_s, "m"), st(outs_s, "h"), st(outs_s, "conv"))
```

```python
import functools
import math

import jax
import jax.numpy as jnp
from jax import lax
from jax.experimental import pallas as pl
from jax.experimental.pallas import tpu as pltpu

F32 = jnp.float32
BF16 = jnp.bfloat16
NEG_INF = float("-inf")

LANE = 128
SUBLANE = 8
BF16_SUBLANE = 16
VMEM_LIMIT_BYTES = 56 * 1024 * 1024

D_MODEL = 1024
M_HEADS = 4
M_HEAD_DIM = 128
M_WIDTH = M_HEADS * M_HEAD_DIM
R_WIDTH = 512
R_BLOCKS = 4
R_BLOCK_DIM = R_WIDTH // R_BLOCKS
CONV_W = 4
LRU_C = 8.0
S_HEADS = 8
S_KV_HEADS = 2
S_HEAD_DIM = 64
S_WIDTH = S_HEADS * S_HEAD_DIM
S_KV_WIDTH = S_KV_HEADS * S_HEAD_DIM
WINDOW = 128
N_BUCKETS = 32
MAX_DISTANCE = 128
N_MEM_HEADS = 4
X_HEAD_DIM = 128
X_WIDTH = N_MEM_HEADS * X_HEAD_DIM
N_GROUPS = 4
EXPERTS_PER_GROUP = 8
N_EXPERTS = N_GROUPS * EXPERTS_PER_GROUP
TOP_K = 2
D_EXPERT = 512
EPS = 1e-6

MLSTM_CHUNK = 512
MLSTM_AUG = 2 * M_HEAD_DIM
MLSTM_SHORT_BATCH = 2
SWA_SHORT_TQ = BF16_SUBLANE
SWA_SHORT_BATCH = 16
XATTN_SHORT_BATCH = 16


def _cparams(*sem):
    return pltpu.CompilerParams(dimension_semantics=sem, vmem_limit_bytes=VMEM_LIMIT_BYTES)


def _rms(x, g):
    ms = jnp.mean(x * x, axis=-1, keepdims=True)
    return x * lax.rsqrt(ms + EPS) * g


def _sigmoid(x):
    return 1.0 / (1.0 + jnp.exp(-x))


def _softplus(x):
    return jnp.maximum(x, 0.0) + jnp.log1p(jnp.exp(-jnp.abs(x)))


def _log_sigmoid(x):
    return -_softplus(-x)


def _gelu_tanh(x):
    return 0.5 * x * (1.0 + jnp.tanh(math.sqrt(2.0 / math.pi) * (x + 0.044715 * (x * x * x))))


def _dot(a, b):
    return jnp.dot(a, b, preferred_element_type=F32)


def _dot_nt(a, b):
    return lax.dot_general(a, b, (((1,), (1,)), ((), ())), preferred_element_type=F32)


def _row_tile(n, pref=256):
    for t in (512, 256, 128, 64, 32, 16, 8):
        if t <= pref and n % t == 0:
            return t
    raise ValueError(f"row count {n} is not a multiple of {SUBLANE}")


def _norm_matmul_kernel(x_ref, g_ref, *refs, splits, normalize):
    w_refs, out_refs = refs[:len(splits)], refs[len(splits):]
    x = x_ref[...]
    if normalize:
        x = _rms(x, g_ref[...])
    xb = x.astype(BF16)
    outs = iter(out_refs)
    for w_ref, w_splits in zip(w_refs, splits):
        off = 0
        for n in w_splits:
            o_ref = next(outs)
            o_ref[...] = _dot(xb, w_ref[0, :, off:off + n]).astype(o_ref.dtype)
            off += n


def norm_matmul(x, g, weights, splits, layer, normalize=True, out_dtypes=None):
    flat = [n for w_splits in splits for n in w_splits]
    out_dtypes = out_dtypes or (F32,) * len(flat)
    T, D = x.shape
    tm = _row_tile(T)
    assert all(sum(ws) == w.shape[2] for ws, w in zip(splits, weights)) and all(n % LANE == 0 for n in flat)
    kern = functools.partial(_norm_matmul_kernel, splits=tuple(tuple(ws) for ws in splits), normalize=normalize)
    return pl.pallas_call(
        kern,
        grid=(T // tm,),
        in_specs=[pl.BlockSpec((tm, D), lambda i: (i, 0)),
                  pl.BlockSpec((1, D), lambda i: (0, 0))]
                 + [pl.BlockSpec((1, D, w.shape[2]), lambda i: (layer, 0, 0)) for w in weights],
        out_specs=[pl.BlockSpec((tm, n), lambda i: (i, 0)) for n in flat],
        out_shape=[jax.ShapeDtypeStruct((T, n), dt) for n, dt in zip(flat, out_dtypes)],
        compiler_params=_cparams("parallel"),
        name="norm_matmul",
    )(x, g, *weights)


def _mlstm_kernel(q_ref, k_ref, v_ref, o_ref, gi_ref, gf_ref, bi_ref, bf_ref, hn_ref,
                  c0_ref, m0_ref, h_out, c_out, m_out, c_sc, m_sc, *, L, NBB):
    @pl.when(pl.program_id(1) == 0)
    def _():
        c_sc[...] = c0_ref[...]
        m_sc[...] = m0_ref[...]

    t_idx = lax.broadcasted_iota(jnp.int32, (L, L), 0)
    s_idx = lax.broadcasted_iota(jnp.int32, (L, L), 1)
    causal = t_idx >= s_idx
    eye = t_idx == s_idx
    row8 = lax.broadcasted_iota(jnp.int32, (SUBLANE, LANE), 0)
    ones_col = (lax.broadcasted_iota(jnp.int32, (L, M_HEAD_DIM), 1) == 0).astype(BF16)
    gi = [gi_ref[b] + bi_ref[...] for b in range(NBB)]
    gf = [_log_sigmoid(gf_ref[b] + bf_ref[...]) for b in range(NBB)]
    m_tiles = [m_sc[b] for b in range(NBB)]
    chains = [dict(b=b, h=h, hs=slice(h * M_HEAD_DIM, (h + 1) * M_HEAD_DIM))
              for b in range(NBB) for h in range(M_HEADS)]

    for c in chains:
        b, h = c["b"], c["h"]
        c["m_prev"] = m_tiles[b][h:h + 1, 0:1]
        c["b_col"] = jnp.sum(jnp.where(causal, gf[b][h:h + 1, :], 0.0), axis=1, keepdims=True)
    for c in chains:
        b, h = c["b"], c["h"]
        b_row = jnp.sum(jnp.where(eye, c["b_col"], 0.0), axis=0, keepdims=True)
        c["r_row"] = gi[b][h:h + 1, :] - b_row
        c["g_end"] = b_row[:, L - 1:L]
        c["log_in"] = jnp.where(causal, c["b_col"] + c["r_row"], NEG_INF)
        c["m_intra"] = jnp.max(c["log_in"], axis=1, keepdims=True)
    for c in chains:
        b, hs = c["b"], c["hs"]
        c["k"] = k_ref[b][:, hs] * (M_HEAD_DIM ** -0.5)
        c["caug"] = c_sc[b, c["h"]]
        c["vaug"] = jnp.concatenate([v_ref[b][:, hs].astype(BF16), ones_col], axis=1)
        qb = q_ref[b][:, hs].astype(BF16)
        c["qk"] = _dot_nt(qb, c["k"].astype(BF16))
        c["qc"] = _dot(qb, c["caug"].astype(BF16))
    for c in chains:
        log_prev = c["b_col"] + c["m_prev"]
        m_t = jnp.maximum(log_prev, c["m_intra"])
        sc = (c["qk"] * jnp.exp(c["log_in"] - m_t)).astype(BF16)
        nd = jnp.exp(log_prev - m_t) * c["qc"] + _dot(sc, c["vaug"])
        den = nd[:, M_HEAD_DIM:M_HEAD_DIM + 1]
        c["hh"] = nd[:, :M_HEAD_DIM] / jnp.maximum(jnp.abs(den), jnp.exp(-m_t))
    for c in chains:
        b, hs = c["b"], c["hs"]
        d = c["hh"] - jnp.mean(c["hh"], axis=-1, keepdims=True)
        y = d * lax.rsqrt(jnp.mean(d * d, axis=-1, keepdims=True) + EPS) * hn_ref[:, hs]
        h_out[b, :, hs] = (y * _sigmoid(o_ref[b][:, hs])).astype(h_out.dtype)
    for c in chains:
        b, h = c["b"], c["h"]
        lie = c["g_end"] + c["r_row"]
        m_end = jnp.maximum(c["g_end"] + c["m_prev"], jnp.max(lie, axis=1, keepdims=True))
        w_pe = jnp.exp(c["g_end"] + c["m_prev"] - m_end)
        ktw = (c["k"].T * jnp.exp(lie - m_end)).astype(BF16)
        c_new = w_pe * c["caug"] + _dot(ktw, c["vaug"])
        c_sc[b, h] = c_new
        c_out[b, h] = c_new
        m_tiles[b] = jnp.where(row8 == h, m_end, m_tiles[b])
    for b in range(NBB):
        m_sc[b] = m_tiles[b]
        m_out[b] = m_tiles[b]


def mlstm(p_m, gates_i, gates_f, b_i, b_f, head_norm, c0aug, m0, L):
    NB, R, _ = p_m.shape
    nc = R // L
    assert R % L == 0
    NBB = 2 if NB % 2 == 0 else 1
    blk = lambda j: pl.BlockSpec((NBB, L, M_WIDTH), lambda b, c, j=j: (b, c, j))
    state_spec = pl.BlockSpec((NBB, M_HEADS, M_HEAD_DIM, MLSTM_AUG), lambda b, c: (b, 0, 0, 0))
    m_spec = pl.BlockSpec((NBB, SUBLANE, LANE), lambda b, c: (b, 0, 0))
    gate_spec = pl.BlockSpec((NBB, SUBLANE, L), lambda b, c: (b, 0, c))
    kern = functools.partial(_mlstm_kernel, L=L, NBB=NBB)
    scratch = [pltpu.VMEM((NBB, M_HEADS, M_HEAD_DIM, MLSTM_AUG), F32), pltpu.VMEM((NBB, SUBLANE, LANE), F32)]
    return pl.pallas_call(
        kern,
        grid=(NB // NBB, nc),
        in_specs=[blk(0), blk(1), blk(2), blk(3), gate_spec, gate_spec,
                  pl.BlockSpec((SUBLANE, 1), lambda b, c: (0, 0)),
                  pl.BlockSpec((SUBLANE, 1), lambda b, c: (0, 0)),
                  pl.BlockSpec((1, M_WIDTH), lambda b, c: (0, 0)),
                  state_spec, m_spec],
        out_specs=[pl.BlockSpec((NBB, L, M_WIDTH), lambda b, c: (b, c, 0)), state_spec, m_spec],
        out_shape=[jax.ShapeDtypeStruct((NB, R, M_WIDTH), BF16),
                   jax.ShapeDtypeStruct((NB, M_HEADS, M_HEAD_DIM, MLSTM_AUG), F32),
                   jax.ShapeDtypeStruct((NB, SUBLANE, LANE), F32)],
        scratch_shapes=scratch,
        compiler_params=_cparams("parallel", "arbitrary"),
        name="mlstm",
    )(p_m, p_m, p_m, p_m, gates_i, gates_f, b_i, b_f, head_norm, c0aug, m0)


def _mlstm_short_kernel(q_ref, k_ref, v_ref, o_ref, gi_ref, gf_ref, bi_ref, bf_ref, hn_ref,
                        c0_ref, n0_ref, m0_ref, h_out, c_out, n_out, m_out, *, NBS, TS):
    L = SUBLANE
    t_idx = lax.broadcasted_iota(jnp.int32, (L, L), 0)
    s_idx = lax.broadcasted_iota(jnp.int32, (L, L), 1)
    causal = t_idx >= s_idx
    eye = t_idx == s_idx
    valid = lax.broadcasted_iota(jnp.int32, (1, L), 1) < TS
    row8 = lax.broadcasted_iota(jnp.int32, (SUBLANE, LANE), 0)
    gi = [gi_ref[b] + bi_ref[...] for b in range(NBS)]
    gf = [_log_sigmoid(gf_ref[b] + bf_ref[...]) for b in range(NBS)]
    m_tiles = [m0_ref[b] for b in range(NBS)]
    chains = [dict(b=b, h=h, hs=slice(h * M_HEAD_DIM, (h + 1) * M_HEAD_DIM))
              for b in range(NBS) for h in range(M_HEADS)]

    for c in chains:
        b, h = c["b"], c["h"]
        c["li"] = jnp.where(valid, gi[b][h:h + 1, 0:L], NEG_INF)
        lf = jnp.where(valid, gf[b][h:h + 1, 0:L], 0.0)
        c["m_prev"] = m_tiles[b][h:h + 1, 0:1]
        c["b_col"] = jnp.sum(jnp.where(causal, lf, 0.0), axis=1, keepdims=True)
    for c in chains:
        b_row = jnp.sum(jnp.where(eye, c["b_col"], 0.0), axis=0, keepdims=True)
        c["r_row"] = c["li"] - b_row
        c["g_end"] = b_row[:, L - 1:L]
        c["log_in"] = jnp.where(causal, c["b_col"] + c["r_row"], NEG_INF)
        c["m_intra"] = jnp.max(c["log_in"], axis=1, keepdims=True)
    for c in chains:
        b, hs = c["b"], c["hs"]
        c["q"] = q_ref[b][:, hs]
        c["k"] = k_ref[b][:, hs] * (M_HEAD_DIM ** -0.5)
        c["vb"] = v_ref[b][:, hs].astype(BF16)
        c["c0"] = c0_ref[0, b, c["h"]]
        c["n0"] = n0_ref[0, b, c["h"]:c["h"] + 1, :]
        qb = c["q"].astype(BF16)
        c["qk"] = _dot_nt(qb, c["k"].astype(BF16))
        c["qc"] = _dot(qb, c["c0"].astype(BF16))
        c["qn"] = jnp.sum(c["q"] * c["n0"], axis=1, keepdims=True)
    for c in chains:
        log_prev = c["b_col"] + c["m_prev"]
        m_t = jnp.maximum(log_prev, c["m_intra"])
        w_prev = jnp.exp(log_prev - m_t)
        sc = c["qk"] * jnp.exp(c["log_in"] - m_t)
        num = w_prev * c["qc"] + _dot(sc.astype(BF16), c["vb"])
        den = w_prev * c["qn"] + jnp.sum(sc, axis=1, keepdims=True)
        c["hh"] = num / jnp.maximum(jnp.abs(den), jnp.exp(-m_t))
    for c in chains:
        b, hs = c["b"], c["hs"]
        d = c["hh"] - jnp.mean(c["hh"], axis=-1, keepdims=True)
        y = d * lax.rsqrt(jnp.mean(d * d, axis=-1, keepdims=True) + EPS) * hn_ref[:, hs]
        h_out[b, :, hs] = y * _sigmoid(o_ref[b][:, hs])
    for c in chains:
        b, h = c["b"], c["h"]
        lie = c["g_end"] + c["r_row"]
        m_end = jnp.maximum(c["g_end"] + c["m_prev"], jnp.max(lie, axis=1, keepdims=True))
        w_pe = jnp.exp(c["g_end"] + c["m_prev"] - m_end)
        w_e = jnp.exp(lie - m_end)
        w_e_col = jnp.sum(jnp.where(eye, w_e, 0.0), axis=1, keepdims=True)
        kw = c["k"] * w_e_col
        c_out[b, h] = w_pe * c["c0"] + lax.dot_general(kw.astype(BF16), c["vb"], (((0,), (0,)), ((), ())),
                                                       preferred_element_type=F32)
        n_out[b, h:h + 1, :] = w_pe * c["n0"] + jnp.sum(kw, axis=0, keepdims=True)
        m_tiles[b] = jnp.where(row8 == h, m_end, m_tiles[b])
    for b in range(NBS):
        m_out[b] = m_tiles[b]


def mlstm_short(p_m, gates_i, gates_f, b_i, b_f, head_norm, c0, n0, m0, layer, NBS, TS):
    NB = p_m.shape[0]
    H, DH = M_HEADS, M_HEAD_DIM
    assert NB % NBS == 0 and TS <= SUBLANE
    blk = lambda j: pl.BlockSpec((NBS, SUBLANE, M_WIDTH), lambda i, j=j: (i, 0, j))
    tile = pl.BlockSpec((NBS, SUBLANE, LANE), lambda i: (i, 0, 0))
    kern = functools.partial(_mlstm_short_kernel, NBS=NBS, TS=TS)
    return pl.pallas_call(
        kern,
        grid=(NB // NBS,),
        in_specs=[blk(0), blk(1), blk(2), blk(3), tile, tile,
                  pl.BlockSpec((SUBLANE, 1), lambda i: (0, 0)),
                  pl.BlockSpec((SUBLANE, 1), lambda i: (0, 0)),
                  pl.BlockSpec((1, M_WIDTH), lambda i: (0, 0)),
                  pl.BlockSpec((1, NBS, H, DH, DH), lambda i: (layer, i, 0, 0, 0)),
                  pl.BlockSpec((1, NBS, H, DH), lambda i: (layer, i, 0, 0)),
                  tile],
        out_specs=[pl.BlockSpec((NBS, SUBLANE, M_WIDTH), lambda i: (i, 0, 0)),
                   pl.BlockSpec((NBS, H, DH, DH), lambda i: (i, 0, 0, 0)),
                   pl.BlockSpec((NBS, H, DH), lambda i: (i, 0, 0)),
                   tile],
        out_shape=[jax.ShapeDtypeStruct((NB, SUBLANE, M_WIDTH), F32),
                   jax.ShapeDtypeStruct((NB, H, DH, DH), F32),
                   jax.ShapeDtypeStruct((NB, H, DH), F32),
                   jax.ShapeDtypeStruct((NB, SUBLANE, LANE), F32)],
        compiler_params=_cparams("parallel"),
        name="mlstm_short",
    )(p_m, p_m, p_m, p_m, gates_i, gates_f, b_i, b_f, head_norm, c0, n0, m0)


def _rglru_kernel(x_ref, g_ref, conv0_ref, h0_ref, cw_ref, cb_ref, wa_ref, wx_ref, ba_ref, bx_ref,
                  lam_ref, y_out, hlast_out, conv_out, xpad_sc, hc_sc, *, TR, Bs):
    CB = (CONV_W - 1) * Bs
    X0 = -(-CB // SUBLANE) * SUBLANE

    @pl.when(pl.program_id(1) == 0)
    def _():
        xpad_sc[X0 - CB:X0, :] = conv0_ref[0]
        hc_sc[...] = h0_ref[0]

    xpad_sc[X0:X0 + TR, :] = x_ref[0]
    y = cb_ref[...]
    for j in range(CONV_W):
        y = y + xpad_sc[X0 - CB + j * Bs:X0 - CB + j * Bs + TR, :] * cw_ref[j:j + 1, :]
    tail = xpad_sc[X0 + TR - CB:X0 + TR, :]
    conv_out[0] = tail
    xpad_sc[X0 - CB:X0, :] = tail

    yb = y.astype(BF16)
    rs, is_ = [], []
    for n in range(R_BLOCKS):
        sl = slice(n * R_BLOCK_DIM, (n + 1) * R_BLOCK_DIM)
        rs.append(_dot(yb[:, sl], wa_ref[n]))
        is_.append(_dot(yb[:, sl], wx_ref[n]))
    r = _sigmoid(jnp.concatenate(rs, axis=1) + ba_ref[...])
    i = _sigmoid(jnp.concatenate(is_, axis=1) + bx_ref[...])
    log_a = -LRU_C * r * _softplus(-lam_ref[...])
    a_cum = jnp.exp(log_a)
    t = jnp.tanh(log_a)
    u_cum = jnp.sqrt(-2.0 * t / (1.0 - t)) * (i * y)

    def scan_rows(a, u, axis, stride):
        n = a.shape[axis]
        idx = lax.broadcasted_iota(jnp.int32, a.shape, axis)
        d = stride
        while d < n:
            keep = idx >= d
            u = jnp.where(keep, a * pltpu.roll(u, d, axis=axis) + u, u)
            a = jnp.where(keep, a * pltpu.roll(a, d, axis=axis), a)
            d *= 2
        return a, u

    hc = hc_sc[...]
    if Bs == 1:
        NG = TR // SUBLANE
        a3, u3 = scan_rows(a_cum.reshape(NG, SUBLANE, R_WIDTH), u_cum.reshape(NG, SUBLANE, R_WIDTH), 1, 1)
        a_tot, u_tot = scan_rows(a3[:, SUBLANE - 1, :], u3[:, SUBLANE - 1, :], 0, 1)
        h_end = u_tot + a_tot * hc
        g_idx = lax.broadcasted_iota(jnp.int32, (NG, R_WIDTH), 0)
        h_start = jnp.where(g_idx == 0, hc, pltpu.roll(h_end, 1, axis=0))
        h = (u3 + a3 * h_start[:, None, :]).reshape(TR, R_WIDTH)
    else:
        a_cum, u_cum = scan_rows(a_cum, u_cum, 0, Bs)
        h = u_cum + a_cum * jnp.concatenate([hc] * (TR // Bs), axis=0)
    y_out[0] = (h * _gelu_tanh(g_ref[0])).astype(y_out.dtype)
    h_last = h[TR - Bs:, :]
    hc_sc[...] = h_last
    hlast_out[0] = h_last


def rglru(p_r, conv0, h0, cw, cb, wa, wx, ba, bx, lam, TR, Bs, out_dtype):
    G, R, _ = p_r.shape
    CB = (CONV_W - 1) * Bs
    X0 = -(-CB // SUBLANE) * SUBLANE
    assert R % TR == 0 and TR % Bs == 0 and TR >= CB
    W = R_WIDTH
    const = lambda shape: pl.BlockSpec(shape, lambda g, r: (0,) * len(shape))
    kern = functools.partial(_rglru_kernel, TR=TR, Bs=Bs)
    return pl.pallas_call(
        kern,
        grid=(G, R // TR),
        in_specs=[pl.BlockSpec((1, TR, W), lambda g, r: (g, r, 0)),
                  pl.BlockSpec((1, TR, W), lambda g, r: (g, r, 1)),
                  pl.BlockSpec((1, CB, W), lambda g, r: (g, 0, 0)),
                  pl.BlockSpec((1, Bs, W), lambda g, r: (g, 0, 0)),
                  const((CONV_W, W)), const((1, W)),
                  const((R_BLOCKS, R_BLOCK_DIM, R_BLOCK_DIM)), const((R_BLOCKS, R_BLOCK_DIM, R_BLOCK_DIM)),
                  const((1, W)), const((1, W)), const((1, W))],
        out_specs=[pl.BlockSpec((1, TR, W), lambda g, r: (g, r, 0)),
                   pl.BlockSpec((1, Bs, W), lambda g, r: (g, 0, 0)),
                   pl.BlockSpec((1, CB, W), lambda g, r: (g, 0, 0))],
        out_shape=[jax.ShapeDtypeStruct((G, R, W), out_dtype),
                   jax.ShapeDtypeStruct((G, Bs, W), F32),
                   jax.ShapeDtypeStruct((G, CB, W), F32)],
        scratch_shapes=[pltpu.VMEM((X0 + TR, W), F32), pltpu.VMEM((Bs, W), F32)],
        compiler_params=_cparams("parallel", "arbitrary"),
        name="rglru",
    )(p_r, p_r, conv0, h0, cw, cb, wa, wx, ba, bx, lam)


def _swa_kernel(q_ref, kp_ref, kc_ref, vp_ref, vc_ref, bias_ref, sink_ref, o_ref, *, NBS, TQ, mask_first):
    GQ = S_HEADS // S_KV_HEADS
    lane = lax.broadcasted_iota(jnp.int32, (1, LANE), 1)
    lo = lane < S_HEAD_DIM

    def both_halves(x, j):
        xr = pltpu.roll(x, S_HEAD_DIM, axis=1)
        return (jnp.where(lo, x, xr) if j == 0 else jnp.where(lo, xr, x)).astype(BF16)

    s1_parts, s2_parts, v_parts = [], [], []
    for b in range(NBS):
        q = q_ref[b] * (S_HEAD_DIM ** -0.5)
        for j in range(S_KV_HEADS):
            stack = []
            for g in range(GQ):
                h = GQ * j + g
                qh = q[:, (h // 2) * LANE:(h // 2 + 1) * LANE]
                stack.append(jnp.where(lo if h % 2 == 0 else jnp.logical_not(lo), qh, 0.0))
            qs = jnp.concatenate(stack, axis=0).astype(BF16)
            s1_parts.append(_dot_nt(qs, both_halves(kp_ref[b], j)))
            s2_parts.append(_dot_nt(qs, both_halves(kc_ref[b], j)))
            v_parts.append((both_halves(vp_ref[b], j), both_halves(vc_ref[b], j)))
    s1 = jnp.concatenate(s1_parts, axis=0) + bias_ref[:, 0:WINDOW]
    s2 = jnp.concatenate(s2_parts, axis=0) + bias_ref[:, WINDOW:WINDOW + TQ]
    if mask_first:
        s1 = jnp.where(pl.program_id(1) == 0, NEG_INF, s1)
    sink = sink_ref[...]
    mx = jnp.maximum(jnp.maximum(jnp.max(s1, axis=1, keepdims=True), jnp.max(s2, axis=1, keepdims=True)), sink)
    p1 = jnp.exp(s1 - mx)
    p2 = jnp.exp(s2 - mx)
    inv = 1.0 / (jnp.sum(p1, axis=1, keepdims=True) + jnp.sum(p2, axis=1, keepdims=True) + jnp.exp(sink - mx))
    p1 = p1.astype(BF16)
    p2 = p2.astype(BF16)
    R = GQ * TQ
    for b in range(NBS):
        for j in range(S_KV_HEADS):
            n = b * S_KV_HEADS + j
            rows = slice(n * R, (n + 1) * R)
            vp, vc = v_parts[n]
            o = (_dot(p1[rows], vp) + _dot(p2[rows], vc)) * inv[rows]
            for pair in range(GQ // 2):
                even = o[(2 * pair) * TQ:(2 * pair + 1) * TQ]
                odd = o[(2 * pair + 1) * TQ:(2 * pair + 2) * TQ]
                blk = (GQ * j) // 2 + pair
                o_ref[b, :, blk * LANE:(blk + 1) * LANE] = jnp.where(lo, even, odd).astype(o_ref.dtype)


def swa(arrs, maps, shapes, bias_rows, sink_rows, grid, NBS, TQ, mask_first, out_dtype):
    kern = functools.partial(_swa_kernel, NBS=NBS, TQ=TQ, mask_first=mask_first)
    in_specs = [pl.BlockSpec(s, m) for s, m in zip(shapes, maps)]
    in_specs += [pl.BlockSpec(bias_rows.shape, lambda b, n: (0, 0)),
                 pl.BlockSpec(sink_rows.shape, lambda b, n: (0, 0))]
    return pl.pallas_call(
        kern,
        grid=grid,
        in_specs=in_specs,
        out_specs=pl.BlockSpec((NBS, TQ, S_WIDTH), lambda b, n: (b, n, 0)),
        out_shape=jax.ShapeDtypeStruct((grid[0] * NBS, grid[1] * TQ, S_WIDTH), out_dtype),
        compiler_params=_cparams("parallel", "arbitrary"),
        name="swa",
    )(*arrs, bias_rows, sink_rows)


def t5_bucket(dist):
    max_exact = N_BUCKETS // 2
    d = jnp.maximum(dist, 0)
    large = max_exact + (jnp.log(jnp.maximum(d, 1).astype(F32) / max_exact)
                         / math.log(MAX_DISTANCE / max_exact) * (N_BUCKETS - max_exact)).astype(jnp.int32)
    return jnp.where(d < max_exact, d, jnp.minimum(large, N_BUCKETS - 1))


def swa_bias_rows(rel_bias, TQ, NBS):
    qi = jnp.arange(TQ)[:, None]
    kj = jnp.arange(WINDOW + TQ)[None, :]
    dist = qi + WINDOW - kj
    onehot = (t5_bucket(dist)[..., None] == jnp.arange(N_BUCKETS)).astype(F32)
    b = jnp.einsum("qkn,nh->hqk", onehot, rel_bias.astype(F32), precision=lax.Precision.HIGHEST)
    b = jnp.where((dist >= 0) & (dist <= WINDOW), b, NEG_INF).reshape(S_HEADS * TQ, WINDOW + TQ)
    return jnp.tile(b, (NBS, 1))


def swa_sink_rows(sinks, TQ, NBS):
    return jnp.tile(jnp.repeat(sinks.astype(F32), TQ), NBS).reshape(-1, 1)


def _merge_kernel(hm, hr, hs, gm, gr, gs, x, wm, wr, ws, wo, out):
    def branch(h_ref, g_ref, w_ref):
        return _sigmoid(g_ref[...].astype(F32)) * _dot(h_ref[...].astype(BF16), w_ref[...])
    merged = branch(hm, gm, wm) + branch(hr, gr, wr) + branch(hs, gs, ws)
    out[...] = x[...] + _dot(merged.astype(BF16), wo[...])


def merge_out(hm, hr, hs, p_g, x, wm, wr, ws, wo):
    T, D = x.shape
    tm = _row_tile(T, 512)
    hspec = lambda w: pl.BlockSpec((tm, w), lambda i: (i, 0))
    gspec = lambda j: pl.BlockSpec((tm, D), lambda i, j=j: (i, j))
    wspec = lambda a: pl.BlockSpec(a.shape, lambda i: (0, 0))
    return pl.pallas_call(
        _merge_kernel,
        grid=(T // tm,),
        in_specs=[hspec(M_WIDTH), hspec(R_WIDTH), hspec(S_WIDTH), gspec(0), gspec(1), gspec(2),
                  hspec(D), wspec(wm), wspec(wr), wspec(ws), wspec(wo)],
        out_specs=hspec(D),
        out_shape=jax.ShapeDtypeStruct((T, D), F32),
        compiler_params=_cparams("parallel"),
        name="merge_out",
    )(hm, hr, hs, p_g, p_g, p_g, x, wm, wr, ws, wo)


def _xattn_kernel(x_ref, g_ref, wq_ref, wo_ref, mk_ref, mv_ref, out_ref, *, nb, tq, interleaved):
    NH = N_MEM_HEADS
    x = x_ref[...].reshape(nb * tq, D_MODEL)
    q = _dot(_rms(x, g_ref[...]).astype(BF16), wq_ref[...])
    head = lambda a, h: a[:, h * X_HEAD_DIM:(h + 1) * X_HEAD_DIM]
    parts = []
    for b in range(nb):
        qb = q[b * tq:(b + 1) * tq]
        if interleaved:
            qs = jnp.concatenate([head(qb, h) for h in range(NH)], axis=0).astype(BF16)
            parts.append(_dot_nt(qs, mk_ref[0, b].astype(BF16)))
        else:
            parts += [_dot_nt(head(qb, h).astype(BF16), head(mk_ref[0, b], h).astype(BF16)) for h in range(NH)]
    s = jnp.concatenate(parts, axis=0) * (X_HEAD_DIM ** -0.5)
    if interleaved:
        row_h = (lax.broadcasted_iota(jnp.int32, s.shape, 0) // tq) % NH
        s = jnp.where(lax.broadcasted_iota(jnp.int32, s.shape, 1) % NH == row_h, s, NEG_INF)
    p = jnp.exp(s - jnp.max(s, axis=1, keepdims=True))
    inv = 1.0 / jnp.sum(p, axis=1, keepdims=True)
    rows = []
    for b in range(nb):
        if interleaved:
            r = slice(b * NH * tq, (b + 1) * NH * tq)
            o = _dot(p[r].astype(BF16), mv_ref[0, b].astype(BF16)) * inv[r]
            heads = [o[h * tq:(h + 1) * tq] for h in range(NH)]
        else:
            heads = []
            for h in range(NH):
                r = slice((b * NH + h) * tq, (b * NH + h + 1) * tq)
                heads.append(_dot(p[r].astype(BF16), head(mv_ref[0, b], h).astype(BF16)) * inv[r])
        rows.append(jnp.concatenate(heads, axis=1))
    o_all = rows[0] if nb == 1 else jnp.concatenate(rows, axis=0)
    y = x + _dot(o_all.astype(BF16), wo_ref[...])
    out_ref[...] = y.reshape(nb, tq, D_MODEL)


def xattn(x, g, wq, wo, mem_k, mem_v, layer, nb, tq, interleaved):
    B, R, D = x.shape
    mem_blk = (1, nb) + mem_k.shape[2:]
    kern = functools.partial(_xattn_kernel, nb=nb, tq=tq, interleaved=interleaved)
    return pl.pallas_call(
        kern,
        grid=(B // nb, R // tq),
        in_specs=[pl.BlockSpec((nb, tq, D), lambda b, r: (b, r, 0)),
                  pl.BlockSpec((1, D), lambda b, r: (0, 0)),
                  pl.BlockSpec(wq.shape, lambda b, r: (0, 0)),
                  pl.BlockSpec(wo.shape, lambda b, r: (0, 0)),
                  pl.BlockSpec(mem_blk, lambda b, r: (layer, b, 0, 0)),
                  pl.BlockSpec(mem_blk, lambda b, r: (layer, b, 0, 0))],
        out_specs=pl.BlockSpec((nb, tq, D), lambda b, r: (b, r, 0)),
        out_shape=jax.ShapeDtypeStruct((B, R, D), F32),
        compiler_params=_cparams("parallel", "arbitrary"),
        name="xattn",
    )(x, g, wq, wo, mem_k, mem_v)


def _router_kernel(x_ref, g_ref, wrt_ref, route_out, route_t_out, counts_out, cnt_sc):
    @pl.when(pl.program_id(0) == 0)
    def _():
        cnt_sc[...] = jnp.zeros_like(cnt_sc)

    xn = _rms(x_ref[...], g_ref[...])
    logits_t = _dot_nt(wrt_ref[...], xn.astype(BF16))
    tm = logits_t.shape[1]
    row_g = lax.broadcasted_iota(jnp.int32, (SUBLANE, tm), 0)
    row_e = lax.broadcasted_iota(jnp.int32, (N_EXPERTS, tm), 0)
    row_e_f = row_e.astype(F32)

    def top1(v, rows):
        mx = jnp.max(v, axis=0, keepdims=True)
        return mx, jnp.min(jnp.where(v == mx, rows, float(LANE)), axis=0, keepdims=True)

    gl = jnp.where(row_g < N_GROUPS, logits_t[0:SUBLANE], NEG_INF)
    g_max, g_idx = top1(gl, row_g.astype(F32))
    g_w = 1.0 / jnp.sum(jnp.exp(gl - g_max), axis=0, keepdims=True)
    el = jnp.where((row_e // EXPERTS_PER_GROUP).astype(F32) == g_idx, logits_t[LANE:LANE + N_EXPERTS], NEG_INF)
    e1, i1 = top1(el, row_e_f)
    e2, i2 = top1(jnp.where(row_e_f == i1, NEG_INF, el), row_e_f)
    t = jnp.exp(e2 - e1)
    p1 = 1.0 / (1.0 + t)

    oh1 = (row_e_f == i1).astype(F32)
    oh2 = (row_e_f == i2).astype(F32)
    oh = oh1 + oh2
    r_idx = lax.broadcasted_iota(jnp.int32, (tm, tm), 0)
    c_idx = lax.broadcasted_iota(jnp.int32, (tm, tm), 1)
    before = (r_idx < c_idx).astype(BF16)
    base = cnt_sc[:, 0:1] + _dot(oh.astype(BF16), before)
    rank1 = jnp.sum(oh1 * base, axis=0, keepdims=True)
    rank2 = jnp.sum(oh2 * base, axis=0, keepdims=True)
    cnt = cnt_sc[...] + jnp.sum(oh, axis=1, keepdims=True)
    cnt_sc[...] = cnt
    counts_out[...] = cnt

    route_t = jnp.zeros((SUBLANE, tm), F32)
    for n, v in enumerate((i1, i2, p1 * g_w, t * p1 * g_w, rank1, rank2)):
        route_t = jnp.where(row_g == n, v, route_t)
    route_t_out[...] = route_t
    route_out[...] = jnp.concatenate([route_t, jnp.zeros((LANE - SUBLANE, tm), F32)], axis=0).T


def router(x, g, w_router_t):
    T, D = x.shape
    tm = _row_tile(T, 512)
    return pl.pallas_call(
        _router_kernel,
        grid=(T // tm,),
        in_specs=[pl.BlockSpec((tm, D), lambda i: (i, 0)),
                  pl.BlockSpec((1, D), lambda i: (0, 0)),
                  pl.BlockSpec(w_router_t.shape, lambda i: (0, 0))],
        out_specs=[pl.BlockSpec((tm, LANE), lambda i: (i, 0)), pl.BlockSpec((SUBLANE, tm), lambda i: (0, i)),
                   pl.BlockSpec((N_EXPERTS, LANE), lambda i: (0, 0))],
        out_shape=[jax.ShapeDtypeStruct((T, LANE), F32), jax.ShapeDtypeStruct((SUBLANE, T), F32),
                   jax.ShapeDtypeStruct((N_EXPERTS, LANE), F32)],
        scratch_shapes=[pltpu.VMEM((N_EXPERTS, LANE), F32)],
        compiler_params=_cparams("arbitrary"),
        name="router",
    )(x, g, w_router_t)


def moe_plan(routes, counts, TB, tms):
    experts = jnp.arange(N_EXPERTS, dtype=jnp.int32)
    counts = [c[:, 0].astype(jnp.int32) for c in counts]
    total = sum(counts)
    padded = (total + TB - 1) // TB * TB
    pad_ends = jnp.cumsum(padded)
    base = pad_ends - padded
    dests = []
    for route_t, cnt, tm in zip(routes, counts, tms):
        T = route_t.shape[1]
        e_idx = route_t[0:TOP_K].astype(jnp.int32)
        rank = route_t[4:4 + TOP_K].astype(jnp.int32)
        dest = jnp.sum(jnp.where(e_idx[None] == experts[:, None, None], base[:, None, None], 0), axis=0) + rank
        dests.append(dest.reshape(TOP_K, T // tm, tm).transpose(1, 0, 2).reshape(T // tm, 1, TOP_K * tm))
        base = base + cnt
    n_tokens = sum(r.shape[1] for r in routes)
    n_blocks = -(-(n_tokens * TOP_K + N_EXPERTS * (TB - 1)) // TB)
    block_start = jnp.arange(n_blocks, dtype=jnp.int32) * TB
    block_e = jnp.minimum(jnp.sum(pad_ends[None, :] <= block_start[:, None], axis=1), N_EXPERTS - 1).astype(jnp.int32)
    end_valid = jnp.sum(jnp.where(block_e[:, None] == experts, pad_ends - padded + total, 0), axis=1)
    n_valid = jnp.clip(end_valid - block_start, 0, TB).astype(jnp.int32)
    return block_e, n_valid, dests


DMA_UNROLL = 8


def _for_row_chunks(n_rows, fn):
    def body(c, carry):
        for u in range(DMA_UNROLL):
            fn(c * DMA_UNROLL + u)
        return carry
    lax.fori_loop(0, n_rows // DMA_UNROLL, body, 0)


def _pack_bf16_pair(x):
    n = x.shape[1] // 2
    bits = lambda a: lax.bitcast_convert_type(a.astype(BF16).astype(F32), jnp.uint32)
    return (bits(x[:, :n]) >> 16) | (bits(x[:, n:]) & jnp.uint32(0xFFFF0000))


def _unpack_bf16_pair(p):
    lo = lax.bitcast_convert_type(p << 16, F32).astype(BF16)
    hi = lax.bitcast_convert_type(p & jnp.uint32(0xFFFF0000), F32).astype(BF16)
    return lo, hi


def _dispatch_kernel(dst_ref, x_ref, g_ref, init_hbm, xs_hbm, buf, sem, *, tm, n_tiles):
    del init_hbm
    i = pl.program_id(0)
    slot = i % 2

    def copy(row, s, row_dst):
        return pltpu.make_async_copy(buf.at[s, pl.ds(row, 1)], xs_hbm.at[pl.ds(row_dst, 1)], sem.at[s])

    def wait_slot(s):
        def wait_row(r):
            for _ in range(TOP_K):
                copy(r, s, 0).wait()
        _for_row_chunks(tm, wait_row)

    @pl.when(i >= 2)
    def _():
        wait_slot(slot)

    buf[slot] = _pack_bf16_pair(_rms(x_ref[...], g_ref[...]))

    def start_row(r):
        for k in range(TOP_K):
            copy(r, slot, dst_ref[0, 0, k * tm + r]).start(priority=k % 2)
    _for_row_chunks(tm, start_row)

    @pl.when(i == n_tiles - 1)
    def _():
        wait_slot(slot)
        if n_tiles >= 2:
            wait_slot(1 - slot)


def moe_dispatch(x, g, dest, xs_init, tm):
    T, D = x.shape
    n_tiles = T // tm
    assert tm % DMA_UNROLL == 0
    kern = functools.partial(_dispatch_kernel, tm=tm, n_tiles=n_tiles)
    return pl.pallas_call(
        kern,
        grid=(n_tiles,),
        in_specs=[pl.BlockSpec((1, 1, TOP_K * tm), lambda i: (i, 0, 0), memory_space=pltpu.SMEM),
                  pl.BlockSpec((tm, D), lambda i: (i, 0)),
                  pl.BlockSpec((1, D), lambda i: (0, 0)),
                  pl.BlockSpec(memory_space=pl.ANY)],
        out_specs=pl.BlockSpec(memory_space=pl.ANY),
        out_shape=jax.ShapeDtypeStruct(xs_init.shape, xs_init.dtype),
        input_output_aliases={3: 0},
        scratch_shapes=[pltpu.VMEM((2, tm, D // 2), jnp.uint32), pltpu.SemaphoreType.DMA((2,))],
        compiler_params=_cparams("arbitrary"),
        name="moe_dispatch",
    )(dest, x, g, xs_init)


def _ffn_kernel(be_ref, nv_ref, xs_ref, wg_ref, wu_ref, wd_ref, ys_ref, wg_bf, wu_bf, wd_bf):
    i = pl.program_id(0)
    nv = nv_ref[i]

    @pl.when((i == 0) | (be_ref[i] != be_ref[jnp.maximum(i - 1, 0)]))
    def _():
        wg_bf[...] = wg_ref[0, 0].astype(BF16)
        wu_bf[...] = wu_ref[0, 0].astype(BF16)
        wd_bf[...] = wd_ref[0, 0].astype(BF16)

    @pl.when(nv > 0)
    def _():
        row = lax.broadcasted_iota(jnp.int32, (xs_ref.shape[0], 1), 0)
        packed = jnp.where(row < nv, xs_ref[...], jnp.uint32(0))
        x_lo, x_hi = _unpack_bf16_pair(packed)
        half = x_lo.shape[1]
        g = _dot(x_lo, wg_bf[:half, :]) + _dot(x_hi, wg_bf[half:, :])
        u = _dot(x_lo, wu_bf[:half, :]) + _dot(x_hi, wu_bf[half:, :])
        hid = (g * _sigmoid(g)) * u
        ys_ref[...] = _dot(hid.astype(BF16), wd_bf[...])

    @pl.when(nv == 0)
    def _():
        ys_ref[...] = jnp.zeros_like(ys_ref)


def expert_ffn(xs, block_e, n_valid, w_gate, w_up, w_down, layer, TB):
    P, D = xs.shape[0], 2 * xs.shape[1]
    n_blocks = P // TB
    wspec = lambda a, b: pl.BlockSpec((1, 1, a, b), lambda i, be, nv: (layer, be[i], 0, 0))
    grid_spec = pltpu.PrefetchScalarGridSpec(
        num_scalar_prefetch=2,
        grid=(n_blocks,),
        in_specs=[pl.BlockSpec((TB, D // 2), lambda i, be, nv: (i, 0)),
                  wspec(D, D_EXPERT), wspec(D, D_EXPERT), wspec(D_EXPERT, D)],
        out_specs=pl.BlockSpec((TB, D), lambda i, be, nv: (i, 0)),
        scratch_shapes=[pltpu.VMEM((D, D_EXPERT), BF16), pltpu.VMEM((D, D_EXPERT), BF16),
                        pltpu.VMEM((D_EXPERT, D), BF16)],
    )
    return pl.pallas_call(
        _ffn_kernel,
        grid_spec=grid_spec,
        out_shape=jax.ShapeDtypeStruct((P, D), F32),
        compiler_params=_cparams("arbitrary"),
        name="expert_ffn",
    )(block_e, n_valid, xs, w_gate, w_up, w_down)


def _combine_kernel(cur_ref, nxt_ref, x_ref, route_ref, gf_ref, ys_hbm, out_ref, ybuf, sem, *,
                    tm, n_tiles, final_norm):
    i = pl.program_id(0)
    slot = i % 2

    def copy(row_src, s, k, row):
        return pltpu.make_async_copy(ys_hbm.at[pl.ds(row_src, 1)], ybuf.at[s, k, pl.ds(row, 1)], sem.at[s])

    def start_tile(idx_ref, s):
        def start_row(r):
            for k in range(TOP_K):
                copy(idx_ref[0, 0, k * tm + r], s, k, r).start(priority=k % 2)
        _for_row_chunks(tm, start_row)

    @pl.when(i == 0)
    def _():
        start_tile(cur_ref, 0)

    @pl.when(i + 1 < n_tiles)
    def _():
        start_tile(nxt_ref, 1 - slot)

    def wait_row(r):
        for k in range(TOP_K):
            copy(0, slot, k, r).wait()
    _for_row_chunks(tm, wait_row)

    route = route_ref[...]
    lane_i = lax.broadcasted_iota(jnp.int32, route.shape, 1)
    g0 = jnp.sum(jnp.where(lane_i == 2, route, 0.0), axis=1, keepdims=True)
    g1 = jnp.sum(jnp.where(lane_i == 3, route, 0.0), axis=1, keepdims=True)
    y = x_ref[...] + (g0 * ybuf[slot, 0] + g1 * ybuf[slot, 1])
    out_ref[...] = _rms(y, gf_ref[...]) if final_norm else y


def moe_combine(x, ys, dest, route, g_final, tm, final_norm):
    T, D = x.shape
    n_tiles = T // tm
    assert tm % DMA_UNROLL == 0
    idx_blk = lambda f: pl.BlockSpec((1, 1, TOP_K * tm), f, memory_space=pltpu.SMEM)
    kern = functools.partial(_combine_kernel, tm=tm, n_tiles=n_tiles, final_norm=final_norm)
    return pl.pallas_call(
        kern,
        grid=(n_tiles,),
        in_specs=[idx_blk(lambda i: (i, 0, 0)),
                  idx_blk(lambda i: (jnp.minimum(i + 1, n_tiles - 1), 0, 0)),
                  pl.BlockSpec((tm, D), lambda i: (i, 0)),
                  pl.BlockSpec((tm, LANE), lambda i: (i, 0)),
                  pl.BlockSpec((1, D), lambda i: (0, 0)),
                  pl.BlockSpec(memory_space=pl.ANY)],
        out_specs=pl.BlockSpec((tm, D), lambda i: (i, 0)),
        out_shape=jax.ShapeDtypeStruct((T, D), F32),
        scratch_shapes=[pltpu.VMEM((2, TOP_K, tm, D), F32), pltpu.SemaphoreType.DMA((2,))],
        compiler_params=_cparams("arbitrary"),
        name="moe_combine",
    )(dest, dest, x, route, g_final, ys)


IN_SPLITS = ((4 * M_WIDTH,), (LANE,), (2 * R_WIDTH, S_WIDTH + 2 * S_KV_WIDTH, 3 * D_MODEL))
IN_DTYPES = (F32, F32, F32, F32, BF16)


def _prep_w_in(w):
    c0 = 4 * M_WIDTH
    c1 = c0 + 2 * M_HEADS
    gates = jnp.pad(w[..., c0:c1], ((0, 0), (0, 0), (0, LANE - 2 * M_HEADS)))
    return [w[..., :c0].astype(BF16), gates.astype(BF16), w[..., c1:].astype(BF16)]


def _prep_router(w_rg, w_re):
    pad = lambda w: jnp.pad(w.T, ((0, LANE - w.shape[1]), (0, 0)))
    return jnp.concatenate([pad(w_rg), pad(w_re)], axis=0).astype(BF16)


def _row(v):
    return v.reshape(1, -1).astype(F32)


def _pad_rows(a, n):
    return jnp.pad(a, ((0, 0), (0, n - a.shape[1])) + ((0, 0),) * (a.ndim - 2))


def _moe_block(T):
    return 512 if T >= 4096 else 128


def moe_layer(xs_in, slots, g_ffn, w_router, w_gate, w_up, w_down, layer, g_final, final_norm):
    D = xs_in[0].shape[1]
    TB = _moe_block(sum(x.shape[0] for x in xs_in))
    tms = [_row_tile(x.shape[0], 512) for x in xs_in]
    routed = [router(x, g_ffn, w_router) for x in xs_in]
    routes, counts = [r[0] for r in routed], [r[2] for r in routed]
    block_e, n_valid, dests = moe_plan([r[1] for r in routed], counts, TB, tms)
    buf = jnp.zeros((block_e.shape[0] * TB, D // 2), jnp.uint32) if slots is None else slots
    for x, dest, tm in zip(xs_in, dests, tms):
        buf = moe_dispatch(x, g_ffn, dest, buf, tm)
    ys = expert_ffn(buf, block_e, n_valid, w_gate, w_up, w_down, layer, TB)
    outs = [moe_combine(x, ys, dest, route, g_final, tm, final_norm)
            for x, dest, route, tm in zip(xs_in, dests, routes, tms)]
    return outs, buf


def kernel(x_prompt, x_sample, mem_prompt, cache_mem_k, cache_mem_v, cache_swa_k, cache_swa_v, state_mlstm_C, state_mlstm_n, state_mlstm_m, state_rglru_h, state_rglru_conv, norm_mix, w_in, m_igate_b, m_fgate_b, m_head_norm, r_conv_w, r_conv_b, r_gate_a_w, r_gate_a_b, r_gate_x_w, r_gate_x_b, r_lambda, swa_sinks, rel_bias, w_branch_m, w_branch_r, w_branch_s, w_out, norm_mem, xq_w, xk_w, xv_w, xo_w, norm_ffn, router_group_w, router_expert_w, moe_w_gate, moe_w_up, moe_w_down, norm_final):
    B, S, D = x_prompt.shape
    NS, TS, _ = x_sample.shape
    depth = w_in.shape[0]
    n_mem = mem_prompt.shape[1]
    Tp, Ts = B * S, NS * TS
    H = M_HEADS
    assert S % MLSTM_CHUNK == 0 and S % WINDOW == 0 and TS <= SUBLANE
    assert NS % XATTN_SHORT_BATCH == 0 and NS % SWA_SHORT_BATCH == 0 and NS % MLSTM_SHORT_BATCH == 0

    xp = x_prompt.reshape(Tp, D)
    xs = x_sample.reshape(Ts, D)
    mem2d = mem_prompt.reshape(B * n_mem, D)
    cache_k_rows = cache_mem_k.reshape(depth, NS, n_mem * N_MEM_HEADS, X_HEAD_DIM)
    cache_v_rows = cache_mem_v.reshape(depth, NS, n_mem * N_MEM_HEADS, X_HEAD_DIM)
    k_caches = cache_swa_k.reshape(depth * NS, WINDOW, S_KV_WIDTH)
    v_caches = cache_swa_v.reshape(depth * NS, WINDOW, S_KV_WIDTH)
    w_in_all = _prep_w_in(w_in)
    w_kv_all = [xk_w.astype(BF16), xv_w.astype(BF16)]
    NBP = 2 if B % 2 == 0 else 1
    bias_p = swa_bias_rows(rel_bias, WINDOW, NBP)
    bias_s = swa_bias_rows(rel_bias, SWA_SHORT_TQ, SWA_SHORT_BATCH)
    outs_p = {k: [] for k in ("mem_k", "mem_v", "swa_k", "swa_v", "C", "n", "m", "h", "conv")}
    outs_s = {k: [] for k in ("swa_k", "swa_v", "C", "n", "m", "h", "conv")}

    slots = None
    for l in range(depth):
        g_mix = _row(norm_mix[l])
        b_i = jnp.pad(m_igate_b[l], (0, SUBLANE - H)).reshape(SUBLANE, 1)
        b_f = jnp.pad(m_fgate_b[l], (0, SUBLANE - H)).reshape(SUBLANE, 1)
        hn = _row(m_head_norm[l])
        r_args = (r_conv_w[l], _row(r_conv_b[l]), r_gate_a_w[l].astype(BF16), r_gate_x_w[l].astype(BF16),
                  _row(r_gate_a_b[l]), _row(r_gate_x_b[l]), _row(r_lambda[l]))
        sink_p = swa_sink_rows(swa_sinks[l], WINDOW, NBP)
        sink_s = swa_sink_rows(swa_sinks[l], SWA_SHORT_TQ, SWA_SHORT_BATCH)
        merge_w = (w_branch_m[l].astype(BF16), w_branch_r[l].astype(BF16), w_branch_s[l].astype(BF16),
                   w_out[l].astype(BF16))
        wq_b, wo_b = xq_w[l].astype(BF16), xo_w[l].astype(BF16)
        w_router = _prep_router(router_group_w[l], router_expert_w[l])
        last = l == depth - 1
        moe_args = (_row(norm_ffn[l]), w_router, moe_w_gate, moe_w_up, moe_w_down, l, _row(norm_final), last)

        mk, mv = norm_matmul(mem2d, g_mix, w_kv_all, ((X_WIDTH,), (X_WIDTH,)), l, normalize=False)
        outs_p["mem_k"].append(mk.reshape(B, n_mem, N_MEM_HEADS, X_HEAD_DIM))
        outs_p["mem_v"].append(mv.reshape(B, n_mem, N_MEM_HEADS, X_HEAD_DIM))
        mk, mv = mk.reshape(1, B, n_mem, X_WIDTH), mv.reshape(1, B, n_mem, X_WIDTH)

        p_m, p_if, p_r, p_s, p_g = norm_matmul(xp, g_mix, w_in_all, IN_SPLITS, l, out_dtypes=IN_DTYPES)
        gts = jnp.swapaxes(p_if[:, :2 * H].reshape(B, S, 2 * H), 1, 2)
        gi = jnp.pad(gts[:, :H], ((0, 0), (0, SUBLANE - H), (0, 0)))
        gf = jnp.pad(gts[:, H:], ((0, 0), (0, SUBLANE - H), (0, 0)))
        h_m, caug, m_o = mlstm(p_m.reshape(B, S, 4 * M_WIDTH), gi, gf, b_i, b_f, hn,
                               jnp.zeros((B, H, M_HEAD_DIM, MLSTM_AUG), F32),
                               jnp.zeros((B, SUBLANE, LANE), F32), L=MLSTM_CHUNK)
        outs_p["C"].append(caug[..., :M_HEAD_DIM])
        outs_p["n"].append(caug[..., M_HEAD_DIM])
        outs_p["m"].append(m_o[:, :H, 0])

        h_r, h_last, conv_last = rglru(p_r.reshape(B, S, 2 * R_WIDTH),
                                       jnp.zeros((B, CONV_W - 1, R_WIDTH), F32),
                                       jnp.zeros((B, 1, R_WIDTH), F32), *r_args,
                                       TR=_row_tile(S), Bs=1, out_dtype=BF16)
        outs_p["h"].append(h_last[:, 0])
        outs_p["conv"].append(conv_last)

        p_s3 = p_s.reshape(B, S, S_WIDTH + 2 * S_KV_WIDTH)
        kcol, vcol = S_WIDTH // S_KV_WIDTH, S_WIDTH // S_KV_WIDTH + 1
        prev = lambda col: (lambda b, n: (b, jnp.maximum(n - 1, 0), col))
        cur = lambda col: (lambda b, n: (b, n, col))
        kv_blk = (NBP, WINDOW, S_KV_WIDTH)
        h_s = swa((p_s3,) * 5, (cur(0), prev(kcol), cur(kcol), prev(vcol), cur(vcol)),
                  ((NBP, WINDOW, S_WIDTH), kv_blk, kv_blk, kv_blk, kv_blk),
                  bias_p, sink_p, grid=(B // NBP, S // WINDOW), NBS=NBP, TQ=WINDOW, mask_first=True,
                  out_dtype=BF16)
        outs_p["swa_k"].append(p_s3[:, S - WINDOW:, S_WIDTH:S_WIDTH + S_KV_WIDTH]
                               .reshape(B, WINDOW, S_KV_HEADS, S_HEAD_DIM))
        outs_p["swa_v"].append(p_s3[:, S - WINDOW:, S_WIDTH + S_KV_WIDTH:]
                               .reshape(B, WINDOW, S_KV_HEADS, S_HEAD_DIM))

        xp = merge_out(h_m.reshape(Tp, M_WIDTH), h_r.reshape(Tp, R_WIDTH), h_s.reshape(Tp, S_WIDTH),
                       p_g, xp, *merge_w)
        xp = xattn(xp.reshape(B, S, D), _row(norm_mem[l]), wq_b, wo_b, mk, mv, layer=0,
                   nb=1, tq=512 if S % 512 == 0 else _row_tile(S), interleaved=False).reshape(Tp, D)

        p_m, p_if, p_r, p_s, p_g = norm_matmul(xs, g_mix, w_in_all, IN_SPLITS, l, out_dtypes=IN_DTYPES)
        gts = jnp.swapaxes(p_if[:, :2 * H].reshape(NS, TS, 2 * H), 1, 2)
        gts = jnp.pad(gts, ((0, 0), (0, 0), (0, LANE - TS)))
        gi = jnp.pad(gts[:, :H], ((0, 0), (0, SUBLANE - H), (0, 0)))
        gf = jnp.pad(gts[:, H:], ((0, 0), (0, SUBLANE - H), (0, 0)))
        m0 = jnp.broadcast_to(jnp.pad(state_mlstm_m[l], ((0, 0), (0, SUBLANE - H)))[:, :, None],
                              (NS, SUBLANE, LANE))
        h_m, c_s, n_s, m_o = mlstm_short(_pad_rows(p_m.reshape(NS, TS, 4 * M_WIDTH), SUBLANE), gi, gf, b_i, b_f,
                                         hn, state_mlstm_C, state_mlstm_n, m0, layer=l,
                                         NBS=MLSTM_SHORT_BATCH, TS=TS)
        h_m = h_m[:, :TS].reshape(Ts, M_WIDTH)
        outs_s["C"].append(c_s)
        outs_s["n"].append(n_s)
        outs_s["m"].append(m_o[:, :H, 0])

        tmaj = lambda a: jnp.swapaxes(a, 0, 1).reshape(1, a.shape[0] * a.shape[1], a.shape[2])
        h_r, h_last, conv_last = rglru(tmaj(p_r.reshape(NS, TS, 2 * R_WIDTH)), tmaj(state_rglru_conv[l]),
                                       state_rglru_h[l].reshape(1, NS, R_WIDTH), *r_args,
                                       TR=TS * NS, Bs=NS, out_dtype=F32)
        h_r = jnp.swapaxes(h_r.reshape(TS, NS, R_WIDTH), 0, 1).reshape(Ts, R_WIDTH)
        outs_s["h"].append(h_last[0])
        outs_s["conv"].append(jnp.swapaxes(conv_last.reshape(CONV_W - 1, NS, R_WIDTH), 0, 1))

        TQ, NBS = SWA_SHORT_TQ, SWA_SHORT_BATCH
        p_s3 = p_s.reshape(NS, TS, S_WIDTH + 2 * S_KV_WIDTH)
        q_s = _pad_rows(p_s3[:, :, :S_WIDTH], TQ)
        k_new = p_s3[:, :, S_WIDTH:S_WIDTH + S_KV_WIDTH]
        v_new = p_s3[:, :, S_WIDTH + S_KV_WIDTH:]
        full = lambda b, n: (b, 0, 0)
        cache = lambda b, n: (l * (NS // NBS) + b, 0, 0)
        h_s = swa((q_s, k_caches, _pad_rows(k_new, TQ), v_caches, _pad_rows(v_new, TQ)),
                  (full, cache, full, cache, full),
                  ((NBS, TQ, S_WIDTH), (NBS, WINDOW, S_KV_WIDTH), (NBS, TQ, S_KV_WIDTH),
                   (NBS, WINDOW, S_KV_WIDTH), (NBS, TQ, S_KV_WIDTH)),
                  bias_s, sink_s, grid=(NS // NBS, 1), NBS=NBS, TQ=TQ, mask_first=False, out_dtype=F32)
        h_s = h_s[:, :TS].reshape(Ts, S_WIDTH)
        outs_s["swa_k"].append(k_new)
        outs_s["swa_v"].append(v_new)

        xs = merge_out(h_m, h_r, h_s, p_g, xs, *merge_w)
        xs8 = xattn(_pad_rows(xs.reshape(NS, TS, D), SUBLANE), _row(norm_mem[l]), wq_b, wo_b,
                    cache_k_rows, cache_v_rows, layer=l, nb=XATTN_SHORT_BATCH, tq=SUBLANE, interleaved=True)

        (xp, xs), slots = moe_layer([xp, xs8[:, :TS].reshape(Ts, D)], slots, *moe_args)

    st = lambda d, k: jnp.stack(d[k])

    def shifted_cache(caches, new):
        out = jnp.concatenate([caches.reshape(depth, NS, WINDOW, S_KV_WIDTH)[:, :, TS:], jnp.stack(new)], axis=2)
        return out.reshape(depth, NS, WINDOW, S_KV_HEADS, S_HEAD_DIM)

    return (xp.reshape(B, S, D), xs.reshape(NS, TS, D), st(outs_p, "mem_k"), st(outs_p, "mem_v"),
            st(outs_p, "swa_k"), st(outs_p, "swa_v"), st(outs_p, "C"), st(outs_p, "n"), st(outs_p, "m"),
            st(outs_p, "h"), st(outs_p, "conv"),
            shifted_cache(k_caches, outs_s["swa_k"]), shifted_cache(v_caches, outs_s["swa_v"]),
            st(outs_s, "C"), st(outs_s, "n"), st(outs_s, "m"), st(outs_s, "h"), st(outs_s, "conv"))
```

```python
import functools
import math

import jax
import jax.numpy as jnp
from jax import lax
from jax.experimental import pallas as pl
from jax.experimental.pallas import tpu as pltpu

F32 = jnp.float32
BF16 = jnp.bfloat16
NEG_INF = float("-inf")

LANE = 128
SUBLANE = 8
BF16_SUBLANE = 16
VMEM_LIMIT_BYTES = 56 * 1024 * 1024

D_MODEL = 1024
M_HEADS = 4
M_HEAD_DIM = 128
M_WIDTH = M_HEADS * M_HEAD_DIM
R_WIDTH = 512
R_BLOCKS = 4
R_BLOCK_DIM = R_WIDTH // R_BLOCKS
CONV_W = 4
LRU_C = 8.0
S_HEADS = 8
S_KV_HEADS = 2
S_HEAD_DIM = 64
S_WIDTH = S_HEADS * S_HEAD_DIM
S_KV_WIDTH = S_KV_HEADS * S_HEAD_DIM
WINDOW = 128
N_BUCKETS = 32
MAX_DISTANCE = 128
N_MEM_HEADS = 4
X_HEAD_DIM = 128
X_WIDTH = N_MEM_HEADS * X_HEAD_DIM
N_GROUPS = 4
EXPERTS_PER_GROUP = 8
N_EXPERTS = N_GROUPS * EXPERTS_PER_GROUP
TOP_K = 2
D_EXPERT = 512
EPS = 1e-6

MLSTM_CHUNK = 512
MLSTM_AUG = 2 * M_HEAD_DIM
MLSTM_SHORT_BATCH = 2
SWA_SHORT_TQ = BF16_SUBLANE
SWA_SHORT_BATCH = 16
XATTN_SHORT_BATCH = 16


def _cparams(*sem):
    return pltpu.CompilerParams(dimension_semantics=sem, vmem_limit_bytes=VMEM_LIMIT_BYTES)


def _rms(x, g):
    ms = jnp.mean(x * x, axis=-1, keepdims=True)
    return x * lax.rsqrt(ms + EPS) * g


def _sigmoid(x):
    return 1.0 / (1.0 + jnp.exp(-x))


def _softplus(x):
    return jnp.maximum(x, 0.0) + jnp.log1p(jnp.exp(-jnp.abs(x)))


def _log_sigmoid(x):
    return -_softplus(-x)


def _gelu_tanh(x):
    return 0.5 * x * (1.0 + jnp.tanh(math.sqrt(2.0 / math.pi) * (x + 0.044715 * (x * x * x))))


def _dot(a, b):
    return jnp.dot(a, b, preferred_element_type=F32)


def _dot_nt(a, b):
    return lax.dot_general(a, b, (((1,), (1,)), ((), ())), preferred_element_type=F32)


def _row_tile(n, pref=256):
    for t in (512, 256, 128, 64, 32, 16, 8):
        if t <= pref and n % t == 0:
            return t
    raise ValueError(f"row count {n} is not a multiple of {SUBLANE}")


def _norm_matmul_kernel(x_ref, g_ref, *refs, splits, normalize):
    w_refs, out_refs = refs[:len(splits)], refs[len(splits):]
    x = x_ref[...]
    if normalize:
        x = _rms(x, g_ref[...])
    xb = x.astype(BF16)
    outs = iter(out_refs)
    for w_ref, w_splits in zip(w_refs, splits):
        off = 0
        for n in w_splits:
            o_ref = next(outs)
            o_ref[...] = _dot(xb, w_ref[0, :, off:off + n]).astype(o_ref.dtype)
            off += n


def norm_matmul(x, g, weights, splits, layer, normalize=True, out_dtypes=None):
    flat = [n for w_splits in splits for n in w_splits]
    out_dtypes = out_dtypes or (F32,) * len(flat)
    T, D = x.shape
    tm = _row_tile(T)
    assert all(sum(ws) == w.shape[2] for ws, w in zip(splits, weights)) and all(n % LANE == 0 for n in flat)
    kern = functools.partial(_norm_matmul_kernel, splits=tuple(tuple(ws) for ws in splits), normalize=normalize)
    return pl.pallas_call(
        kern,
        grid=(T // tm,),
        in_specs=[pl.BlockSpec((tm, D), lambda i: (i, 0)),
                  pl.BlockSpec((1, D), lambda i: (0, 0))]
                 + [pl.BlockSpec((1, D, w.shape[2]), lambda i: (layer, 0, 0)) for w in weights],
        out_specs=[pl.BlockSpec((tm, n), lambda i: (i, 0)) for n in flat],
        out_shape=[jax.ShapeDtypeStruct((T, n), dt) for n, dt in zip(flat, out_dtypes)],
        compiler_params=_cparams("parallel"),
        name="norm_matmul",
    )(x, g, *weights)


def _mlstm_kernel(q_ref, k_ref, v_ref, o_ref, gi_ref, gf_ref, bi_ref, bf_ref, hn_ref,
                  c0_ref, m0_ref, h_out, c_out, m_out, c_sc, m_sc, *, L, NBB):
    @pl.when(pl.program_id(1) == 0)
    def _():
        c_sc[...] = c0_ref[...]
        m_sc[...] = m0_ref[...]

    t_idx = lax.broadcasted_iota(jnp.int32, (L, L), 0)
    s_idx = lax.broadcasted_iota(jnp.int32, (L, L), 1)
    causal = t_idx >= s_idx
    eye = t_idx == s_idx
    row8 = lax.broadcasted_iota(jnp.int32, (SUBLANE, LANE), 0)
    ones_col = (lax.broadcasted_iota(jnp.int32, (L, M_HEAD_DIM), 1) == 0).astype(BF16)
    gi = [gi_ref[b] + bi_ref[...] for b in range(NBB)]
    gf = [_log_sigmoid(gf_ref[b] + bf_ref[...]) for b in range(NBB)]
    m_tiles = [m_sc[b] for b in range(NBB)]
    chains = [dict(b=b, h=h, hs=slice(h * M_HEAD_DIM, (h + 1) * M_HEAD_DIM))
              for b in range(NBB) for h in range(M_HEADS)]

    for c in chains:
        b, h = c["b"], c["h"]
        c["m_prev"] = m_tiles[b][h:h + 1, 0:1]
        c["b_col"] = jnp.sum(jnp.where(causal, gf[b][h:h + 1, :], 0.0), axis=1, keepdims=True)
    for c in chains:
        b, h = c["b"], c["h"]
        b_row = jnp.sum(jnp.where(eye, c["b_col"], 0.0), axis=0, keepdims=True)
        c["r_row"] = gi[b][h:h + 1, :] - b_row
        c["g_end"] = b_row[:, L - 1:L]
        c["log_in"] = jnp.where(causal, c["b_col"] + c["r_row"], NEG_INF)
        c["m_intra"] = jnp.max(c["log_in"], axis=1, keepdims=True)
    for c in chains:
        b, hs = c["b"], c["hs"]
        c["k"] = k_ref[b][:, hs] * (M_HEAD_DIM ** -0.5)
        c["caug"] = c_sc[b, c["h"]]
        c["vaug"] = jnp.concatenate([v_ref[b][:, hs].astype(BF16), ones_col], axis=1)
        qb = q_ref[b][:, hs].astype(BF16)
        c["qk"] = _dot_nt(qb, c["k"].astype(BF16))
        c["qc"] = _dot(qb, c["caug"].astype(BF16))
    for c in chains:
        log_prev = c["b_col"] + c["m_prev"]
        m_t = jnp.maximum(log_prev, c["m_intra"])
        sc = (c["qk"] * jnp.exp(c["log_in"] - m_t)).astype(BF16)
        nd = jnp.exp(log_prev - m_t) * c["qc"] + _dot(sc, c["vaug"])
        den = nd[:, M_HEAD_DIM:M_HEAD_DIM + 1]
        c["hh"] = nd[:, :M_HEAD_DIM] / jnp.maximum(jnp.abs(den), jnp.exp(-m_t))
    for c in chains:
        b, hs = c["b"], c["hs"]
        d = c["hh"] - jnp.mean(c["hh"], axis=-1, keepdims=True)
        y = d * lax.rsqrt(jnp.mean(d * d, axis=-1, keepdims=True) + EPS) * hn_ref[:, hs]
        h_out[b, :, hs] = (y * _sigmoid(o_ref[b][:, hs])).astype(h_out.dtype)
    for c in chains:
        b, h = c["b"], c["h"]
        lie = c["g_end"] + c["r_row"]
        m_end = jnp.maximum(c["g_end"] + c["m_prev"], jnp.max(lie, axis=1, keepdims=True))
        w_pe = jnp.exp(c["g_end"] + c["m_prev"] - m_end)
        ktw = (c["k"].T * jnp.exp(lie - m_end)).astype(BF16)
        c_new = w_pe * c["caug"] + _dot(ktw, c["vaug"])
        c_sc[b, h] = c_new
        c_out[b, h] = c_new
        m_tiles[b] = jnp.where(row8 == h, m_end, m_tiles[b])
    for b in range(NBB):
        m_sc[b] = m_tiles[b]
        m_out[b] = m_tiles[b]


def mlstm(p_m, gates_i, gates_f, b_i, b_f, head_norm, c0aug, m0, L):
    NB, R, _ = p_m.shape
    nc = R // L
    assert R % L == 0
    NBB = 2 if NB % 2 == 0 else 1
    blk = lambda j: pl.BlockSpec((NBB, L, M_WIDTH), lambda b, c, j=j: (b, c, j))
    state_spec = pl.BlockSpec((NBB, M_HEADS, M_HEAD_DIM, MLSTM_AUG), lambda b, c: (b, 0, 0, 0))
    m_spec = pl.BlockSpec((NBB, SUBLANE, LANE), lambda b, c: (b, 0, 0))
    gate_spec = pl.BlockSpec((NBB, SUBLANE, L), lambda b, c: (b, 0, c))
    kern = functools.partial(_mlstm_kernel, L=L, NBB=NBB)
    scratch = [pltpu.VMEM((NBB, M_HEADS, M_HEAD_DIM, MLSTM_AUG), F32), pltpu.VMEM((NBB, SUBLANE, LANE), F32)]
    return pl.pallas_call(
        kern,
        grid=(NB // NBB, nc),
        in_specs=[blk(0), blk(1), blk(2), blk(3), gate_spec, gate_spec,
                  pl.BlockSpec((SUBLANE, 1), lambda b, c: (0, 0)),
                  pl.BlockSpec((SUBLANE, 1), lambda b, c: (0, 0)),
                  pl.BlockSpec((1, M_WIDTH), lambda b, c: (0, 0)),
                  state_spec, m_spec],
        out_specs=[pl.BlockSpec((NBB, L, M_WIDTH), lambda b, c: (b, c, 0)), state_spec, m_spec],
        out_shape=[jax.ShapeDtypeStruct((NB, R, M_WIDTH), BF16),
                   jax.ShapeDtypeStruct((NB, M_HEADS, M_HEAD_DIM, MLSTM_AUG), F32),
                   jax.ShapeDtypeStruct((NB, SUBLANE, LANE), F32)],
        scratch_shapes=scratch,
        compiler_params=_cparams("parallel", "arbitrary"),
        name="mlstm",
    )(p_m, p_m, p_m, p_m, gates_i, gates_f, b_i, b_f, head_norm, c0aug, m0)


def _mlstm_short_kernel(q_ref, k_ref, v_ref, o_ref, gi_ref, gf_ref, bi_ref, bf_ref, hn_ref,
                        c0_ref, n0_ref, m0_ref, h_out, c_out, n_out, m_out, *, NBS, TS):
    L = SUBLANE
    t_idx = lax.broadcasted_iota(jnp.int32, (L, L), 0)
    s_idx = lax.broadcasted_iota(jnp.int32, (L, L), 1)
    causal = t_idx >= s_idx
    eye = t_idx == s_idx
    valid = lax.broadcasted_iota(jnp.int32, (1, L), 1) < TS
    row8 = lax.broadcasted_iota(jnp.int32, (SUBLANE, LANE), 0)
    gi = [gi_ref[b] + bi_ref[...] for b in range(NBS)]
    gf = [_log_sigmoid(gf_ref[b] + bf_ref[...]) for b in range(NBS)]
    m_tiles = [m0_ref[b] for b in range(NBS)]
    chains = [dict(b=b, h=h, hs=slice(h * M_HEAD_DIM, (h + 1) * M_HEAD_DIM))
              for b in range(NBS) for h in range(M_HEADS)]

    for c in chains:
        b, h = c["b"], c["h"]
        c["li"] = jnp.where(valid, gi[b][h:h + 1, 0:L], NEG_INF)
        lf = jnp.where(valid, gf[b][h:h + 1, 0:L], 0.0)
        c["m_prev"] = m_tiles[b][h:h + 1, 0:1]
        c["b_col"] = jnp.sum(jnp.where(causal, lf, 0.0), axis=1, keepdims=True)
    for c in chains:
        b_row = jnp.sum(jnp.where(eye, c["b_col"], 0.0), axis=0, keepdims=True)
        c["r_row"] = c["li"] - b_row
        c["g_end"] = b_row[:, L - 1:L]
        c["log_in"] = jnp.where(causal, c["b_col"] + c["r_row"], NEG_INF)
        c["m_intra"] = jnp.max(c["log_in"], axis=1, keepdims=True)
    for c in chains:
        b, hs = c["b"], c["hs"]
        c["q"] = q_ref[b][:, hs]
        c["k"] = k_ref[b][:, hs] * (M_HEAD_DIM ** -0.5)
        c["vb"] = v_ref[b][:, hs].astype(BF16)
        c["c0"] = c0_ref[0, b, c["h"]]
        c["n0"] = n0_ref[0, b, c["h"]:c["h"] + 1, :]
        qb = c["q"].astype(BF16)
        c["qk"] = _dot_nt(qb, c["k"].astype(BF16))
        c["qc"] = _dot(qb, c["c0"].astype(BF16))
        c["qn"] = jnp.sum(c["q"] * c["n0"], axis=1, keepdims=True)
    for c in chains:
        log_prev = c["b_col"] + c["m_prev"]
        m_t = jnp.maximum(log_prev, c["m_intra"])
        w_prev = jnp.exp(log_prev - m_t)
        sc = c["qk"] * jnp.exp(c["log_in"] - m_t)
        num = w_prev * c["qc"] + _dot(sc.astype(BF16), c["vb"])
        den = w_prev * c["qn"] + jnp.sum(sc, axis=1, keepdims=True)
        c["hh"] = num / jnp.maximum(jnp.abs(den), jnp.exp(-m_t))
    for c in chains:
        b, hs = c["b"], c["hs"]
        d = c["hh"] - jnp.mean(c["hh"], axis=-1, keepdims=True)
        y = d * lax.rsqrt(jnp.mean(d * d, axis=-1, keepdims=True) + EPS) * hn_ref[:, hs]
        h_out[b, :, hs] = y * _sigmoid(o_ref[b][:, hs])
    for c in chains:
        b, h = c["b"], c["h"]
        lie = c["g_end"] + c["r_row"]
        m_end = jnp.maximum(c["g_end"] + c["m_prev"], jnp.max(lie, axis=1, keepdims=True))
        w_pe = jnp.exp(c["g_end"] + c["m_prev"] - m_end)
        w_e = jnp.exp(lie - m_end)
        w_e_col = jnp.sum(jnp.where(eye, w_e, 0.0), axis=1, keepdims=True)
        kw = c["k"] * w_e_col
        c_out[b, h] = w_pe * c["c0"] + lax.dot_general(kw.astype(BF16), c["vb"], (((0,), (0,)), ((), ())),
                                                       preferred_element_type=F32)
        n_out[b, h:h + 1, :] = w_pe * c["n0"] + jnp.sum(kw, axis=0, keepdims=True)
        m_tiles[b] = jnp.where(row8 == h, m_end, m_tiles[b])
    for b in range(NBS):
        m_out[b] = m_tiles[b]


def mlstm_short(p_m, gates_i, gates_f, b_i, b_f, head_norm, c0, n0, m0, layer, NBS, TS):
    NB = p_m.shape[0]
    H, DH = M_HEADS, M_HEAD_DIM
    assert NB % NBS == 0 and TS <= SUBLANE
    blk = lambda j: pl.BlockSpec((NBS, SUBLANE, M_WIDTH), lambda i, j=j: (i, 0, j))
    tile = pl.BlockSpec((NBS, SUBLANE, LANE), lambda i: (i, 0, 0))
    kern = functools.partial(_mlstm_short_kernel, NBS=NBS, TS=TS)
    return pl.pallas_call(
        kern,
        grid=(NB // NBS,),
        in_specs=[blk(0), blk(1), blk(2), blk(3), tile, tile,
                  pl.BlockSpec((SUBLANE, 1), lambda i: (0, 0)),
                  pl.BlockSpec((SUBLANE, 1), lambda i: (0, 0)),
                  pl.BlockSpec((1, M_WIDTH), lambda i: (0, 0)),
                  pl.BlockSpec((1, NBS, H, DH, DH), lambda i: (layer, i, 0, 0, 0)),
                  pl.BlockSpec((1, NBS, H, DH), lambda i: (layer, i, 0, 0)),
                  tile],
        out_specs=[pl.BlockSpec((NBS, SUBLANE, M_WIDTH), lambda i: (i, 0, 0)),
                   pl.BlockSpec((NBS, H, DH, DH), lambda i: (i, 0, 0, 0)),
                   pl.BlockSpec((NBS, H, DH), lambda i: (i, 0, 0)),
                   tile],
        out_shape=[jax.ShapeDtypeStruct((NB, SUBLANE, M_WIDTH), F32),
                   jax.ShapeDtypeStruct((NB, H, DH, DH), F32),
                   jax.ShapeDtypeStruct((NB, H, DH), F32),
                   jax.ShapeDtypeStruct((NB, SUBLANE, LANE), F32)],
        compiler_params=_cparams("parallel"),
        name="mlstm_short",
    )(p_m, p_m, p_m, p_m, gates_i, gates_f, b_i, b_f, head_norm, c0, n0, m0)


def _rglru_kernel(x_ref, g_ref, conv0_ref, h0_ref, cw_ref, cb_ref, wa_ref, wx_ref, ba_ref, bx_ref,
                  lam_ref, y_out, hlast_out, conv_out, xpad_sc, hc_sc, *, TR, Bs):
    CB = (CONV_W - 1) * Bs
    X0 = -(-CB // SUBLANE) * SUBLANE

    @pl.when(pl.program_id(1) == 0)
    def _():
        xpad_sc[X0 - CB:X0, :] = conv0_ref[0]
        hc_sc[...] = h0_ref[0]

    xpad_sc[X0:X0 + TR, :] = x_ref[0]
    y = cb_ref[...]
    for j in range(CONV_W):
        y = y + xpad_sc[X0 - CB + j * Bs:X0 - CB + j * Bs + TR, :] * cw_ref[j:j + 1, :]
    tail = xpad_sc[X0 + TR - CB:X0 + TR, :]
    conv_out[0] = tail
    xpad_sc[X0 - CB:X0, :] = tail

    yb = y.astype(BF16)
    rs, is_ = [], []
    for n in range(R_BLOCKS):
        sl = slice(n * R_BLOCK_DIM, (n + 1) * R_BLOCK_DIM)
        rs.append(_dot(yb[:, sl], wa_ref[n]))
        is_.append(_dot(yb[:, sl], wx_ref[n]))
    r = _sigmoid(jnp.concatenate(rs, axis=1) + ba_ref[...])
    i = _sigmoid(jnp.concatenate(is_, axis=1) + bx_ref[...])
    log_a = -LRU_C * r * _softplus(-lam_ref[...])
    a_cum = jnp.exp(log_a)
    t = jnp.tanh(log_a)
    u_cum = jnp.sqrt(-2.0 * t / (1.0 - t)) * (i * y)

    def scan_rows(a, u, axis, stride):
        n = a.shape[axis]
        idx = lax.broadcasted_iota(jnp.int32, a.shape, axis)
        d = stride
        while d < n:
            keep = idx >= d
            u = jnp.where(keep, a * pltpu.roll(u, d, axis=axis) + u, u)
            a = jnp.where(keep, a * pltpu.roll(a, d, axis=axis), a)
            d *= 2
        return a, u

    hc = hc_sc[...]
    if Bs == 1:
        NG = TR // SUBLANE
        a3, u3 = scan_rows(a_cum.reshape(NG, SUBLANE, R_WIDTH), u_cum.reshape(NG, SUBLANE, R_WIDTH), 1, 1)
        a_tot, u_tot = scan_rows(a3[:, SUBLANE - 1, :], u3[:, SUBLANE - 1, :], 0, 1)
        h_end = u_tot + a_tot * hc
        g_idx = lax.broadcasted_iota(jnp.int32, (NG, R_WIDTH), 0)
        h_start = jnp.where(g_idx == 0, hc, pltpu.roll(h_end, 1, axis=0))
        h = (u3 + a3 * h_start[:, None, :]).reshape(TR, R_WIDTH)
    else:
        a_cum, u_cum = scan_rows(a_cum, u_cum, 0, Bs)
        h = u_cum + a_cum * jnp.concatenate([hc] * (TR // Bs), axis=0)
    y_out[0] = (h * _gelu_tanh(g_ref[0])).astype(y_out.dtype)
    h_last = h[TR - Bs:, :]
    hc_sc[...] = h_last
    hlast_out[0] = h_last


def rglru(p_r, conv0, h0, cw, cb, wa, wx, ba, bx, lam, TR, Bs, out_dtype):
    G, R, _ = p_r.shape
    CB = (CONV_W - 1) * Bs
    X0 = -(-CB // SUBLANE) * SUBLANE
    assert R % TR == 0 and TR % Bs == 0 and TR >= CB
    W = R_WIDTH
    const = lambda shape: pl.BlockSpec(shape, lambda g, r: (0,) * len(shape))
    kern = functools.partial(_rglru_kernel, TR=TR, Bs=Bs)
    return pl.pallas_call(
        kern,
        grid=(G, R // TR),
        in_specs=[pl.BlockSpec((1, TR, W), lambda g, r: (g, r, 0)),
                  pl.BlockSpec((1, TR, W), lambda g, r: (g, r, 1)),
                  pl.BlockSpec((1, CB, W), lambda g, r: (g, 0, 0)),
                  pl.BlockSpec((1, Bs, W), lambda g, r: (g, 0, 0)),
                  const((CONV_W, W)), const((1, W)),
                  const((R_BLOCKS, R_BLOCK_DIM, R_BLOCK_DIM)), const((R_BLOCKS, R_BLOCK_DIM, R_BLOCK_DIM)),
                  const((1, W)), const((1, W)), const((1, W))],
        out_specs=[pl.BlockSpec((1, TR, W), lambda g, r: (g, r, 0)),
                   pl.BlockSpec((1, Bs, W), lambda g, r: (g, 0, 0)),
                   pl.BlockSpec((1, CB, W), lambda g, r: (g, 0, 0))],
        out_shape=[jax.ShapeDtypeStruct((G, R, W), out_dtype),
                   jax.ShapeDtypeStruct((G, Bs, W), F32),
                   jax.ShapeDtypeStruct((G, CB, W), F32)],
        scratch_shapes=[pltpu.VMEM((X0 + TR, W), F32), pltpu.VMEM((Bs, W), F32)],
        compiler_params=_cparams("parallel", "arbitrary"),
        name="rglru",
    )(p_r, p_r, conv0, h0, cw, cb, wa, wx, ba, bx, lam)


def _swa_kernel(q_ref, kp_ref, kc_ref, vp_ref, vc_ref, bias_ref, sink_ref, o_ref, *, NBS, TQ, mask_first):
    GQ = S_HEADS // S_KV_HEADS
    lane = lax.broadcasted_iota(jnp.int32, (1, LANE), 1)
    lo = lane < S_HEAD_DIM

    def both_halves(x, j):
        xr = pltpu.roll(x, S_HEAD_DIM, axis=1)
        return (jnp.where(lo, x, xr) if j == 0 else jnp.where(lo, xr, x)).astype(BF16)

    s1_parts, s2_parts, v_parts = [], [], []
    for b in range(NBS):
        q = q_ref[b] * (S_HEAD_DIM ** -0.5)
        for j in range(S_KV_HEADS):
            stack = []
            for g in range(GQ):
                h = GQ * j + g
                qh = q[:, (h // 2) * LANE:(h // 2 + 1) * LANE]
                stack.append(jnp.where(lo if h % 2 == 0 else jnp.logical_not(lo), qh, 0.0))
            qs = jnp.concatenate(stack, axis=0).astype(BF16)
            s1_parts.append(_dot_nt(qs, both_halves(kp_ref[b], j)))
            s2_parts.append(_dot_nt(qs, both_halves(kc_ref[b], j)))
            v_parts.append((both_halves(vp_ref[b], j), both_halves(vc_ref[b], j)))
    s1 = jnp.concatenate(s1_parts, axis=0) + bias_ref[:, 0:WINDOW]
    s2 = jnp.concatenate(s2_parts, axis=0) + bias_ref[:, WINDOW:WINDOW + TQ]
    if mask_first:
        s1 = jnp.where(pl.program_id(1) == 0, NEG_INF, s1)
    sink = sink_ref[...]
    mx = jnp.maximum(jnp.maximum(jnp.max(s1, axis=1, keepdims=True), jnp.max(s2, axis=1, keepdims=True)), sink)
    p1 = jnp.exp(s1 - mx)
    p2 = jnp.exp(s2 - mx)
    inv = 1.0 / (jnp.sum(p1, axis=1, keepdims=True) + jnp.sum(p2, axis=1, keepdims=True) + jnp.exp(sink - mx))
    p1 = p1.astype(BF16)
    p2 = p2.astype(BF16)
    R = GQ * TQ
    for b in range(NBS):
        for j in range(S_KV_HEADS):
            n = b * S_KV_HEADS + j
            rows = slice(n * R, (n + 1) * R)
            vp, vc = v_parts[n]
            o = (_dot(p1[rows], vp) + _dot(p2[rows], vc)) * inv[rows]
            for pair in range(GQ // 2):
                even = o[(2 * pair) * TQ:(2 * pair + 1) * TQ]
                odd = o[(2 * pair + 1) * TQ:(2 * pair + 2) * TQ]
                blk = (GQ * j) // 2 + pair
                o_ref[b, :, blk * LANE:(blk + 1) * LANE] = jnp.where(lo, even, odd).astype(o_ref.dtype)


def swa(arrs, maps, shapes, bias_rows, sink_rows, grid, NBS, TQ, mask_first, out_dtype):
    kern = functools.partial(_swa_kernel, NBS=NBS, TQ=TQ, mask_first=mask_first)
    in_specs = [pl.BlockSpec(s, m) for s, m in zip(shapes, maps)]
    in_specs += [pl.BlockSpec(bias_rows.shape, lambda b, n: (0, 0)),
                 pl.BlockSpec(sink_rows.shape, lambda b, n: (0, 0))]
    return pl.pallas_call(
        kern,
        grid=grid,
        in_specs=in_specs,
        out_specs=pl.BlockSpec((NBS, TQ, S_WIDTH), lambda b, n: (b, n, 0)),
        out_shape=jax.ShapeDtypeStruct((grid[0] * NBS, grid[1] * TQ, S_WIDTH), out_dtype),
        compiler_params=_cparams("parallel", "arbitrary"),
        name="swa",
    )(*arrs, bias_rows, sink_rows)


def t5_bucket(dist):
    max_exact = N_BUCKETS // 2
    d = jnp.maximum(dist, 0)
    large = max_exact + (jnp.log(jnp.maximum(d, 1).astype(F32) / max_exact)
                         / math.log(MAX_DISTANCE / max_exact) * (N_BUCKETS - max_exact)).astype(jnp.int32)
    return jnp.where(d < max_exact, d, jnp.minimum(large, N_BUCKETS - 1))


def swa_bias_rows(rel_bias, TQ, NBS):
    qi = jnp.arange(TQ)[:, None]
    kj = jnp.arange(WINDOW + TQ)[None, :]
    dist = qi + WINDOW - kj
    onehot = (t5_bucket(dist)[..., None] == jnp.arange(N_BUCKETS)).astype(F32)
    b = jnp.einsum("qkn,nh->hqk", onehot, rel_bias.astype(F32), precision=lax.Precision.HIGHEST)
    b = jnp.where((dist >= 0) & (dist <= WINDOW), b, NEG_INF).reshape(S_HEADS * TQ, WINDOW + TQ)
    return jnp.tile(b, (NBS, 1))


def swa_sink_rows(sinks, TQ, NBS):
    return jnp.tile(jnp.repeat(sinks.astype(F32), TQ), NBS).reshape(-1, 1)


def _merge_kernel(hm, hr, hs, gm, gr, gs, x, wm, wr, ws, wo, out):
    def branch(h_ref, g_ref, w_ref):
        return _sigmoid(g_ref[...].astype(F32)) * _dot(h_ref[...].astype(BF16), w_ref[...])
    merged = branch(hm, gm, wm) + branch(hr, gr, wr) + branch(hs, gs, ws)
    out[...] = x[...] + _dot(merged.astype(BF16), wo[...])


def merge_out(hm, hr, hs, p_g, x, wm, wr, ws, wo):
    T, D = x.shape
    tm = _row_tile(T, 512)
    hspec = lambda w: pl.BlockSpec((tm, w), lambda i: (i, 0))
    gspec = lambda j: pl.BlockSpec((tm, D), lambda i, j=j: (i, j))
    wspec = lambda a: pl.BlockSpec(a.shape, lambda i: (0, 0))
    return pl.pallas_call(
        _merge_kernel,
        grid=(T // tm,),
        in_specs=[hspec(M_WIDTH), hspec(R_WIDTH), hspec(S_WIDTH), gspec(0), gspec(1), gspec(2),
                  hspec(D), wspec(wm), wspec(wr), wspec(ws), wspec(wo)],
        out_specs=hspec(D),
        out_shape=jax.ShapeDtypeStruct((T, D), F32),
        compiler_params=_cparams("parallel"),
        name="merge_out",
    )(hm, hr, hs, p_g, p_g, p_g, x, wm, wr, ws, wo)


def _xattn_kernel(x_ref, g_ref, wq_ref, wo_ref, mk_ref, mv_ref, out_ref, *, nb, tq, interleaved):
    NH = N_MEM_HEADS
    x = x_ref[...].reshape(nb * tq, D_MODEL)
    q = _dot(_rms(x, g_ref[...]).astype(BF16), wq_ref[...])
    head = lambda a, h: a[:, h * X_HEAD_DIM:(h + 1) * X_HEAD_DIM]
    parts = []
    for b in range(nb):
        qb = q[b * tq:(b + 1) * tq]
        if interleaved:
            qs = jnp.concatenate([head(qb, h) for h in range(NH)], axis=0).astype(BF16)
            parts.append(_dot_nt(qs, mk_ref[0, b].astype(BF16)))
        else:
            parts += [_dot_nt(head(qb, h).astype(BF16), head(mk_ref[0, b], h).astype(BF16)) for h in range(NH)]
    s = jnp.concatenate(parts, axis=0) * (X_HEAD_DIM ** -0.5)
    if interleaved:
        row_h = (lax.broadcasted_iota(jnp.int32, s.shape, 0) // tq) % NH
        s = jnp.where(lax.broadcasted_iota(jnp.int32, s.shape, 1) % NH == row_h, s, NEG_INF)
    p = jnp.exp(s - jnp.max(s, axis=1, keepdims=True))
    inv = 1.0 / jnp.sum(p, axis=1, keepdims=True)
    rows = []
    for b in range(nb):
        if interleaved:
            r = slice(b * NH * tq, (b + 1) * NH * tq)
            o = _dot(p[r].astype(BF16), mv_ref[0, b].astype(BF16)) * inv[r]
            heads = [o[h * tq:(h + 1) * tq] for h in range(NH)]
        else:
            heads = []
            for h in range(NH):
                r = slice((b * NH + h) * tq, (b * NH + h + 1) * tq)
                heads.append(_dot(p[r].astype(BF16), head(mv_ref[0, b], h).astype(BF16)) * inv[r])
        rows.append(jnp.concatenate(heads, axis=1))
    o_all = rows[0] if nb == 1 else jnp.concatenate(rows, axis=0)
    y = x + _dot(o_all.astype(BF16), wo_ref[...])
    out_ref[...] = y.reshape(nb, tq, D_MODEL)


def xattn(x, g, wq, wo, mem_k, mem_v, layer, nb, tq, interleaved):
    B, R, D = x.shape
    mem_blk = (1, nb) + mem_k.shape[2:]
    kern = functools.partial(_xattn_kernel, nb=nb, tq=tq, interleaved=interleaved)
    return pl.pallas_call(
        kern,
        grid=(B // nb, R // tq),
        in_specs=[pl.BlockSpec((nb, tq, D), lambda b, r: (b, r, 0)),
                  pl.BlockSpec((1, D), lambda b, r: (0, 0)),
                  pl.BlockSpec(wq.shape, lambda b, r: (0, 0)),
                  pl.BlockSpec(wo.shape, lambda b, r: (0, 0)),
                  pl.BlockSpec(mem_blk, lambda b, r: (layer, b, 0, 0)),
                  pl.BlockSpec(mem_blk, lambda b, r: (layer, b, 0, 0))],
        out_specs=pl.BlockSpec((nb, tq, D), lambda b, r: (b, r, 0)),
        out_shape=jax.ShapeDtypeStruct((B, R, D), F32),
        compiler_params=_cparams("parallel", "arbitrary"),
        name="xattn",
    )(x, g, wq, wo, mem_k, mem_v)


def _router_kernel(x_ref, g_ref, wrt_ref, route_out, route_t_out, counts_out, cnt_sc):
    @pl.when(pl.program_id(0) == 0)
    def _():
        cnt_sc[...] = jnp.zeros_like(cnt_sc)

    xn = _rms(x_ref[...], g_ref[...])
    logits_t = _dot_nt(wrt_ref[...], xn.astype(BF16))
    tm = logits_t.shape[1]
    row_g = lax.broadcasted_iota(jnp.int32, (SUBLANE, tm), 0)
    row_e = lax.broadcasted_iota(jnp.int32, (N_EXPERTS, tm), 0)
    row_e_f = row_e.astype(F32)

    def top1(v, rows):
        mx = jnp.max(v, axis=0, keepdims=True)
        return mx, jnp.min(jnp.where(v == mx, rows, float(LANE)), axis=0, keepdims=True)

    gl = jnp.where(row_g < N_GROUPS, logits_t[0:SUBLANE], NEG_INF)
    g_max, g_idx = top1(gl, row_g.astype(F32))
    g_w = 1.0 / jnp.sum(jnp.exp(gl - g_max), axis=0, keepdims=True)
    el = jnp.where((row_e // EXPERTS_PER_GROUP).astype(F32) == g_idx, logits_t[LANE:LANE + N_EXPERTS], NEG_INF)
    e1, i1 = top1(el, row_e_f)
    e2, i2 = top1(jnp.where(row_e_f == i1, NEG_INF, el), row_e_f)
    t = jnp.exp(e2 - e1)
    p1 = 1.0 / (1.0 + t)

    oh1 = (row_e_f == i1).astype(F32)
    oh2 = (row_e_f == i2).astype(F32)
    oh = oh1 + oh2
    r_idx = lax.broadcasted_iota(jnp.int32, (tm, tm), 0)
    c_idx = lax.broadcasted_iota(jnp.int32, (tm, tm), 1)
    before = (r_idx < c_idx).astype(BF16)
    base = cnt_sc[:, 0:1] + _dot(oh.astype(BF16), before)
    rank1 = jnp.sum(oh1 * base, axis=0, keepdims=True)
    rank2 = jnp.sum(oh2 * base, axis=0, keepdims=True)
    cnt = cnt_sc[...] + jnp.sum(oh, axis=1, keepdims=True)
    cnt_sc[...] = cnt
    counts_out[...] = cnt

    route_t = jnp.zeros((SUBLANE, tm), F32)
    for n, v in enumerate((i1, i2, p1 * g_w, t * p1 * g_w, rank1, rank2)):
        route_t = jnp.where(row_g == n, v, route_t)
    route_t_out[...] = route_t
    route_out[...] = jnp.concatenate([route_t, jnp.zeros((LANE - SUBLANE, tm), F32)], axis=0).T


def router(x, g, w_router_t):
    T, D = x.shape
    tm = _row_tile(T, 512)
    return pl.pallas_call(
        _router_kernel,
        grid=(T // tm,),
        in_specs=[pl.BlockSpec((tm, D), lambda i: (i, 0)),
                  pl.BlockSpec((1, D), lambda i: (0, 0)),
                  pl.BlockSpec(w_router_t.shape, lambda i: (0, 0))],
        out_specs=[pl.BlockSpec((tm, LANE), lambda i: (i, 0)), pl.BlockSpec((SUBLANE, tm), lambda i: (0, i)),
                   pl.BlockSpec((N_EXPERTS, LANE), lambda i: (0, 0))],
        out_shape=[jax.ShapeDtypeStruct((T, LANE), F32), jax.ShapeDtypeStruct((SUBLANE, T), F32),
                   jax.ShapeDtypeStruct((N_EXPERTS, LANE), F32)],
        scratch_shapes=[pltpu.VMEM((N_EXPERTS, LANE), F32)],
        compiler_params=_cparams("arbitrary"),
        name="router",
    )(x, g, w_router_t)


def moe_plan(routes, counts, TB, tms):
    experts = jnp.arange(N_EXPERTS, dtype=jnp.int32)
    counts = [c[:, 0].astype(jnp.int32) for c in counts]
    total = sum(counts)
    padded = (total + TB - 1) // TB * TB
    pad_ends = jnp.cumsum(padded)
    base = pad_ends - padded
    dests = []
    for route_t, cnt, tm in zip(routes, counts, tms):
        T = route_t.shape[1]
        e_idx = route_t[0:TOP_K].astype(jnp.int32)
        rank = route_t[4:4 + TOP_K].astype(jnp.int32)
        dest = jnp.sum(jnp.where(e_idx[None] == experts[:, None, None], base[:, None, None], 0), axis=0) + rank
        dests.append(dest.reshape(TOP_K, T // tm, tm).transpose(1, 0, 2).reshape(T // tm, 1, TOP_K * tm))
        base = base + cnt
    n_tokens = sum(r.shape[1] for r in routes)
    n_blocks = -(-(n_tokens * TOP_K + N_EXPERTS * (TB - 1)) // TB)
    block_start = jnp.arange(n_blocks, dtype=jnp.int32) * TB
    block_e = jnp.minimum(jnp.sum(pad_ends[None, :] <= block_start[:, None], axis=1), N_EXPERTS - 1).astype(jnp.int32)
    end_valid = jnp.sum(jnp.where(block_e[:, None] == experts, pad_ends - padded + total, 0), axis=1)
    n_valid = jnp.clip(end_valid - block_start, 0, TB).astype(jnp.int32)
    return block_e, n_valid, dests


DMA_UNROLL = 8


def _for_row_chunks(n_rows, fn):
    def body(c, carry):
        for u in range(DMA_UNROLL):
            fn(c * DMA_UNROLL + u)
        return carry
    lax.fori_loop(0, n_rows // DMA_UNROLL, body, 0)


def _pack_bf16_pair(x):
    n = x.shape[1] // 2
    bits = lambda a: lax.bitcast_convert_type(a.astype(BF16).astype(F32), jnp.uint32)
    return (bits(x[:, :n]) >> 16) | (bits(x[:, n:]) & jnp.uint32(0xFFFF0000))


def _unpack_bf16_pair(p):
    lo = lax.bitcast_convert_type(p << 16, F32).astype(BF16)
    hi = lax.bitcast_convert_type(p & jnp.uint32(0xFFFF0000), F32).astype(BF16)
    return lo, hi


def _dispatch_kernel(dst_ref, x_ref, g_ref, init_hbm, xs_hbm, buf, sem, *, tm, n_tiles):
    del init_hbm
    i = pl.program_id(0)
    slot = i % 2

    def copy(row, s, row_dst):
        return pltpu.make_async_copy(buf.at[s, pl.ds(row, 1)], xs_hbm.at[pl.ds(row_dst, 1)], sem.at[s])

    def wait_slot(s):
        def wait_row(r):
            for _ in range(TOP_K):
                copy(r, s, 0).wait()
        _for_row_chunks(tm, wait_row)

    @pl.when(i >= 2)
    def _():
        wait_slot(slot)

    buf[slot] = _pack_bf16_pair(_rms(x_ref[...], g_ref[...]))

    def start_row(r):
        for k in range(TOP_K):
            copy(r, slot, dst_ref[0, 0, k * tm + r]).start(priority=k % 2)
    _for_row_chunks(tm, start_row)

    @pl.when(i == n_tiles - 1)
    def _():
        wait_slot(slot)
        if n_tiles >= 2:
            wait_slot(1 - slot)


def moe_dispatch(x, g, dest, xs_init, tm):
    T, D = x.shape
    n_tiles = T // tm
    assert tm % DMA_UNROLL == 0
    kern = functools.partial(_dispatch_kernel, tm=tm, n_tiles=n_tiles)
    return pl.pallas_call(
        kern,
        grid=(n_tiles,),
        in_specs=[pl.BlockSpec((1, 1, TOP_K * tm), lambda i: (i, 0, 0), memory_space=pltpu.SMEM),
                  pl.BlockSpec((tm, D), lambda i: (i, 0)),
                  pl.BlockSpec((1, D), lambda i: (0, 0)),
                  pl.BlockSpec(memory_space=pl.ANY)],
        out_specs=pl.BlockSpec(memory_space=pl.ANY),
        out_shape=jax.ShapeDtypeStruct(xs_init.shape, xs_init.dtype),
        input_output_aliases={3: 0},
        scratch_shapes=[pltpu.VMEM((2, tm, D // 2), jnp.uint32), pltpu.SemaphoreType.DMA((2,))],
        compiler_params=_cparams("arbitrary"),
        name="moe_dispatch",
    )(dest, x, g, xs_init)


def _ffn_kernel(be_ref, nv_ref, xs_ref, wg_ref, wu_ref, wd_ref, ys_ref, wg_bf, wu_bf, wd_bf):
    i = pl.program_id(0)
    nv = nv_ref[i]

    @pl.when((i == 0) | (be_ref[i] != be_ref[jnp.maximum(i - 1, 0)]))
    def _():
        wg_bf[...] = wg_ref[0, 0].astype(BF16)
        wu_bf[...] = wu_ref[0, 0].astype(BF16)
        wd_bf[...] = wd_ref[0, 0].astype(BF16)

    @pl.when(nv > 0)
    def _():
        row = lax.broadcasted_iota(jnp.int32, (xs_ref.shape[0], 1), 0)
        packed = jnp.where(row < nv, xs_ref[...], jnp.uint32(0))
        x_lo, x_hi = _unpack_bf16_pair(packed)
        half = x_lo.shape[1]
        g = _dot(x_lo, wg_bf[:half, :]) + _dot(x_hi, wg_bf[half:, :])
        u = _dot(x_lo, wu_bf[:half, :]) + _dot(x_hi, wu_bf[half:, :])
        hid = (g * _sigmoid(g)) * u
        ys_ref[...] = _dot(hid.astype(BF16), wd_bf[...])

    @pl.when(nv == 0)
    def _():
        ys_ref[...] = jnp.zeros_like(ys_ref)


def expert_ffn(xs, block_e, n_valid, w_gate, w_up, w_down, layer, TB):
    P, D = xs.shape[0], 2 * xs.shape[1]
    n_blocks = P // TB
    wspec = lambda a, b: pl.BlockSpec((1, 1, a, b), lambda i, be, nv: (layer, be[i], 0, 0))
    grid_spec = pltpu.PrefetchScalarGridSpec(
        num_scalar_prefetch=2,
        grid=(n_blocks,),
        in_specs=[pl.BlockSpec((TB, D // 2), lambda i, be, nv: (i, 0)),
                  wspec(D, D_EXPERT), wspec(D, D_EXPERT), wspec(D_EXPERT, D)],
        out_specs=pl.BlockSpec((TB, D), lambda i, be, nv: (i, 0)),
        scratch_shapes=[pltpu.VMEM((D, D_EXPERT), BF16), pltpu.VMEM((D, D_EXPERT), BF16),
                        pltpu.VMEM((D_EXPERT, D), BF16)],
    )
    return pl.pallas_call(
        _ffn_kernel,
        grid_spec=grid_spec,
        out_shape=jax.ShapeDtypeStruct((P, D), F32),
        compiler_params=_cparams("arbitrary"),
        name="expert_ffn",
    )(block_e, n_valid, xs, w_gate, w_up, w_down)


def _combine_kernel(cur_ref, nxt_ref, x_ref, route_ref, gf_ref, ys_hbm, out_ref, ybuf, sem, *,
                    tm, n_tiles, final_norm):
    i = pl.program_id(0)
    slot = i % 2

    def copy(row_src, s, k, row):
        return pltpu.make_async_copy(ys_hbm.at[pl.ds(row_src, 1)], ybuf.at[s, k, pl.ds(row, 1)], sem.at[s])

    def start_tile(idx_ref, s):
        def start_row(r):
            for k in range(TOP_K):
                copy(idx_ref[0, 0, k * tm + r], s, k, r).start(priority=k % 2)
        _for_row_chunks(tm, start_row)

    @pl.when(i == 0)
    def _():
        start_tile(cur_ref, 0)

    @pl.when(i + 1 < n_tiles)
    def _():
        start_tile(nxt_ref, 1 - slot)

    def wait_row(r):
        for k in range(TOP_K):
            copy(0, slot, k, r).wait()
    _for_row_chunks(tm, wait_row)

    route = route_ref[...]
    lane_i = lax.broadcasted_iota(jnp.int32, route.shape, 1)
    g0 = jnp.sum(jnp.where(lane_i == 2, route, 0.0), axis=1, keepdims=True)
    g1 = jnp.sum(jnp.where(lane_i == 3, route, 0.0), axis=1, keepdims=True)
    y = x_ref[...] + (g0 * ybuf[slot, 0] + g1 * ybuf[slot, 1])
    out_ref[...] = _rms(y, gf_ref[...]) if final_norm else y


def moe_combine(x, ys, dest, route, g_final, tm, final_norm):
    T, D = x.shape
    n_tiles = T // tm
    assert tm % DMA_UNROLL == 0
    idx_blk = lambda f: pl.BlockSpec((1, 1, TOP_K * tm), f, memory_space=pltpu.SMEM)
    kern = functools.partial(_combine_kernel, tm=tm, n_tiles=n_tiles, final_norm=final_norm)
    return pl.pallas_call(
        kern,
        grid=(n_tiles,),
        in_specs=[idx_blk(lambda i: (i, 0, 0)),
                  idx_blk(lambda i: (jnp.minimum(i + 1, n_tiles - 1), 0, 0)),
                  pl.BlockSpec((tm, D), lambda i: (i, 0)),
                  pl.BlockSpec((tm, LANE), lambda i: (i, 0)),
                  pl.BlockSpec((1, D), lambda i: (0, 0)),
                  pl.BlockSpec(memory_space=pl.ANY)],
        out_specs=pl.BlockSpec((tm, D), lambda i: (i, 0)),
        out_shape=jax.ShapeDtypeStruct((T, D), F32),
        scratch_shapes=[pltpu.VMEM((2, TOP_K, tm, D), F32), pltpu.SemaphoreType.DMA((2,))],
        compiler_params=_cparams("arbitrary"),
        name="moe_combine",
    )(dest, dest, x, route, g_final, ys)


IN_SPLITS = ((4 * M_WIDTH,), (LANE,), (2 * R_WIDTH, S_WIDTH + 2 * S_KV_WIDTH, 3 * D_MODEL))
IN_DTYPES = (F32, F32, F32, F32, BF16)


def _prep_w_in(w):
    c0 = 4 * M_WIDTH
    c1 = c0 + 2 * M_HEADS
    gates = jnp.pad(w[..., c0:c1], ((0, 0), (0, 0), (0, LANE - 2 * M_HEADS)))
    return [w[..., :c0].astype(BF16), gates.astype(BF16), w[..., c1:].astype(BF16)]


def _prep_router(w_rg, w_re):
    pad = lambda w: jnp.pad(w.T, ((0, LANE - w.shape[1]), (0, 0)))
    return jnp.concatenate([pad(w_rg), pad(w_re)], axis=0).astype(BF16)


def _row(v):
    return v.reshape(1, -1).astype(F32)


def _pad_rows(a, n):
    return jnp.pad(a, ((0, 0), (0, n - a.shape[1])) + ((0, 0),) * (a.ndim - 2))


def _moe_block(T):
    return 512 if T >= 4096 else 128


def moe_layer(xs_in, slots, g_ffn, w_router, w_gate, w_up, w_down, layer, g_final, final_norm):
    D = xs_in[0].shape[1]
    TB = _moe_block(sum(x.shape[0] for x in xs_in))
    tms = [_row_tile(x.shape[0], 512) for x in xs_in]
    routed = [router(x, g_ffn, w_router) for x in xs_in]
    routes, counts = [r[0] for r in routed], [r[2] for r in routed]
    block_e, n_valid, dests = moe_plan([r[1] for r in routed], counts, TB, tms)
    buf = jnp.zeros((block_e.shape[0] * TB, D // 2), jnp.uint32) if slots is None else slots
    for x, dest, tm in zip(xs_in, dests, tms):
        buf = moe_dispatch(x, g_ffn, dest, buf, tm)
    ys = expert_ffn(buf, block_e, n_valid, w_gate, w_up, w_down, layer, TB)
    outs = [moe_combine(x, ys, dest, route, g_final, tm, final_norm)
            for x, dest, route, tm in zip(xs_in, dests, routes, tms)]
    return outs, buf


def kernel(x_prompt, x_sample, mem_prompt, cache_mem_k, cache_mem_v, cache_swa_k, cache_swa_v, state_mlstm_C, state_mlstm_n, state_mlstm_m, state_rglru_h, state_rglru_conv, norm_mix, w_in, m_igate_b, m_fgate_b, m_head_norm, r_conv_w, r_conv_b, r_gate_a_w, r_gate_a_b, r_gate_x_w, r_gate_x_b, r_lambda, swa_sinks, rel_bias, w_branch_m, w_branch_r, w_branch_s, w_out, norm_mem, xq_w, xk_w, xv_w, xo_w, norm_ffn, router_group_w, router_expert_w, moe_w_gate, moe_w_up, moe_w_down, norm_final):
    B, S, D = x_prompt.shape
    NS, TS, _ = x_sample.shape
    depth = w_in.shape[0]
    n_mem = mem_prompt.shape[1]
    Tp, Ts = B * S, NS * TS
    H = M_HEADS
    assert S % MLSTM_CHUNK == 0 and S % WINDOW == 0 and TS <= SUBLANE
    assert NS % XATTN_SHORT_BATCH == 0 and NS % SWA_SHORT_BATCH == 0 and NS % MLSTM_SHORT_BATCH == 0

    xp = x_prompt.reshape(Tp, D)
    xs = x_sample.reshape(Ts, D)
    mem2d = mem_prompt.reshape(B * n_mem, D)
    cache_k_rows = cache_mem_k.reshape(depth, NS, n_mem * N_MEM_HEADS, X_HEAD_DIM)
    cache_v_rows = cache_mem_v.reshape(depth, NS, n_mem * N_MEM_HEADS, X_HEAD_DIM)
    k_caches = cache_swa_k.reshape(depth * NS, WINDOW, S_KV_WIDTH)
    v_caches = cache_swa_v.reshape(depth * NS, WINDOW, S_KV_WIDTH)
    w_in_all = _prep_w_in(w_in)
    w_kv_all = [xk_w.astype(BF16), xv_w.astype(BF16)]
    NBP = 2 if B % 2 == 0 else 1
    bias_p = swa_bias_rows(rel_bias, WINDOW, NBP)
    bias_s = swa_bias_rows(rel_bias, SWA_SHORT_TQ, SWA_SHORT_BATCH)
    outs_p = {k: [] for k in ("mem_k", "mem_v", "swa_k", "swa_v", "C", "n", "m", "h", "conv")}
    outs_s = {k: [] for k in ("swa_k", "swa_v", "C", "n", "m", "h", "conv")}

    slots = None
    for l in range(depth):
        g_mix = _row(norm_mix[l])
        b_i = jnp.pad(m_igate_b[l], (0, SUBLANE - H)).reshape(SUBLANE, 1)
        b_f = jnp.pad(m_fgate_b[l], (0, SUBLANE - H)).reshape(SUBLANE, 1)
        hn = _row(m_head_norm[l])
        r_args = (r_conv_w[l], _row(r_conv_b[l]), r_gate_a_w[l].astype(BF16), r_gate_x_w[l].astype(BF16),
                  _row(r_gate_a_b[l]), _row(r_gate_x_b[l]), _row(r_lambda[l]))
        sink_p = swa_sink_rows(swa_sinks[l], WINDOW, NBP)
        sink_s = swa_sink_rows(swa_sinks[l], SWA_SHORT_TQ, SWA_SHORT_BATCH)
        merge_w = (w_branch_m[l].astype(BF16), w_branch_r[l].astype(BF16), w_branch_s[l].astype(BF16),
                   w_out[l].astype(BF16))
        wq_b, wo_b = xq_w[l].astype(BF16), xo_w[l].astype(BF16)
        w_router = _prep_router(router_group_w[l], router_expert_w[l])
        last = l == depth - 1
        moe_args = (_row(norm_ffn[l]), w_router, moe_w_gate, moe_w_up, moe_w_down, l, _row(norm_final), last)

        mk, mv = norm_matmul(mem2d, g_mix, w_kv_all, ((X_WIDTH,), (X_WIDTH,)), l, normalize=False)
        outs_p["mem_k"].append(mk.reshape(B, n_mem, N_MEM_HEADS, X_HEAD_DIM))
        outs_p["mem_v"].append(mv.reshape(B, n_mem, N_MEM_HEADS, X_HEAD_DIM))
        mk, mv = mk.reshape(1, B, n_mem, X_WIDTH), mv.reshape(1, B, n_mem, X_WIDTH)

        p_m, p_if, p_r, p_s, p_g = norm_matmul(xp, g_mix, w_in_all, IN_SPLITS, l, out_dtypes=IN_DTYPES)
        gts = jnp.swapaxes(p_if[:, :2 * H].reshape(B, S, 2 * H), 1, 2)
        gi = jnp.pad(gts[:, :H], ((0, 0), (0, SUBLANE - H), (0, 0)))
        gf = jnp.pad(gts[:, H:], ((0, 0), (0, SUBLANE - H), (0, 0)))
        h_m, caug, m_o = mlstm(p_m.reshape(B, S, 4 * M_WIDTH), gi, gf, b_i, b_f, hn,
                               jnp.zeros((B, H, M_HEAD_DIM, MLSTM_AUG), F32),
                               jnp.zeros((B, SUBLANE, LANE), F32), L=MLSTM_CHUNK)
        outs_p["C"].append(caug[..., :M_HEAD_DIM])
        outs_p["n"].append(caug[..., M_HEAD_DIM])
        outs_p["m"].append(m_o[:, :H, 0])

        h_r, h_last, conv_last = rglru(p_r.reshape(B, S, 2 * R_WIDTH),
                                       jnp.zeros((B, CONV_W - 1, R_WIDTH), F32),
                                       jnp.zeros((B, 1, R_WIDTH), F32), *r_args,
                                       TR=_row_tile(S, 512), Bs=1, out_dtype=BF16)
        outs_p["h"].append(h_last[:, 0])
        outs_p["conv"].append(conv_last)

        p_s3 = p_s.reshape(B, S, S_WIDTH + 2 * S_KV_WIDTH)
        kcol, vcol = S_WIDTH // S_KV_WIDTH, S_WIDTH // S_KV_WIDTH + 1
        prev = lambda col: (lambda b, n: (b, jnp.maximum(n - 1, 0), col))
        cur = lambda col: (lambda b, n: (b, n, col))
        kv_blk = (NBP, WINDOW, S_KV_WIDTH)
        h_s = swa((p_s3,) * 5, (cur(0), prev(kcol), cur(kcol), prev(vcol), cur(vcol)),
                  ((NBP, WINDOW, S_WIDTH), kv_blk, kv_blk, kv_blk, kv_blk),
                  bias_p, sink_p, grid=(B // NBP, S // WINDOW), NBS=NBP, TQ=WINDOW, mask_first=True,
                  out_dtype=BF16)
        outs_p["swa_k"].append(p_s3[:, S - WINDOW:, S_WIDTH:S_WIDTH + S_KV_WIDTH]
                               .reshape(B, WINDOW, S_KV_HEADS, S_HEAD_DIM))
        outs_p["swa_v"].append(p_s3[:, S - WINDOW:, S_WIDTH + S_KV_WIDTH:]
                               .reshape(B, WINDOW, S_KV_HEADS, S_HEAD_DIM))

        xp = merge_out(h_m.reshape(Tp, M_WIDTH), h_r.reshape(Tp, R_WIDTH), h_s.reshape(Tp, S_WIDTH),
                       p_g, xp, *merge_w)
        xp = xattn(xp.reshape(B, S, D), _row(norm_mem[l]), wq_b, wo_b, mk, mv, layer=0,
                   nb=1, tq=512 if S % 512 == 0 else _row_tile(S), interleaved=False).reshape(Tp, D)

        p_m, p_if, p_r, p_s, p_g = norm_matmul(xs, g_mix, w_in_all, IN_SPLITS, l, out_dtypes=IN_DTYPES)
        gts = jnp.swapaxes(p_if[:, :2 * H].reshape(NS, TS, 2 * H), 1, 2)
        gts = jnp.pad(gts, ((0, 0), (0, 0), (0, LANE - TS)))
        gi = jnp.pad(gts[:, :H], ((0, 0), (0, SUBLANE - H), (0, 0)))
        gf = jnp.pad(gts[:, H:], ((0, 0), (0, SUBLANE - H), (0, 0)))
        m0 = jnp.broadcast_to(jnp.pad(state_mlstm_m[l], ((0, 0), (0, SUBLANE - H)))[:, :, None],
                              (NS, SUBLANE, LANE))
        h_m, c_s, n_s, m_o = mlstm_short(_pad_rows(p_m.reshape(NS, TS, 4 * M_WIDTH), SUBLANE), gi, gf, b_i, b_f,
                                         hn, state_mlstm_C, state_mlstm_n, m0, layer=l,
                                         NBS=MLSTM_SHORT_BATCH, TS=TS)
        h_m = h_m[:, :TS].reshape(Ts, M_WIDTH)
        outs_s["C"].append(c_s)
        outs_s["n"].append(n_s)
        outs_s["m"].append(m_o[:, :H, 0])

        tmaj = lambda a: jnp.swapaxes(a, 0, 1).reshape(1, a.shape[0] * a.shape[1], a.shape[2])
        h_r, h_last, conv_last = rglru(tmaj(p_r.reshape(NS, TS, 2 * R_WIDTH)), tmaj(state_rglru_conv[l]),
                                       state_rglru_h[l].reshape(1, NS, R_WIDTH), *r_args,
                                       TR=TS * NS, Bs=NS, out_dtype=F32)
        h_r = jnp.swapaxes(h_r.reshape(TS, NS, R_WIDTH), 0, 1).reshape(Ts, R_WIDTH)
        outs_s["h"].append(h_last[0])
        outs_s["conv"].append(jnp.swapaxes(conv_last.reshape(CONV_W - 1, NS, R_WIDTH), 0, 1))

        TQ, NBS = SWA_SHORT_TQ, SWA_SHORT_BATCH
        p_s3 = p_s.reshape(NS, TS, S_WIDTH + 2 * S_KV_WIDTH)
        q_s = _pad_rows(p_s3[:, :, :S_WIDTH], TQ)
        k_new = p_s3[:, :, S_WIDTH:S_WIDTH + S_KV_WIDTH]
        v_new = p_s3[:, :, S_WIDTH + S_KV_WIDTH:]
        full = lambda b, n: (b, 0, 0)
        cache = lambda b, n: (l * (NS // NBS) + b, 0, 0)
        h_s = swa((q_s, k_caches, _pad_rows(k_new, TQ), v_caches, _pad_rows(v_new, TQ)),
                  (full, cache, full, cache, full),
                  ((NBS, TQ, S_WIDTH), (NBS, WINDOW, S_KV_WIDTH), (NBS, TQ, S_KV_WIDTH),
                   (NBS, WINDOW, S_KV_WIDTH), (NBS, TQ, S_KV_WIDTH)),
                  bias_s, sink_s, grid=(NS // NBS, 1), NBS=NBS, TQ=TQ, mask_first=False, out_dtype=F32)
        h_s = h_s[:, :TS].reshape(Ts, S_WIDTH)
        outs_s["swa_k"].append(k_new)
        outs_s["swa_v"].append(v_new)

        xs = merge_out(h_m, h_r, h_s, p_g, xs, *merge_w)
        xs8 = xattn(_pad_rows(xs.reshape(NS, TS, D), SUBLANE), _row(norm_mem[l]), wq_b, wo_b,
                    cache_k_rows, cache_v_rows, layer=l, nb=XATTN_SHORT_BATCH, tq=SUBLANE, interleaved=True)

        (xp, xs), slots = moe_layer([xp, xs8[:, :TS].reshape(Ts, D)], slots, *moe_args)

    st = lambda d, k: jnp.stack(d[k])

    def shifted_cache(caches, new):
        out = jnp.concatenate([caches.reshape(depth, NS, WINDOW, S_KV_WIDTH)[:, :, TS:], jnp.stack(new)], axis=2)
        return out.reshape(depth, NS, WINDOW, S_KV_HEADS, S_HEAD_DIM)

    return (xp.reshape(B, S, D), xs.reshape(NS, TS, D), st(outs_p, "mem_k"), st(outs_p, "mem_v"),
            st(outs_p, "swa_k"), st(outs_p, "swa_v"), st(outs_p, "C"), st(outs_p, "n"), st(outs_p, "m"),
            st(outs_p, "h"), st(outs_p, "conv"),
            shifted_cache(k_caches, outs_s["swa_k"]), shifted_cache(v_caches, outs_s["swa_v"]),
            st(outs_s, "C"), st(outs_s, "n"), st(outs_s, "m"), st(outs_s, "h"), st(outs_s, "conv"))
```

```python
import functools
import math

import jax
import jax.numpy as jnp
from jax import lax
from jax.experimental import pallas as pl
from jax.experimental.pallas import tpu as pltpu

F32 = jnp.float32
BF16 = jnp.bfloat16
NEG_INF = float("-inf")

LANE = 128
SUBLANE = 8
BF16_SUBLANE = 16
VMEM_LIMIT_BYTES = 56 * 1024 * 1024

D_MODEL = 1024
M_HEADS = 4
M_HEAD_DIM = 128
M_WIDTH = M_HEADS * M_HEAD_DIM
R_WIDTH = 512
R_BLOCKS = 4
R_BLOCK_DIM = R_WIDTH // R_BLOCKS
CONV_W = 4
LRU_C = 8.0
S_HEADS = 8
S_KV_HEADS = 2
S_HEAD_DIM = 64
S_WIDTH = S_HEADS * S_HEAD_DIM
S_KV_WIDTH = S_KV_HEADS * S_HEAD_DIM
WINDOW = 128
N_BUCKETS = 32
MAX_DISTANCE = 128
N_MEM_HEADS = 4
X_HEAD_DIM = 128
X_WIDTH = N_MEM_HEADS * X_HEAD_DIM
N_GROUPS = 4
EXPERTS_PER_GROUP = 8
N_EXPERTS = N_GROUPS * EXPERTS_PER_GROUP
TOP_K = 2
D_EXPERT = 512
EPS = 1e-6

MLSTM_CHUNK = 512
MLSTM_AUG = 2 * M_HEAD_DIM
MLSTM_SHORT_BATCH = 2
SWA_SHORT_TQ = BF16_SUBLANE
SWA_SHORT_BATCH = 16
XATTN_SHORT_BATCH = 16


def _cparams(*sem):
    return pltpu.CompilerParams(dimension_semantics=sem, vmem_limit_bytes=VMEM_LIMIT_BYTES)


def _rms(x, g):
    ms = jnp.mean(x * x, axis=-1, keepdims=True)
    return x * lax.rsqrt(ms + EPS) * g


def _sigmoid(x):
    return 1.0 / (1.0 + jnp.exp(-x))


def _softplus(x):
    return jnp.maximum(x, 0.0) + jnp.log1p(jnp.exp(-jnp.abs(x)))


def _log_sigmoid(x):
    return -_softplus(-x)


def _gelu_tanh(x):
    return 0.5 * x * (1.0 + jnp.tanh(math.sqrt(2.0 / math.pi) * (x + 0.044715 * (x * x * x))))


def _dot(a, b):
    return jnp.dot(a, b, preferred_element_type=F32)


def _dot_nt(a, b):
    return lax.dot_general(a, b, (((1,), (1,)), ((), ())), preferred_element_type=F32)


def _row_tile(n, pref=256):
    for t in (512, 256, 128, 64, 32, 16, 8):
        if t <= pref and n % t == 0:
            return t
    raise ValueError(f"row count {n} is not a multiple of {SUBLANE}")


def _norm_matmul_kernel(x_ref, g_ref, *refs, splits, normalize):
    w_refs, out_refs = refs[:len(splits)], refs[len(splits):]
    x = x_ref[...]
    if normalize:
        x = _rms(x, g_ref[...])
    xb = x.astype(BF16)
    outs = iter(out_refs)
    for w_ref, w_splits in zip(w_refs, splits):
        off = 0
        for n in w_splits:
            o_ref = next(outs)
            o_ref[...] = _dot(xb, w_ref[0, :, off:off + n]).astype(o_ref.dtype)
            off += n


def norm_matmul(x, g, weights, splits, layer, normalize=True, out_dtypes=None):
    flat = [n for w_splits in splits for n in w_splits]
    out_dtypes = out_dtypes or (F32,) * len(flat)
    T, D = x.shape
    tm = _row_tile(T, 512)
    assert all(sum(ws) == w.shape[2] for ws, w in zip(splits, weights)) and all(n % LANE == 0 for n in flat)
    kern = functools.partial(_norm_matmul_kernel, splits=tuple(tuple(ws) for ws in splits), normalize=normalize)
    return pl.pallas_call(
        kern,
        grid=(T // tm,),
        in_specs=[pl.BlockSpec((tm, D), lambda i: (i, 0)),
                  pl.BlockSpec((1, D), lambda i: (0, 0))]
                 + [pl.BlockSpec((1, D, w.shape[2]), lambda i: (layer, 0, 0), pipeline_mode=pl.Buffered(1))
                    for w in weights],
        out_specs=[pl.BlockSpec((tm, n), lambda i: (i, 0)) for n in flat],
        out_shape=[jax.ShapeDtypeStruct((T, n), dt) for n, dt in zip(flat, out_dtypes)],
        compiler_params=_cparams("parallel"),
        name="norm_matmul",
    )(x, g, *weights)


def _mlstm_kernel(q_ref, k_ref, v_ref, o_ref, gi_ref, gf_ref, bi_ref, bf_ref, hn_ref,
                  c0_ref, m0_ref, h_out, c_out, m_out, c_sc, m_sc, *, L, NBB):
    @pl.when(pl.program_id(1) == 0)
    def _():
        c_sc[...] = c0_ref[...]
        m_sc[...] = m0_ref[...]

    t_idx = lax.broadcasted_iota(jnp.int32, (L, L), 0)
    s_idx = lax.broadcasted_iota(jnp.int32, (L, L), 1)
    causal = t_idx >= s_idx
    eye = t_idx == s_idx
    row8 = lax.broadcasted_iota(jnp.int32, (SUBLANE, LANE), 0)
    ones_col = (lax.broadcasted_iota(jnp.int32, (L, M_HEAD_DIM), 1) == 0).astype(BF16)
    gi = [gi_ref[b] + bi_ref[...] for b in range(NBB)]
    gf = [_log_sigmoid(gf_ref[b] + bf_ref[...]) for b in range(NBB)]
    m_tiles = [m_sc[b] for b in range(NBB)]
    chains = [dict(b=b, h=h, hs=slice(h * M_HEAD_DIM, (h + 1) * M_HEAD_DIM))
              for b in range(NBB) for h in range(M_HEADS)]

    for c in chains:
        b, h = c["b"], c["h"]
        c["m_prev"] = m_tiles[b][h:h + 1, 0:1]
        c["b_col"] = jnp.sum(jnp.where(causal, gf[b][h:h + 1, :], 0.0), axis=1, keepdims=True)
    for c in chains:
        b, h = c["b"], c["h"]
        b_row = jnp.sum(jnp.where(eye, c["b_col"], 0.0), axis=0, keepdims=True)
        c["r_row"] = gi[b][h:h + 1, :] - b_row
        c["g_end"] = b_row[:, L - 1:L]
        c["log_in"] = jnp.where(causal, c["b_col"] + c["r_row"], NEG_INF)
        c["m_intra"] = jnp.max(c["log_in"], axis=1, keepdims=True)
    for c in chains:
        b, hs = c["b"], c["hs"]
        c["k"] = k_ref[b][:, hs] * (M_HEAD_DIM ** -0.5)
        c["caug"] = c_sc[b, c["h"]]
        c["vaug"] = jnp.concatenate([v_ref[b][:, hs].astype(BF16), ones_col], axis=1)
        qb = q_ref[b][:, hs].astype(BF16)
        c["qk"] = _dot_nt(qb, c["k"].astype(BF16))
        c["qc"] = _dot(qb, c["caug"].astype(BF16))
    for c in chains:
        log_prev = c["b_col"] + c["m_prev"]
        m_t = jnp.maximum(log_prev, c["m_intra"])
        sc = (c["qk"] * jnp.exp(c["log_in"] - m_t)).astype(BF16)
        nd = jnp.exp(log_prev - m_t) * c["qc"] + _dot(sc, c["vaug"])
        den = nd[:, M_HEAD_DIM:M_HEAD_DIM + 1]
        c["hh"] = nd[:, :M_HEAD_DIM] / jnp.maximum(jnp.abs(den), jnp.exp(-m_t))
    for c in chains:
        b, hs = c["b"], c["hs"]
        d = c["hh"] - jnp.mean(c["hh"], axis=-1, keepdims=True)
        y = d * lax.rsqrt(jnp.mean(d * d, axis=-1, keepdims=True) + EPS) * hn_ref[:, hs]
        h_out[b, :, hs] = (y * _sigmoid(o_ref[b][:, hs])).astype(h_out.dtype)
    for c in chains:
        b, h = c["b"], c["h"]
        lie = c["g_end"] + c["r_row"]
        m_end = jnp.maximum(c["g_end"] + c["m_prev"], jnp.max(lie, axis=1, keepdims=True))
        w_pe = jnp.exp(c["g_end"] + c["m_prev"] - m_end)
        ktw = (c["k"].T * jnp.exp(lie - m_end)).astype(BF16)
        c_new = w_pe * c["caug"] + _dot(ktw, c["vaug"])
        c_sc[b, h] = c_new
        c_out[b, h] = c_new
        m_tiles[b] = jnp.where(row8 == h, m_end, m_tiles[b])
    for b in range(NBB):
        m_sc[b] = m_tiles[b]
        m_out[b] = m_tiles[b]


def mlstm(p_m, gates_i, gates_f, b_i, b_f, head_norm, c0aug, m0, L):
    NB, R, _ = p_m.shape
    nc = R // L
    assert R % L == 0
    NBB = 2 if NB % 2 == 0 else 1
    blk = lambda j: pl.BlockSpec((NBB, L, M_WIDTH), lambda b, c, j=j: (b, c, j))
    state_spec = pl.BlockSpec((NBB, M_HEADS, M_HEAD_DIM, MLSTM_AUG), lambda b, c: (b, 0, 0, 0))
    m_spec = pl.BlockSpec((NBB, SUBLANE, LANE), lambda b, c: (b, 0, 0))
    gate_spec = pl.BlockSpec((NBB, SUBLANE, L), lambda b, c: (b, 0, c))
    kern = functools.partial(_mlstm_kernel, L=L, NBB=NBB)
    scratch = [pltpu.VMEM((NBB, M_HEADS, M_HEAD_DIM, MLSTM_AUG), F32), pltpu.VMEM((NBB, SUBLANE, LANE), F32)]
    return pl.pallas_call(
        kern,
        grid=(NB // NBB, nc),
        in_specs=[blk(0), blk(1), blk(2), blk(3), gate_spec, gate_spec,
                  pl.BlockSpec((SUBLANE, 1), lambda b, c: (0, 0)),
                  pl.BlockSpec((SUBLANE, 1), lambda b, c: (0, 0)),
                  pl.BlockSpec((1, M_WIDTH), lambda b, c: (0, 0)),
                  state_spec, m_spec],
        out_specs=[pl.BlockSpec((NBB, L, M_WIDTH), lambda b, c: (b, c, 0)), state_spec, m_spec],
        out_shape=[jax.ShapeDtypeStruct((NB, R, M_WIDTH), BF16),
                   jax.ShapeDtypeStruct((NB, M_HEADS, M_HEAD_DIM, MLSTM_AUG), F32),
                   jax.ShapeDtypeStruct((NB, SUBLANE, LANE), F32)],
        scratch_shapes=scratch,
        compiler_params=_cparams("parallel", "arbitrary"),
        name="mlstm",
    )(p_m, p_m, p_m, p_m, gates_i, gates_f, b_i, b_f, head_norm, c0aug, m0)


def _mlstm_short_kernel(q_ref, k_ref, v_ref, o_ref, gi_ref, gf_ref, bi_ref, bf_ref, hn_ref,
                        c0_ref, n0_ref, m0_ref, h_out, c_out, n_out, m_out, *, NBS, TS):
    L = SUBLANE
    t_idx = lax.broadcasted_iota(jnp.int32, (L, L), 0)
    s_idx = lax.broadcasted_iota(jnp.int32, (L, L), 1)
    causal = t_idx >= s_idx
    eye = t_idx == s_idx
    valid = lax.broadcasted_iota(jnp.int32, (1, L), 1) < TS
    row8 = lax.broadcasted_iota(jnp.int32, (SUBLANE, LANE), 0)
    gi = [gi_ref[b] + bi_ref[...] for b in range(NBS)]
    gf = [_log_sigmoid(gf_ref[b] + bf_ref[...]) for b in range(NBS)]
    m_tiles = [m0_ref[b] for b in range(NBS)]
    chains = [dict(b=b, h=h, hs=slice(h * M_HEAD_DIM, (h + 1) * M_HEAD_DIM))
              for b in range(NBS) for h in range(M_HEADS)]

    for c in chains:
        b, h = c["b"], c["h"]
        c["li"] = jnp.where(valid, gi[b][h:h + 1, 0:L], NEG_INF)
        lf = jnp.where(valid, gf[b][h:h + 1, 0:L], 0.0)
        c["m_prev"] = m_tiles[b][h:h + 1, 0:1]
        c["b_col"] = jnp.sum(jnp.where(causal, lf, 0.0), axis=1, keepdims=True)
    for c in chains:
        b_row = jnp.sum(jnp.where(eye, c["b_col"], 0.0), axis=0, keepdims=True)
        c["r_row"] = c["li"] - b_row
        c["g_end"] = b_row[:, L - 1:L]
        c["log_in"] = jnp.where(causal, c["b_col"] + c["r_row"], NEG_INF)
        c["m_intra"] = jnp.max(c["log_in"], axis=1, keepdims=True)
    for c in chains:
        b, hs = c["b"], c["hs"]
        c["q"] = q_ref[b][:, hs]
        c["k"] = k_ref[b][:, hs] * (M_HEAD_DIM ** -0.5)
        c["vb"] = v_ref[b][:, hs].astype(BF16)
        c["c0"] = c0_ref[0, b, c["h"]]
        c["n0"] = n0_ref[0, b, c["h"]:c["h"] + 1, :]
        qb = c["q"].astype(BF16)
        c["qk"] = _dot_nt(qb, c["k"].astype(BF16))
        c["qc"] = _dot(qb, c["c0"].astype(BF16))
        c["qn"] = jnp.sum(c["q"] * c["n0"], axis=1, keepdims=True)
    for c in chains:
        log_prev = c["b_col"] + c["m_prev"]
        m_t = jnp.maximum(log_prev, c["m_intra"])
        w_prev = jnp.exp(log_prev - m_t)
        sc = c["qk"] * jnp.exp(c["log_in"] - m_t)
        num = w_prev * c["qc"] + _dot(sc.astype(BF16), c["vb"])
        den = w_prev * c["qn"] + jnp.sum(sc, axis=1, keepdims=True)
        c["hh"] = num / jnp.maximum(jnp.abs(den), jnp.exp(-m_t))
    for c in chains:
        b, hs = c["b"], c["hs"]
        d = c["hh"] - jnp.mean(c["hh"], axis=-1, keepdims=True)
        y = d * lax.rsqrt(jnp.mean(d * d, axis=-1, keepdims=True) + EPS) * hn_ref[:, hs]
        h_out[b, :, hs] = y * _sigmoid(o_ref[b][:, hs])
    for c in chains:
        b, h = c["b"], c["h"]
        lie = c["g_end"] + c["r_row"]
        m_end = jnp.maximum(c["g_end"] + c["m_prev"], jnp.max(lie, axis=1, keepdims=True))
        w_pe = jnp.exp(c["g_end"] + c["m_prev"] - m_end)
        w_e = jnp.exp(lie - m_end)
        w_e_col = jnp.sum(jnp.where(eye, w_e, 0.0), axis=1, keepdims=True)
        kw = c["k"] * w_e_col
        c_out[b, h] = w_pe * c["c0"] + lax.dot_general(kw.astype(BF16), c["vb"], (((0,), (0,)), ((), ())),
                                                       preferred_element_type=F32)
        n_out[b, h:h + 1, :] = w_pe * c["n0"] + jnp.sum(kw, axis=0, keepdims=True)
        m_tiles[b] = jnp.where(row8 == h, m_end, m_tiles[b])
    for b in range(NBS):
        m_out[b] = m_tiles[b]


def mlstm_short(p_m, gates_i, gates_f, b_i, b_f, head_norm, c0, n0, m0, layer, NBS, TS):
    NB = p_m.shape[0]
    H, DH = M_HEADS, M_HEAD_DIM
    assert NB % NBS == 0 and TS <= SUBLANE
    blk = lambda j: pl.BlockSpec((NBS, SUBLANE, M_WIDTH), lambda i, j=j: (i, 0, j))
    tile = pl.BlockSpec((NBS, SUBLANE, LANE), lambda i: (i, 0, 0))
    kern = functools.partial(_mlstm_short_kernel, NBS=NBS, TS=TS)
    return pl.pallas_call(
        kern,
        grid=(NB // NBS,),
        in_specs=[blk(0), blk(1), blk(2), blk(3), tile, tile,
                  pl.BlockSpec((SUBLANE, 1), lambda i: (0, 0)),
                  pl.BlockSpec((SUBLANE, 1), lambda i: (0, 0)),
                  pl.BlockSpec((1, M_WIDTH), lambda i: (0, 0)),
                  pl.BlockSpec((1, NBS, H, DH, DH), lambda i: (layer, i, 0, 0, 0)),
                  pl.BlockSpec((1, NBS, H, DH), lambda i: (layer, i, 0, 0)),
                  tile],
        out_specs=[pl.BlockSpec((NBS, SUBLANE, M_WIDTH), lambda i: (i, 0, 0)),
                   pl.BlockSpec((NBS, H, DH, DH), lambda i: (i, 0, 0, 0)),
                   pl.BlockSpec((NBS, H, DH), lambda i: (i, 0, 0)),
                   tile],
        out_shape=[jax.ShapeDtypeStruct((NB, SUBLANE, M_WIDTH), F32),
                   jax.ShapeDtypeStruct((NB, H, DH, DH), F32),
                   jax.ShapeDtypeStruct((NB, H, DH), F32),
                   jax.ShapeDtypeStruct((NB, SUBLANE, LANE), F32)],
        compiler_params=_cparams("parallel"),
        name="mlstm_short",
    )(p_m, p_m, p_m, p_m, gates_i, gates_f, b_i, b_f, head_norm, c0, n0, m0)


def _rglru_kernel(x_ref, g_ref, conv0_ref, h0_ref, cw_ref, cb_ref, wa_ref, wx_ref, ba_ref, bx_ref,
                  lam_ref, y_out, hlast_out, conv_out, xpad_sc, hc_sc, *, TR, Bs):
    CB = (CONV_W - 1) * Bs
    X0 = -(-CB // SUBLANE) * SUBLANE

    @pl.when(pl.program_id(1) == 0)
    def _():
        xpad_sc[X0 - CB:X0, :] = conv0_ref[0]
        hc_sc[...] = h0_ref[0]

    xpad_sc[X0:X0 + TR, :] = x_ref[0]
    y = cb_ref[...]
    for j in range(CONV_W):
        y = y + xpad_sc[X0 - CB + j * Bs:X0 - CB + j * Bs + TR, :] * cw_ref[j:j + 1, :]
    tail = xpad_sc[X0 + TR - CB:X0 + TR, :]
    conv_out[0] = tail
    xpad_sc[X0 - CB:X0, :] = tail

    yb = y.astype(BF16)
    rs, is_ = [], []
    for n in range(R_BLOCKS):
        sl = slice(n * R_BLOCK_DIM, (n + 1) * R_BLOCK_DIM)
        rs.append(_dot(yb[:, sl], wa_ref[n]))
        is_.append(_dot(yb[:, sl], wx_ref[n]))
    r = _sigmoid(jnp.concatenate(rs, axis=1) + ba_ref[...])
    i = _sigmoid(jnp.concatenate(is_, axis=1) + bx_ref[...])
    log_a = -LRU_C * r * _softplus(-lam_ref[...])
    a_cum = jnp.exp(log_a)
    t = jnp.tanh(log_a)
    u_cum = jnp.sqrt(-2.0 * t / (1.0 - t)) * (i * y)

    def scan_rows(a, u, axis, stride):
        n = a.shape[axis]
        idx = lax.broadcasted_iota(jnp.int32, a.shape, axis)
        d = stride
        while d < n:
            keep = idx >= d
            u = jnp.where(keep, a * pltpu.roll(u, d, axis=axis) + u, u)
            a = jnp.where(keep, a * pltpu.roll(a, d, axis=axis), a)
            d *= 2
        return a, u

    hc = hc_sc[...]
    if Bs == 1:
        NG = TR // SUBLANE
        a3, u3 = scan_rows(a_cum.reshape(NG, SUBLANE, R_WIDTH), u_cum.reshape(NG, SUBLANE, R_WIDTH), 1, 1)
        a_tot, u_tot = scan_rows(a3[:, SUBLANE - 1, :], u3[:, SUBLANE - 1, :], 0, 1)
        h_end = u_tot + a_tot * hc
        g_idx = lax.broadcasted_iota(jnp.int32, (NG, R_WIDTH), 0)
        h_start = jnp.where(g_idx == 0, hc, pltpu.roll(h_end, 1, axis=0))
        h = (u3 + a3 * h_start[:, None, :]).reshape(TR, R_WIDTH)
    else:
        a_cum, u_cum = scan_rows(a_cum, u_cum, 0, Bs)
        h = u_cum + a_cum * jnp.concatenate([hc] * (TR // Bs), axis=0)
    y_out[0] = (h * _gelu_tanh(g_ref[0])).astype(y_out.dtype)
    h_last = h[TR - Bs:, :]
    hc_sc[...] = h_last
    hlast_out[0] = h_last


def rglru(p_r, conv0, h0, cw, cb, wa, wx, ba, bx, lam, TR, Bs, out_dtype):
    G, R, _ = p_r.shape
    CB = (CONV_W - 1) * Bs
    X0 = -(-CB // SUBLANE) * SUBLANE
    assert R % TR == 0 and TR % Bs == 0 and TR >= CB
    W = R_WIDTH
    const = lambda shape: pl.BlockSpec(shape, lambda g, r: (0,) * len(shape))
    kern = functools.partial(_rglru_kernel, TR=TR, Bs=Bs)
    return pl.pallas_call(
        kern,
        grid=(G, R // TR),
        in_specs=[pl.BlockSpec((1, TR, W), lambda g, r: (g, r, 0)),
                  pl.BlockSpec((1, TR, W), lambda g, r: (g, r, 1)),
                  pl.BlockSpec((1, CB, W), lambda g, r: (g, 0, 0)),
                  pl.BlockSpec((1, Bs, W), lambda g, r: (g, 0, 0)),
                  const((CONV_W, W)), const((1, W)),
                  const((R_BLOCKS, R_BLOCK_DIM, R_BLOCK_DIM)), const((R_BLOCKS, R_BLOCK_DIM, R_BLOCK_DIM)),
                  const((1, W)), const((1, W)), const((1, W))],
        out_specs=[pl.BlockSpec((1, TR, W), lambda g, r: (g, r, 0)),
                   pl.BlockSpec((1, Bs, W), lambda g, r: (g, 0, 0)),
                   pl.BlockSpec((1, CB, W), lambda g, r: (g, 0, 0))],
        out_shape=[jax.ShapeDtypeStruct((G, R, W), out_dtype),
                   jax.ShapeDtypeStruct((G, Bs, W), F32),
                   jax.ShapeDtypeStruct((G, CB, W), F32)],
        scratch_shapes=[pltpu.VMEM((X0 + TR, W), F32), pltpu.VMEM((Bs, W), F32)],
        compiler_params=_cparams("parallel", "arbitrary"),
        name="rglru",
    )(p_r, p_r, conv0, h0, cw, cb, wa, wx, ba, bx, lam)


def _swa_kernel(q_ref, kp_ref, kc_ref, vp_ref, vc_ref, bias_ref, sink_ref, o_ref, *, NBS, TQ, mask_first):
    GQ = S_HEADS // S_KV_HEADS
    lane = lax.broadcasted_iota(jnp.int32, (1, LANE), 1)
    lo = lane < S_HEAD_DIM

    def both_halves(x, j):
        xr = pltpu.roll(x, S_HEAD_DIM, axis=1)
        return (jnp.where(lo, x, xr) if j == 0 else jnp.where(lo, xr, x)).astype(BF16)

    s1_parts, s2_parts, v_parts = [], [], []
    for b in range(NBS):
        q = q_ref[b] * (S_HEAD_DIM ** -0.5)
        for j in range(S_KV_HEADS):
            stack = []
            for g in range(GQ):
                h = GQ * j + g
                qh = q[:, (h // 2) * LANE:(h // 2 + 1) * LANE]
                stack.append(jnp.where(lo if h % 2 == 0 else jnp.logical_not(lo), qh, 0.0))
            qs = jnp.concatenate(stack, axis=0).astype(BF16)
            s1_parts.append(_dot_nt(qs, both_halves(kp_ref[b], j)))
            s2_parts.append(_dot_nt(qs, both_halves(kc_ref[b], j)))
            v_parts.append((both_halves(vp_ref[b], j), both_halves(vc_ref[b], j)))
    s1 = jnp.concatenate(s1_parts, axis=0) + bias_ref[:, 0:WINDOW]
    s2 = jnp.concatenate(s2_parts, axis=0) + bias_ref[:, WINDOW:WINDOW + TQ]
    if mask_first:
        s1 = jnp.where(pl.program_id(1) == 0, NEG_INF, s1)
    sink = sink_ref[...]
    mx = jnp.maximum(jnp.maximum(jnp.max(s1, axis=1, keepdims=True), jnp.max(s2, axis=1, keepdims=True)), sink)
    p1 = jnp.exp(s1 - mx)
    p2 = jnp.exp(s2 - mx)
    inv = 1.0 / (jnp.sum(p1, axis=1, keepdims=True) + jnp.sum(p2, axis=1, keepdims=True) + jnp.exp(sink - mx))
    p1 = p1.astype(BF16)
    p2 = p2.astype(BF16)
    R = GQ * TQ
    for b in range(NBS):
        for j in range(S_KV_HEADS):
            n = b * S_KV_HEADS + j
            rows = slice(n * R, (n + 1) * R)
            vp, vc = v_parts[n]
            o = (_dot(p1[rows], vp) + _dot(p2[rows], vc)) * inv[rows]
            for pair in range(GQ // 2):
                even = o[(2 * pair) * TQ:(2 * pair + 1) * TQ]
                odd = o[(2 * pair + 1) * TQ:(2 * pair + 2) * TQ]
                blk = (GQ * j) // 2 + pair
                o_ref[b, :, blk * LANE:(blk + 1) * LANE] = jnp.where(lo, even, odd).astype(o_ref.dtype)


def swa(arrs, maps, shapes, bias_rows, sink_rows, grid, NBS, TQ, mask_first, out_dtype):
    kern = functools.partial(_swa_kernel, NBS=NBS, TQ=TQ, mask_first=mask_first)
    in_specs = [pl.BlockSpec(s, m) for s, m in zip(shapes, maps)]
    in_specs += [pl.BlockSpec(bias_rows.shape, lambda b, n: (0, 0)),
                 pl.BlockSpec(sink_rows.shape, lambda b, n: (0, 0))]
    return pl.pallas_call(
        kern,
        grid=grid,
        in_specs=in_specs,
        out_specs=pl.BlockSpec((NBS, TQ, S_WIDTH), lambda b, n: (b, n, 0)),
        out_shape=jax.ShapeDtypeStruct((grid[0] * NBS, grid[1] * TQ, S_WIDTH), out_dtype),
        compiler_params=_cparams("parallel", "arbitrary"),
        name="swa",
    )(*arrs, bias_rows, sink_rows)


def t5_bucket(dist):
    max_exact = N_BUCKETS // 2
    d = jnp.maximum(dist, 0)
    large = max_exact + (jnp.log(jnp.maximum(d, 1).astype(F32) / max_exact)
                         / math.log(MAX_DISTANCE / max_exact) * (N_BUCKETS - max_exact)).astype(jnp.int32)
    return jnp.where(d < max_exact, d, jnp.minimum(large, N_BUCKETS - 1))


def swa_bias_rows(rel_bias, TQ, NBS):
    qi = jnp.arange(TQ)[:, None]
    kj = jnp.arange(WINDOW + TQ)[None, :]
    dist = qi + WINDOW - kj
    onehot = (t5_bucket(dist)[..., None] == jnp.arange(N_BUCKETS)).astype(F32)
    b = jnp.einsum("qkn,nh->hqk", onehot, rel_bias.astype(F32), precision=lax.Precision.HIGHEST)
    b = jnp.where((dist >= 0) & (dist <= WINDOW), b, NEG_INF).reshape(S_HEADS * TQ, WINDOW + TQ)
    return jnp.tile(b, (NBS, 1))


def swa_sink_rows(sinks, TQ, NBS):
    return jnp.tile(jnp.repeat(sinks.astype(F32), TQ), NBS).reshape(-1, 1)


def _merge_kernel(hm, hr, hs, gm, gr, gs, x, wm, wr, ws, wo, out):
    def branch(h_ref, g_ref, w_ref):
        return _sigmoid(g_ref[...].astype(F32)) * _dot(h_ref[...].astype(BF16), w_ref[...])
    merged = branch(hm, gm, wm) + branch(hr, gr, wr) + branch(hs, gs, ws)
    out[...] = x[...] + _dot(merged.astype(BF16), wo[...])


def merge_out(hm, hr, hs, p_g, x, wm, wr, ws, wo):
    T, D = x.shape
    tm = _row_tile(T, 512)
    hspec = lambda w: pl.BlockSpec((tm, w), lambda i: (i, 0))
    gspec = lambda j: pl.BlockSpec((tm, D), lambda i, j=j: (i, j))
    wspec = lambda a: pl.BlockSpec(a.shape, lambda i: (0, 0))
    return pl.pallas_call(
        _merge_kernel,
        grid=(T // tm,),
        in_specs=[hspec(M_WIDTH), hspec(R_WIDTH), hspec(S_WIDTH), gspec(0), gspec(1), gspec(2),
                  hspec(D), wspec(wm), wspec(wr), wspec(ws), wspec(wo)],
        out_specs=hspec(D),
        out_shape=jax.ShapeDtypeStruct((T, D), F32),
        compiler_params=_cparams("parallel"),
        name="merge_out",
    )(hm, hr, hs, p_g, p_g, p_g, x, wm, wr, ws, wo)


def _xattn_kernel(x_ref, g_ref, wq_ref, wo_ref, mk_ref, mv_ref, out_ref, *, nb, tq, interleaved):
    NH = N_MEM_HEADS
    x = x_ref[...].reshape(nb * tq, D_MODEL)
    q = _dot(_rms(x, g_ref[...]).astype(BF16), wq_ref[...])
    head = lambda a, h: a[:, h * X_HEAD_DIM:(h + 1) * X_HEAD_DIM]
    parts = []
    for b in range(nb):
        qb = q[b * tq:(b + 1) * tq]
        if interleaved:
            qs = jnp.concatenate([head(qb, h) for h in range(NH)], axis=0).astype(BF16)
            parts.append(_dot_nt(qs, mk_ref[0, b].astype(BF16)))
        else:
            parts += [_dot_nt(head(qb, h).astype(BF16), head(mk_ref[0, b], h).astype(BF16)) for h in range(NH)]
    s = jnp.concatenate(parts, axis=0) * (X_HEAD_DIM ** -0.5)
    if interleaved:
        row_h = (lax.broadcasted_iota(jnp.int32, s.shape, 0) // tq) % NH
        s = jnp.where(lax.broadcasted_iota(jnp.int32, s.shape, 1) % NH == row_h, s, NEG_INF)
    p = jnp.exp(s - jnp.max(s, axis=1, keepdims=True))
    inv = 1.0 / jnp.sum(p, axis=1, keepdims=True)
    rows = []
    for b in range(nb):
        if interleaved:
            r = slice(b * NH * tq, (b + 1) * NH * tq)
            o = _dot(p[r].astype(BF16), mv_ref[0, b].astype(BF16)) * inv[r]
            heads = [o[h * tq:(h + 1) * tq] for h in range(NH)]
        else:
            heads = []
            for h in range(NH):
                r = slice((b * NH + h) * tq, (b * NH + h + 1) * tq)
                heads.append(_dot(p[r].astype(BF16), head(mv_ref[0, b], h).astype(BF16)) * inv[r])
        rows.append(jnp.concatenate(heads, axis=1))
    o_all = rows[0] if nb == 1 else jnp.concatenate(rows, axis=0)
    y = x + _dot(o_all.astype(BF16), wo_ref[...])
    out_ref[...] = y.reshape(nb, tq, D_MODEL)


def xattn(x, g, wq, wo, mem_k, mem_v, layer, nb, tq, interleaved):
    B, R, D = x.shape
    mem_blk = (1, nb) + mem_k.shape[2:]
    kern = functools.partial(_xattn_kernel, nb=nb, tq=tq, interleaved=interleaved)
    return pl.pallas_call(
        kern,
        grid=(B // nb, R // tq),
        in_specs=[pl.BlockSpec((nb, tq, D), lambda b, r: (b, r, 0)),
                  pl.BlockSpec((1, D), lambda b, r: (0, 0)),
                  pl.BlockSpec(wq.shape, lambda b, r: (0, 0)),
                  pl.BlockSpec(wo.shape, lambda b, r: (0, 0)),
                  pl.BlockSpec(mem_blk, lambda b, r: (layer, b, 0, 0)),
                  pl.BlockSpec(mem_blk, lambda b, r: (layer, b, 0, 0))],
        out_specs=pl.BlockSpec((nb, tq, D), lambda b, r: (b, r, 0)),
        out_shape=jax.ShapeDtypeStruct((B, R, D), F32),
        compiler_params=_cparams("parallel", "arbitrary"),
        name="xattn",
    )(x, g, wq, wo, mem_k, mem_v)


def _router_kernel(x_ref, g_ref, wrt_ref, route_out, route_t_out, counts_out, cnt_sc):
    @pl.when(pl.program_id(0) == 0)
    def _():
        cnt_sc[...] = jnp.zeros_like(cnt_sc)

    xn = _rms(x_ref[...], g_ref[...])
    logits_t = _dot_nt(wrt_ref[...], xn.astype(BF16))
    tm = logits_t.shape[1]
    row_g = lax.broadcasted_iota(jnp.int32, (SUBLANE, tm), 0)
    row_e = lax.broadcasted_iota(jnp.int32, (N_EXPERTS, tm), 0)
    row_e_f = row_e.astype(F32)

    def top1(v, rows):
        mx = jnp.max(v, axis=0, keepdims=True)
        return mx, jnp.min(jnp.where(v == mx, rows, float(LANE)), axis=0, keepdims=True)

    gl = jnp.where(row_g < N_GROUPS, logits_t[0:SUBLANE], NEG_INF)
    g_max, g_idx = top1(gl, row_g.astype(F32))
    g_w = 1.0 / jnp.sum(jnp.exp(gl - g_max), axis=0, keepdims=True)
    el = jnp.where((row_e // EXPERTS_PER_GROUP).astype(F32) == g_idx, logits_t[LANE:LANE + N_EXPERTS], NEG_INF)
    e1, i1 = top1(el, row_e_f)
    e2, i2 = top1(jnp.where(row_e_f == i1, NEG_INF, el), row_e_f)
    t = jnp.exp(e2 - e1)
    p1 = 1.0 / (1.0 + t)

    oh1 = (row_e_f == i1).astype(F32)
    oh2 = (row_e_f == i2).astype(F32)
    oh = oh1 + oh2
    r_idx = lax.broadcasted_iota(jnp.int32, (tm, tm), 0)
    c_idx = lax.broadcasted_iota(jnp.int32, (tm, tm), 1)
    before = (r_idx < c_idx).astype(BF16)
    base = cnt_sc[:, 0:1] + _dot(oh.astype(BF16), before)
    rank1 = jnp.sum(oh1 * base, axis=0, keepdims=True)
    rank2 = jnp.sum(oh2 * base, axis=0, keepdims=True)
    cnt = cnt_sc[...] + jnp.sum(oh, axis=1, keepdims=True)
    cnt_sc[...] = cnt
    counts_out[...] = cnt

    route_t = jnp.zeros((SUBLANE, tm), F32)
    for n, v in enumerate((i1, i2, p1 * g_w, t * p1 * g_w, rank1, rank2)):
        route_t = jnp.where(row_g == n, v, route_t)
    route_t_out[...] = route_t
    route_out[...] = jnp.concatenate([route_t, jnp.zeros((LANE - SUBLANE, tm), F32)], axis=0).T


def router(x, g, w_router_t):
    T, D = x.shape
    tm = _row_tile(T, 512)
    return pl.pallas_call(
        _router_kernel,
        grid=(T // tm,),
        in_specs=[pl.BlockSpec((tm, D), lambda i: (i, 0)),
                  pl.BlockSpec((1, D), lambda i: (0, 0)),
                  pl.BlockSpec(w_router_t.shape, lambda i: (0, 0))],
        out_specs=[pl.BlockSpec((tm, LANE), lambda i: (i, 0)), pl.BlockSpec((SUBLANE, tm), lambda i: (0, i)),
                   pl.BlockSpec((N_EXPERTS, LANE), lambda i: (0, 0))],
        out_shape=[jax.ShapeDtypeStruct((T, LANE), F32), jax.ShapeDtypeStruct((SUBLANE, T), F32),
                   jax.ShapeDtypeStruct((N_EXPERTS, LANE), F32)],
        scratch_shapes=[pltpu.VMEM((N_EXPERTS, LANE), F32)],
        compiler_params=_cparams("arbitrary"),
        name="router",
    )(x, g, w_router_t)


def moe_plan(routes, counts, TB, tms):
    experts = jnp.arange(N_EXPERTS, dtype=jnp.int32)
    counts = [c[:, 0].astype(jnp.int32) for c in counts]
    total = sum(counts)
    padded = (total + TB - 1) // TB * TB
    pad_ends = jnp.cumsum(padded)
    base = pad_ends - padded
    dests = []
    for route_t, cnt, tm in zip(routes, counts, tms):
        T = route_t.shape[1]
        e_idx = route_t[0:TOP_K].astype(jnp.int32)
        rank = route_t[4:4 + TOP_K].astype(jnp.int32)
        dest = jnp.sum(jnp.where(e_idx[None] == experts[:, None, None], base[:, None, None], 0), axis=0) + rank
        dests.append(dest.reshape(TOP_K, T // tm, tm).transpose(1, 0, 2).reshape(T // tm, 1, TOP_K * tm))
        base = base + cnt
    n_tokens = sum(r.shape[1] for r in routes)
    n_blocks = -(-(n_tokens * TOP_K + N_EXPERTS * (TB - 1)) // TB)
    block_start = jnp.arange(n_blocks, dtype=jnp.int32) * TB
    block_e = jnp.minimum(jnp.sum(pad_ends[None, :] <= block_start[:, None], axis=1), N_EXPERTS - 1).astype(jnp.int32)
    end_valid = jnp.sum(jnp.where(block_e[:, None] == experts, pad_ends - padded + total, 0), axis=1)
    n_valid = jnp.clip(end_valid - block_start, 0, TB).astype(jnp.int32)
    return block_e, n_valid, dests


DMA_UNROLL = 8


def _for_row_chunks(n_rows, fn):
    def body(c, carry):
        for u in range(DMA_UNROLL):
            fn(c * DMA_UNROLL + u)
        return carry
    lax.fori_loop(0, n_rows // DMA_UNROLL, body, 0)


def _pack_bf16_pair(x):
    n = x.shape[1] // 2
    bits = lambda a: lax.bitcast_convert_type(a.astype(BF16).astype(F32), jnp.uint32)
    return (bits(x[:, :n]) >> 16) | (bits(x[:, n:]) & jnp.uint32(0xFFFF0000))


def _unpack_bf16_pair(p):
    lo = lax.bitcast_convert_type(p << 16, F32).astype(BF16)
    hi = lax.bitcast_convert_type(p & jnp.uint32(0xFFFF0000), F32).astype(BF16)
    return lo, hi


def _dispatch_kernel(dst_ref, x_ref, g_ref, init_hbm, xs_hbm, buf, sem, *, tm, n_tiles):
    del init_hbm
    i = pl.program_id(0)
    slot = i % 2

    def copy(row, s, row_dst):
        return pltpu.make_async_copy(buf.at[s, pl.ds(row, 1)], xs_hbm.at[pl.ds(row_dst, 1)], sem.at[s])

    def wait_slot(s):
        def wait_row(r):
            for _ in range(TOP_K):
                copy(r, s, 0).wait()
        _for_row_chunks(tm, wait_row)

    @pl.when(i >= 2)
    def _():
        wait_slot(slot)

    buf[slot] = _pack_bf16_pair(_rms(x_ref[...], g_ref[...]))

    def start_row(r):
        for k in range(TOP_K):
            copy(r, slot, dst_ref[0, 0, k * tm + r]).start(priority=k % 2)
    _for_row_chunks(tm, start_row)

    @pl.when(i == n_tiles - 1)
    def _():
        wait_slot(slot)
        if n_tiles >= 2:
            wait_slot(1 - slot)


def moe_dispatch(x, g, dest, xs_init, tm):
    T, D = x.shape
    n_tiles = T // tm
    assert tm % DMA_UNROLL == 0
    kern = functools.partial(_dispatch_kernel, tm=tm, n_tiles=n_tiles)
    return pl.pallas_call(
        kern,
        grid=(n_tiles,),
        in_specs=[pl.BlockSpec((1, 1, TOP_K * tm), lambda i: (i, 0, 0), memory_space=pltpu.SMEM),
                  pl.BlockSpec((tm, D), lambda i: (i, 0)),
                  pl.BlockSpec((1, D), lambda i: (0, 0)),
                  pl.BlockSpec(memory_space=pl.ANY)],
        out_specs=pl.BlockSpec(memory_space=pl.ANY),
        out_shape=jax.ShapeDtypeStruct(xs_init.shape, xs_init.dtype),
        input_output_aliases={3: 0},
        scratch_shapes=[pltpu.VMEM((2, tm, D // 2), jnp.uint32), pltpu.SemaphoreType.DMA((2,))],
        compiler_params=_cparams("arbitrary"),
        name="moe_dispatch",
    )(dest, x, g, xs_init)


def _ffn_kernel(be_ref, nv_ref, xs_ref, wg_ref, wu_ref, wd_ref, ys_ref, wg_bf, wu_bf, wd_bf):
    i = pl.program_id(0)
    nv = nv_ref[i]

    @pl.when((i == 0) | (be_ref[i] != be_ref[jnp.maximum(i - 1, 0)]))
    def _():
        wg_bf[...] = wg_ref[0, 0].astype(BF16)
        wu_bf[...] = wu_ref[0, 0].astype(BF16)
        wd_bf[...] = wd_ref[0, 0].astype(BF16)

    @pl.when(nv > 0)
    def _():
        row = lax.broadcasted_iota(jnp.int32, (xs_ref.shape[0], 1), 0)
        packed = jnp.where(row < nv, xs_ref[...], jnp.uint32(0))
        x_lo, x_hi = _unpack_bf16_pair(packed)
        half = x_lo.shape[1]
        g = _dot(x_lo, wg_bf[:half, :]) + _dot(x_hi, wg_bf[half:, :])
        u = _dot(x_lo, wu_bf[:half, :]) + _dot(x_hi, wu_bf[half:, :])
        hid = (g * _sigmoid(g)) * u
        ys_ref[...] = _dot(hid.astype(BF16), wd_bf[...])

    @pl.when(nv == 0)
    def _():
        ys_ref[...] = jnp.zeros_like(ys_ref)


def expert_ffn(xs, block_e, n_valid, w_gate, w_up, w_down, layer, TB):
    P, D = xs.shape[0], 2 * xs.shape[1]
    n_blocks = P // TB
    wspec = lambda a, b: pl.BlockSpec((1, 1, a, b), lambda i, be, nv: (layer, be[i], 0, 0))
    grid_spec = pltpu.PrefetchScalarGridSpec(
        num_scalar_prefetch=2,
        grid=(n_blocks,),
        in_specs=[pl.BlockSpec((TB, D // 2), lambda i, be, nv: (i, 0)),
                  wspec(D, D_EXPERT), wspec(D, D_EXPERT), wspec(D_EXPERT, D)],
        out_specs=pl.BlockSpec((TB, D), lambda i, be, nv: (i, 0)),
        scratch_shapes=[pltpu.VMEM((D, D_EXPERT), BF16), pltpu.VMEM((D, D_EXPERT), BF16),
                        pltpu.VMEM((D_EXPERT, D), BF16)],
    )
    return pl.pallas_call(
        _ffn_kernel,
        grid_spec=grid_spec,
        out_shape=jax.ShapeDtypeStruct((P, D), F32),
        compiler_params=_cparams("arbitrary"),
        name="expert_ffn",
    )(block_e, n_valid, xs, w_gate, w_up, w_down)


def _combine_kernel(cur_ref, nxt_ref, x_ref, route_ref, gf_ref, ys_hbm, out_ref, ybuf, sem, *,
                    tm, n_tiles, final_norm):
    i = pl.program_id(0)
    slot = i % 2

    def copy(row_src, s, k, row):
        return pltpu.make_async_copy(ys_hbm.at[pl.ds(row_src, 1)], ybuf.at[s, k, pl.ds(row, 1)], sem.at[s])

    def start_tile(idx_ref, s):
        def start_row(r):
            for k in range(TOP_K):
                copy(idx_ref[0, 0, k * tm + r], s, k, r).start(priority=k % 2)
        _for_row_chunks(tm, start_row)

    @pl.when(i == 0)
    def _():
        start_tile(cur_ref, 0)

    @pl.when(i + 1 < n_tiles)
    def _():
        start_tile(nxt_ref, 1 - slot)

    def wait_row(r):
        for k in range(TOP_K):
            copy(0, slot, k, r).wait()
    _for_row_chunks(tm, wait_row)

    route = route_ref[...]
    lane_i = lax.broadcasted_iota(jnp.int32, route.shape, 1)
    g0 = jnp.sum(jnp.where(lane_i == 2, route, 0.0), axis=1, keepdims=True)
    g1 = jnp.sum(jnp.where(lane_i == 3, route, 0.0), axis=1, keepdims=True)
    y = x_ref[...] + (g0 * ybuf[slot, 0] + g1 * ybuf[slot, 1])
    out_ref[...] = _rms(y, gf_ref[...]) if final_norm else y


def moe_combine(x, ys, dest, route, g_final, tm, final_norm):
    T, D = x.shape
    n_tiles = T // tm
    assert tm % DMA_UNROLL == 0
    idx_blk = lambda f: pl.BlockSpec((1, 1, TOP_K * tm), f, memory_space=pltpu.SMEM)
    kern = functools.partial(_combine_kernel, tm=tm, n_tiles=n_tiles, final_norm=final_norm)
    return pl.pallas_call(
        kern,
        grid=(n_tiles,),
        in_specs=[idx_blk(lambda i: (i, 0, 0)),
                  idx_blk(lambda i: (jnp.minimum(i + 1, n_tiles - 1), 0, 0)),
                  pl.BlockSpec((tm, D), lambda i: (i, 0)),
                  pl.BlockSpec((tm, LANE), lambda i: (i, 0)),
                  pl.BlockSpec((1, D), lambda i: (0, 0)),
                  pl.BlockSpec(memory_space=pl.ANY)],
        out_specs=pl.BlockSpec((tm, D), lambda i: (i, 0)),
        out_shape=jax.ShapeDtypeStruct((T, D), F32),
        scratch_shapes=[pltpu.VMEM((2, TOP_K, tm, D), F32), pltpu.SemaphoreType.DMA((2,))],
        compiler_params=_cparams("arbitrary"),
        name="moe_combine",
    )(dest, dest, x, route, g_final, ys)


IN_SPLITS = ((4 * M_WIDTH,), (LANE,), (2 * R_WIDTH, S_WIDTH + 2 * S_KV_WIDTH, 3 * D_MODEL))
IN_DTYPES = (F32, F32, F32, F32, BF16)


def _prep_w_in(w):
    c0 = 4 * M_WIDTH
    c1 = c0 + 2 * M_HEADS
    gates = jnp.pad(w[..., c0:c1], ((0, 0), (0, 0), (0, LANE - 2 * M_HEADS)))
    return [w[..., :c0].astype(BF16), gates.astype(BF16), w[..., c1:].astype(BF16)]


def _prep_router(w_rg, w_re):
    pad = lambda w: jnp.pad(w.T, ((0, LANE - w.shape[1]), (0, 0)))
    return jnp.concatenate([pad(w_rg), pad(w_re)], axis=0).astype(BF16)


def _row(v):
    return v.reshape(1, -1).astype(F32)


def _pad_rows(a, n):
    return jnp.pad(a, ((0, 0), (0, n - a.shape[1])) + ((0, 0),) * (a.ndim - 2))


def _moe_block(T):
    return 512 if T >= 4096 else 128


def moe_layer(xs_in, slots, g_ffn, w_router, w_gate, w_up, w_down, layer, g_final, final_norm):
    D = xs_in[0].shape[1]
    TB = _moe_block(sum(x.shape[0] for x in xs_in))
    tms = [_row_tile(x.shape[0], 512) for x in xs_in]
    routed = [router(x, g_ffn, w_router) for x in xs_in]
    routes, counts = [r[0] for r in routed], [r[2] for r in routed]
    block_e, n_valid, dests = moe_plan([r[1] for r in routed], counts, TB, tms)
    buf = jnp.zeros((block_e.shape[0] * TB, D // 2), jnp.uint32) if slots is None else slots
    for x, dest, tm in zip(xs_in, dests, tms):
        buf = moe_dispatch(x, g_ffn, dest, buf, tm)
    ys = expert_ffn(buf, block_e, n_valid, w_gate, w_up, w_down, layer, TB)
    outs = [moe_combine(x, ys, dest, route, g_final, tm, final_norm)
            for x, dest, route, tm in zip(xs_in, dests, routes, tms)]
    return outs, buf


def kernel(x_prompt, x_sample, mem_prompt, cache_mem_k, cache_mem_v, cache_swa_k, cache_swa_v, state_mlstm_C, state_mlstm_n, state_mlstm_m, state_rglru_h, state_rglru_conv, norm_mix, w_in, m_igate_b, m_fgate_b, m_head_norm, r_conv_w, r_conv_b, r_gate_a_w, r_gate_a_b, r_gate_x_w, r_gate_x_b, r_lambda, swa_sinks, rel_bias, w_branch_m, w_branch_r, w_branch_s, w_out, norm_mem, xq_w, xk_w, xv_w, xo_w, norm_ffn, router_group_w, router_expert_w, moe_w_gate, moe_w_up, moe_w_down, norm_final):
    B, S, D = x_prompt.shape
    NS, TS, _ = x_sample.shape
    depth = w_in.shape[0]
    n_mem = mem_prompt.shape[1]
    Tp, Ts = B * S, NS * TS
    H = M_HEADS
    assert S % MLSTM_CHUNK == 0 and S % WINDOW == 0 and TS <= SUBLANE
    assert NS % XATTN_SHORT_BATCH == 0 and NS % SWA_SHORT_BATCH == 0 and NS % MLSTM_SHORT_BATCH == 0

    xp = x_prompt.reshape(Tp, D)
    xs = x_sample.reshape(Ts, D)
    mem2d = mem_prompt.reshape(B * n_mem, D)
    cache_k_rows = cache_mem_k.reshape(depth, NS, n_mem * N_MEM_HEADS, X_HEAD_DIM)
    cache_v_rows = cache_mem_v.reshape(depth, NS, n_mem * N_MEM_HEADS, X_HEAD_DIM)
    k_caches = cache_swa_k.reshape(depth * NS, WINDOW, S_KV_WIDTH)
    v_caches = cache_swa_v.reshape(depth * NS, WINDOW, S_KV_WIDTH)
    w_in_all = _prep_w_in(w_in)
    w_kv_all = [xk_w.astype(BF16), xv_w.astype(BF16)]
    NBP = 2 if B % 2 == 0 else 1
    bias_p = swa_bias_rows(rel_bias, WINDOW, NBP)
    bias_s = swa_bias_rows(rel_bias, SWA_SHORT_TQ, SWA_SHORT_BATCH)
    outs_p = {k: [] for k in ("mem_k", "mem_v", "swa_k", "swa_v", "C", "n", "m", "h", "conv")}
    outs_s = {k: [] for k in ("swa_k", "swa_v", "C", "n", "m", "h", "conv")}

    slots = None
    for l in range(depth):
        g_mix = _row(norm_mix[l])
        b_i = jnp.pad(m_igate_b[l], (0, SUBLANE - H)).reshape(SUBLANE, 1)
        b_f = jnp.pad(m_fgate_b[l], (0, SUBLANE - H)).reshape(SUBLANE, 1)
        hn = _row(m_head_norm[l])
        r_args = (r_conv_w[l], _row(r_conv_b[l]), r_gate_a_w[l].astype(BF16), r_gate_x_w[l].astype(BF16),
                  _row(r_gate_a_b[l]), _row(r_gate_x_b[l]), _row(r_lambda[l]))
        sink_p = swa_sink_rows(swa_sinks[l], WINDOW, NBP)
        sink_s = swa_sink_rows(swa_sinks[l], SWA_SHORT_TQ, SWA_SHORT_BATCH)
        merge_w = (w_branch_m[l].astype(BF16), w_branch_r[l].astype(BF16), w_branch_s[l].astype(BF16),
                   w_out[l].astype(BF16))
        wq_b, wo_b = xq_w[l].astype(BF16), xo_w[l].astype(BF16)
        w_router = _prep_router(router_group_w[l], router_expert_w[l])
        last = l == depth - 1
        moe_args = (_row(norm_ffn[l]), w_router, moe_w_gate, moe_w_up, moe_w_down, l, _row(norm_final), last)

        mk, mv = norm_matmul(mem2d, g_mix, w_kv_all, ((X_WIDTH,), (X_WIDTH,)), l, normalize=False)
        outs_p["mem_k"].append(mk.reshape(B, n_mem, N_MEM_HEADS, X_HEAD_DIM))
        outs_p["mem_v"].append(mv.reshape(B, n_mem, N_MEM_HEADS, X_HEAD_DIM))
        mk, mv = mk.reshape(1, B, n_mem, X_WIDTH), mv.reshape(1, B, n_mem, X_WIDTH)

        p_m, p_if, p_r, p_s, p_g = norm_matmul(xp, g_mix, w_in_all, IN_SPLITS, l, out_dtypes=IN_DTYPES)
        gts = jnp.swapaxes(p_if[:, :2 * H].reshape(B, S, 2 * H), 1, 2)
        gi = jnp.pad(gts[:, :H], ((0, 0), (0, SUBLANE - H), (0, 0)))
        gf = jnp.pad(gts[:, H:], ((0, 0), (0, SUBLANE - H), (0, 0)))
        h_m, caug, m_o = mlstm(p_m.reshape(B, S, 4 * M_WIDTH), gi, gf, b_i, b_f, hn,
                               jnp.zeros((B, H, M_HEAD_DIM, MLSTM_AUG), F32),
                               jnp.zeros((B, SUBLANE, LANE), F32), L=MLSTM_CHUNK)
        outs_p["C"].append(caug[..., :M_HEAD_DIM])
        outs_p["n"].append(caug[..., M_HEAD_DIM])
        outs_p["m"].append(m_o[:, :H, 0])

        h_r, h_last, conv_last = rglru(p_r.reshape(B, S, 2 * R_WIDTH),
                                       jnp.zeros((B, CONV_W - 1, R_WIDTH), F32),
                                       jnp.zeros((B, 1, R_WIDTH), F32), *r_args,
                                       TR=_row_tile(S), Bs=1, out_dtype=BF16)
        outs_p["h"].append(h_last[:, 0])
        outs_p["conv"].append(conv_last)

        p_s3 = p_s.reshape(B, S, S_WIDTH + 2 * S_KV_WIDTH)
        kcol, vcol = S_WIDTH // S_KV_WIDTH, S_WIDTH // S_KV_WIDTH + 1
        prev = lambda col: (lambda b, n: (b, jnp.maximum(n - 1, 0), col))
        cur = lambda col: (lambda b, n: (b, n, col))
        kv_blk = (NBP, WINDOW, S_KV_WIDTH)
        h_s = swa((p_s3,) * 5, (cur(0), prev(kcol), cur(kcol), prev(vcol), cur(vcol)),
                  ((NBP, WINDOW, S_WIDTH), kv_blk, kv_blk, kv_blk, kv_blk),
                  bias_p, sink_p, grid=(B // NBP, S // WINDOW), NBS=NBP, TQ=WINDOW, mask_first=True,
                  out_dtype=BF16)
        outs_p["swa_k"].append(p_s3[:, S - WINDOW:, S_WIDTH:S_WIDTH + S_KV_WIDTH]
                               .reshape(B, WINDOW, S_KV_HEADS, S_HEAD_DIM))
        outs_p["swa_v"].append(p_s3[:, S - WINDOW:, S_WIDTH + S_KV_WIDTH:]
                               .reshape(B, WINDOW, S_KV_HEADS, S_HEAD_DIM))

        xp = merge_out(h_m.reshape(Tp, M_WIDTH), h_r.reshape(Tp, R_WIDTH), h_s.reshape(Tp, S_WIDTH),
                       p_g, xp, *merge_w)
        xp = xattn(xp.reshape(B, S, D), _row(norm_mem[l]), wq_b, wo_b, mk, mv, layer=0,
                   nb=1, tq=512 if S % 512 == 0 else _row_tile(S), interleaved=False).reshape(Tp, D)

        p_m, p_if, p_r, p_s, p_g = norm_matmul(xs, g_mix, w_in_all, IN_SPLITS, l, out_dtypes=IN_DTYPES)
        gts = jnp.swapaxes(p_if[:, :2 * H].reshape(NS, TS, 2 * H), 1, 2)
        gts = jnp.pad(gts, ((0, 0), (0, 0), (0, LANE - TS)))
        gi = jnp.pad(gts[:, :H], ((0, 0), (0, SUBLANE - H), (0, 0)))
        gf = jnp.pad(gts[:, H:], ((0, 0), (0, SUBLANE - H), (0, 0)))
        m0 = jnp.broadcast_to(jnp.pad(state_mlstm_m[l], ((0, 0), (0, SUBLANE - H)))[:, :, None],
                              (NS, SUBLANE, LANE))
        h_m, c_s, n_s, m_o = mlstm_short(_pad_rows(p_m.reshape(NS, TS, 4 * M_WIDTH), SUBLANE), gi, gf, b_i, b_f,
                                         hn, state_mlstm_C, state_mlstm_n, m0, layer=l,
                                         NBS=MLSTM_SHORT_BATCH, TS=TS)
        h_m = h_m[:, :TS].reshape(Ts, M_WIDTH)
        outs_s["C"].append(c_s)
        outs_s["n"].append(n_s)
        outs_s["m"].append(m_o[:, :H, 0])

        tmaj = lambda a: jnp.swapaxes(a, 0, 1).reshape(1, a.shape[0] * a.shape[1], a.shape[2])
        h_r, h_last, conv_last = rglru(tmaj(p_r.reshape(NS, TS, 2 * R_WIDTH)), tmaj(state_rglru_conv[l]),
                                       state_rglru_h[l].reshape(1, NS, R_WIDTH), *r_args,
                                       TR=TS * NS, Bs=NS, out_dtype=F32)
        h_r = jnp.swapaxes(h_r.reshape(TS, NS, R_WIDTH), 0, 1).reshape(Ts, R_WIDTH)
        outs_s["h"].append(h_last[0])
        outs_s["conv"].append(jnp.swapaxes(conv_last.reshape(CONV_W - 1, NS, R_WIDTH), 0, 1))

        TQ, NBS = SWA_SHORT_TQ, SWA_SHORT_BATCH
        p_s3 = p_s.reshape(NS, TS, S_WIDTH + 2 * S_KV_WIDTH)
        q_s = _pad_rows(p_s3[:, :, :S_WIDTH], TQ)
        k_new = p_s3[:, :, S_WIDTH:S_WIDTH + S_KV_WIDTH]
        v_new = p_s3[:, :, S_WIDTH + S_KV_WIDTH:]
        full = lambda b, n: (b, 0, 0)
        cache = lambda b, n: (l * (NS // NBS) + b, 0, 0)
        h_s = swa((q_s, k_caches, _pad_rows(k_new, TQ), v_caches, _pad_rows(v_new, TQ)),
                  (full, cache, full, cache, full),
                  ((NBS, TQ, S_WIDTH), (NBS, WINDOW, S_KV_WIDTH), (NBS, TQ, S_KV_WIDTH),
                   (NBS, WINDOW, S_KV_WIDTH), (NBS, TQ, S_KV_WIDTH)),
                  bias_s, sink_s, grid=(NS // NBS, 1), NBS=NBS, TQ=TQ, mask_first=False, out_dtype=F32)
        h_s = h_s[:, :TS].reshape(Ts, S_WIDTH)
        outs_s["swa_k"].append(k_new)
        outs_s["swa_v"].append(v_new)

        xs = merge_out(h_m, h_r, h_s, p_g, xs, *merge_w)
        xs8 = xattn(_pad_rows(xs.reshape(NS, TS, D), SUBLANE), _row(norm_mem[l]), wq_b, wo_b,
                    cache_k_rows, cache_v_rows, layer=l, nb=XATTN_SHORT_BATCH, tq=SUBLANE, interleaved=True)

        (xp, xs), slots = moe_layer([xp, xs8[:, :TS].reshape(Ts, D)], slots, *moe_args)

    st = lambda d, k: jnp.stack(d[k])

    def shifted_cache(caches, new):
        out = jnp.concatenate([caches.reshape(depth, NS, WINDOW, S_KV_WIDTH)[:, :, TS:], jnp.stack(new)], axis=2)
        return out.reshape(depth, NS, WINDOW, S_KV_HEADS, S_HEAD_DIM)

    return (xp.reshape(B, S, D), xs.reshape(NS, TS, D), st(outs_p, "mem_k"), st(outs_p, "mem_v"),
            st(outs_p, "swa_k"), st(outs_p, "swa_v"), st(outs_p, "C"), st(outs_p, "n"), st(outs_p, "m"),
            st(outs_p, "h"), st(outs_p, "conv"),
            shifted_cache(k_caches, outs_s["swa_k"]), shifted_cache(v_caches, outs_s["swa_v"]),
            st(outs_s, "C"), st(outs_s, "n"), st(outs_s, "m"), st(outs_s, "h"), st(outs_s, "conv"))
```
